```python
import jax
import jax.numpy as jnp
from jax import lax

D_MODEL = 2048
BATCH = 4
SEQ = 2048
DEPTH = 1
DEC_BATCH = 8
DEC_SEQ = 8
PAST_LEN = 16384
PAGE_SIZE = 128

HEAD_DIM = 128
FOX_HEADS = 8
FOX_WIDTH = FOX_HEADS * HEAD_DIM
NSA_HEADS = 8
NSA_KV_GROUPS = 2
HEADS_PER_GROUP = NSA_HEADS // NSA_KV_GROUPS
NSA_WIDTH = NSA_HEADS * HEAD_DIM
KV_WIDTH = NSA_KV_GROUPS * HEAD_DIM
N_NSA_BRANCHES = 3
N_MIXERS = 2
CMP_BLOCK = 32
CMP_STRIDE = 16
SEL_BLOCK = 64
N_SELECT = 16
N_LOCAL_SEL = 2
WINDOW = 512
N_GROUPS = 4
EXPERTS_PER_GROUP = 8
N_EXPERTS = N_GROUPS * EXPERTS_PER_GROUP
TOP_K_IN_GROUP = 2
D_FF_EXPERT = 256
ROPE_THETA = 10000.0
NORM_EPS = 1e-6
Q_BLOCK = 128
SEL_Q_BLOCK = 32
POOL_NUM = 5
POOL_DEN = 4
NEG_INF = -1e30
FORCE_SCORE = 1e9
IN_SIZES = (FOX_WIDTH, FOX_WIDTH, FOX_WIDTH, FOX_HEADS, NSA_WIDTH, KV_WIDTH, KV_WIDTH, KV_WIDTH, KV_WIDTH, KV_WIDTH, KV_WIDTH, NSA_HEADS * N_NSA_BRANCHES, N_MIXERS * D_MODEL)
IN_COLS = sum(IN_SIZES)

kernel_name = 'fox_nsa_gated_hybrid_hier_moe_step'

F32 = jnp.float32


def _rms_norm(x, g):
    xf = x.astype(F32)
    y = xf * lax.rsqrt(jnp.mean(xf * xf, axis=-1, keepdims=True) + NORM_EPS)
    return (y * g.astype(F32)).astype(x.dtype)


def _rope(x, pos):
    half = HEAD_DIM // 2
    inv = ROPE_THETA ** (-jnp.arange(half, dtype=F32) / half)
    ang = pos.astype(F32)[:, None] * inv[None, :]
    cos = jnp.cos(ang)[None, :, None, :]
    sin = jnp.sin(ang)[None, :, None, :]
    xf = x.astype(F32)
    x1, x2 = xf[..., :half], xf[..., half:]
    return jnp.concatenate([x1 * cos - x2 * sin, x2 * cos + x1 * sin], axis=-1).astype(x.dtype)


def _split_in(z):
    cuts = []
    acc = 0
    for size in IN_SIZES[:-1]:
        acc += size
        cuts.append(acc)
    return jnp.split(z, cuts, axis=-1)


def _fox_attention(q, k, v, cum, q_start):
    b, tq, h, d = q.shape
    tk = k.shape[1]
    qb = Q_BLOCK if tq % Q_BLOCK == 0 else tq
    nb = tq // qb
    scale = d ** -0.5
    kf = k.astype(F32)
    vf = v.astype(F32)
    cum_t = jnp.transpose(cum.astype(F32), (0, 2, 1))
    k_pos = jnp.arange(tk)
    q_blocks = jnp.transpose(q.astype(F32).reshape(b, nb, qb, h, d), (1, 0, 2, 3, 4))

    def one_block(args):
        q_blk, blk = args
        start = q_start + blk * qb
        q_pos = start + jnp.arange(qb)
        cum_q = lax.dynamic_slice_in_dim(cum_t, start, qb, axis=2)
        s = jnp.einsum('bqhd,bkhd->bhqk', q_blk, kf) * scale
        s = s + cum_q[..., None] - cum_t[:, :, None, :]
        s = jnp.where(k_pos[None, :] <= q_pos[:, None], s, NEG_INF)
        p = jax.nn.softmax(s, axis=-1)
        return jnp.einsum('bhqk,bkhd->bqhd', p, vf)

    out = lax.map(one_block, (q_blocks, jnp.arange(nb)))
    return jnp.transpose(out, (1, 0, 2, 3, 4)).reshape(b, tq, h, d)


def _compress(rows, pos_emb, w1, w2):
    b, tk, g, d = rows.shape
    nc = (tk - CMP_BLOCK) // CMP_STRIDE + 1
    idx = jnp.arange(nc)[:, None] * CMP_STRIDE + jnp.arange(CMP_BLOCK)[None, :]
    blk = rows[:, idx].astype(F32) + pos_emb.astype(F32)[None, None, :, None, :]
    blk = jnp.transpose(blk, (0, 1, 3, 2, 4)).reshape(b, nc, g, CMP_BLOCK * d)
    hid = jax.nn.gelu(blk @ w1.astype(F32))
    return hid @ w2.astype(F32)


def _cmp_attention(q, kc, vc, q_start):
    tq = q.shape[1]
    nc = kc.shape[1]
    s = jnp.einsum('bqgrd,bngd->bgrqn', q.astype(F32), kc) * (HEAD_DIM ** -0.5)
    q_pos = q_start + jnp.arange(tq)
    blk_end = jnp.arange(nc) * CMP_STRIDE + CMP_BLOCK - 1
    visible = blk_end[None, :] <= q_pos[:, None]
    p = jax.nn.softmax(jnp.where(visible, s, NEG_INF), axis=-1) * visible
    out = jnp.einsum('bgrqn,bngd->bqgrd', p, vc)
    return out, p


def _select_blocks(p_grp, q_start, n_keys):
    tq, nc = p_grp.shape[2], p_grp.shape[3]
    ns = -(-n_keys // SEL_BLOCK)
    c0 = jnp.arange(nc)[:, None] * CMP_STRIDE
    s0 = jnp.arange(ns)[None, :] * SEL_BLOCK
    cover = jnp.clip(jnp.minimum(c0 + CMP_BLOCK, s0 + SEL_BLOCK) - jnp.maximum(c0, s0), 0, None).astype(F32) / CMP_BLOCK
    imp = jnp.einsum('bgqn,nj->bgqj', p_grp, cover)
    cur = (q_start + jnp.arange(tq)) // SEL_BLOCK
    j = jnp.arange(ns)
    dist = cur[:, None] - j[None, :]
    forced = (j[None, :] == 0) | ((dist >= 0) & (dist < N_LOCAL_SEL))
    score = jnp.where(forced, FORCE_SCORE, imp)
    score = jnp.where(dist >= 0, score, NEG_INF)
    _, idx = lax.top_k(score, min(N_SELECT, ns))
    return idx


def _slc_attention(q, k, v, idx, q_start):
    b, tq, g, r, d = q.shape
    tk = k.shape[1]
    ns = -(-tk // SEL_BLOCK)
    pad = ns * SEL_BLOCK - tk

    def blocks(t):
        t = jnp.pad(t.astype(F32), ((0, 0), (0, pad), (0, 0), (0, 0)))
        return jnp.transpose(t.reshape(b, ns, SEL_BLOCK, g, d), (0, 3, 1, 2, 4))

    kb, vb = blocks(k), blocks(v)
    n_sel = idx.shape[-1]
    qb = SEL_Q_BLOCK if tq % SEL_Q_BLOCK == 0 else tq
    nb = tq // qb
    q_blocks = jnp.transpose(q.astype(F32).reshape(b, nb, qb, g, r, d), (1, 0, 2, 3, 4, 5))
    i_blocks = jnp.transpose(idx.reshape(b, g, nb, qb, n_sel), (2, 0, 1, 3, 4))
    bi = jnp.arange(b)[:, None, None, None]
    gi = jnp.arange(g)[None, :, None, None]
    offs = jnp.arange(SEL_BLOCK)
    scale = d ** -0.5

    def one_block(args):
        q_blk, i_blk, blk = args
        q_pos = q_start + blk * qb + jnp.arange(qb)
        kg = kb[bi, gi, i_blk]
        vg = vb[bi, gi, i_blk]
        k_pos = i_blk[..., None] * SEL_BLOCK + offs
        causal = k_pos <= q_pos[None, None, :, None, None]
        s = jnp.einsum('bqgrd,bgqnld->bgrqnl', q_blk, kg) * scale
        s = jnp.where(causal[:, :, None], s, NEG_INF).reshape(b, g, r, qb, n_sel * SEL_BLOCK)
        p = jax.nn.softmax(s, axis=-1).reshape(b, g, r, qb, n_sel, SEL_BLOCK)
        return jnp.einsum('bgrqnl,bgqnld->bqgrd', p, vg)

    out = lax.map(one_block, (q_blocks, i_blocks, jnp.arange(nb)))
    return jnp.transpose(out, (1, 0, 2, 3, 4, 5)).reshape(b, tq, g, r, d)


def _win_attention(q, k, v, q_start, k_start):
    b, tq, g, r, d = q.shape
    off = q_start - k_start
    kp = jnp.pad(k.astype(F32), ((0, 0), (WINDOW, 0), (0, 0), (0, 0)))
    vp = jnp.pad(v.astype(F32), ((0, 0), (WINDOW, 0), (0, 0), (0, 0)))
    qb = Q_BLOCK if tq % Q_BLOCK == 0 else tq
    nb = tq // qb
    span = qb + WINDOW
    q_blocks = jnp.transpose(q.astype(F32).reshape(b, nb, qb, g, r, d), (1, 0, 2, 3, 4, 5))
    scale = d ** -0.5

    def one_block(args):
        q_blk, blk = args
        start = off + blk * qb
        kw = lax.dynamic_slice_in_dim(kp, start, span, axis=1)
        vw = lax.dynamic_slice_in_dim(vp, start, span, axis=1)
        key_i = start - WINDOW + jnp.arange(span)
        qry_i = start + jnp.arange(qb)
        band = (key_i[None, :] >= 0) & (key_i[None, :] <= qry_i[:, None]) & (key_i[None, :] > qry_i[:, None] - WINDOW)
        s = jnp.einsum('bqgrd,bkgd->bgrqk', q_blk, kw) * scale
        p = jax.nn.softmax(jnp.where(band, s, NEG_INF), axis=-1)
        return jnp.einsum('bgrqk,bkgd->bqgrd', p, vw)

    out = lax.map(one_block, (q_blocks, jnp.arange(nb)))
    return jnp.transpose(out, (1, 0, 2, 3, 4, 5)).reshape(b, tq, g, r, d)


def _hier_moe(u, w_rg, b_rg, w_re, b_re, w_gate_e, w_up_e, w_down_e):
    shp = u.shape
    ut = u.reshape(-1, D_MODEL)
    n = ut.shape[0]
    rows = jnp.arange(n)
    g_logits = (ut @ w_rg).astype(F32) + b_rg.astype(F32)
    g_prob = jax.nn.softmax(g_logits, axis=-1)
    g_sel = jnp.argmax(g_logits, axis=-1)
    e_logits = ((ut @ w_re).astype(F32) + b_re.astype(F32)).reshape(n, N_GROUPS, EXPERTS_PER_GROUP)
    e_prob = jax.nn.softmax(e_logits[rows, g_sel], axis=-1)
    top_p, top_i = lax.top_k(e_prob, TOP_K_IN_GROUP)
    top_p = top_p / jnp.sum(top_p, axis=-1, keepdims=True)
    weights = g_prob[rows, g_sel][:, None] * top_p
    expert_id = g_sel[:, None] * EXPERTS_PER_GROUP + top_i
    combine = jnp.sum(jax.nn.one_hot(expert_id, N_EXPERTS, dtype=F32) * weights[..., None], axis=1)
    hg = jnp.einsum('nd,edf->nef', ut, w_gate_e).astype(F32)
    hu = jnp.einsum('nd,edf->nef', ut, w_up_e).astype(F32)
    hid = (jax.nn.silu(hg) * hu * combine[..., None]).astype(u.dtype)
    out = jnp.einsum('nef,efd->nd', hid, w_down_e)
    return out.reshape(shp)


def setup_inputs(seed: int = 0) -> dict:
    key = jax.random.key(seed)
    ks = jax.random.split(key, 48)
    n_pages = PAST_LEN // PAGE_SIZE
    n_used = DEC_BATCH * n_pages
    n_phys = (n_used * POOL_NUM + POOL_DEN - 1) // POOL_DEN
    win_buf = min(WINDOW, PAST_LEN)

    def nrm(i, shape, scale):
        return jax.random.normal(ks[i], shape, F32) * scale

    def gain(i, shape):
        return 1.0 + nrm(i, shape, 0.05)

    page_table = jax.random.permutation(ks[11], n_phys)[:n_used].reshape(DEC_BATCH, n_pages).astype(jnp.int32)
    return {
        'x_prompt': nrm(0, (BATCH, SEQ, D_MODEL), 1.0),
        'x_sample': nrm(1, (DEC_BATCH, DEC_SEQ, D_MODEL), 1.0),
        'cache_fox_k': nrm(2, (DEPTH, n_phys, PAGE_SIZE, FOX_HEADS, HEAD_DIM), 1.0),
        'cache_fox_v': nrm(3, (DEPTH, n_phys, PAGE_SIZE, FOX_HEADS, HEAD_DIM), 1.0),
        'cache_fox_logf': jax.nn.log_sigmoid(2.5 + nrm(4, (DEPTH, n_phys, PAGE_SIZE, FOX_HEADS), 1.0)),
        'cache_cmp_k': nrm(5, (DEPTH, n_phys, PAGE_SIZE, NSA_KV_GROUPS, HEAD_DIM), 1.0),
        'cache_cmp_v': nrm(6, (DEPTH, n_phys, PAGE_SIZE, NSA_KV_GROUPS, HEAD_DIM), 1.0),
        'cache_slc_k': nrm(7, (DEPTH, n_phys, PAGE_SIZE, NSA_KV_GROUPS, HEAD_DIM), 1.0),
        'cache_slc_v': nrm(8, (DEPTH, n_phys, PAGE_SIZE, NSA_KV_GROUPS, HEAD_DIM), 1.0),
        'cache_win_k': nrm(9, (DEPTH, DEC_BATCH, win_buf, NSA_KV_GROUPS, HEAD_DIM), 1.0),
        'cache_win_v': nrm(10, (DEPTH, DEC_BATCH, win_buf, NSA_KV_GROUPS, HEAD_DIM), 1.0),
        'page_table': page_table,
        'norm_mix': gain(12, (DEPTH, D_MODEL)),
        'w_in': nrm(13, (DEPTH, D_MODEL, IN_COLS), D_MODEL ** -0.5),
        'fox_f_bias': jax.random.uniform(ks[14], (DEPTH, FOX_HEADS), F32, 1.0, 4.0),
        'fox_q_norm': gain(15, (DEPTH, HEAD_DIM)),
        'fox_k_norm': gain(16, (DEPTH, HEAD_DIM)),
        'nsa_q_norm': gain(17, (DEPTH, HEAD_DIM)),
        'nsa_k_norm': gain(18, (DEPTH, HEAD_DIM)),
        'cmp_pos_k': nrm(19, (DEPTH, CMP_BLOCK, HEAD_DIM), 0.5),
        'cmp_w1_k': nrm(20, (DEPTH, CMP_BLOCK * HEAD_DIM, HEAD_DIM), (CMP_BLOCK * HEAD_DIM) ** -0.5),
        'cmp_w2_k': nrm(21, (DEPTH, HEAD_DIM, HEAD_DIM), HEAD_DIM ** -0.5),
        'cmp_pos_v': nrm(22, (DEPTH, CMP_BLOCK, HEAD_DIM), 0.5),
        'cmp_w1_v': nrm(23, (DEPTH, CMP_BLOCK * HEAD_DIM, HEAD_DIM), (CMP_BLOCK * HEAD_DIM) ** -0.5),
        'cmp_w2_v': nrm(24, (DEPTH, HEAD_DIM, HEAD_DIM), HEAD_DIM ** -0.5),
        'nsa_gate_bias': nrm(25, (DEPTH, NSA_HEADS * N_NSA_BRANCHES), 0.1),
        'w_up_fox': nrm(26, (DEPTH, FOX_WIDTH, D_MODEL), FOX_WIDTH ** -0.5),
        'w_up_nsa': nrm(27, (DEPTH, NSA_WIDTH, D_MODEL), NSA_WIDTH ** -0.5),
        'w_out': nrm(28, (DEPTH, D_MODEL, D_MODEL), D_MODEL ** -0.5),
        'norm_ffn': gain(29, (DEPTH, D_MODEL)),
        'w_router_group': nrm(30, (DEPTH, D_MODEL, N_GROUPS), D_MODEL ** -0.5),
        'b_router_group': nrm(31, (DEPTH, N_GROUPS), 0.01),
        'w_router_expert': nrm(32, (DEPTH, D_MODEL, N_EXPERTS), D_MODEL ** -0.5),
        'b_router_expert': nrm(33, (DEPTH, N_EXPERTS), 0.01),
        'w_gate_e': nrm(34, (DEPTH, N_EXPERTS, D_MODEL, D_FF_EXPERT), D_MODEL ** -0.5),
        'w_up_e': nrm(35, (DEPTH, N_EXPERTS, D_MODEL, D_FF_EXPERT), D_MODEL ** -0.5),
        'w_down_e': nrm(36, (DEPTH, N_EXPERTS, D_FF_EXPERT, D_MODEL), D_FF_EXPERT ** -0.5),
    }


def reference(x_prompt, x_sample, cache_fox_k, cache_fox_v, cache_fox_logf, cache_cmp_k, cache_cmp_v, cache_slc_k, cache_slc_v, cache_win_k, cache_win_v, page_table, norm_mix, w_in, fox_f_bias, fox_q_norm, fox_k_norm, nsa_q_norm, nsa_k_norm, cmp_pos_k, cmp_w1_k, cmp_w2_k, cmp_pos_v, cmp_w1_v, cmp_w2_v, nsa_gate_bias, w_up_fox, w_up_nsa, w_out, norm_ffn, w_router_group, b_router_group, w_router_expert, b_router_expert, w_gate_e, w_up_e, w_down_e):
    n_seq_pages = page_table.shape[1]
    past_len = n_seq_pages * cache_fox_k.shape[2]
    dec_b = page_table.shape[0]

    def gather_pages(pool):
        return pool[page_table].reshape((dec_b, past_len) + pool.shape[2:])

    def layer(x, l, q_start, past):
        b, t, _ = x.shape
        u = _rms_norm(x, norm_mix[l])
        fq, fk, fv, ff, nq, ck, cv, sk, sv, wk, wv, ng, mg = _split_in(u @ w_in[l])
        pos = q_start + jnp.arange(t)

        def heads(a, h):
            return a.reshape(b, t, h, HEAD_DIM)

        def nsa_key(a):
            return _rope(_rms_norm(heads(a, NSA_KV_GROUPS), nsa_k_norm[l]), pos)

        fq = _rms_norm(heads(fq, FOX_HEADS), fox_q_norm[l])
        fk = _rms_norm(heads(fk, FOX_HEADS), fox_k_norm[l])
        fv = heads(fv, FOX_HEADS)
        logf = jax.nn.log_sigmoid(ff.astype(F32) + fox_f_bias[l].astype(F32))
        nq = _rope(_rms_norm(heads(nq, NSA_HEADS), nsa_q_norm[l]), pos).reshape(b, t, NSA_KV_GROUPS, HEADS_PER_GROUP, HEAD_DIM)
        ck, sk, wk = nsa_key(ck), nsa_key(sk), nsa_key(wk)
        cv, sv, wv = heads(cv, NSA_KV_GROUPS), heads(sv, NSA_KV_GROUPS), heads(wv, NSA_KV_GROUPS)

        if past is None:
            fk_all, fv_all, lf_all = fk, fv, logf
            ck_all, cv_all, sk_all, sv_all = ck, cv, sk, sv
            wk_all, wv_all = wk, wv
            k_start = 0
        else:
            pfk, pfv, pfl, pck, pcv, psk, psv, pwk, pwv = past

            def cat(old, new):
                return jnp.concatenate([old.astype(new.dtype), new], axis=1)

            fk_all, fv_all, lf_all = cat(pfk, fk), cat(pfv, fv), cat(pfl, logf)
            ck_all, cv_all, sk_all, sv_all = cat(pck, ck), cat(pcv, cv), cat(psk, sk), cat(psv, sv)
            wk_all, wv_all = cat(pwk, wk), cat(pwv, wv)
            k_start = q_start - pwk.shape[1]

        cum = jnp.cumsum(lf_all.astype(F32), axis=1)
        o_fox = _fox_attention(fq, fk_all, fv_all, cum, q_start)

        kc = _compress(ck_all, cmp_pos_k[l], cmp_w1_k[l], cmp_w2_k[l])
        vc = _compress(cv_all, cmp_pos_v[l], cmp_w1_v[l], cmp_w2_v[l])
        o_cmp, p_cmp = _cmp_attention(nq, kc, vc, q_start)
        idx = _select_blocks(jnp.sum(p_cmp, axis=2), q_start, sk_all.shape[1])
        o_slc = _slc_attention(nq, sk_all, sv_all, idx, q_start)
        o_win = _win_attention(nq, wk_all, wv_all, q_start, k_start)
        gates = jax.nn.sigmoid(ng.astype(F32) + nsa_gate_bias[l].astype(F32)).reshape(b, t, NSA_KV_GROUPS, HEADS_PER_GROUP, N_NSA_BRANCHES)
        o_nsa = gates[..., 0:1] * o_cmp + gates[..., 1:2] * o_slc + gates[..., 2:3] * o_win

        branch_a = o_fox.reshape(b, t, FOX_WIDTH).astype(x.dtype) @ w_up_fox[l]
        branch_b = o_nsa.reshape(b, t, NSA_WIDTH).astype(x.dtype) @ w_up_nsa[l]
        g_a, g_b = jnp.split(jax.nn.sigmoid(mg), N_MIXERS, axis=-1)
        h = x + (g_a * branch_a + g_b * branch_b) @ w_out[l]
        y = h + _hier_moe(_rms_norm(h, norm_ffn[l]), w_router_group[l], b_router_group[l], w_router_expert[l], b_router_expert[l], w_gate_e[l], w_up_e[l], w_down_e[l])
        keep = min(WINDOW, wk_all.shape[1])
        state = (fk, fv, logf, ck, cv, sk, sv, wk_all[:, -keep:], wv_all[:, -keep:])
        return y, state

    h_p = x_prompt
    h_s = x_sample
    p_states = []
    s_states = []
    for l in range(DEPTH):
        h_p, st_p = layer(h_p, l, 0, None)
        p_states.append(st_p)
        past = (gather_pages(cache_fox_k[l]), gather_pages(cache_fox_v[l]), gather_pages(cache_fox_logf[l]), gather_pages(cache_cmp_k[l]), gather_pages(cache_cmp_v[l]), gather_pages(cache_slc_k[l]), gather_pages(cache_slc_v[l]), cache_win_k[l], cache_win_v[l])
        h_s, st_s = layer(h_s, l, past_len, past)
        s_states.append(st_s)

    p_fox_k, p_fox_v, p_fox_logf, p_cmp_k, p_cmp_v, p_slc_k, p_slc_v, p_win_k, p_win_v = [jnp.stack([st[i] for st in p_states], axis=0) for i in range(9)]
    s_fox_k, s_fox_v, s_fox_logf, s_cmp_k, s_cmp_v, s_slc_k, s_slc_v, s_win_k, s_win_v = [jnp.stack([st[i] for st in s_states], axis=0) for i in range(9)]
    return (h_p, h_s, p_fox_k, p_fox_v, p_fox_logf, p_cmp_k, p_cmp_v, p_slc_k, p_slc_v, p_win_k, p_win_v, s_fox_k, s_fox_v, s_fox_logf, s_cmp_k, s_cmp_v, s_slc_k, s_slc_v, s_win_k, s_win_v)
```

```python
import functools

import numpy as np
import jax
import jax.numpy as jnp
from jax import lax
from jax.experimental import pallas as pl
from jax.experimental.pallas import tpu as pltpu

F32 = jnp.float32
BF16 = jnp.bfloat16

HEAD_DIM = 128
FOX_HEADS = 8
NSA_HEADS = 8
NSA_KV_GROUPS = 2
HEADS_PER_GROUP = NSA_HEADS // NSA_KV_GROUPS
N_NSA_BRANCHES = 3
CMP_BLOCK = 32
CMP_STRIDE = 16
SEL_BLOCK = 64
N_SELECT = 16
N_LOCAL_SEL = 2
WINDOW = 512
N_GROUPS = 4
EXPERTS_PER_GROUP = 8
N_EXPERTS = N_GROUPS * EXPERTS_PER_GROUP
ROPE_THETA = 10000.0
NORM_EPS = 1e-6
NEG_INF = -1e30
FORCE_SCORE = 1e9
PAD_SCORE = -2e38
TAKEN_SCORE = -3e38

LANES = 128
SUBLANES = 8
VMEM_LIMIT_BYTES = 48 * 1024 * 1024

HIGHEST = lax.Precision.HIGHEST


def _cparams(*sem):
    return pltpu.CompilerParams(dimension_semantics=tuple(sem), vmem_limit_bytes=VMEM_LIMIT_BYTES)


def _row_tile(n, pref):
    t = min(n, pref)
    assert n % t == 0, (n, t)
    return t


def _rmsnorm_body(x_ref, g_ref, o_ref):
    x = x_ref[...]
    y = x * lax.rsqrt(jnp.mean(x * x, axis=-1, keepdims=True) + NORM_EPS)
    o_ref[...] = (y * g_ref[...]).astype(o_ref.dtype)


def _rmsnorm(x2d, gain):
    n, d = x2d.shape
    tm = _row_tile(n, 512)
    return pl.pallas_call(
        _rmsnorm_body,
        grid=(n // tm,),
        in_specs=[pl.BlockSpec((tm, d), lambda i: (i, 0)), pl.BlockSpec((1, d), lambda i: (0, 0))],
        out_specs=pl.BlockSpec((tm, d), lambda i: (i, 0)),
        out_shape=jax.ShapeDtypeStruct((n, d), BF16),
        compiler_params=_cparams("parallel"),
        name="rmsnorm",
    )(x2d, gain.reshape(1, d))


def _mm_body(epilogue, n_extra, a_ref, w_ref, *rest):
    acc = jnp.dot(a_ref[...], w_ref[...], preferred_element_type=F32)
    epilogue(acc, rest[:n_extra], rest[n_extra:])


def _matmul(a, w, epilogue, extras, extra_specs, out_shapes, out_specs, tm, tn, name):
    n, k = a.shape
    c = w.shape[1]
    assert n % tm == 0 and c % tn == 0, (n, tm, c, tn)
    return pl.pallas_call(
        functools.partial(_mm_body, epilogue, len(extras)),
        grid=(n // tm, c // tn),
        in_specs=[pl.BlockSpec((tm, k), lambda i, j: (i, 0)), pl.BlockSpec((k, tn), lambda i, j: (0, j))]
        + list(extra_specs),
        out_specs=out_specs,
        out_shape=out_shapes,
        compiler_params=_cparams("parallel", "parallel"),
        name=name,
    )(a, w, *extras)


def _head_norm(x, gain):
    return x * lax.rsqrt(jnp.mean(x * x, axis=-1, keepdims=True) + NORM_EPS) * gain


def _rope_rot(x, cos, sin_signed):
    return x * cos + pltpu.roll(x, HEAD_DIM // 2, 1) * sin_signed


def _ep_head_norm(scale, acc, extras, outs):
    (gain_ref,) = extras
    (o_ref,) = outs
    for h in range(acc.shape[1] // HEAD_DIM):
        sl = slice(h * HEAD_DIM, (h + 1) * HEAD_DIM)
        o_ref[:, sl] = (_head_norm(acc[:, sl], gain_ref[...]) * scale).astype(o_ref.dtype)


def _ep_head_norm_rope(scale, acc, extras, outs):
    gain_ref, cos_ref, sin_ref = extras
    (o_ref,) = outs
    for h in range(acc.shape[1] // HEAD_DIM):
        sl = slice(h * HEAD_DIM, (h + 1) * HEAD_DIM)
        y = _rope_rot(_head_norm(acc[:, sl], gain_ref[...]), cos_ref[...], sin_ref[...])
        o_ref[:, sl] = (y * scale).astype(o_ref.dtype)


def _ep_identity(acc, extras, outs):
    (o_ref,) = outs
    o_ref[...] = acc.astype(o_ref.dtype)


def _ep_sigmoid(acc, extras, outs):
    (o_ref,) = outs
    o_ref[...] = jax.nn.sigmoid(acc).astype(o_ref.dtype)


def _ep_small(acc, extras, outs):
    (bias_ref,) = extras
    (o_ref,) = outs
    z = acc + bias_ref[...]
    col = lax.broadcasted_iota(jnp.int32, z.shape, 1)
    e = jnp.exp(-jnp.abs(z))
    log_sig = jnp.minimum(z, 0.0) - jnp.log(1.0 + e)
    sig = jnp.where(z >= 0, 1.0, e) / (1.0 + e)
    o_ref[...] = jnp.where(col < FOX_HEADS, log_sig, sig)


def _ep_residual(acc, extras, outs):
    (x_ref,) = extras
    (o_ref,) = outs
    o_ref[...] = x_ref[...] + acc


def _cumsum_body(x_ref, o_ref):
    x = x_ref[0]
    p, w = x.shape
    lane = lax.broadcasted_iota(jnp.int32, x.shape, 1)
    row = lax.broadcasted_iota(jnp.int32, x.shape, 0)
    within = x
    total = x
    s = FOX_HEADS
    while s < w:
        within = within + jnp.where(lane >= s, pltpu.roll(within, s, 1), 0.0)
        total = total + pltpu.roll(total, s, 1)
        s *= 2
    incl = total
    s = 1
    while s < p:
        incl = incl + jnp.where(row >= s, pltpu.roll(incl, s, 0), 0.0)
        s *= 2
    o_ref[0] = within + (incl - total)


def _cumsum_pages(lf_pages):
    b, p, w = lf_pages.shape
    return pl.pallas_call(
        _cumsum_body,
        grid=(b,),
        in_specs=[pl.BlockSpec((1, p, w), lambda i: (i, 0, 0))],
        out_specs=pl.BlockSpec((1, p, w), lambda i: (i, 0, 0)),
        out_shape=jax.ShapeDtypeStruct((b, p, w), F32),
        compiler_params=_cparams("parallel"),
        name="fox_cumsum",
    )(lf_pages)


def _fox_key_bias(logf_rows, t_total):
    b, t, h = logf_rows.shape
    rows_per_page = LANES
    n_pages = -(-t // rows_per_page)
    p_pad = -(-n_pages // SUBLANES) * SUBLANES
    padded = jnp.pad(logf_rows, ((0, 0), (0, p_pad * rows_per_page - t), (0, 0)))
    cum = _cumsum_pages(padded.reshape(b, p_pad, rows_per_page * h))
    cum = cum.reshape(b, p_pad * rows_per_page, h)[:, :t_total]
    return -jnp.transpose(cum, (0, 2, 1))


def _flash_body(mode, heads, tq, tk, n_kv, gate_col, *refs):
    if mode == "fox":
        q_ref, k_ref, v_ref, bias_ref, o_ref, kb_ref, vb_ref = refs
    elif mode == "slc":
        q_ref, k_ref, v_ref, sel_ref, expand_ref, gate_ref, o_ref, kb_ref, vb_ref, selx_ref = refs
    else:
        q_ref, k_ref, v_ref, gate_ref, o_ref, kb_ref, vb_ref = refs
    unit = pl.program_id(1)
    qi = pl.program_id(2)

    @pl.when(qi == 0)
    def _():
        kb_ref[...] = k_ref[0].astype(BF16)
        vb_ref[...] = v_ref[0].astype(BF16)

    if mode == "slc":
        sel = sel_ref[0, 0].astype(BF16)
        for j in range(n_kv):
            selx_ref[j] = jnp.dot(sel, expand_ref[:, j * tk:(j + 1) * tk], preferred_element_type=F32)

    q_pos = qi * tq + lax.broadcasted_iota(jnp.int32, (tq, tk), 0)
    col = lax.broadcasted_iota(jnp.int32, (tq, tk), 1)
    if mode == "win":
        lo = jnp.maximum(qi - (-(-WINDOW // tk)), 0)
    else:
        lo = 0

    for r in range(heads):
        q = q_ref[0, :, r * HEAD_DIM:(r + 1) * HEAD_DIM]

        def step(j, carry):
            m, l, acc = carry
            ks = pl.multiple_of(j * tk, tk)
            kb = kb_ref[pl.ds(ks, tk), :]
            vb = vb_ref[pl.ds(ks, tk), :]
            s = lax.dot_general(q, kb, (((1,), (1,)), ((), ())), preferred_element_type=F32)
            k_pos = ks + col
            ok = k_pos <= q_pos
            if mode == "fox":
                s = s + bias_ref[0, :, pl.ds(ks, tk)]
            elif mode == "slc":
                ok = ok & (selx_ref[j] > 0.5)
            else:
                ok = ok & (k_pos > q_pos - WINDOW)
            s = jnp.where(ok, s, NEG_INF)
            m_new = jnp.maximum(m, jnp.max(s, axis=-1, keepdims=True))
            alpha = jnp.exp(m - m_new)
            p = jnp.exp(s - m_new)
            l = alpha * l + jnp.sum(p, axis=-1, keepdims=True)
            acc = alpha * acc + jnp.dot(p.astype(BF16), vb, preferred_element_type=F32)
            return m_new, l, acc

        init = (jnp.full((tq, 1), NEG_INF, F32), jnp.zeros((tq, 1), F32), jnp.zeros((tq, HEAD_DIM), F32))
        m, l, acc = lax.fori_loop(lo, qi + 1, step, init)
        w = 1.0 / l
        if mode != "fox":
            gate = gate_ref[0]
            lane = lax.broadcasted_iota(jnp.int32, gate.shape, 1)
            want = FOX_HEADS + (unit * heads + r) * N_NSA_BRANCHES + gate_col
            w = w * jnp.sum(jnp.where(lane == want, gate, 0.0), axis=-1, keepdims=True)
        o_ref[0, :, r * HEAD_DIM:(r + 1) * HEAD_DIM] = (acc * w).astype(o_ref.dtype)


def _flash_attention(mode, q, k, v, k_col0, t, *, bias=None, sel=None, gate=None, gate_col=0):
    b = q.shape[0]
    heads = 1 if mode == "fox" else HEADS_PER_GROUP
    units = q.shape[2] // (heads * HEAD_DIM)
    tq = tk = _row_tile(t, 256)
    n_kv = t // tk
    qw = heads * HEAD_DIM
    in_specs = [
        pl.BlockSpec((1, tq, qw), lambda bi, u, i: (bi, i, u)),
        pl.BlockSpec((1, t, HEAD_DIM), lambda bi, u, i: (bi, 0, k_col0 + u)),
        pl.BlockSpec((1, t, HEAD_DIM), lambda bi, u, i: (bi, 0, k_col0 + u)),
    ]
    args = [q, k, v]
    scratch = [pltpu.VMEM((t, HEAD_DIM), BF16), pltpu.VMEM((t, HEAD_DIM), BF16)]
    if mode == "fox":
        in_specs.append(pl.BlockSpec((1, 1, t), lambda bi, u, i: (bi * units + u, 0, 0)))
        args.append(bias.reshape(b * units, 1, t))
    else:
        if mode == "slc":
            nsp = sel.shape[-1]
            expand = (np.arange(nsp)[:, None] == (np.arange(t)[None, :] // SEL_BLOCK)).astype(np.float32)
            in_specs.append(pl.BlockSpec((1, 1, tq, nsp), lambda bi, u, i: (bi, u, i, 0)))
            in_specs.append(pl.BlockSpec((nsp, t), lambda bi, u, i: (0, 0)))
            args += [sel, jnp.asarray(expand, BF16)]
            scratch.append(pltpu.VMEM((n_kv, tq, tk), F32))
        in_specs.append(pl.BlockSpec((1, tq, LANES), lambda bi, u, i: (bi, i, 0)))
        args.append(gate)
    return pl.pallas_call(
        functools.partial(_flash_body, mode, heads, tq, tk, n_kv, gate_col),
        grid=(b, units, t // tq),
        in_specs=in_specs,
        out_specs=pl.BlockSpec((1, tq, qw), lambda bi, u, i: (bi, i, u)),
        out_shape=jax.ShapeDtypeStruct((b, t, units * qw), BF16),
        scratch_shapes=scratch,
        compiler_params=_cparams("parallel", "parallel", "arbitrary"),
        name="attn_" + mode,
    )(*args)


def _paged_body(units, rows, rows_b, n_pages, use_gate, *refs):
    if use_gate:
        (tbl_ref, q_ref, kp_ref, vp_ref, bias_ref, kn_ref, vn_ref, biasn_ref, gate_ref, o_ref,
         m_ref, l_ref, acc_ref) = refs
    else:
        (tbl_ref, q_ref, kp_ref, vp_ref, bias_ref, kn_ref, vn_ref, biasn_ref, o_ref,
         m_ref, l_ref, acc_ref) = refs
        gate_ref = None
    p = pl.program_id(1)

    @pl.when(p == 0)
    def _():
        m_ref[...] = jnp.full(m_ref.shape, NEG_INF, F32)
        l_ref[...] = jnp.zeros(l_ref.shape, F32)
        acc_ref[...] = jnp.zeros(acc_ref.shape, F32)

    def update(u, s, vb):
        m = m_ref[u]
        m_new = jnp.maximum(m, jnp.max(s, axis=-1, keepdims=True))
        alpha = jnp.exp(m - m_new)
        pr = jnp.exp(s - m_new)
        l_ref[u] = alpha * l_ref[u] + jnp.sum(pr, axis=-1, keepdims=True)
        acc_ref[u] = alpha * acc_ref[u] + jnp.dot(pr.astype(BF16), vb, preferred_element_type=F32)
        m_ref[u] = m_new

    def expand_bias(bb):
        if rows_b == 1 or rows_b == rows:
            return bb
        return jnp.concatenate([bb] * (rows // rows_b), axis=0)

    @pl.when(p < n_pages)
    def _():
        for u in range(units):
            sl = slice(u * HEAD_DIM, (u + 1) * HEAD_DIM)
            kb = kp_ref[0, :, sl].astype(BF16)
            vb = vp_ref[0, :, sl].astype(BF16)
            s = lax.dot_general(q_ref[0, u], kb, (((1,), (1,)), ((), ())), preferred_element_type=F32)
            update(u, s + expand_bias(bias_ref[0, u]), vb)

    @pl.when(p == n_pages)
    def _():
        for u in range(units):
            sl = slice(u * HEAD_DIM, (u + 1) * HEAD_DIM)
            kb = kn_ref[0, :, sl].astype(BF16)
            vb = vn_ref[0, :, sl].astype(BF16)
            s = lax.dot_general(q_ref[0, u], kb, (((1,), (1,)), ((), ())), preferred_element_type=F32)
            update(u, s + biasn_ref[0, u], vb)
            w = 1.0 / l_ref[u]
            if use_gate:
                w = w * gate_ref[0, u]
            o_ref[0, u] = acc_ref[u] * w


def _paged_attention(q, k_pool, v_pool, table, bias, k_new, v_new, bias_new, gate=None):
    b, units, rows, _ = q.shape
    n_pages = table.shape[1]
    page = k_pool.shape[1]
    width = k_pool.shape[2]
    rows_b = bias.shape[2]
    n_new = k_new.shape[1]
    last = n_pages - 1

    def pool_map(bi, p, tbl):
        return (tbl[bi * n_pages + jnp.minimum(p, last)], 0, 0)

    in_specs = [
        pl.BlockSpec((1, units, rows, HEAD_DIM), lambda bi, p, tbl: (bi, 0, 0, 0)),
        pl.BlockSpec((1, page, width), pool_map),
        pl.BlockSpec((1, page, width), pool_map),
        pl.BlockSpec((1, units, rows_b, page), lambda bi, p, tbl: (bi, 0, 0, jnp.minimum(p, last))),
        pl.BlockSpec((1, n_new, width), lambda bi, p, tbl: (bi, 0, 0)),
        pl.BlockSpec((1, n_new, width), lambda bi, p, tbl: (bi, 0, 0)),
        pl.BlockSpec((1, units, rows, n_new), lambda bi, p, tbl: (bi, 0, 0, 0)),
    ]
    args = [q, k_pool, v_pool, bias, k_new, v_new, bias_new]
    if gate is not None:
        in_specs.append(pl.BlockSpec((1, units, rows, 1), lambda bi, p, tbl: (bi, 0, 0, 0)))
        args.append(gate)
    grid_spec = pltpu.PrefetchScalarGridSpec(
        num_scalar_prefetch=1,
        grid=(b, n_pages + 1),
        in_specs=in_specs,
        out_specs=pl.BlockSpec((1, units, rows, HEAD_DIM), lambda bi, p, tbl: (bi, 0, 0, 0)),
        scratch_shapes=[pltpu.VMEM((units, rows, 1), F32), pltpu.VMEM((units, rows, 1), F32),
                        pltpu.VMEM((units, rows, HEAD_DIM), F32)],
    )
    return pl.pallas_call(
        functools.partial(_paged_body, units, rows, rows_b, n_pages, gate is not None),
        grid_spec=grid_spec,
        out_shape=jax.ShapeDtypeStruct((b, units, rows, HEAD_DIM), F32),
        compiler_params=_cparams("parallel", "arbitrary"),
        name="attn_paged",
    )(table.reshape(-1), *args)


def _gather_body(n, tbl_ref, src_ref, dst_ref, sem):
    def copy(i):
        return pltpu.make_async_copy(src_ref.at[tbl_ref[i]], dst_ref.at[i], sem)

    def start(i, c):
        copy(i).start()
        return c

    def wait(i, c):
        copy(i).wait()
        return c

    lax.fori_loop(0, n, start, 0)
    lax.fori_loop(0, n, wait, 0)


def _gather_pages(pool, table_flat):
    n = table_flat.shape[0]
    grid_spec = pltpu.PrefetchScalarGridSpec(
        num_scalar_prefetch=1,
        grid=(1,),
        in_specs=[pl.BlockSpec(memory_space=pl.ANY)],
        out_specs=pl.BlockSpec(memory_space=pl.ANY),
        scratch_shapes=[pltpu.SemaphoreType.DMA(())],
    )
    return pl.pallas_call(
        functools.partial(_gather_body, n),
        grid_spec=grid_spec,
        out_shape=jax.ShapeDtypeStruct((n,) + pool.shape[1:], pool.dtype),
        compiler_params=pltpu.CompilerParams(dimension_semantics=("arbitrary",)),
        name="page_gather",
    )(table_flat, pool)


def _cmp_proj_body(x_ref, pa_ref, pb_ref, wa_ref, wb_ref, za_ref, zb_ref):
    x = x_ref[0]
    za_ref[0] = jnp.dot((x + pa_ref[...]).astype(BF16), wa_ref[...], preferred_element_type=F32)
    zb_ref[0] = jnp.dot((x + pb_ref[...]).astype(BF16), wb_ref[...], preferred_element_type=F32)


def _cmp_mlp_body(n_chunks, za_ref, zb_ref, w2_ref, o_ref):
    zb_next = pltpu.roll(zb_ref[0], n_chunks - 1, 0)
    hid = za_ref[0] + zb_next
    c = 0.7978845608028654
    hid = 0.5 * hid * (1.0 + jnp.tanh(c * (hid + 0.044715 * (hid * hid * hid))))
    hb = hid.astype(BF16)
    w2 = w2_ref[...]
    for g in range(NSA_KV_GROUPS):
        sl = slice(g * HEAD_DIM, (g + 1) * HEAD_DIM)
        o_ref[0, :, sl] = jnp.dot(hb[:, sl], w2, preferred_element_type=F32)


def _compress(rows, pos_emb, w1, w2):
    b, tc, gw = rows.shape
    g = NSA_KV_GROUPS
    n_chunks = tc // CMP_STRIDE
    cw = CMP_STRIDE * gw
    x = rows.reshape(b, n_chunks, cw)
    w1r = w1.reshape(2, CMP_STRIDE, HEAD_DIM, HEAD_DIM)
    eye = jnp.eye(g, dtype=w1.dtype)
    wcat = jnp.einsum("hldj,ge->hlgdej", w1r, eye).reshape(2, cw, g * HEAD_DIM).astype(BF16)
    pos = jnp.broadcast_to(pos_emb.reshape(2, CMP_STRIDE, 1, HEAD_DIM), (2, CMP_STRIDE, g, HEAD_DIM)).reshape(2, 1, cw)
    cm = _row_tile(n_chunks, 128)
    za, zb = pl.pallas_call(
        _cmp_proj_body,
        grid=(b, n_chunks // cm),
        in_specs=[
            pl.BlockSpec((1, cm, cw), lambda bi, i: (bi, i, 0)),
            pl.BlockSpec((1, cw), lambda bi, i: (0, 0)),
            pl.BlockSpec((1, cw), lambda bi, i: (0, 0)),
            pl.BlockSpec((cw, g * HEAD_DIM), lambda bi, i: (0, 0)),
            pl.BlockSpec((cw, g * HEAD_DIM), lambda bi, i: (0, 0)),
        ],
        out_specs=[pl.BlockSpec((1, cm, g * HEAD_DIM), lambda bi, i: (bi, i, 0))] * 2,
        out_shape=[jax.ShapeDtypeStruct((b, n_chunks, g * HEAD_DIM), F32)] * 2,
        compiler_params=_cparams("parallel", "parallel"),
        name="cmp_proj",
    )(x, pos[0], pos[1], wcat[0], wcat[1])
    return pl.pallas_call(
        functools.partial(_cmp_mlp_body, n_chunks),
        grid=(b,),
        in_specs=[
            pl.BlockSpec((1, n_chunks, g * HEAD_DIM), lambda bi: (bi, 0, 0)),
            pl.BlockSpec((1, n_chunks, g * HEAD_DIM), lambda bi: (bi, 0, 0)),
            pl.BlockSpec((HEAD_DIM, HEAD_DIM), lambda bi: (0, 0)),
        ],
        out_specs=pl.BlockSpec((1, n_chunks, g * HEAD_DIM), lambda bi: (bi, 0, 0)),
        out_shape=jax.ShapeDtypeStruct((b, n_chunks, g * HEAD_DIM), F32),
        compiler_params=_cparams("parallel"),
        name="cmp_mlp",
    )(za, zb, w2.astype(BF16))


def _cmp_attn_body(tq, nc, ns, n_pick, q_start, q_ref, kc_ref, vc_ref, cover_ref, gate_ref, o_ref, sel_ref):
    unit = pl.program_id(1)
    qi = pl.program_id(2)
    kc = kc_ref[0].astype(BF16)
    vc = vc_ref[0].astype(BF16)
    ncp = kc.shape[0]
    q_pos = q_start + qi * tq + lax.broadcasted_iota(jnp.int32, (tq, ncp), 0)
    n_idx = lax.broadcasted_iota(jnp.int32, (tq, ncp), 1)
    visible = (n_idx * CMP_STRIDE + (CMP_BLOCK - 1) <= q_pos) & (n_idx < nc)
    gate = gate_ref[0]
    glane = lax.broadcasted_iota(jnp.int32, gate.shape, 1)
    p_grp = jnp.zeros((tq, ncp), F32)
    for r in range(HEADS_PER_GROUP):
        q = q_ref[0, :, r * HEAD_DIM:(r + 1) * HEAD_DIM]
        s = lax.dot_general(q, kc, (((1,), (1,)), ((), ())), preferred_element_type=F32)
        s = jnp.where(visible, s, NEG_INF)
        m = jnp.max(s, axis=-1, keepdims=True)
        e = jnp.where(visible, jnp.exp(s - m), 0.0)
        l = jnp.sum(e, axis=-1, keepdims=True)
        p = e / jnp.where(l > 0.0, l, 1.0)
        p_grp = p_grp + p
        want = FOX_HEADS + (unit * HEADS_PER_GROUP + r) * N_NSA_BRANCHES
        gcol = jnp.sum(jnp.where(glane == want, gate, 0.0), axis=-1, keepdims=True)
        o = jnp.dot(p.astype(BF16), vc, preferred_element_type=F32)
        o_ref[0, :, r * HEAD_DIM:(r + 1) * HEAD_DIM] = (o * gcol).astype(o_ref.dtype)

    imp = jnp.dot(p_grp, cover_ref[...], preferred_element_type=F32, precision=HIGHEST)
    nsp = imp.shape[1]
    j = lax.broadcasted_iota(jnp.int32, (tq, nsp), 1)
    cur = (q_start + qi * tq + lax.broadcasted_iota(jnp.int32, (tq, nsp), 0)) // SEL_BLOCK
    dist = cur - j
    forced = (j == 0) | ((dist >= 0) & (dist < N_LOCAL_SEL))
    score = jnp.where(forced, FORCE_SCORE, imp)
    score = jnp.where(dist >= 0, score, NEG_INF)
    score = jnp.where(j < ns, score, PAD_SCORE)
    sel = jnp.zeros((tq, nsp), F32)
    for _ in range(n_pick):
        mx = jnp.max(score, axis=-1, keepdims=True)
        first = jnp.min(jnp.where(score == mx, j, nsp), axis=-1, keepdims=True)
        hit = j == first
        sel = jnp.where(hit, 1.0, sel)
        score = jnp.where(hit, TAKEN_SCORE, score)
    sel_ref[0, 0] = sel


def _cmp_attention(nq, kc, vc, gate, t, q_start, n_keys):
    b = nq.shape[0]
    ncp = kc.shape[1]
    nc = (n_keys - CMP_BLOCK) // CMP_STRIDE + 1
    ns = -(-n_keys // SEL_BLOCK)
    nsp = -(-ns // LANES) * LANES
    n_pick = min(N_SELECT, ns)
    c0 = np.arange(ncp)[:, None] * CMP_STRIDE
    s0 = np.arange(nsp)[None, :] * SEL_BLOCK
    cover = np.clip(np.minimum(c0 + CMP_BLOCK, s0 + SEL_BLOCK) - np.maximum(c0, s0), 0, None).astype(np.float32) / CMP_BLOCK
    cover = cover * (np.arange(ncp)[:, None] < nc) * (np.arange(nsp)[None, :] < ns)
    tq = _row_tile(t, 256)
    qw = HEADS_PER_GROUP * HEAD_DIM
    return pl.pallas_call(
        functools.partial(_cmp_attn_body, tq, nc, ns, n_pick, q_start),
        grid=(b, NSA_KV_GROUPS, t // tq),
        in_specs=[
            pl.BlockSpec((1, tq, qw), lambda bi, u, i: (bi, i, u)),
            pl.BlockSpec((1, ncp, HEAD_DIM), lambda bi, u, i: (bi, 0, u)),
            pl.BlockSpec((1, ncp, HEAD_DIM), lambda bi, u, i: (bi, 0, u)),
            pl.BlockSpec((ncp, nsp), lambda bi, u, i: (0, 0)),
            pl.BlockSpec((1, tq, LANES), lambda bi, u, i: (bi, i, 0)),
        ],
        out_specs=[
            pl.BlockSpec((1, tq, qw), lambda bi, u, i: (bi, i, u)),
            pl.BlockSpec((1, 1, tq, nsp), lambda bi, u, i: (bi, u, i, 0)),
        ],
        out_shape=[
            jax.ShapeDtypeStruct((b, t, NSA_KV_GROUPS * qw), BF16),
            jax.ShapeDtypeStruct((b, NSA_KV_GROUPS, t, nsp), F32),
        ],
        compiler_params=_cparams("parallel", "parallel", "parallel"),
        name="attn_cmp_select",
    )(nq, kc, vc, jnp.asarray(cover), gate)


def _merge_body(a_ref, b1_ref, b2_ref, b3_ref, ua_ref, ub_ref, ga_ref, gb_ref, o_ref):
    a = a_ref[...]
    bsum = (b1_ref[...].astype(F32) + b2_ref[...].astype(F32) + b3_ref[...].astype(F32)).astype(BF16)
    ya = jnp.dot(a, ua_ref[...], preferred_element_type=F32)
    yb = jnp.dot(bsum, ub_ref[...], preferred_element_type=F32)
    o_ref[...] = (ga_ref[...] * ya + gb_ref[...] * yb).astype(o_ref.dtype)


def _merge(o_fox, o_cmp, o_slc, o_win, u_fox, u_nsa, mg, d_model):
    n, w = o_fox.shape
    tm = _row_tile(n, 512)
    tn = 512
    nj = d_model // tn
    row = pl.BlockSpec((tm, w), lambda i, j: (i, 0))
    return pl.pallas_call(
        _merge_body,
        grid=(n // tm, nj),
        in_specs=[row, row, row, row,
                  pl.BlockSpec((w, tn), lambda i, j: (0, j)),
                  pl.BlockSpec((w, tn), lambda i, j: (0, j)),
                  pl.BlockSpec((tm, tn), lambda i, j: (i, j)),
                  pl.BlockSpec((tm, tn), lambda i, j: (i, j + nj))],
        out_specs=pl.BlockSpec((tm, tn), lambda i, j: (i, j)),
        out_shape=jax.ShapeDtypeStruct((n, d_model), BF16),
        compiler_params=_cparams("parallel", "parallel"),
        name="mixer_merge",
    )(o_fox, o_cmp, o_slc, o_win, u_fox, u_nsa, mg, mg)


def _router_body(h_ref, g_ref, w_ref, b_ref, u_ref, route_ref):
    x = h_ref[...]
    xn = x * lax.rsqrt(jnp.mean(x * x, axis=-1, keepdims=True) + NORM_EPS) * g_ref[...]
    u_ref[...] = xn.astype(u_ref.dtype)
    z = jnp.dot(xn, w_ref[...], preferred_element_type=F32, precision=HIGHEST) + b_ref[...]
    c = lax.broadcasted_iota(jnp.int32, z.shape, 1)
    big = z.shape[1]
    is_g = c < N_GROUPS
    gmax = jnp.max(jnp.where(is_g, z, -jnp.inf), axis=-1, keepdims=True)
    gsel = jnp.min(jnp.where(is_g & (z == gmax), c, big), axis=-1, keepdims=True)
    gsum = jnp.sum(jnp.where(is_g, jnp.exp(z - gmax), 0.0), axis=-1, keepdims=True)
    lo = N_GROUPS + gsel * EXPERTS_PER_GROUP
    in_grp = (c >= lo) & (c < lo + EXPERTS_PER_GROUP)
    z1 = jnp.max(jnp.where(in_grp, z, -jnp.inf), axis=-1, keepdims=True)
    c1 = jnp.min(jnp.where(in_grp & (z == z1), c, big), axis=-1, keepdims=True)
    rest = in_grp & (c != c1)
    z2 = jnp.max(jnp.where(rest, z, -jnp.inf), axis=-1, keepdims=True)
    c2 = jnp.min(jnp.where(rest & (z == z2), c, big), axis=-1, keepdims=True)
    e2 = jnp.exp(z2 - z1)
    p1 = 1.0 / (1.0 + e2)
    p2 = e2 / (1.0 + e2)
    gp = 1.0 / gsum
    out = jnp.where(c == 0, (c1 - N_GROUPS).astype(F32), 0.0)
    out = jnp.where(c == 1, (c2 - N_GROUPS).astype(F32), out)
    out = jnp.where(c == 2, gp * p1, out)
    out = jnp.where(c == 3, gp * p2, out)
    route_ref[...] = out


def _router(h, gain, w_cat, b_cat):
    n, d = h.shape
    tm = _row_tile(n, 512)
    return pl.pallas_call(
        _router_body,
        grid=(n // tm,),
        in_specs=[pl.BlockSpec((tm, d), lambda i: (i, 0)), pl.BlockSpec((1, d), lambda i: (0, 0)),
                  pl.BlockSpec((d, LANES), lambda i: (0, 0)), pl.BlockSpec((1, LANES), lambda i: (0, 0))],
        out_specs=[pl.BlockSpec((tm, d), lambda i: (i, 0)), pl.BlockSpec((tm, LANES), lambda i: (i, 0))],
        out_shape=[jax.ShapeDtypeStruct((n, d), BF16), jax.ShapeDtypeStruct((n, LANES), F32)],
        compiler_params=_cparams("parallel"),
        name="moe_router",
    )(h, gain.reshape(1, d), w_cat, b_cat)


def _expert_body(te_ref, nv_ref, x_ref, cw_ref, wg_ref, wu_ref, wd_ref, o_ref):
    i = pl.program_id(0)

    @pl.when(i < nv_ref[0])
    def _():
        x = x_ref[...]
        hg = jnp.dot(x, wg_ref[0], preferred_element_type=F32)
        hu = jnp.dot(x, wu_ref[0], preferred_element_type=F32)
        hid = (hg * jax.nn.sigmoid(hg)) * hu * cw_ref[...]
        o_ref[...] = jnp.dot(hid.astype(BF16), wd_ref[0], preferred_element_type=F32)

    @pl.when(i >= nv_ref[0])
    def _():
        o_ref[...] = jnp.zeros(o_ref.shape, o_ref.dtype)


def _experts(x_sorted, w_sorted, tile_expert, n_valid, wg, wu, wd, tm):
    rows, d = x_sorted.shape
    n_tiles = rows // tm
    f = wg.shape[2]

    def row_map(i, te, nv):
        return (jnp.minimum(i, nv[0] - 1), 0)

    def w_map(i, te, nv):
        return (te[i], 0, 0)

    grid_spec = pltpu.PrefetchScalarGridSpec(
        num_scalar_prefetch=2,
        grid=(n_tiles,),
        in_specs=[pl.BlockSpec((tm, d), row_map), pl.BlockSpec((tm, 1), row_map),
                  pl.BlockSpec((1, d, f), w_map), pl.BlockSpec((1, d, f), w_map), pl.BlockSpec((1, f, d), w_map)],
        out_specs=pl.BlockSpec((tm, d), lambda i, te, nv: (i, 0)),
    )
    return pl.pallas_call(
        _expert_body,
        grid_spec=grid_spec,
        out_shape=jax.ShapeDtypeStruct((rows, d), F32),
        compiler_params=_cparams("arbitrary"),
        name="moe_experts",
    )(tile_expert, n_valid, x_sorted, w_sorted, wg, wu, wd)


def _moe(h, gain, w_router, b_router, wg, wu, wd):
    n, d = h.shape
    ut, route = _router(h, gain, w_router, b_router)
    expert = route[:, :2].astype(jnp.int32).reshape(-1)
    weight = route[:, 2:4].reshape(-1)
    n_asg = 2 * n
    tm = 256 if n_asg >= 8192 else 16
    n_tiles = n_asg // tm + N_EXPERTS
    order = jnp.argsort(expert, stable=True)
    counts = jnp.sum(expert[:, None] == jnp.arange(N_EXPERTS)[None, :], axis=0)
    tiles_per = (counts + tm - 1) // tm
    tile_end = jnp.cumsum(tiles_per)
    start_row = (tile_end - tiles_per) * tm
    first = jnp.cumsum(counts) - counts
    sorted_expert = expert[order]
    slot_sorted = start_row[sorted_expert] + jnp.arange(n_asg) - first[sorted_expert]
    slot = jnp.zeros((n_asg,), jnp.int32).at[order].set(slot_sorted.astype(jnp.int32))
    row_token = jnp.zeros((n_tiles * tm,), jnp.int32).at[slot].set(jnp.arange(n_asg, dtype=jnp.int32) // 2)
    row_weight = jnp.zeros((n_tiles * tm,), F32).at[slot].set(weight)
    n_valid = tile_end[-1].astype(jnp.int32)
    tile_expert = jnp.searchsorted(tile_end, jnp.arange(n_tiles), side="right").astype(jnp.int32)
    tile_expert = jnp.minimum(tile_expert, tile_expert[jnp.maximum(n_valid - 1, 0)])
    x_sorted = jnp.take(ut, row_token, axis=0)
    out_sorted = _experts(x_sorted, row_weight[:, None], tile_expert, n_valid.reshape(1), wg, wu, wd, tm)
    slot2 = slot.reshape(n, 2)
    return h + jnp.take(out_sorted, slot2[:, 0], axis=0) + jnp.take(out_sorted, slot2[:, 1], axis=0)


def _rope_tables(positions):
    half = HEAD_DIM // 2
    inv = ROPE_THETA ** (-np.arange(half, dtype=np.float64) / half)
    ang = np.asarray(positions, np.float64)[:, None] * inv[None, :]
    cos = np.concatenate([np.cos(ang), np.cos(ang)], axis=-1)
    sin = np.concatenate([-np.sin(ang), np.sin(ang)], axis=-1)
    return jnp.asarray(cos, F32), jnp.asarray(sin, F32)


def _project(u, wts, p, b, t, q_start):
    n = u.shape[0]
    tm = _row_tile(n, 1024)
    tn = 256
    scale = HEAD_DIM ** -0.5
    gain_spec = pl.BlockSpec((1, HEAD_DIM), lambda i, j: (0, 0))

    def run(w, epilogue, extras, specs, dtype, name, tn_=tn):
        shape = jax.ShapeDtypeStruct((n, w.shape[1]), dtype)
        spec = pl.BlockSpec((tm, tn_), lambda i, j: (i, j))
        return _matmul(u, w, epilogue, extras, specs, shape, spec, tm, tn_, name)

    cos, sin = _rope_tables(q_start + (np.arange(max(tm, t)) % t))
    n_tab = cos.shape[0] // tm
    tab_spec = pl.BlockSpec((tm, HEAD_DIM), lambda i, j: (i % n_tab, 0))

    fq = run(wts["fq"], functools.partial(_ep_head_norm, scale), [p["fox_q_norm"]], [gain_spec], BF16, "proj_fq")
    fk = run(wts["fk"], functools.partial(_ep_head_norm, 1.0), [p["fox_k_norm"]], [gain_spec], F32, "proj_fk")
    fv = run(wts["fv"], _ep_identity, [], [], F32, "proj_fv")
    nq = run(wts["nq"], functools.partial(_ep_head_norm_rope, scale), [p["nsa_q_norm"], cos, sin],
             [gain_spec, tab_spec, tab_spec], BF16, "proj_nq")
    nk = run(wts["nk"], functools.partial(_ep_head_norm_rope, 1.0), [p["nsa_k_norm"], cos, sin],
             [gain_spec, tab_spec, tab_spec], F32, "proj_nk")
    nv = run(wts["nv"], _ep_identity, [], [], F32, "proj_nv")
    mg = run(wts["mg"], _ep_sigmoid, [], [], F32, "proj_mg")
    small = run(wts["small"], _ep_small, [p["small_bias"]], [pl.BlockSpec((1, LANES), lambda i, j: (0, 0))],
                F32, "proj_small", LANES)
    return fq, fk, fv, nq, nk, nv, mg, small


def _layer(x, p, wts, q_start, past):
    b, t, d = x.shape
    n = b * t
    g = NSA_KV_GROUPS
    kvw = g * HEAD_DIM
    x2d = x.reshape(n, d)
    u = _rmsnorm(x2d, p["norm_mix"])
    fq, fk, fv, nq, nk, nv, mg, small = _project(u, wts, p, b, t, q_start)

    logf = small[:, :FOX_HEADS].reshape(b, t, FOX_HEADS)
    gate3 = small.reshape(b, t, LANES)
    fq3 = fq.reshape(b, t, -1)
    nq3 = nq.reshape(b, t, -1)
    fk3 = fk.reshape(b, t, -1)
    fv3 = fv.reshape(b, t, -1)
    nk3 = nk.reshape(b, t, 3 * kvw)
    nv3 = nv.reshape(b, t, 3 * kvw)
    ck, sk, wk = (nk3[:, :, i * kvw:(i + 1) * kvw] for i in range(3))
    cv, sv, wv = (nv3[:, :, i * kvw:(i + 1) * kvw] for i in range(3))

    if past is None:
        n_keys = t
        bias = _fox_key_bias(logf, t)
        o_fox = _flash_attention("fox", fq3, fk3, fv3, 0, t, bias=bias)
        kc = _compress(ck, p["cmp_pos_k"], p["cmp_w1_k"], p["cmp_w2_k"])
        vc = _compress(cv, p["cmp_pos_v"], p["cmp_w1_v"], p["cmp_w2_v"])
        o_cmp, sel = _cmp_attention(nq3, kc, vc, gate3, t, q_start, n_keys)
        o_slc = _flash_attention("slc", nq3, nk3, nv3, g, t, sel=sel, gate=gate3, gate_col=1)
        o_win = _flash_attention("win", nq3, nk3, nv3, 2 * g, t, gate=gate3, gate_col=2)
        keep = min(WINDOW, t)
        win_k, win_v = wk[:, t - keep:], wv[:, t - keep:]
    else:
        table = past["table"]
        n_pages = table.shape[1]
        page = past["fox_k"].shape[1]
        past_len = n_pages * page
        n_keys = past_len + t
        tbl = table.reshape(-1)
        lf_cache = _gather_pages(past["fox_logf"].reshape(-1, SUBLANES, (page * FOX_HEADS) // SUBLANES), tbl)
        lf_all = jnp.concatenate([lf_cache.reshape(b, past_len, FOX_HEADS), logf], axis=1)
        bias = _fox_key_bias(lf_all, n_keys)
        causal = jnp.where(jnp.arange(t)[None, :] <= jnp.arange(t)[:, None], 0.0, NEG_INF).astype(F32)
        bias_new = bias[:, :, None, past_len:] + causal[None, None]
        qf = jnp.transpose(fq3.reshape(b, t, FOX_HEADS, HEAD_DIM), (0, 2, 1, 3))
        o = _paged_attention(qf, past["fox_k"], past["fox_v"], table, bias[:, :, None, :past_len],
                             fk3, fv3, bias_new)
        o_fox = jnp.transpose(o, (0, 2, 1, 3)).reshape(b, t, -1).astype(BF16)
        nc = (n_keys - CMP_BLOCK) // CMP_STRIDE + 1
        assert (nc + 1) * CMP_STRIDE <= past_len and past_len % (CMP_STRIDE * SUBLANES) == 0
        ck_all = _gather_pages(past["cmp_k"], tbl).reshape(b, past_len, kvw)
        cv_all = _gather_pages(past["cmp_v"], tbl).reshape(b, past_len, kvw)
        kc = _compress(ck_all, p["cmp_pos_k"], p["cmp_w1_k"], p["cmp_w2_k"])
        vc = _compress(cv_all, p["cmp_pos_v"], p["cmp_w1_v"], p["cmp_w2_v"])
        o_cmp, sel = _cmp_attention(nq3, kc, vc, gate3, t, q_start, n_keys)
        rows = HEADS_PER_GROUP * t
        qn = jnp.transpose(nq3.reshape(b, t, g, HEADS_PER_GROUP, HEAD_DIM), (0, 2, 3, 1, 4)).reshape(b, g, rows, HEAD_DIM)
        gates = small[:, FOX_HEADS:FOX_HEADS + NSA_HEADS * N_NSA_BRANCHES].reshape(b, t, g, HEADS_PER_GROUP, N_NSA_BRANCHES)
        gates = jnp.transpose(gates, (4, 0, 2, 3, 1)).reshape(N_NSA_BRANCHES, b, g, rows, 1)
        ns = -(-n_keys // SEL_BLOCK)
        key_sel = jnp.repeat(sel[..., :ns], SEL_BLOCK, axis=-1)[..., :n_keys]
        sel_bias = jnp.where(key_sel > 0.5, 0.0, NEG_INF).astype(F32)
        sel_new = jnp.tile(sel_bias[..., past_len:] + causal[None, None], (1, 1, HEADS_PER_GROUP, 1))
        o = _paged_attention(qn, past["slc_k"], past["slc_v"], table, sel_bias[..., :past_len],
                             sk, sv, sel_new, gate=gates[1])
        o_slc = _rows_to_tokens(o, b, t)
        win_len = past["win_k"].shape[1]
        assert win_len % page == 0
        k_start = q_start - win_len
        q_abs = q_start + jnp.arange(t)[:, None]
        k_abs = k_start + jnp.arange(win_len)[None, :]
        band = jnp.where((k_abs >= 0) & (k_abs > q_abs - WINDOW), 0.0, NEG_INF).astype(F32)
        band = jnp.broadcast_to(band[None, None], (b, g, t, win_len))
        band_new = jnp.where(jnp.arange(t)[None, :] > jnp.arange(t)[:, None] - WINDOW, causal, NEG_INF)
        band_new = jnp.broadcast_to(jnp.tile(band_new, (HEADS_PER_GROUP, 1))[None, None], (b, g, rows, t))
        win_table = jnp.arange(b * (win_len // page), dtype=jnp.int32).reshape(b, win_len // page)
        o = _paged_attention(qn, past["win_k"].reshape(-1, page, kvw), past["win_v"].reshape(-1, page, kvw),
                             win_table, band, wk, wv, band_new, gate=gates[2])
        o_win = _rows_to_tokens(o, b, t)
        wk_all = jnp.concatenate([past["win_k"], wk], axis=1)
        wv_all = jnp.concatenate([past["win_v"], wv], axis=1)
        keep = min(WINDOW, wk_all.shape[1])
        win_k, win_v = wk_all[:, -keep:], wv_all[:, -keep:]

    m = _merge(o_fox.reshape(n, -1), o_cmp.reshape(n, -1), o_slc.reshape(n, -1), o_win.reshape(n, -1),
               wts["up_fox"], wts["up_nsa"], mg, d)
    tm = _row_tile(n, 512)
    h = _matmul(m, wts["out"], _ep_residual, [x2d], [pl.BlockSpec((tm, 512), lambda i, j: (i, j))],
                jax.ShapeDtypeStruct((n, d), F32), pl.BlockSpec((tm, 512), lambda i, j: (i, j)), tm, 512, "out_proj")
    y = _moe(h, p["norm_ffn"], wts["router"], p["router_bias"], wts["gate_e"], wts["up_e"], wts["down_e"])

    def heads(a, hh):
        return a.reshape(b, -1, hh, HEAD_DIM)

    state = (heads(fk3, FOX_HEADS), heads(fv3, FOX_HEADS), logf, heads(ck, g), heads(cv, g), heads(sk, g),
             heads(sv, g), heads(win_k, g), heads(win_v, g))
    return y.reshape(b, t, d), state


def _rows_to_tokens(o, b, t):
    o = o.reshape(b, NSA_KV_GROUPS, HEADS_PER_GROUP, t, HEAD_DIM)
    return jnp.transpose(o, (0, 3, 1, 2, 4)).reshape(b, t, -1).astype(BF16)


def _prepare(l, norm_mix, w_in, fox_f_bias, fox_q_norm, fox_k_norm, nsa_q_norm, nsa_k_norm, cmp_pos_k, cmp_w1_k,
             cmp_w2_k, cmp_pos_v, cmp_w1_v, cmp_w2_v, nsa_gate_bias, w_up_fox, w_up_nsa, w_out, norm_ffn,
             w_router_group, b_router_group, w_router_expert, b_router_expert, w_gate_e, w_up_e, w_down_e):
    d = w_in.shape[1]
    fw = FOX_HEADS * HEAD_DIM
    nw = NSA_HEADS * HEAD_DIM
    kvw = NSA_KV_GROUPS * HEAD_DIM
    sizes = (fw, fw, fw, FOX_HEADS, nw, kvw, kvw, kvw, kvw, kvw, kvw, NSA_HEADS * N_NSA_BRANCHES, 2 * d)
    offs = np.concatenate([[0], np.cumsum(sizes)])
    w = w_in[l]

    def seg(i):
        return w[:, offs[i]:offs[i + 1]]

    n_small = FOX_HEADS + NSA_HEADS * N_NSA_BRANCHES
    small_w = jnp.pad(jnp.concatenate([seg(3), seg(11)], axis=1), ((0, 0), (0, LANES - n_small)))
    small_b = jnp.pad(jnp.concatenate([fox_f_bias[l], nsa_gate_bias[l]]), (0, LANES - n_small)).reshape(1, LANES)
    n_route = N_GROUPS + N_EXPERTS
    router_w = jnp.pad(jnp.concatenate([w_router_group[l], w_router_expert[l]], axis=1), ((0, 0), (0, LANES - n_route)))
    router_b = jnp.pad(jnp.concatenate([b_router_group[l], b_router_expert[l]]), (0, LANES - n_route)).reshape(1, LANES)
    wts = {
        "fq": seg(0).astype(BF16), "fk": seg(1).astype(BF16), "fv": seg(2).astype(BF16), "nq": seg(4).astype(BF16),
        "nk": jnp.concatenate([seg(5), seg(7), seg(9)], axis=1).astype(BF16),
        "nv": jnp.concatenate([seg(6), seg(8), seg(10)], axis=1).astype(BF16),
        "mg": seg(12).astype(BF16), "small": small_w.astype(BF16),
        "up_fox": w_up_fox[l].astype(BF16), "up_nsa": w_up_nsa[l].astype(BF16), "out": w_out[l].astype(BF16),
        "router": router_w,
        "gate_e": w_gate_e[l].astype(BF16), "up_e": w_up_e[l].astype(BF16), "down_e": w_down_e[l].astype(BF16),
    }
    p = {
        "norm_mix": norm_mix[l], "fox_q_norm": fox_q_norm[l].reshape(1, -1), "fox_k_norm": fox_k_norm[l].reshape(1, -1),
        "nsa_q_norm": nsa_q_norm[l].reshape(1, -1), "nsa_k_norm": nsa_k_norm[l].reshape(1, -1),
        "small_bias": small_b, "cmp_pos_k": cmp_pos_k[l], "cmp_w1_k": cmp_w1_k[l], "cmp_w2_k": cmp_w2_k[l],
        "cmp_pos_v": cmp_pos_v[l], "cmp_w1_v": cmp_w1_v[l], "cmp_w2_v": cmp_w2_v[l],
        "norm_ffn": norm_ffn[l], "router_bias": router_b,
    }
    return wts, p


def kernel(x_prompt, x_sample, cache_fox_k, cache_fox_v, cache_fox_logf, cache_cmp_k, cache_cmp_v, cache_slc_k, cache_slc_v, cache_win_k, cache_win_v, page_table, norm_mix, w_in, fox_f_bias, fox_q_norm, fox_k_norm, nsa_q_norm, nsa_k_norm, cmp_pos_k, cmp_w1_k, cmp_w2_k, cmp_pos_v, cmp_w1_v, cmp_w2_v, nsa_gate_bias, w_up_fox, w_up_nsa, w_out, norm_ffn, w_router_group, b_router_group, w_router_expert, b_router_expert, w_gate_e, w_up_e, w_down_e):
    depth = w_in.shape[0]
    n_phys, page = cache_fox_k.shape[1], cache_fox_k.shape[2]
    past_len = page_table.shape[1] * page
    h_p, h_s = x_prompt, x_sample
    p_states, s_states = [], []
    for l in range(depth):
        wts, p = _prepare(l, norm_mix, w_in, fox_f_bias, fox_q_norm, fox_k_norm, nsa_q_norm, nsa_k_norm, cmp_pos_k,
                          cmp_w1_k, cmp_w2_k, cmp_pos_v, cmp_w1_v, cmp_w2_v, nsa_gate_bias, w_up_fox, w_up_nsa,
                          w_out, norm_ffn, w_router_group, b_router_group, w_router_expert, b_router_expert,
                          w_gate_e, w_up_e, w_down_e)
        h_p, st_p = _layer(h_p, p, wts, 0, None)
        p_states.append(st_p)
        past = {
            "table": page_table,
            "fox_k": cache_fox_k[l].reshape(n_phys, page, -1), "fox_v": cache_fox_v[l].reshape(n_phys, page, -1),
            "fox_logf": cache_fox_logf[l],
            "cmp_k": cache_cmp_k[l].reshape(n_phys, page, -1), "cmp_v": cache_cmp_v[l].reshape(n_phys, page, -1),
            "slc_k": cache_slc_k[l].reshape(n_phys, page, -1), "slc_v": cache_slc_v[l].reshape(n_phys, page, -1),
            "win_k": cache_win_k[l].reshape(cache_win_k.shape[1], cache_win_k.shape[2], -1),
            "win_v": cache_win_v[l].reshape(cache_win_v.shape[1], cache_win_v.shape[2], -1),
        }
        h_s, st_s = _layer(h_s, p, wts, past_len, past)
        s_states.append(st_s)
    p_out = [jnp.stack([st[i] for st in p_states], axis=0) for i in range(9)]
    s_out = [jnp.stack([st[i] for st in s_states], axis=0) for i in range(9)]
    return (h_p, h_s, *p_out, *s_out)
```

```python
import functools

import numpy as np
import jax
import jax.numpy as jnp
from jax import lax
from jax.experimental import pallas as pl
from jax.experimental.pallas import tpu as pltpu

F32 = jnp.float32
BF16 = jnp.bfloat16

HEAD_DIM = 128
FOX_HEADS = 8
NSA_HEADS = 8
NSA_KV_GROUPS = 2
HEADS_PER_GROUP = NSA_HEADS // NSA_KV_GROUPS
N_NSA_BRANCHES = 3
CMP_BLOCK = 32
CMP_STRIDE = 16
SEL_BLOCK = 64
N_SELECT = 16
N_LOCAL_SEL = 2
WINDOW = 512
N_GROUPS = 4
EXPERTS_PER_GROUP = 8
N_EXPERTS = N_GROUPS * EXPERTS_PER_GROUP
ROPE_THETA = 10000.0
NORM_EPS = 1e-6
NEG_INF = -1e30
FORCE_SCORE = 1e9
PAD_SCORE = -2e38
TAKEN_SCORE = -3e38

LANES = 128
SUBLANES = 8
VMEM_LIMIT_BYTES = 48 * 1024 * 1024

HIGHEST = lax.Precision.HIGHEST


def _cparams(*sem):
    return pltpu.CompilerParams(dimension_semantics=tuple(sem), vmem_limit_bytes=VMEM_LIMIT_BYTES)


def _row_tile(n, pref):
    t = min(n, pref)
    assert n % t == 0, (n, t)
    return t


def _rmsnorm_body(x_ref, g_ref, o_ref):
    x = x_ref[...]
    y = x * lax.rsqrt(jnp.mean(x * x, axis=-1, keepdims=True) + NORM_EPS)
    o_ref[...] = (y * g_ref[...]).astype(o_ref.dtype)


def _rmsnorm(x2d, gain, out_dtype):
    n, d = x2d.shape
    tm = _row_tile(n, 512)
    return pl.pallas_call(
        _rmsnorm_body,
        grid=(n // tm,),
        in_specs=[pl.BlockSpec((tm, d), lambda i: (i, 0)), pl.BlockSpec((1, d), lambda i: (0, 0))],
        out_specs=pl.BlockSpec((tm, d), lambda i: (i, 0)),
        out_shape=jax.ShapeDtypeStruct((n, d), out_dtype),
        compiler_params=_cparams("parallel"),
        name="rmsnorm",
    )(x2d, gain.reshape(1, d))


def _dot_precision(dtype):
    return HIGHEST if dtype == F32 else None


def _mm_body(epilogue, n_extra, a_ref, w_ref, *rest):
    acc = jnp.dot(a_ref[...], w_ref[...], preferred_element_type=F32, precision=_dot_precision(a_ref.dtype))
    epilogue(acc, rest[:n_extra], rest[n_extra:])


def _matmul(a, w, epilogue, extras, extra_specs, out_shapes, out_specs, tm, tn, name):
    n, k = a.shape
    c = w.shape[1]
    assert n % tm == 0 and c % tn == 0, (n, tm, c, tn)
    return pl.pallas_call(
        functools.partial(_mm_body, epilogue, len(extras)),
        grid=(n // tm, c // tn),
        in_specs=[pl.BlockSpec((tm, k), lambda i, j: (i, 0)), pl.BlockSpec((k, tn), lambda i, j: (0, j))]
        + list(extra_specs),
        out_specs=out_specs,
        out_shape=out_shapes,
        compiler_params=_cparams("parallel", "parallel"),
        name=name,
    )(a, w, *extras)


def _head_norm(x, gain):
    return x * lax.rsqrt(jnp.mean(x * x, axis=-1, keepdims=True) + NORM_EPS) * gain


def _rope_rot(x, cos, sin_signed):
    return x * cos + pltpu.roll(x, HEAD_DIM // 2, 1) * sin_signed


def _ep_head_norm(scale, acc, extras, outs):
    (gain_ref,) = extras
    (o_ref,) = outs
    for h in range(acc.shape[1] // HEAD_DIM):
        sl = slice(h * HEAD_DIM, (h + 1) * HEAD_DIM)
        o_ref[:, sl] = (_head_norm(acc[:, sl], gain_ref[...]) * scale).astype(o_ref.dtype)


def _ep_head_norm_rope(scale, acc, extras, outs):
    gain_ref, cos_ref, sin_ref = extras
    (o_ref,) = outs
    for h in range(acc.shape[1] // HEAD_DIM):
        sl = slice(h * HEAD_DIM, (h + 1) * HEAD_DIM)
        y = _rope_rot(_head_norm(acc[:, sl], gain_ref[...]), cos_ref[...], sin_ref[...])
        o_ref[:, sl] = (y * scale).astype(o_ref.dtype)


def _ep_identity(acc, extras, outs):
    (o_ref,) = outs
    o_ref[...] = acc.astype(o_ref.dtype)


def _ep_sigmoid(acc, extras, outs):
    (o_ref,) = outs
    o_ref[...] = jax.nn.sigmoid(acc).astype(o_ref.dtype)


def _ep_small(acc, extras, outs):
    (bias_ref,) = extras
    (o_ref,) = outs
    z = acc + bias_ref[...]
    col = lax.broadcasted_iota(jnp.int32, z.shape, 1)
    e = jnp.exp(-jnp.abs(z))
    log_sig = jnp.minimum(z, 0.0) - jnp.log(1.0 + e)
    sig = jnp.where(z >= 0, 1.0, e) / (1.0 + e)
    o_ref[...] = jnp.where(col < FOX_HEADS, log_sig, sig)


def _ep_residual(acc, extras, outs):
    (x_ref,) = extras
    (o_ref,) = outs
    o_ref[...] = x_ref[...] + acc


def _fox_bias_body(n_pages, has_new, *refs):
    if has_new:
        tbl_ref, new_ref, pool_ref, o_ref, buf, sem = refs
    else:
        tbl_ref, pool_ref, o_ref, buf, sem = refs
    bi = pl.program_id(0)

    def copy(p):
        return pltpu.make_async_copy(pool_ref.at[tbl_ref[bi * n_pages + p]], buf.at[p], sem)

    def start(p, c):
        copy(p).start()
        return c

    def wait(p, c):
        copy(p).wait()
        return c

    lax.fori_loop(0, n_pages, start, 0)
    lax.fori_loop(0, n_pages, wait, 0)
    if has_new:
        buf[n_pages] = new_ref[0]
    x = buf[...]
    p, h, w = x.shape
    within = x.reshape(p * h, w)
    lane = lax.broadcasted_iota(jnp.int32, within.shape, 1)
    s = 1
    while s < w:
        within = within + jnp.where(lane >= s, pltpu.roll(within, s, 1), 0.0)
        s *= 2
    within = within.reshape(p, h, w)
    total = jnp.broadcast_to(within[:, :, w - 1:w], x.shape)
    incl = total
    s = 1
    while s < p:
        incl = incl + jnp.concatenate([jnp.zeros((s, h, w), F32), incl[:p - s]], axis=0)
        s *= 2
    o_ref[0] = -(within + (incl - total))


def _fox_key_bias(pool, table, new_page):
    b, n_pages = table.shape
    h, w = pool.shape[1:]
    has_new = new_page is not None
    p_tot = n_pages + (1 if has_new else 0)
    in_specs = [pl.BlockSpec(memory_space=pl.ANY)]
    args = [pool]
    if has_new:
        in_specs.insert(0, pl.BlockSpec((1, h, w), lambda i, tbl: (i, 0, 0)))
        args.insert(0, new_page)
    grid_spec = pltpu.PrefetchScalarGridSpec(
        num_scalar_prefetch=1,
        grid=(b,),
        in_specs=in_specs,
        out_specs=pl.BlockSpec((1, p_tot, h, w), lambda i, tbl: (i, 0, 0, 0)),
        scratch_shapes=[pltpu.VMEM((p_tot, h, w), F32), pltpu.SemaphoreType.DMA(())],
    )
    out = pl.pallas_call(
        functools.partial(_fox_bias_body, n_pages, has_new),
        grid_spec=grid_spec,
        out_shape=jax.ShapeDtypeStruct((b, p_tot, h, w), F32),
        compiler_params=_cparams("arbitrary"),
        name="fox_key_bias",
    )(table.reshape(-1), *args)
    return jnp.transpose(out, (0, 2, 1, 3)).reshape(b, h, p_tot * w)


def _flash_body(mode, heads, tq, tk, n_kv, gate_col, *refs):
    if mode == "fox":
        q_ref, k_ref, v_ref, bias_ref, o_ref, kb_ref, vb_ref = refs
    elif mode == "slc":
        q_ref, k_ref, v_ref, sel_ref, expand_ref, gate_ref, o_ref, kb_ref, vb_ref, selx_ref = refs
    else:
        q_ref, k_ref, v_ref, gate_ref, o_ref, kb_ref, vb_ref = refs
    unit = pl.program_id(1)
    qi = pl.program_id(2)

    @pl.when(qi == 0)
    def _():
        kb_ref[...] = k_ref[0].astype(BF16)
        vb_ref[...] = v_ref[0].astype(BF16)

    if mode == "slc":
        sel = sel_ref[0, 0].astype(BF16)
        for j in range(n_kv):
            selx_ref[j] = jnp.dot(sel, expand_ref[:, j * tk:(j + 1) * tk], preferred_element_type=F32)

    q_pos = qi * tq + lax.broadcasted_iota(jnp.int32, (tq, tk), 0)
    col = lax.broadcasted_iota(jnp.int32, (tq, tk), 1)
    if mode == "win":
        lo = jnp.maximum(qi - (-(-WINDOW // tk)), 0)
    else:
        lo = 0

    for r in range(heads):
        q = q_ref[0, :, r * HEAD_DIM:(r + 1) * HEAD_DIM]

        def step(j, carry):
            m, l, acc = carry
            ks = pl.multiple_of(j * tk, tk)
            kb = kb_ref[pl.ds(ks, tk), :]
            vb = vb_ref[pl.ds(ks, tk), :]
            s = lax.dot_general(q, kb, (((1,), (1,)), ((), ())), preferred_element_type=F32)
            k_pos = ks + col
            ok = k_pos <= q_pos
            if mode == "fox":
                s = s + bias_ref[0, :, pl.ds(ks, tk)]
            elif mode == "slc":
                ok = ok & (selx_ref[j] > 0.5)
            else:
                ok = ok & (k_pos > q_pos - WINDOW)
            s = jnp.where(ok, s, NEG_INF)
            m_new = jnp.maximum(m, jnp.max(s, axis=-1, keepdims=True))
            alpha = jnp.exp(m - m_new)
            p = jnp.exp(s - m_new)
            l = alpha * l + jnp.sum(p, axis=-1, keepdims=True)
            acc = alpha * acc + jnp.dot(p.astype(BF16), vb, preferred_element_type=F32)
            return m_new, l, acc

        init = (jnp.full((tq, 1), NEG_INF, F32), jnp.zeros((tq, 1), F32), jnp.zeros((tq, HEAD_DIM), F32))
        m, l, acc = lax.fori_loop(lo, qi + 1, step, init)
        w = 1.0 / l
        if mode != "fox":
            gate = gate_ref[0]
            lane = lax.broadcasted_iota(jnp.int32, gate.shape, 1)
            want = FOX_HEADS + (unit * heads + r) * N_NSA_BRANCHES + gate_col
            w = w * jnp.sum(jnp.where(lane == want, gate, 0.0), axis=-1, keepdims=True)
        o_ref[0, :, r * HEAD_DIM:(r + 1) * HEAD_DIM] = (acc * w).astype(o_ref.dtype)


def _flash_attention(mode, q, k, v, k_col0, t, *, bias=None, sel=None, gate=None, gate_col=0):
    b = q.shape[0]
    heads = 1 if mode == "fox" else HEADS_PER_GROUP
    units = q.shape[2] // (heads * HEAD_DIM)
    tq = tk = _row_tile(t, 256)
    n_kv = t // tk
    qw = heads * HEAD_DIM
    in_specs = [
        pl.BlockSpec((1, tq, qw), lambda bi, u, i: (bi, i, u)),
        pl.BlockSpec((1, t, HEAD_DIM), lambda bi, u, i: (bi, 0, k_col0 + u)),
        pl.BlockSpec((1, t, HEAD_DIM), lambda bi, u, i: (bi, 0, k_col0 + u)),
    ]
    args = [q, k, v]
    scratch = [pltpu.VMEM((t, HEAD_DIM), BF16), pltpu.VMEM((t, HEAD_DIM), BF16)]
    if mode == "fox":
        in_specs.append(pl.BlockSpec((1, 1, t), lambda bi, u, i: (bi * units + u, 0, 0)))
        args.append(bias.reshape(b * units, 1, t))
    else:
        if mode == "slc":
            nsp = sel.shape[-1]
            expand = (np.arange(nsp)[:, None] == (np.arange(t)[None, :] // SEL_BLOCK)).astype(np.float32)
            in_specs.append(pl.BlockSpec((1, 1, tq, nsp), lambda bi, u, i: (bi, u, i, 0)))
            in_specs.append(pl.BlockSpec((nsp, t), lambda bi, u, i: (0, 0)))
            args += [sel, jnp.asarray(expand, BF16)]
            scratch.append(pltpu.VMEM((n_kv, tq, tk), F32))
        in_specs.append(pl.BlockSpec((1, tq, LANES), lambda bi, u, i: (bi, i, 0)))
        args.append(gate)
    return pl.pallas_call(
        functools.partial(_flash_body, mode, heads, tq, tk, n_kv, gate_col),
        grid=(b, units, t // tq),
        in_specs=in_specs,
        out_specs=pl.BlockSpec((1, tq, qw), lambda bi, u, i: (bi, i, u)),
        out_shape=jax.ShapeDtypeStruct((b, t, units * qw), BF16),
        scratch_shapes=scratch,
        compiler_params=_cparams("parallel", "parallel", "arbitrary"),
        name="attn_" + mode,
    )(*args)


PAGED_BUFFER_BYTES = 2 * 1024 * 1024


def _pages_per_step(n_pages, page_bytes):
    pg = max(1, min(n_pages, PAGED_BUFFER_BYTES // page_bytes))
    while n_pages % pg:
        pg -= 1
    return pg


def _paged_body(units, rows, rows_b, n_pages, pg, page, use_gate, *refs):
    if use_gate:
        (tbl_ref, q_ref, bias_ref, kn_ref, vn_ref, biasn_ref, gate_ref, kpool, vpool, o_ref,
         kbuf, vbuf, sem, m_ref, l_ref, acc_ref) = refs
    else:
        (tbl_ref, q_ref, bias_ref, kn_ref, vn_ref, biasn_ref, kpool, vpool, o_ref,
         kbuf, vbuf, sem, m_ref, l_ref, acc_ref) = refs
        gate_ref = None
    bi = pl.program_id(0)
    s = pl.program_id(1)
    n_steps = n_pages // pg
    rpp = page * units
    n_keys = pg * page

    def copies(step, slot):
        out = []
        for k in range(pg):
            phys = tbl_ref[bi * n_pages + step * pg + k]
            out.append(pltpu.make_async_copy(kpool.at[phys], kbuf.at[slot, pl.ds(k * rpp, rpp)], sem.at[0, slot]))
            out.append(pltpu.make_async_copy(vpool.at[phys], vbuf.at[slot, pl.ds(k * rpp, rpp)], sem.at[1, slot]))
        return out

    @pl.when(s == 0)
    def _():
        m_ref[...] = jnp.full(m_ref.shape, NEG_INF, F32)
        l_ref[...] = jnp.zeros(l_ref.shape, F32)
        acc_ref[...] = jnp.zeros(acc_ref.shape, F32)
        for c in copies(0, 0):
            c.start()

    @pl.when(s + 1 < n_steps)
    def _():
        for c in copies(s + 1, lax.rem(s + 1, 2)):
            c.start()

    def attend(u, k, v, bias):
        k_hi, k_lo = _split_bf16(k)
        v_hi, v_lo = _split_bf16(v)
        sc = _dot_split(q_ref[0, u], k_hi, k_lo, _NT) + bias
        m = m_ref[u]
        m_new = jnp.maximum(m, jnp.max(sc, axis=-1, keepdims=True))
        alpha = jnp.exp(m - m_new)
        pr = jnp.exp(sc - m_new)
        l_ref[u] = alpha * l_ref[u] + jnp.sum(pr, axis=-1, keepdims=True)
        acc_ref[u] = alpha * acc_ref[u] + _dot_split(pr, v_hi, v_lo, _NN)
        m_ref[u] = m_new

    def expand_bias(bb):
        if rows_b == 1 or rows_b == rows:
            return bb
        return jnp.concatenate([bb] * (rows // rows_b), axis=0)

    @pl.when(s < n_steps)
    def _():
        slot = lax.rem(s, 2)
        for c in copies(s, slot):
            c.wait()
        for u in range(units):
            k = kbuf[slot, pl.ds(u, n_keys, stride=units), :]
            v = vbuf[slot, pl.ds(u, n_keys, stride=units), :]
            attend(u, k, v, expand_bias(bias_ref[0, u]))

    @pl.when(s == n_steps)
    def _():
        for u in range(units):
            sl = slice(u * HEAD_DIM, (u + 1) * HEAD_DIM)
            attend(u, kn_ref[0, :, sl], vn_ref[0, :, sl], biasn_ref[0, u])
            w = 1.0 / l_ref[u]
            if use_gate:
                w = w * gate_ref[0, u]
            o_ref[0, u] = acc_ref[u] * w


def _paged_attention(q, k_pool, v_pool, table, bias, k_new, v_new, bias_new, gate=None):
    b, units, rows, _ = q.shape
    n_pages = table.shape[1]
    rpp = k_pool.shape[1]
    page = rpp // units
    rows_b = bias.shape[2]
    n_new = k_new.shape[1]
    width = k_new.shape[2]
    pg = _pages_per_step(n_pages, rpp * HEAD_DIM * 4)
    n_steps = n_pages // pg
    last = n_steps - 1
    in_specs = [
        pl.BlockSpec((1, units, rows, HEAD_DIM), lambda bi, p, tbl: (bi, 0, 0, 0)),
        pl.BlockSpec((1, units, rows_b, pg * page), lambda bi, p, tbl: (bi, 0, 0, jnp.minimum(p, last))),
        pl.BlockSpec((1, n_new, width), lambda bi, p, tbl: (bi, 0, 0)),
        pl.BlockSpec((1, n_new, width), lambda bi, p, tbl: (bi, 0, 0)),
        pl.BlockSpec((1, units, rows, n_new), lambda bi, p, tbl: (bi, 0, 0, 0)),
    ]
    args = [q, bias, k_new, v_new, bias_new]
    if gate is not None:
        in_specs.append(pl.BlockSpec((1, units, rows, 1), lambda bi, p, tbl: (bi, 0, 0, 0)))
        args.append(gate)
    in_specs += [pl.BlockSpec(memory_space=pl.ANY), pl.BlockSpec(memory_space=pl.ANY)]
    args += [k_pool, v_pool]
    grid_spec = pltpu.PrefetchScalarGridSpec(
        num_scalar_prefetch=1,
        grid=(b, n_steps + 1),
        in_specs=in_specs,
        out_specs=pl.BlockSpec((1, units, rows, HEAD_DIM), lambda bi, p, tbl: (bi, 0, 0, 0)),
        scratch_shapes=[pltpu.VMEM((2, pg * rpp, HEAD_DIM), F32), pltpu.VMEM((2, pg * rpp, HEAD_DIM), F32),
                        pltpu.SemaphoreType.DMA((2, 2)),
                        pltpu.VMEM((units, rows, 1), F32), pltpu.VMEM((units, rows, 1), F32),
                        pltpu.VMEM((units, rows, HEAD_DIM), F32)],
    )
    return pl.pallas_call(
        functools.partial(_paged_body, units, rows, rows_b, n_pages, pg, page, gate is not None),
        grid_spec=grid_spec,
        out_shape=jax.ShapeDtypeStruct((b, units, rows, HEAD_DIM), F32),
        compiler_params=_cparams("arbitrary", "arbitrary"),
        name="attn_paged",
    )(table.reshape(-1), *args)


def _cmp_proj_body(x_ref, pa_ref, pb_ref, wa_ref, wb_ref, za_ref, zb_ref):
    x = x_ref[0]
    za_ref[0] = jnp.dot((x + pa_ref[...]).astype(BF16), wa_ref[...], preferred_element_type=F32)
    zb_ref[0] = jnp.dot((x + pb_ref[...]).astype(BF16), wb_ref[...], preferred_element_type=F32)


def _cmp_mlp_body(n_chunks, za_ref, zb_ref, w2_ref, o_ref):
    zb_next = pltpu.roll(zb_ref[0], n_chunks - 1, 0)
    hid = za_ref[0] + zb_next
    c = 0.7978845608028654
    hid = 0.5 * hid * (1.0 + jnp.tanh(c * (hid + 0.044715 * (hid * hid * hid))))
    w2 = w2_ref[...]
    hb = hid.astype(w2.dtype)
    for g in range(NSA_KV_GROUPS):
        sl = slice(g * HEAD_DIM, (g + 1) * HEAD_DIM)
        o_ref[0, :, sl] = jnp.dot(hb[:, sl], w2, preferred_element_type=F32, precision=_dot_precision(w2.dtype))


def _compress(rows, pos_emb, w1, w2):
    b, tc, gw = rows.shape
    g = NSA_KV_GROUPS
    n_chunks = tc // CMP_STRIDE
    cw = CMP_STRIDE * gw
    x = rows.reshape(b, n_chunks, cw)
    w1r = w1.reshape(2, CMP_STRIDE, HEAD_DIM, HEAD_DIM)
    eye = jnp.eye(g, dtype=w1.dtype)
    wcat = jnp.einsum("hldj,ge->hlgdej", w1r, eye).reshape(2, cw, g * HEAD_DIM).astype(BF16)
    pos = jnp.broadcast_to(pos_emb.reshape(2, CMP_STRIDE, 1, HEAD_DIM), (2, CMP_STRIDE, g, HEAD_DIM)).reshape(2, 1, cw)
    cm = _row_tile(n_chunks, 128)
    za, zb = pl.pallas_call(
        _cmp_proj_body,
        grid=(b, n_chunks // cm),
        in_specs=[
            pl.BlockSpec((1, cm, cw), lambda bi, i: (bi, i, 0)),
            pl.BlockSpec((1, cw), lambda bi, i: (0, 0)),
            pl.BlockSpec((1, cw), lambda bi, i: (0, 0)),
            pl.BlockSpec((cw, g * HEAD_DIM), lambda bi, i: (0, 0)),
            pl.BlockSpec((cw, g * HEAD_DIM), lambda bi, i: (0, 0)),
        ],
        out_specs=[pl.BlockSpec((1, cm, g * HEAD_DIM), lambda bi, i: (bi, i, 0))] * 2,
        out_shape=[jax.ShapeDtypeStruct((b, n_chunks, g * HEAD_DIM), F32)] * 2,
        compiler_params=_cparams("parallel", "parallel"),
        name="cmp_proj",
    )(x, pos[0], pos[1], wcat[0], wcat[1])
    return _cmp_mlp(za, zb, w2.astype(BF16))


def _cmp_mlp(za, zb, w2):
    b, n_chunks, gw = za.shape
    g = NSA_KV_GROUPS
    return pl.pallas_call(
        functools.partial(_cmp_mlp_body, n_chunks),
        grid=(b,),
        in_specs=[
            pl.BlockSpec((1, n_chunks, g * HEAD_DIM), lambda bi: (bi, 0, 0)),
            pl.BlockSpec((1, n_chunks, g * HEAD_DIM), lambda bi: (bi, 0, 0)),
            pl.BlockSpec((HEAD_DIM, HEAD_DIM), lambda bi: (0, 0)),
        ],
        out_specs=pl.BlockSpec((1, n_chunks, g * HEAD_DIM), lambda bi: (bi, 0, 0)),
        out_shape=jax.ShapeDtypeStruct((b, n_chunks, g * HEAD_DIM), F32),
        compiler_params=_cparams("parallel"),
        name="cmp_mlp",
    )(za, zb, w2)


def _split_bf16(x):
    hi = x.astype(BF16)
    return hi, (x - hi.astype(F32)).astype(BF16)


def _dot_split(a, b_hi, b_lo, dims):
    a_hi, a_lo = _split_bf16(a)
    n = a.shape[0]
    both = lax.dot_general(jnp.concatenate([a_hi, a_lo], axis=0), b_hi, dims, preferred_element_type=F32)
    return both[:n] + both[n:] + lax.dot_general(a_hi, b_lo, dims, preferred_element_type=F32)


_NT = (((1,), (1,)), ((), ()))
_NN = (((1,), (0,)), ((), ()))


def _cmp_paged_body(n_pages, pg, page, tbl_ref, pos_ref, whi_ref, wlo_ref, pool, za_ref, zb_ref, xbuf, sem):
    bi = pl.program_id(0)
    s = pl.program_id(1)
    n_steps = n_pages // pg
    g_n = NSA_KV_GROUPS
    rpp = page * g_n
    n_rows = pg * page // CMP_STRIDE

    def copies(step, slot):
        return [pltpu.make_async_copy(pool.at[tbl_ref[bi * n_pages + step * pg + k]],
                                      xbuf.at[slot, pl.ds(k * rpp, rpp)], sem.at[slot]) for k in range(pg)]

    @pl.when(s == 0)
    def _():
        for c in copies(0, 0):
            c.start()

    @pl.when(s + 1 < n_steps)
    def _():
        for c in copies(s + 1, lax.rem(s + 1, 2)):
            c.start()

    slot = lax.rem(s, 2)
    for c in copies(s, slot):
        c.wait()
    w_hi = whi_ref[...]
    w_lo = wlo_ref[...]
    pw = _dot_split(pos_ref[...], w_hi, w_lo, _NN)
    col = lax.broadcasted_iota(jnp.int32, (1, 2 * HEAD_DIM), 1)
    pos_term = jnp.where(col < HEAD_DIM, pw[0:1, :], pw[1:2, :])
    for g in range(g_n):
        x_cat = jnp.concatenate(
            [xbuf[slot, pl.ds(l * g_n + g, n_rows, stride=CMP_STRIDE * g_n), :] for l in range(CMP_STRIDE)], axis=1)
        z = _dot_split(x_cat, w_hi, w_lo, _NN) + pos_term
        za_ref[0, :, g * HEAD_DIM:(g + 1) * HEAD_DIM] = z[:, :HEAD_DIM]
        zb_ref[0, :, g * HEAD_DIM:(g + 1) * HEAD_DIM] = z[:, HEAD_DIM:]


def _compress_paged(pool, table, pos_emb, w1, w2):
    b, n_pages = table.shape
    rpp = pool.shape[1]
    page = rpp // NSA_KV_GROUPS
    gw = NSA_KV_GROUPS * HEAD_DIM
    pg = _pages_per_step(n_pages, rpp * HEAD_DIM * 4)
    n_steps = n_pages // pg
    n_rows = pg * page // CMP_STRIDE
    n_chunks = n_pages * page // CMP_STRIDE
    half = CMP_STRIDE * HEAD_DIM
    w_cat = jnp.concatenate([w1[:half], w1[half:]], axis=1)
    w_hi = w_cat.astype(BF16)
    w_lo = (w_cat - w_hi.astype(F32)).astype(BF16)
    pos_rows = jnp.pad(pos_emb.reshape(2, half), ((0, SUBLANES - 2), (0, 0)))
    grid_spec = pltpu.PrefetchScalarGridSpec(
        num_scalar_prefetch=1,
        grid=(b, n_steps),
        in_specs=[pl.BlockSpec((SUBLANES, half), lambda bi, i, tbl: (0, 0)),
                  pl.BlockSpec((half, 2 * HEAD_DIM), lambda bi, i, tbl: (0, 0)),
                  pl.BlockSpec((half, 2 * HEAD_DIM), lambda bi, i, tbl: (0, 0)),
                  pl.BlockSpec(memory_space=pl.ANY)],
        out_specs=[pl.BlockSpec((1, n_rows, gw), lambda bi, i, tbl: (bi, i, 0))] * 2,
        scratch_shapes=[pltpu.VMEM((2, pg * rpp, HEAD_DIM), F32), pltpu.SemaphoreType.DMA((2,))],
    )
    za, zb = pl.pallas_call(
        functools.partial(_cmp_paged_body, n_pages, pg, page),
        grid_spec=grid_spec,
        out_shape=[jax.ShapeDtypeStruct((b, n_chunks, gw), F32)] * 2,
        compiler_params=_cparams("arbitrary", "arbitrary"),
        name="cmp_proj_paged",
    )(table.reshape(-1), pos_rows, w_hi, w_lo, pool)
    return _cmp_mlp(za, zb, w2)


def _cmp_attn_body(tq, nc, ns, n_pick, q_start, q_ref, kc_ref, vc_ref, cover_ref, gate_ref, o_ref, sel_ref):
    unit = pl.program_id(1)
    qi = pl.program_id(2)
    cdt = q_ref.dtype
    prec = _dot_precision(cdt)
    kc = kc_ref[0].astype(cdt)
    vc = vc_ref[0].astype(cdt)
    ncp = kc.shape[0]
    q_pos = q_start + qi * tq + lax.broadcasted_iota(jnp.int32, (tq, ncp), 0)
    n_idx = lax.broadcasted_iota(jnp.int32, (tq, ncp), 1)
    visible = (n_idx * CMP_STRIDE + (CMP_BLOCK - 1) <= q_pos) & (n_idx < nc)
    gate = gate_ref[0]
    glane = lax.broadcasted_iota(jnp.int32, gate.shape, 1)
    p_grp = jnp.zeros((tq, ncp), F32)
    for r in range(HEADS_PER_GROUP):
        q = q_ref[0, :, r * HEAD_DIM:(r + 1) * HEAD_DIM]
        s = lax.dot_general(q, kc, _NT, preferred_element_type=F32, precision=prec)
        s = jnp.where(visible, s, NEG_INF)
        m = jnp.max(s, axis=-1, keepdims=True)
        e = jnp.where(visible, jnp.exp(s - m), 0.0)
        l = jnp.sum(e, axis=-1, keepdims=True)
        p = e / jnp.where(l > 0.0, l, 1.0)
        p_grp = p_grp + p
        want = FOX_HEADS + (unit * HEADS_PER_GROUP + r) * N_NSA_BRANCHES
        gcol = jnp.sum(jnp.where(glane == want, gate, 0.0), axis=-1, keepdims=True)
        o = jnp.dot(p.astype(cdt), vc, preferred_element_type=F32, precision=prec)
        o_ref[0, :, r * HEAD_DIM:(r + 1) * HEAD_DIM] = (o * gcol).astype(o_ref.dtype)

    imp = jnp.dot(p_grp, cover_ref[...], preferred_element_type=F32, precision=HIGHEST)
    nsp = imp.shape[1]
    j = lax.broadcasted_iota(jnp.int32, (tq, nsp), 1)
    cur = (q_start + qi * tq + lax.broadcasted_iota(jnp.int32, (tq, nsp), 0)) // SEL_BLOCK
    dist = cur - j
    forced = (j == 0) | ((dist >= 0) & (dist < N_LOCAL_SEL))
    score = jnp.where(forced, FORCE_SCORE, imp)
    score = jnp.where(dist >= 0, score, NEG_INF)
    score = jnp.where(j < ns, score, PAD_SCORE)
    sel = jnp.zeros((tq, nsp), F32)
    for _ in range(n_pick):
        mx = jnp.max(score, axis=-1, keepdims=True)
        first = jnp.min(jnp.where(score == mx, j, nsp), axis=-1, keepdims=True)
        hit = j == first
        sel = jnp.where(hit, 1.0, sel)
        score = jnp.where(hit, TAKEN_SCORE, score)
    sel_ref[0, 0] = sel


def _cmp_attention(nq, kc, vc, gate, t, q_start, n_keys, out_dtype):
    b = nq.shape[0]
    ncp = kc.shape[1]
    nc = (n_keys - CMP_BLOCK) // CMP_STRIDE + 1
    ns = -(-n_keys // SEL_BLOCK)
    nsp = -(-ns // LANES) * LANES
    n_pick = min(N_SELECT, ns)
    c0 = np.arange(ncp)[:, None] * CMP_STRIDE
    s0 = np.arange(nsp)[None, :] * SEL_BLOCK
    cover = np.clip(np.minimum(c0 + CMP_BLOCK, s0 + SEL_BLOCK) - np.maximum(c0, s0), 0, None).astype(np.float32) / CMP_BLOCK
    cover = cover * (np.arange(ncp)[:, None] < nc) * (np.arange(nsp)[None, :] < ns)
    tq = _row_tile(t, 256)
    qw = HEADS_PER_GROUP * HEAD_DIM
    return pl.pallas_call(
        functools.partial(_cmp_attn_body, tq, nc, ns, n_pick, q_start),
        grid=(b, NSA_KV_GROUPS, t // tq),
        in_specs=[
            pl.BlockSpec((1, tq, qw), lambda bi, u, i: (bi, i, u)),
            pl.BlockSpec((1, ncp, HEAD_DIM), lambda bi, u, i: (bi, 0, u)),
            pl.BlockSpec((1, ncp, HEAD_DIM), lambda bi, u, i: (bi, 0, u)),
            pl.BlockSpec((ncp, nsp), lambda bi, u, i: (0, 0)),
            pl.BlockSpec((1, tq, LANES), lambda bi, u, i: (bi, i, 0)),
        ],
        out_specs=[
            pl.BlockSpec((1, tq, qw), lambda bi, u, i: (bi, i, u)),
            pl.BlockSpec((1, 1, tq, nsp), lambda bi, u, i: (bi, u, i, 0)),
        ],
        out_shape=[
            jax.ShapeDtypeStruct((b, t, NSA_KV_GROUPS * qw), out_dtype),
            jax.ShapeDtypeStruct((b, NSA_KV_GROUPS, t, nsp), F32),
        ],
        compiler_params=_cparams("parallel", "parallel", "parallel"),
        name="attn_cmp_select",
    )(nq, kc, vc, jnp.asarray(cover), gate)


def _merge_body(a_ref, b1_ref, b2_ref, b3_ref, ua_ref, ub_ref, ga_ref, gb_ref, o_ref):
    a = a_ref[...]
    prec = _dot_precision(a.dtype)
    bsum = (b1_ref[...].astype(F32) + b2_ref[...].astype(F32) + b3_ref[...].astype(F32)).astype(a.dtype)
    ya = jnp.dot(a, ua_ref[...], preferred_element_type=F32, precision=prec)
    yb = jnp.dot(bsum, ub_ref[...], preferred_element_type=F32, precision=prec)
    o_ref[...] = (ga_ref[...] * ya + gb_ref[...] * yb).astype(o_ref.dtype)


def _merge(o_fox, o_cmp, o_slc, o_win, u_fox, u_nsa, mg, d_model):
    n, w = o_fox.shape
    tm = _row_tile(n, 512)
    tn = 512
    nj = d_model // tn
    row = pl.BlockSpec((tm, w), lambda i, j: (i, 0))
    return pl.pallas_call(
        _merge_body,
        grid=(n // tm, nj),
        in_specs=[row, row, row, row,
                  pl.BlockSpec((w, tn), lambda i, j: (0, j)),
                  pl.BlockSpec((w, tn), lambda i, j: (0, j)),
                  pl.BlockSpec((tm, tn), lambda i, j: (i, j)),
                  pl.BlockSpec((tm, tn), lambda i, j: (i, j + nj))],
        out_specs=pl.BlockSpec((tm, tn), lambda i, j: (i, j)),
        out_shape=jax.ShapeDtypeStruct((n, d_model), o_fox.dtype),
        compiler_params=_cparams("parallel", "parallel"),
        name="mixer_merge",
    )(o_fox, o_cmp, o_slc, o_win, u_fox, u_nsa, mg, mg)


def _router_body(h_ref, g_ref, w_ref, b_ref, u_ref, route_ref):
    x = h_ref[...]
    xn = x * lax.rsqrt(jnp.mean(x * x, axis=-1, keepdims=True) + NORM_EPS) * g_ref[...]
    u_ref[...] = xn.astype(u_ref.dtype)
    z = jnp.dot(xn, w_ref[...], preferred_element_type=F32, precision=HIGHEST) + b_ref[...]
    c = lax.broadcasted_iota(jnp.int32, z.shape, 1)
    big = z.shape[1]
    is_g = c < N_GROUPS
    gmax = jnp.max(jnp.where(is_g, z, -jnp.inf), axis=-1, keepdims=True)
    gsel = jnp.min(jnp.where(is_g & (z == gmax), c, big), axis=-1, keepdims=True)
    gsum = jnp.sum(jnp.where(is_g, jnp.exp(z - gmax), 0.0), axis=-1, keepdims=True)
    lo = N_GROUPS + gsel * EXPERTS_PER_GROUP
    in_grp = (c >= lo) & (c < lo + EXPERTS_PER_GROUP)
    z1 = jnp.max(jnp.where(in_grp, z, -jnp.inf), axis=-1, keepdims=True)
    c1 = jnp.min(jnp.where(in_grp & (z == z1), c, big), axis=-1, keepdims=True)
    rest = in_grp & (c != c1)
    z2 = jnp.max(jnp.where(rest, z, -jnp.inf), axis=-1, keepdims=True)
    c2 = jnp.min(jnp.where(rest & (z == z2), c, big), axis=-1, keepdims=True)
    e2 = jnp.exp(z2 - z1)
    p1 = 1.0 / (1.0 + e2)
    p2 = e2 / (1.0 + e2)
    gp = 1.0 / gsum
    out = jnp.where(c == 0, (c1 - N_GROUPS).astype(F32), 0.0)
    out = jnp.where(c == 1, (c2 - N_GROUPS).astype(F32), out)
    out = jnp.where(c == 2, gp * p1, out)
    out = jnp.where(c == 3, gp * p2, out)
    route_ref[...] = out


def _router(h, gain, w_cat, b_cat):
    n, d = h.shape
    tm = _row_tile(n, 512)
    return pl.pallas_call(
        _router_body,
        grid=(n // tm,),
        in_specs=[pl.BlockSpec((tm, d), lambda i: (i, 0)), pl.BlockSpec((1, d), lambda i: (0, 0)),
                  pl.BlockSpec((d, LANES), lambda i: (0, 0)), pl.BlockSpec((1, LANES), lambda i: (0, 0))],
        out_specs=[pl.BlockSpec((tm, d), lambda i: (i, 0)), pl.BlockSpec((tm, LANES), lambda i: (i, 0))],
        out_shape=[jax.ShapeDtypeStruct((n, d), BF16), jax.ShapeDtypeStruct((n, LANES), F32)],
        compiler_params=_cparams("parallel"),
        name="moe_router",
    )(h, gain.reshape(1, d), w_cat, b_cat)


def _expert_body(te_ref, nv_ref, x_ref, cw_ref, wg_ref, wu_ref, wd_ref, o_ref):
    i = pl.program_id(0)

    @pl.when(i < nv_ref[0])
    def _():
        x = x_ref[...]
        hg = jnp.dot(x, wg_ref[0], preferred_element_type=F32)
        hu = jnp.dot(x, wu_ref[0], preferred_element_type=F32)
        hid = (hg * jax.nn.sigmoid(hg)) * hu * cw_ref[...]
        o_ref[...] = jnp.dot(hid.astype(BF16), wd_ref[0], preferred_element_type=F32)

    @pl.when(i >= nv_ref[0])
    def _():
        o_ref[...] = jnp.zeros(o_ref.shape, o_ref.dtype)


def _experts(x_sorted, w_sorted, tile_expert, n_valid, wg, wu, wd, tm):
    rows, d = x_sorted.shape
    n_tiles = rows // tm
    f = wg.shape[2]

    def row_map(i, te, nv):
        return (jnp.minimum(i, nv[0] - 1), 0)

    def w_map(i, te, nv):
        return (te[i], 0, 0)

    grid_spec = pltpu.PrefetchScalarGridSpec(
        num_scalar_prefetch=2,
        grid=(n_tiles,),
        in_specs=[pl.BlockSpec((tm, d), row_map), pl.BlockSpec((tm, 1), row_map),
                  pl.BlockSpec((1, d, f), w_map), pl.BlockSpec((1, d, f), w_map), pl.BlockSpec((1, f, d), w_map)],
        out_specs=pl.BlockSpec((tm, d), lambda i, te, nv: (i, 0)),
    )
    return pl.pallas_call(
        _expert_body,
        grid_spec=grid_spec,
        out_shape=jax.ShapeDtypeStruct((rows, d), F32),
        compiler_params=_cparams("arbitrary"),
        name="moe_experts",
    )(tile_expert, n_valid, x_sorted, w_sorted, wg, wu, wd)


def _moe(h, gain, w_router, b_router, wg, wu, wd):
    n, d = h.shape
    ut, route = _router(h, gain, w_router, b_router)
    expert = route[:, :2].astype(jnp.int32).reshape(-1)
    weight = route[:, 2:4].reshape(-1)
    n_asg = 2 * n
    tm = 256 if n_asg >= 8192 else 16
    n_tiles = n_asg // tm + N_EXPERTS
    onehot = (expert[:, None] == jnp.arange(N_EXPERTS, dtype=jnp.int32)[None, :]).astype(jnp.int32)
    running = jnp.cumsum(onehot, axis=0)
    counts = running[-1]
    rank = jnp.sum(onehot * running, axis=1) - 1
    tiles_per = (counts + tm - 1) // tm
    tile_end = jnp.cumsum(tiles_per)
    start_row = (tile_end - tiles_per) * tm
    slot = (jnp.sum(onehot * start_row[None, :], axis=1) + rank).astype(jnp.int32)
    row_token = jnp.zeros((n_tiles * tm,), jnp.int32).at[slot].set(jnp.arange(n_asg, dtype=jnp.int32) // 2)
    row_weight = jnp.zeros((n_tiles * tm,), F32).at[slot].set(weight)
    n_valid = tile_end[-1].astype(jnp.int32)
    tile_ids = jnp.minimum(jnp.arange(n_tiles, dtype=jnp.int32), n_valid - 1)
    tile_expert = jnp.sum((tile_end[None, :] <= tile_ids[:, None]).astype(jnp.int32), axis=1)
    x_sorted = jnp.take(ut, row_token, axis=0)
    out_sorted = _experts(x_sorted, row_weight[:, None], tile_expert, n_valid.reshape(1), wg, wu, wd, tm)
    slot2 = slot.reshape(n, 2)
    return h + jnp.take(out_sorted, slot2[:, 0], axis=0) + jnp.take(out_sorted, slot2[:, 1], axis=0)


def _rope_tables(positions):
    half = HEAD_DIM // 2
    inv = ROPE_THETA ** (-np.arange(half, dtype=np.float64) / half)
    ang = np.asarray(positions, np.float64)[:, None] * inv[None, :]
    cos = np.concatenate([np.cos(ang), np.cos(ang)], axis=-1)
    sin = np.concatenate([-np.sin(ang), np.sin(ang)], axis=-1)
    return jnp.asarray(cos, F32), jnp.asarray(sin, F32)


def _project(u, wts, p, b, t, q_start):
    n = u.shape[0]
    tm = _row_tile(n, 1024)
    tn = 256
    scale = HEAD_DIM ** -0.5
    gain_spec = pl.BlockSpec((1, HEAD_DIM), lambda i, j: (0, 0))

    def run(w, epilogue, extras, specs, dtype, name, tn_=tn):
        shape = jax.ShapeDtypeStruct((n, w.shape[1]), dtype)
        spec = pl.BlockSpec((tm, tn_), lambda i, j: (i, j))
        return _matmul(u, w, epilogue, extras, specs, shape, spec, tm, tn_, name)

    cos, sin = _rope_tables(q_start + (np.arange(max(tm, t)) % t))
    n_tab = cos.shape[0] // tm
    tab_spec = pl.BlockSpec((tm, HEAD_DIM), lambda i, j: (i % n_tab, 0))

    fq = run(wts["fq"], functools.partial(_ep_head_norm, scale), [p["fox_q_norm"]], [gain_spec], u.dtype, "proj_fq")
    fk = run(wts["fk"], functools.partial(_ep_head_norm, 1.0), [p["fox_k_norm"]], [gain_spec], F32, "proj_fk")
    fv = run(wts["fv"], _ep_identity, [], [], F32, "proj_fv")
    nq = run(wts["nq"], functools.partial(_ep_head_norm_rope, scale), [p["nsa_q_norm"], cos, sin],
             [gain_spec, tab_spec, tab_spec], u.dtype, "proj_nq")
    nk = run(wts["nk"], functools.partial(_ep_head_norm_rope, 1.0), [p["nsa_k_norm"], cos, sin],
             [gain_spec, tab_spec, tab_spec], F32, "proj_nk")
    nv = run(wts["nv"], _ep_identity, [], [], F32, "proj_nv")
    mg = run(wts["mg"], _ep_sigmoid, [], [], F32, "proj_mg")
    small = run(wts["small"], _ep_small, [p["small_bias"]], [pl.BlockSpec((1, LANES), lambda i, j: (0, 0))],
                F32, "proj_small", LANES)
    return fq, fk, fv, nq, nk, nv, mg, small


def _layer(x, p, wts, q_start, past):
    b, t, d = x.shape
    n = b * t
    g = NSA_KV_GROUPS
    kvw = g * HEAD_DIM
    x2d = x.reshape(n, d)
    dense = wts["f32"] if past is not None else wts
    u = _rmsnorm(x2d, p["norm_mix"], dense["fq"].dtype)
    fq, fk, fv, nq, nk, nv, mg, small = _project(u, dense, p, b, t, q_start)

    logf = small[:, :FOX_HEADS].reshape(b, t, FOX_HEADS)
    gate3 = small.reshape(b, t, LANES)
    fq3 = fq.reshape(b, t, -1)
    nq3 = nq.reshape(b, t, -1)
    fk3 = fk.reshape(b, t, -1)
    fv3 = fv.reshape(b, t, -1)
    nk3 = nk.reshape(b, t, 3 * kvw)
    nv3 = nv.reshape(b, t, 3 * kvw)
    ck, sk, wk = (nk3[:, :, i * kvw:(i + 1) * kvw] for i in range(3))
    cv, sv, wv = (nv3[:, :, i * kvw:(i + 1) * kvw] for i in range(3))

    if past is None:
        n_keys = t
        n_lp = t // LANES
        lf_pool = jnp.transpose(logf.reshape(b * n_lp, LANES, FOX_HEADS), (0, 2, 1))
        bias = _fox_key_bias(lf_pool, jnp.arange(b * n_lp, dtype=jnp.int32).reshape(b, n_lp), None)
        o_fox = _flash_attention("fox", fq3, fk3, fv3, 0, t, bias=bias)
        kc = _compress(ck, p["cmp_pos_k"], p["cmp_w1_k"], p["cmp_w2_k"])
        vc = _compress(cv, p["cmp_pos_v"], p["cmp_w1_v"], p["cmp_w2_v"])
        o_cmp, sel = _cmp_attention(nq3, kc, vc, gate3, t, q_start, n_keys, BF16)
        o_slc = _flash_attention("slc", nq3, nk3, nv3, g, t, sel=sel, gate=gate3, gate_col=1)
        o_win = _flash_attention("win", nq3, nk3, nv3, 2 * g, t, gate=gate3, gate_col=2)
        keep = min(WINDOW, t)
        win_k, win_v = wk[:, t - keep:], wv[:, t - keep:]
    else:
        table = past["table"]
        n_pages = table.shape[1]
        page = past["page"]
        past_len = n_pages * page
        n_keys = past_len + t
        new_page = jnp.pad(jnp.transpose(logf, (0, 2, 1)), ((0, 0), (0, 0), (0, page - t)))
        bias = _fox_key_bias(past["fox_logf"], table, new_page)[:, :, :n_keys]
        causal = jnp.where(jnp.arange(t)[None, :] <= jnp.arange(t)[:, None], 0.0, NEG_INF).astype(F32)
        bias_new = bias[:, :, None, past_len:] + causal[None, None]
        qf = jnp.transpose(fq3.reshape(b, t, FOX_HEADS, HEAD_DIM), (0, 2, 1, 3))
        o = _paged_attention(qf, past["fox_k"], past["fox_v"], table, bias[:, :, None, :past_len],
                             fk3, fv3, bias_new)
        o_fox = jnp.transpose(o, (0, 2, 1, 3)).reshape(b, t, -1)
        nc = (n_keys - CMP_BLOCK) // CMP_STRIDE + 1
        assert (nc + 1) * CMP_STRIDE <= past_len and page % CMP_STRIDE == 0
        kc = _compress_paged(past["cmp_k"], table, p["cmp_pos_k"], p["cmp_w1_k"], p["cmp_w2_k"])
        vc = _compress_paged(past["cmp_v"], table, p["cmp_pos_v"], p["cmp_w1_v"], p["cmp_w2_v"])
        o_cmp, sel = _cmp_attention(nq3, kc, vc, gate3, t, q_start, n_keys, F32)
        rows = HEADS_PER_GROUP * t
        qn = jnp.transpose(nq3.reshape(b, t, g, HEADS_PER_GROUP, HEAD_DIM), (0, 2, 3, 1, 4)).reshape(b, g, rows, HEAD_DIM)
        gates = small[:, FOX_HEADS:FOX_HEADS + NSA_HEADS * N_NSA_BRANCHES].reshape(b, t, g, HEADS_PER_GROUP, N_NSA_BRANCHES)
        gates = jnp.transpose(gates, (4, 0, 2, 3, 1)).reshape(N_NSA_BRANCHES, b, g, rows, 1)
        ns = -(-n_keys // SEL_BLOCK)
        key_sel = jnp.repeat(sel[..., :ns], SEL_BLOCK, axis=-1)[..., :n_keys]
        sel_bias = jnp.where(key_sel > 0.5, 0.0, NEG_INF).astype(F32)
        sel_new = jnp.tile(sel_bias[..., past_len:] + causal[None, None], (1, 1, HEADS_PER_GROUP, 1))
        o = _paged_attention(qn, past["slc_k"], past["slc_v"], table, sel_bias[..., :past_len],
                             sk, sv, sel_new, gate=gates[1])
        o_slc = _rows_to_tokens(o, b, t)
        win_len = past["win_k"].shape[1]
        assert win_len % page == 0
        k_start = q_start - win_len
        q_abs = q_start + jnp.arange(t)[:, None]
        k_abs = k_start + jnp.arange(win_len)[None, :]
        band = jnp.where((k_abs >= 0) & (k_abs > q_abs - WINDOW), 0.0, NEG_INF).astype(F32)
        band = jnp.broadcast_to(band[None, None], (b, g, t, win_len))
        band_new = jnp.where(jnp.arange(t)[None, :] > jnp.arange(t)[:, None] - WINDOW, causal, NEG_INF)
        band_new = jnp.broadcast_to(jnp.tile(band_new, (HEADS_PER_GROUP, 1))[None, None], (b, g, rows, t))
        o = _paged_attention(qn, past["win_k_pool"], past["win_v_pool"], past["win_table"], band, wk, wv,
                             band_new, gate=gates[2])
        o_win = _rows_to_tokens(o, b, t)
        wk_all = jnp.concatenate([past["win_k"], wk], axis=1)
        wv_all = jnp.concatenate([past["win_v"], wv], axis=1)
        keep = min(WINDOW, wk_all.shape[1])
        win_k, win_v = wk_all[:, -keep:], wv_all[:, -keep:]

    m = _merge(o_fox.reshape(n, -1), o_cmp.reshape(n, -1), o_slc.reshape(n, -1), o_win.reshape(n, -1),
               dense["up_fox"], dense["up_nsa"], mg, d)
    tm = _row_tile(n, 512)
    h = _matmul(m, dense["out"], _ep_residual, [x2d], [pl.BlockSpec((tm, 512), lambda i, j: (i, j))],
                jax.ShapeDtypeStruct((n, d), F32), pl.BlockSpec((tm, 512), lambda i, j: (i, j)), tm, 512, "out_proj")
    y = _moe(h, p["norm_ffn"], wts["router"], p["router_bias"], wts["gate_e"], wts["up_e"], wts["down_e"])

    def heads(a, hh):
        return a.reshape(b, -1, hh, HEAD_DIM)

    state = (heads(fk3, FOX_HEADS), heads(fv3, FOX_HEADS), logf, heads(ck, g), heads(cv, g), heads(sk, g),
             heads(sv, g), heads(win_k, g), heads(win_v, g))
    return y.reshape(b, t, d), state


def _rows_to_tokens(o, b, t):
    o = o.reshape(b, NSA_KV_GROUPS, HEADS_PER_GROUP, t, HEAD_DIM)
    return jnp.transpose(o, (0, 3, 1, 2, 4)).reshape(b, t, -1)


def _prepare(l, norm_mix, w_in, fox_f_bias, fox_q_norm, fox_k_norm, nsa_q_norm, nsa_k_norm, cmp_pos_k, cmp_w1_k,
             cmp_w2_k, cmp_pos_v, cmp_w1_v, cmp_w2_v, nsa_gate_bias, w_up_fox, w_up_nsa, w_out, norm_ffn,
             w_router_group, b_router_group, w_router_expert, b_router_expert, w_gate_e, w_up_e, w_down_e):
    d = w_in.shape[1]
    fw = FOX_HEADS * HEAD_DIM
    nw = NSA_HEADS * HEAD_DIM
    kvw = NSA_KV_GROUPS * HEAD_DIM
    sizes = (fw, fw, fw, FOX_HEADS, nw, kvw, kvw, kvw, kvw, kvw, kvw, NSA_HEADS * N_NSA_BRANCHES, 2 * d)
    offs = np.concatenate([[0], np.cumsum(sizes)])
    w = w_in[l]

    def seg(i):
        return w[:, offs[i]:offs[i + 1]]

    n_small = FOX_HEADS + NSA_HEADS * N_NSA_BRANCHES
    small_w = jnp.pad(jnp.concatenate([seg(3), seg(11)], axis=1), ((0, 0), (0, LANES - n_small)))
    small_b = jnp.pad(jnp.concatenate([fox_f_bias[l], nsa_gate_bias[l]]), (0, LANES - n_small)).reshape(1, LANES)
    n_route = N_GROUPS + N_EXPERTS
    router_w = jnp.pad(jnp.concatenate([w_router_group[l], w_router_expert[l]], axis=1), ((0, 0), (0, LANES - n_route)))
    router_b = jnp.pad(jnp.concatenate([b_router_group[l], b_router_expert[l]]), (0, LANES - n_route)).reshape(1, LANES)
    dense = {
        "fq": seg(0), "fk": seg(1), "fv": seg(2), "nq": seg(4),
        "nk": jnp.concatenate([seg(5), seg(7), seg(9)], axis=1),
        "nv": jnp.concatenate([seg(6), seg(8), seg(10)], axis=1),
        "mg": seg(12), "small": small_w,
        "up_fox": w_up_fox[l], "up_nsa": w_up_nsa[l], "out": w_out[l],
    }
    wts = {
        **{k: v.astype(BF16) for k, v in dense.items()},
        "f32": dense,
        "router": router_w,
        "gate_e": w_gate_e[l].astype(BF16), "up_e": w_up_e[l].astype(BF16), "down_e": w_down_e[l].astype(BF16),
    }
    p = {
        "norm_mix": norm_mix[l], "fox_q_norm": fox_q_norm[l].reshape(1, -1), "fox_k_norm": fox_k_norm[l].reshape(1, -1),
        "nsa_q_norm": nsa_q_norm[l].reshape(1, -1), "nsa_k_norm": nsa_k_norm[l].reshape(1, -1),
        "small_bias": small_b, "cmp_pos_k": cmp_pos_k[l], "cmp_w1_k": cmp_w1_k[l], "cmp_w2_k": cmp_w2_k[l],
        "cmp_pos_v": cmp_pos_v[l], "cmp_w1_v": cmp_w1_v[l], "cmp_w2_v": cmp_w2_v[l],
        "norm_ffn": norm_ffn[l], "router_bias": router_b,
    }
    return wts, p


def kernel(x_prompt, x_sample, cache_fox_k, cache_fox_v, cache_fox_logf, cache_cmp_k, cache_cmp_v, cache_slc_k, cache_slc_v, cache_win_k, cache_win_v, page_table, norm_mix, w_in, fox_f_bias, fox_q_norm, fox_k_norm, nsa_q_norm, nsa_k_norm, cmp_pos_k, cmp_w1_k, cmp_w2_k, cmp_pos_v, cmp_w1_v, cmp_w2_v, nsa_gate_bias, w_up_fox, w_up_nsa, w_out, norm_ffn, w_router_group, b_router_group, w_router_expert, b_router_expert, w_gate_e, w_up_e, w_down_e):
    depth = w_in.shape[0]
    n_phys, page = cache_fox_k.shape[1], cache_fox_k.shape[2]
    past_len = page_table.shape[1] * page
    h_p, h_s = x_prompt, x_sample
    p_states, s_states = [], []
    for l in range(depth):
        wts, p = _prepare(l, norm_mix, w_in, fox_f_bias, fox_q_norm, fox_k_norm, nsa_q_norm, nsa_k_norm, cmp_pos_k,
                          cmp_w1_k, cmp_w2_k, cmp_pos_v, cmp_w1_v, cmp_w2_v, nsa_gate_bias, w_up_fox, w_up_nsa,
                          w_out, norm_ffn, w_router_group, b_router_group, w_router_expert, b_router_expert,
                          w_gate_e, w_up_e, w_down_e)
        h_p, st_p = _layer(h_p, p, wts, 0, None)
        p_states.append(st_p)
        win_len = cache_win_k.shape[2]
        n_win = win_len // page
        dec_b = page_table.shape[0]

        def pool(c):
            return c.reshape(-1, page * c.shape[3], HEAD_DIM)

        past = {
            "table": page_table + l * n_phys, "page": page,
            "fox_k": pool(cache_fox_k), "fox_v": pool(cache_fox_v),
            "fox_logf": jnp.transpose(cache_fox_logf.reshape(-1, page, cache_fox_logf.shape[3]), (0, 2, 1)),
            "cmp_k": pool(cache_cmp_k), "cmp_v": pool(cache_cmp_v),
            "slc_k": pool(cache_slc_k), "slc_v": pool(cache_slc_v),
            "win_k_pool": pool(cache_win_k), "win_v_pool": pool(cache_win_v),
            "win_table": (l * dec_b + jnp.arange(dec_b, dtype=jnp.int32))[:, None] * n_win
                         + jnp.arange(n_win, dtype=jnp.int32)[None, :],
            "win_k": cache_win_k[l].reshape(dec_b, win_len, -1), "win_v": cache_win_v[l].reshape(dec_b, win_len, -1),
        }
        h_s, st_s = _layer(h_s, p, wts, past_len, past)
        s_states.append(st_s)
    p_out = [jnp.stack([st[i] for st in p_states], axis=0) for i in range(9)]
    s_out = [jnp.stack([st[i] for st in s_states], axis=0) for i in range(9)]
    return (h_p, h_s, *p_out, *s_out)
```

```python
import functools

import numpy as np
import jax
import jax.numpy as jnp
from jax import lax
from jax.experimental import pallas as pl
from jax.experimental.pallas import tpu as pltpu

F32 = jnp.float32
BF16 = jnp.bfloat16

HEAD_DIM = 128
FOX_HEADS = 8
NSA_HEADS = 8
NSA_KV_GROUPS = 2
HEADS_PER_GROUP = NSA_HEADS // NSA_KV_GROUPS
N_NSA_BRANCHES = 3
CMP_BLOCK = 32
CMP_STRIDE = 16
SEL_BLOCK = 64
N_SELECT = 16
N_LOCAL_SEL = 2
WINDOW = 512
N_GROUPS = 4
EXPERTS_PER_GROUP = 8
N_EXPERTS = N_GROUPS * EXPERTS_PER_GROUP
ROPE_THETA = 10000.0
NORM_EPS = 1e-6
NEG_INF = -1e30
FORCE_SCORE = 1e9
PAD_SCORE = -2e38
TAKEN_SCORE = -3e38

LANES = 128
SUBLANES = 8
VMEM_LIMIT_BYTES = 48 * 1024 * 1024

HIGHEST = lax.Precision.HIGHEST


def _cparams(*sem):
    return pltpu.CompilerParams(dimension_semantics=tuple(sem), vmem_limit_bytes=VMEM_LIMIT_BYTES)


def _row_tile(n, pref):
    t = min(n, pref)
    assert n % t == 0, (n, t)
    return t


def _rmsnorm_body(x_ref, g_ref, o_ref):
    x = x_ref[...]
    y = x * lax.rsqrt(jnp.mean(x * x, axis=-1, keepdims=True) + NORM_EPS)
    o_ref[...] = (y * g_ref[...]).astype(o_ref.dtype)


def _rmsnorm(x2d, gain, out_dtype):
    n, d = x2d.shape
    tm = _row_tile(n, 512)
    return pl.pallas_call(
        _rmsnorm_body,
        grid=(n // tm,),
        in_specs=[pl.BlockSpec((tm, d), lambda i: (i, 0)), pl.BlockSpec((1, d), lambda i: (0, 0))],
        out_specs=pl.BlockSpec((tm, d), lambda i: (i, 0)),
        out_shape=jax.ShapeDtypeStruct((n, d), out_dtype),
        compiler_params=_cparams("parallel"),
        name="rmsnorm",
    )(x2d, gain.reshape(1, d))


def _dot_precision(dtype):
    return HIGHEST if dtype == F32 else None


def _mm_body(epilogue, n_extra, a_ref, w_ref, *rest):
    acc = jnp.dot(a_ref[...], w_ref[...], preferred_element_type=F32, precision=_dot_precision(a_ref.dtype))
    epilogue(acc, rest[:n_extra], rest[n_extra:])


def _matmul(a, w, epilogue, extras, extra_specs, out_shapes, out_specs, tm, tn, name):
    n, k = a.shape
    c = w.shape[1]
    assert n % tm == 0 and c % tn == 0, (n, tm, c, tn)
    return pl.pallas_call(
        functools.partial(_mm_body, epilogue, len(extras)),
        grid=(n // tm, c // tn),
        in_specs=[pl.BlockSpec((tm, k), lambda i, j: (i, 0)), pl.BlockSpec((k, tn), lambda i, j: (0, j))]
        + list(extra_specs),
        out_specs=out_specs,
        out_shape=out_shapes,
        compiler_params=_cparams("parallel", "parallel"),
        name=name,
    )(a, w, *extras)


def _head_norm(x, gain):
    return x * lax.rsqrt(jnp.mean(x * x, axis=-1, keepdims=True) + NORM_EPS) * gain


def _rope_rot(x, cos, sin_signed):
    return x * cos + pltpu.roll(x, HEAD_DIM // 2, 1) * sin_signed


def _ep_head_norm(scale, acc, extras, outs):
    (gain_ref,) = extras
    (o_ref,) = outs
    for h in range(acc.shape[1] // HEAD_DIM):
        sl = slice(h * HEAD_DIM, (h + 1) * HEAD_DIM)
        o_ref[:, sl] = (_head_norm(acc[:, sl], gain_ref[...]) * scale).astype(o_ref.dtype)


def _ep_head_norm_rope(scale, acc, extras, outs):
    gain_ref, cos_ref, sin_ref = extras
    (o_ref,) = outs
    for h in range(acc.shape[1] // HEAD_DIM):
        sl = slice(h * HEAD_DIM, (h + 1) * HEAD_DIM)
        y = _rope_rot(_head_norm(acc[:, sl], gain_ref[...]), cos_ref[...], sin_ref[...])
        o_ref[:, sl] = (y * scale).astype(o_ref.dtype)


def _ep_identity(acc, extras, outs):
    (o_ref,) = outs
    o_ref[...] = acc.astype(o_ref.dtype)


def _ep_sigmoid(acc, extras, outs):
    (o_ref,) = outs
    o_ref[...] = jax.nn.sigmoid(acc).astype(o_ref.dtype)


def _ep_small(acc, extras, outs):
    (bias_ref,) = extras
    (o_ref,) = outs
    z = acc + bias_ref[...]
    col = lax.broadcasted_iota(jnp.int32, z.shape, 1)
    e = jnp.exp(-jnp.abs(z))
    log_sig = jnp.minimum(z, 0.0) - jnp.log(1.0 + e)
    sig = jnp.where(z >= 0, 1.0, e) / (1.0 + e)
    o_ref[...] = jnp.where(col < FOX_HEADS, log_sig, sig)


def _ep_residual(acc, extras, outs):
    (x_ref,) = extras
    (o_ref,) = outs
    o_ref[...] = x_ref[...] + acc


def _fox_bias_body(n_pages, has_new, *refs):
    if has_new:
        tbl_ref, new_ref, pool_ref, o_ref, buf, sem = refs
    else:
        tbl_ref, pool_ref, o_ref, buf, sem = refs
    bi = pl.program_id(0)

    def copy(p):
        return pltpu.make_async_copy(pool_ref.at[tbl_ref[bi * n_pages + p]], buf.at[p], sem)

    def start(p, c):
        copy(p).start()
        return c

    def wait(p, c):
        copy(p).wait()
        return c

    lax.fori_loop(0, n_pages, start, 0)
    lax.fori_loop(0, n_pages, wait, 0)
    if has_new:
        buf[n_pages] = new_ref[0]
    x = buf[...]
    p, h, w = x.shape
    within = x.reshape(p * h, w)
    lane = lax.broadcasted_iota(jnp.int32, within.shape, 1)
    s = 1
    while s < w:
        within = within + jnp.where(lane >= s, pltpu.roll(within, s, 1), 0.0)
        s *= 2
    within = within.reshape(p, h, w)
    total = jnp.broadcast_to(within[:, :, w - 1:w], x.shape)
    incl = total
    s = 1
    while s < p:
        incl = incl + jnp.concatenate([jnp.zeros((s, h, w), F32), incl[:p - s]], axis=0)
        s *= 2
    o_ref[0] = -(within + (incl - total))


def _fox_key_bias(pool, table, new_page):
    b, n_pages = table.shape
    h, w = pool.shape[1:]
    has_new = new_page is not None
    p_tot = n_pages + (1 if has_new else 0)
    in_specs = [pl.BlockSpec(memory_space=pl.ANY)]
    args = [pool]
    if has_new:
        in_specs.insert(0, pl.BlockSpec((1, h, w), lambda i, tbl: (i, 0, 0)))
        args.insert(0, new_page)
    grid_spec = pltpu.PrefetchScalarGridSpec(
        num_scalar_prefetch=1,
        grid=(b,),
        in_specs=in_specs,
        out_specs=pl.BlockSpec((1, p_tot, h, w), lambda i, tbl: (i, 0, 0, 0)),
        scratch_shapes=[pltpu.VMEM((p_tot, h, w), F32), pltpu.SemaphoreType.DMA(())],
    )
    out = pl.pallas_call(
        functools.partial(_fox_bias_body, n_pages, has_new),
        grid_spec=grid_spec,
        out_shape=jax.ShapeDtypeStruct((b, p_tot, h, w), F32),
        compiler_params=_cparams("arbitrary"),
        name="fox_key_bias",
    )(table.reshape(-1), *args)
    return jnp.transpose(out, (0, 2, 1, 3)).reshape(b, h, p_tot * w)


def _flash_body(mode, heads, tq, tk, n_kv, gate_col, *refs):
    if mode == "fox":
        q_ref, k_ref, v_ref, bias_ref, o_ref, kb_ref, vb_ref = refs
    elif mode == "slc":
        q_ref, k_ref, v_ref, sel_ref, expand_ref, gate_ref, o_ref, kb_ref, vb_ref, selx_ref = refs
    else:
        q_ref, k_ref, v_ref, gate_ref, o_ref, kb_ref, vb_ref = refs
    unit = pl.program_id(1)
    qi = pl.program_id(2)

    @pl.when(qi == 0)
    def _():
        kb_ref[...] = k_ref[0].astype(BF16)
        vb_ref[...] = v_ref[0].astype(BF16)

    if mode == "slc":
        sel = sel_ref[0, 0].astype(BF16)
        for j in range(n_kv):
            selx_ref[j] = jnp.dot(sel, expand_ref[:, j * tk:(j + 1) * tk], preferred_element_type=F32)

    q_pos = qi * tq + lax.broadcasted_iota(jnp.int32, (tq, tk), 0)
    col = lax.broadcasted_iota(jnp.int32, (tq, tk), 1)
    if mode == "win":
        lo = jnp.maximum(qi - (-(-WINDOW // tk)), 0)
    else:
        lo = 0

    for r in range(heads):
        q = q_ref[0, :, r * HEAD_DIM:(r + 1) * HEAD_DIM]

        def step(j, carry):
            m, l, acc = carry
            ks = pl.multiple_of(j * tk, tk)
            kb = kb_ref[pl.ds(ks, tk), :]
            vb = vb_ref[pl.ds(ks, tk), :]
            s = lax.dot_general(q, kb, (((1,), (1,)), ((), ())), preferred_element_type=F32)
            k_pos = ks + col
            ok = k_pos <= q_pos
            if mode == "fox":
                s = s + bias_ref[0, :, pl.ds(ks, tk)]
            elif mode == "slc":
                ok = ok & (selx_ref[j] > 0.5)
            else:
                ok = ok & (k_pos > q_pos - WINDOW)
            s = jnp.where(ok, s, NEG_INF)
            m_new = jnp.maximum(m, jnp.max(s, axis=-1, keepdims=True))
            alpha = jnp.exp(m - m_new)
            p = jnp.exp(s - m_new)
            l = alpha * l + jnp.sum(p, axis=-1, keepdims=True)
            acc = alpha * acc + jnp.dot(p.astype(BF16), vb, preferred_element_type=F32)
            return m_new, l, acc

        init = (jnp.full((tq, 1), NEG_INF, F32), jnp.zeros((tq, 1), F32), jnp.zeros((tq, HEAD_DIM), F32))
        m, l, acc = lax.fori_loop(lo, qi + 1, step, init)
        w = 1.0 / l
        if mode != "fox":
            gate = gate_ref[0]
            lane = lax.broadcasted_iota(jnp.int32, gate.shape, 1)
            want = FOX_HEADS + (unit * heads + r) * N_NSA_BRANCHES + gate_col
            w = w * jnp.sum(jnp.where(lane == want, gate, 0.0), axis=-1, keepdims=True)
        o_ref[0, :, r * HEAD_DIM:(r + 1) * HEAD_DIM] = (acc * w).astype(o_ref.dtype)


def _flash_attention(mode, q, k, v, k_col0, t, *, bias=None, sel=None, gate=None, gate_col=0):
    b = q.shape[0]
    heads = 1 if mode == "fox" else HEADS_PER_GROUP
    units = q.shape[2] // (heads * HEAD_DIM)
    tq = tk = _row_tile(t, 256)
    n_kv = t // tk
    qw = heads * HEAD_DIM
    in_specs = [
        pl.BlockSpec((1, tq, qw), lambda bi, u, i: (bi, i, u)),
        pl.BlockSpec((1, t, HEAD_DIM), lambda bi, u, i: (bi, 0, k_col0 + u)),
        pl.BlockSpec((1, t, HEAD_DIM), lambda bi, u, i: (bi, 0, k_col0 + u)),
    ]
    args = [q, k, v]
    scratch = [pltpu.VMEM((t, HEAD_DIM), BF16), pltpu.VMEM((t, HEAD_DIM), BF16)]
    if mode == "fox":
        in_specs.append(pl.BlockSpec((1, 1, t), lambda bi, u, i: (bi * units + u, 0, 0)))
        args.append(bias.reshape(b * units, 1, t))
    else:
        if mode == "slc":
            nsp = sel.shape[-1]
            expand = (np.arange(nsp)[:, None] == (np.arange(t)[None, :] // SEL_BLOCK)).astype(np.float32)
            in_specs.append(pl.BlockSpec((1, 1, tq, nsp), lambda bi, u, i: (bi, u, i, 0)))
            in_specs.append(pl.BlockSpec((nsp, t), lambda bi, u, i: (0, 0)))
            args += [sel, jnp.asarray(expand, BF16)]
            scratch.append(pltpu.VMEM((n_kv, tq, tk), F32))
        in_specs.append(pl.BlockSpec((1, tq, LANES), lambda bi, u, i: (bi, i, 0)))
        args.append(gate)
    return pl.pallas_call(
        functools.partial(_flash_body, mode, heads, tq, tk, n_kv, gate_col),
        grid=(b, units, t // tq),
        in_specs=in_specs,
        out_specs=pl.BlockSpec((1, tq, qw), lambda bi, u, i: (bi, i, u)),
        out_shape=jax.ShapeDtypeStruct((b, t, units * qw), BF16),
        scratch_shapes=scratch,
        compiler_params=_cparams("parallel", "parallel", "arbitrary"),
        name="attn_" + mode,
    )(*args)


PAGED_BUFFER_BYTES = 2 * 1024 * 1024


def _pages_per_step(n_pages, page_bytes):
    pg = max(1, min(n_pages, PAGED_BUFFER_BYTES // page_bytes))
    while n_pages % pg:
        pg -= 1
    return pg


def _paged_body(units, rows, rows_b, n_pages, pg, page, use_gate, *refs):
    if use_gate:
        (tbl_ref, q_ref, bias_ref, kn_ref, vn_ref, biasn_ref, gate_ref, kpool, vpool, o_ref,
         kbuf, vbuf, sem, m_ref, l_ref, acc_ref) = refs
    else:
        (tbl_ref, q_ref, bias_ref, kn_ref, vn_ref, biasn_ref, kpool, vpool, o_ref,
         kbuf, vbuf, sem, m_ref, l_ref, acc_ref) = refs
        gate_ref = None
    bi = pl.program_id(0)
    s = pl.program_id(1)
    n_steps = n_pages // pg
    rpp = page * units
    n_keys = pg * page

    def copies(step, slot):
        out = []
        for k in range(pg):
            phys = tbl_ref[bi * n_pages + step * pg + k]
            out.append(pltpu.make_async_copy(kpool.at[phys], kbuf.at[slot, pl.ds(k * rpp, rpp)], sem.at[0, slot]))
            out.append(pltpu.make_async_copy(vpool.at[phys], vbuf.at[slot, pl.ds(k * rpp, rpp)], sem.at[1, slot]))
        return out

    @pl.when(s == 0)
    def _():
        m_ref[...] = jnp.full(m_ref.shape, NEG_INF, F32)
        l_ref[...] = jnp.zeros(l_ref.shape, F32)
        acc_ref[...] = jnp.zeros(acc_ref.shape, F32)
        for c in copies(0, 0):
            c.start()

    @pl.when(s + 1 < n_steps)
    def _():
        for c in copies(s + 1, lax.rem(s + 1, 2)):
            c.start()

    def attend(k, v, bias, own_blocks):
        k_hi, k_lo = _split_bf16(k)
        v_hi, v_lo = _split_bf16(v)
        sc = _dot_split(q_ref[0], k_hi, k_lo, _NT)
        nk = k.shape[0] // units
        if own_blocks:
            sc = jnp.concatenate([sc[u * rows:(u + 1) * rows, u * nk:(u + 1) * nk] for u in range(units)], axis=0)
        sc = sc + bias
        m = m_ref[...]
        m_new = jnp.maximum(m, jnp.max(sc, axis=-1, keepdims=True))
        alpha = jnp.exp(m - m_new)
        pr = jnp.exp(sc - m_new)
        l_ref[...] = alpha * l_ref[...] + jnp.sum(pr, axis=-1, keepdims=True)
        if own_blocks:
            zero = jnp.zeros((rows, nk), F32)
            pr = jnp.concatenate(
                [jnp.concatenate([pr[u * rows:(u + 1) * rows] if u2 == u else zero for u2 in range(units)], axis=1)
                 for u in range(units)], axis=0)
        acc_ref[...] = alpha * acc_ref[...] + _dot_split(pr, v_hi, v_lo, _NN)
        m_ref[...] = m_new

    def unit_bias(u):
        bb = bias_ref[0, u]
        if rows_b == 1:
            return jnp.broadcast_to(bb, (rows, n_keys))
        return jnp.concatenate([bb] * (rows // rows_b), axis=0)

    @pl.when(s < n_steps)
    def _():
        slot = lax.rem(s, 2)
        for c in copies(s, slot):
            c.wait()
        k = jnp.concatenate([kbuf[slot, pl.ds(u, n_keys, stride=units), :] for u in range(units)], axis=0)
        v = jnp.concatenate([vbuf[slot, pl.ds(u, n_keys, stride=units), :] for u in range(units)], axis=0)
        attend(k, v, jnp.concatenate([unit_bias(u) for u in range(units)], axis=0), True)

    @pl.when(s == n_steps)
    def _():
        k = jnp.concatenate([kn_ref[0, :, u * HEAD_DIM:(u + 1) * HEAD_DIM] for u in range(units)], axis=0)
        v = jnp.concatenate([vn_ref[0, :, u * HEAD_DIM:(u + 1) * HEAD_DIM] for u in range(units)], axis=0)
        attend(k, v, biasn_ref[0], False)
        w = 1.0 / l_ref[...]
        if use_gate:
            w = w * gate_ref[0]
        o_ref[0] = acc_ref[...] * w


def _paged_attention(q, k_pool, v_pool, table, bias, k_new, v_new, bias_new, gate=None):
    b, units, rows, _ = q.shape
    ur = units * rows
    n_pages = table.shape[1]
    rpp = k_pool.shape[1]
    page = rpp // units
    rows_b = bias.shape[2]
    n_new = k_new.shape[1]
    width = k_new.shape[2]
    pg = _pages_per_step(n_pages, rpp * HEAD_DIM * 4)
    n_steps = n_pages // pg
    last = n_steps - 1
    same_unit = jnp.arange(units)[:, None, None, None] == jnp.arange(units)[None, None, :, None]
    bias_new_full = jnp.where(same_unit[None], bias_new[:, :, :, None, :], NEG_INF).reshape(b, ur, units * n_new)
    in_specs = [
        pl.BlockSpec((1, ur, HEAD_DIM), lambda bi, p, tbl: (bi, 0, 0)),
        pl.BlockSpec((1, units, rows_b, pg * page), lambda bi, p, tbl: (bi, 0, 0, jnp.minimum(p, last))),
        pl.BlockSpec((1, n_new, width), lambda bi, p, tbl: (bi, 0, 0)),
        pl.BlockSpec((1, n_new, width), lambda bi, p, tbl: (bi, 0, 0)),
        pl.BlockSpec((1, ur, units * n_new), lambda bi, p, tbl: (bi, 0, 0)),
    ]
    args = [q.reshape(b, ur, HEAD_DIM), bias, k_new, v_new, bias_new_full]
    if gate is not None:
        in_specs.append(pl.BlockSpec((1, ur, 1), lambda bi, p, tbl: (bi, 0, 0)))
        args.append(gate.reshape(b, ur, 1))
    in_specs += [pl.BlockSpec(memory_space=pl.ANY), pl.BlockSpec(memory_space=pl.ANY)]
    args += [k_pool, v_pool]
    grid_spec = pltpu.PrefetchScalarGridSpec(
        num_scalar_prefetch=1,
        grid=(b, n_steps + 1),
        in_specs=in_specs,
        out_specs=pl.BlockSpec((1, ur, HEAD_DIM), lambda bi, p, tbl: (bi, 0, 0)),
        scratch_shapes=[pltpu.VMEM((2, pg * rpp, HEAD_DIM), F32), pltpu.VMEM((2, pg * rpp, HEAD_DIM), F32),
                        pltpu.SemaphoreType.DMA((2, 2)),
                        pltpu.VMEM((ur, 1), F32), pltpu.VMEM((ur, 1), F32), pltpu.VMEM((ur, HEAD_DIM), F32)],
    )
    out = pl.pallas_call(
        functools.partial(_paged_body, units, rows, rows_b, n_pages, pg, page, gate is not None),
        grid_spec=grid_spec,
        out_shape=jax.ShapeDtypeStruct((b, ur, HEAD_DIM), F32),
        compiler_params=_cparams("arbitrary", "arbitrary"),
        name="attn_paged",
    )(table.reshape(-1), *args)
    return out.reshape(b, units, rows, HEAD_DIM)


def _cmp_proj_body(x_ref, pa_ref, pb_ref, wa_ref, wb_ref, za_ref, zb_ref):
    x = x_ref[0]
    za_ref[0] = jnp.dot((x + pa_ref[...]).astype(BF16), wa_ref[...], preferred_element_type=F32)
    zb_ref[0] = jnp.dot((x + pb_ref[...]).astype(BF16), wb_ref[...], preferred_element_type=F32)


def _cmp_mlp_body(n_chunks, za_ref, zb_ref, w2_ref, o_ref):
    zb_next = pltpu.roll(zb_ref[0], n_chunks - 1, 0)
    hid = za_ref[0] + zb_next
    c = 0.7978845608028654
    hid = 0.5 * hid * (1.0 + jnp.tanh(c * (hid + 0.044715 * (hid * hid * hid))))
    w2 = w2_ref[...]
    hb = hid.astype(w2.dtype)
    for g in range(NSA_KV_GROUPS):
        sl = slice(g * HEAD_DIM, (g + 1) * HEAD_DIM)
        o_ref[0, :, sl] = jnp.dot(hb[:, sl], w2, preferred_element_type=F32, precision=_dot_precision(w2.dtype))


def _compress(rows, pos_emb, w1, w2):
    b, tc, gw = rows.shape
    g = NSA_KV_GROUPS
    n_chunks = tc // CMP_STRIDE
    cw = CMP_STRIDE * gw
    x = rows.reshape(b, n_chunks, cw)
    w1r = w1.reshape(2, CMP_STRIDE, HEAD_DIM, HEAD_DIM)
    eye = jnp.eye(g, dtype=w1.dtype)
    wcat = jnp.einsum("hldj,ge->hlgdej", w1r, eye).reshape(2, cw, g * HEAD_DIM).astype(BF16)
    pos = jnp.broadcast_to(pos_emb.reshape(2, CMP_STRIDE, 1, HEAD_DIM), (2, CMP_STRIDE, g, HEAD_DIM)).reshape(2, 1, cw)
    cm = _row_tile(n_chunks, 128)
    za, zb = pl.pallas_call(
        _cmp_proj_body,
        grid=(b, n_chunks // cm),
        in_specs=[
            pl.BlockSpec((1, cm, cw), lambda bi, i: (bi, i, 0)),
            pl.BlockSpec((1, cw), lambda bi, i: (0, 0)),
            pl.BlockSpec((1, cw), lambda bi, i: (0, 0)),
            pl.BlockSpec((cw, g * HEAD_DIM), lambda bi, i: (0, 0)),
            pl.BlockSpec((cw, g * HEAD_DIM), lambda bi, i: (0, 0)),
        ],
        out_specs=[pl.BlockSpec((1, cm, g * HEAD_DIM), lambda bi, i: (bi, i, 0))] * 2,
        out_shape=[jax.ShapeDtypeStruct((b, n_chunks, g * HEAD_DIM), F32)] * 2,
        compiler_params=_cparams("parallel", "parallel"),
        name="cmp_proj",
    )(x, pos[0], pos[1], wcat[0], wcat[1])
    return _cmp_mlp(za, zb, w2.astype(BF16))


def _cmp_mlp(za, zb, w2):
    b, n_chunks, gw = za.shape
    g = NSA_KV_GROUPS
    return pl.pallas_call(
        functools.partial(_cmp_mlp_body, n_chunks),
        grid=(b,),
        in_specs=[
            pl.BlockSpec((1, n_chunks, g * HEAD_DIM), lambda bi: (bi, 0, 0)),
            pl.BlockSpec((1, n_chunks, g * HEAD_DIM), lambda bi: (bi, 0, 0)),
            pl.BlockSpec((HEAD_DIM, HEAD_DIM), lambda bi: (0, 0)),
        ],
        out_specs=pl.BlockSpec((1, n_chunks, g * HEAD_DIM), lambda bi: (bi, 0, 0)),
        out_shape=jax.ShapeDtypeStruct((b, n_chunks, g * HEAD_DIM), F32),
        compiler_params=_cparams("parallel"),
        name="cmp_mlp",
    )(za, zb, w2)


def _split_bf16(x):
    hi = x.astype(BF16)
    return hi, (x - hi.astype(F32)).astype(BF16)


def _dot_split(a, b_hi, b_lo, dims):
    a_hi, a_lo = _split_bf16(a)
    n = a.shape[0]
    both = lax.dot_general(jnp.concatenate([a_hi, a_lo], axis=0), b_hi, dims, preferred_element_type=F32)
    return both[:n] + both[n:] + lax.dot_general(a_hi, b_lo, dims, preferred_element_type=F32)


_NT = (((1,), (1,)), ((), ()))
_NN = (((1,), (0,)), ((), ()))


def _cmp_paged_body(n_pages, pg, page, tbl_ref, pos_ref, whi_ref, wlo_ref, pool, za_ref, zb_ref, xbuf, sem):
    bi = pl.program_id(0)
    s = pl.program_id(1)
    n_steps = n_pages // pg
    g_n = NSA_KV_GROUPS
    rpp = page * g_n
    n_rows = pg * page // CMP_STRIDE

    def copies(step, slot):
        return [pltpu.make_async_copy(pool.at[tbl_ref[bi * n_pages + step * pg + k]],
                                      xbuf.at[slot, pl.ds(k * rpp, rpp)], sem.at[slot]) for k in range(pg)]

    @pl.when(s == 0)
    def _():
        for c in copies(0, 0):
            c.start()

    @pl.when(s + 1 < n_steps)
    def _():
        for c in copies(s + 1, lax.rem(s + 1, 2)):
            c.start()

    slot = lax.rem(s, 2)
    for c in copies(s, slot):
        c.wait()
    w_hi = whi_ref[...]
    w_lo = wlo_ref[...]
    pw = _dot_split(pos_ref[...], w_hi, w_lo, _NN)
    col = lax.broadcasted_iota(jnp.int32, (1, 2 * HEAD_DIM), 1)
    pos_term = jnp.where(col < HEAD_DIM, pw[0:1, :], pw[1:2, :])
    for g in range(g_n):
        x_cat = jnp.concatenate(
            [xbuf[slot, pl.ds(l * g_n + g, n_rows, stride=CMP_STRIDE * g_n), :] for l in range(CMP_STRIDE)], axis=1)
        z = _dot_split(x_cat, w_hi, w_lo, _NN) + pos_term
        za_ref[0, :, g * HEAD_DIM:(g + 1) * HEAD_DIM] = z[:, :HEAD_DIM]
        zb_ref[0, :, g * HEAD_DIM:(g + 1) * HEAD_DIM] = z[:, HEAD_DIM:]


def _compress_paged(pool, table, pos_emb, w1, w2):
    b, n_pages = table.shape
    rpp = pool.shape[1]
    page = rpp // NSA_KV_GROUPS
    gw = NSA_KV_GROUPS * HEAD_DIM
    pg = _pages_per_step(n_pages, rpp * HEAD_DIM * 4)
    n_steps = n_pages // pg
    n_rows = pg * page // CMP_STRIDE
    n_chunks = n_pages * page // CMP_STRIDE
    half = CMP_STRIDE * HEAD_DIM
    w_cat = jnp.concatenate([w1[:half], w1[half:]], axis=1)
    w_hi = w_cat.astype(BF16)
    w_lo = (w_cat - w_hi.astype(F32)).astype(BF16)
    pos_rows = jnp.pad(pos_emb.reshape(2, half), ((0, SUBLANES - 2), (0, 0)))
    grid_spec = pltpu.PrefetchScalarGridSpec(
        num_scalar_prefetch=1,
        grid=(b, n_steps),
        in_specs=[pl.BlockSpec((SUBLANES, half), lambda bi, i, tbl: (0, 0)),
                  pl.BlockSpec((half, 2 * HEAD_DIM), lambda bi, i, tbl: (0, 0)),
                  pl.BlockSpec((half, 2 * HEAD_DIM), lambda bi, i, tbl: (0, 0)),
                  pl.BlockSpec(memory_space=pl.ANY)],
        out_specs=[pl.BlockSpec((1, n_rows, gw), lambda bi, i, tbl: (bi, i, 0))] * 2,
        scratch_shapes=[pltpu.VMEM((2, pg * rpp, HEAD_DIM), F32), pltpu.SemaphoreType.DMA((2,))],
    )
    za, zb = pl.pallas_call(
        functools.partial(_cmp_paged_body, n_pages, pg, page),
        grid_spec=grid_spec,
        out_shape=[jax.ShapeDtypeStruct((b, n_chunks, gw), F32)] * 2,
        compiler_params=_cparams("arbitrary", "arbitrary"),
        name="cmp_proj_paged",
    )(table.reshape(-1), pos_rows, w_hi, w_lo, pool)
    return _cmp_mlp(za, zb, w2)


def _cmp_attn_body(tq, nc, ns, n_pick, q_start, q_ref, kc_ref, vc_ref, cover_ref, gate_ref, o_ref, sel_ref):
    unit = pl.program_id(1)
    qi = pl.program_id(2)
    cdt = q_ref.dtype
    prec = _dot_precision(cdt)
    kc = kc_ref[0].astype(cdt)
    vc = vc_ref[0].astype(cdt)
    ncp = kc.shape[0]
    q_pos = q_start + qi * tq + lax.broadcasted_iota(jnp.int32, (tq, ncp), 0)
    n_idx = lax.broadcasted_iota(jnp.int32, (tq, ncp), 1)
    visible = (n_idx * CMP_STRIDE + (CMP_BLOCK - 1) <= q_pos) & (n_idx < nc)
    gate = gate_ref[0]
    glane = lax.broadcasted_iota(jnp.int32, gate.shape, 1)
    p_grp = jnp.zeros((tq, ncp), F32)
    for r in range(HEADS_PER_GROUP):
        q = q_ref[0, :, r * HEAD_DIM:(r + 1) * HEAD_DIM]
        s = lax.dot_general(q, kc, _NT, preferred_element_type=F32, precision=prec)
        s = jnp.where(visible, s, NEG_INF)
        m = jnp.max(s, axis=-1, keepdims=True)
        e = jnp.where(visible, jnp.exp(s - m), 0.0)
        l = jnp.sum(e, axis=-1, keepdims=True)
        p = e / jnp.where(l > 0.0, l, 1.0)
        p_grp = p_grp + p
        want = FOX_HEADS + (unit * HEADS_PER_GROUP + r) * N_NSA_BRANCHES
        gcol = jnp.sum(jnp.where(glane == want, gate, 0.0), axis=-1, keepdims=True)
        o = jnp.dot(p.astype(cdt), vc, preferred_element_type=F32, precision=prec)
        o_ref[0, :, r * HEAD_DIM:(r + 1) * HEAD_DIM] = (o * gcol).astype(o_ref.dtype)

    imp = jnp.dot(p_grp, cover_ref[...], preferred_element_type=F32, precision=HIGHEST)
    nsp = imp.shape[1]
    j = lax.broadcasted_iota(jnp.int32, (tq, nsp), 1)
    cur = (q_start + qi * tq + lax.broadcasted_iota(jnp.int32, (tq, nsp), 0)) // SEL_BLOCK
    dist = cur - j
    forced = (j == 0) | ((dist >= 0) & (dist < N_LOCAL_SEL))
    score = jnp.where(forced, FORCE_SCORE, imp)
    score = jnp.where(dist >= 0, score, NEG_INF)
    score = jnp.where(j < ns, score, PAD_SCORE)
    sel = jnp.zeros((tq, nsp), F32)
    for _ in range(n_pick):
        mx = jnp.max(score, axis=-1, keepdims=True)
        first = jnp.min(jnp.where(score == mx, j, nsp), axis=-1, keepdims=True)
        hit = j == first
        sel = jnp.where(hit, 1.0, sel)
        score = jnp.where(hit, TAKEN_SCORE, score)
    sel_ref[0, 0] = sel


def _cmp_attention(nq, kc, vc, gate, t, q_start, n_keys, out_dtype):
    b = nq.shape[0]
    ncp = kc.shape[1]
    nc = (n_keys - CMP_BLOCK) // CMP_STRIDE + 1
    ns = -(-n_keys // SEL_BLOCK)
    nsp = -(-ns // LANES) * LANES
    n_pick = min(N_SELECT, ns)
    c0 = np.arange(ncp)[:, None] * CMP_STRIDE
    s0 = np.arange(nsp)[None, :] * SEL_BLOCK
    cover = np.clip(np.minimum(c0 + CMP_BLOCK, s0 + SEL_BLOCK) - np.maximum(c0, s0), 0, None).astype(np.float32) / CMP_BLOCK
    cover = cover * (np.arange(ncp)[:, None] < nc) * (np.arange(nsp)[None, :] < ns)
    tq = _row_tile(t, 256)
    qw = HEADS_PER_GROUP * HEAD_DIM
    return pl.pallas_call(
        functools.partial(_cmp_attn_body, tq, nc, ns, n_pick, q_start),
        grid=(b, NSA_KV_GROUPS, t // tq),
        in_specs=[
            pl.BlockSpec((1, tq, qw), lambda bi, u, i: (bi, i, u)),
            pl.BlockSpec((1, ncp, HEAD_DIM), lambda bi, u, i: (bi, 0, u)),
            pl.BlockSpec((1, ncp, HEAD_DIM), lambda bi, u, i: (bi, 0, u)),
            pl.BlockSpec((ncp, nsp), lambda bi, u, i: (0, 0)),
            pl.BlockSpec((1, tq, LANES), lambda bi, u, i: (bi, i, 0)),
        ],
        out_specs=[
            pl.BlockSpec((1, tq, qw), lambda bi, u, i: (bi, i, u)),
            pl.BlockSpec((1, 1, tq, nsp), lambda bi, u, i: (bi, u, i, 0)),
        ],
        out_shape=[
            jax.ShapeDtypeStruct((b, t, NSA_KV_GROUPS * qw), out_dtype),
            jax.ShapeDtypeStruct((b, NSA_KV_GROUPS, t, nsp), F32),
        ],
        compiler_params=_cparams("parallel", "parallel", "parallel"),
        name="attn_cmp_select",
    )(nq, kc, vc, jnp.asarray(cover), gate)


def _merge_body(a_ref, b1_ref, b2_ref, b3_ref, ua_ref, ub_ref, ga_ref, gb_ref, o_ref):
    a = a_ref[...]
    prec = _dot_precision(a.dtype)
    bsum = (b1_ref[...].astype(F32) + b2_ref[...].astype(F32) + b3_ref[...].astype(F32)).astype(a.dtype)
    ya = jnp.dot(a, ua_ref[...], preferred_element_type=F32, precision=prec)
    yb = jnp.dot(bsum, ub_ref[...], preferred_element_type=F32, precision=prec)
    o_ref[...] = (ga_ref[...] * ya + gb_ref[...] * yb).astype(o_ref.dtype)


def _merge(o_fox, o_cmp, o_slc, o_win, u_fox, u_nsa, mg, d_model):
    n, w = o_fox.shape
    tm = _row_tile(n, 512)
    tn = 512
    nj = d_model // tn
    row = pl.BlockSpec((tm, w), lambda i, j: (i, 0))
    return pl.pallas_call(
        _merge_body,
        grid=(n // tm, nj),
        in_specs=[row, row, row, row,
                  pl.BlockSpec((w, tn), lambda i, j: (0, j)),
                  pl.BlockSpec((w, tn), lambda i, j: (0, j)),
                  pl.BlockSpec((tm, tn), lambda i, j: (i, j)),
                  pl.BlockSpec((tm, tn), lambda i, j: (i, j + nj))],
        out_specs=pl.BlockSpec((tm, tn), lambda i, j: (i, j)),
        out_shape=jax.ShapeDtypeStruct((n, d_model), o_fox.dtype),
        compiler_params=_cparams("parallel", "parallel"),
        name="mixer_merge",
    )(o_fox, o_cmp, o_slc, o_win, u_fox, u_nsa, mg, mg)


def _router_body(h_ref, g_ref, w_ref, b_ref, u_ref, route_ref):
    x = h_ref[...]
    xn = x * lax.rsqrt(jnp.mean(x * x, axis=-1, keepdims=True) + NORM_EPS) * g_ref[...]
    u_ref[...] = xn.astype(u_ref.dtype)
    z = jnp.dot(xn, w_ref[...], preferred_element_type=F32, precision=HIGHEST) + b_ref[...]
    c = lax.broadcasted_iota(jnp.int32, z.shape, 1)
    big = z.shape[1]
    is_g = c < N_GROUPS
    gmax = jnp.max(jnp.where(is_g, z, -jnp.inf), axis=-1, keepdims=True)
    gsel = jnp.min(jnp.where(is_g & (z == gmax), c, big), axis=-1, keepdims=True)
    gsum = jnp.sum(jnp.where(is_g, jnp.exp(z - gmax), 0.0), axis=-1, keepdims=True)
    lo = N_GROUPS + gsel * EXPERTS_PER_GROUP
    in_grp = (c >= lo) & (c < lo + EXPERTS_PER_GROUP)
    z1 = jnp.max(jnp.where(in_grp, z, -jnp.inf), axis=-1, keepdims=True)
    c1 = jnp.min(jnp.where(in_grp & (z == z1), c, big), axis=-1, keepdims=True)
    rest = in_grp & (c != c1)
    z2 = jnp.max(jnp.where(rest, z, -jnp.inf), axis=-1, keepdims=True)
    c2 = jnp.min(jnp.where(rest & (z == z2), c, big), axis=-1, keepdims=True)
    e2 = jnp.exp(z2 - z1)
    p1 = 1.0 / (1.0 + e2)
    p2 = e2 / (1.0 + e2)
    gp = 1.0 / gsum
    out = jnp.where(c == 0, (c1 - N_GROUPS).astype(F32), 0.0)
    out = jnp.where(c == 1, (c2 - N_GROUPS).astype(F32), out)
    out = jnp.where(c == 2, gp * p1, out)
    out = jnp.where(c == 3, gp * p2, out)
    route_ref[...] = out


def _router(h, gain, w_cat, b_cat):
    n, d = h.shape
    tm = _row_tile(n, 512)
    return pl.pallas_call(
        _router_body,
        grid=(n // tm,),
        in_specs=[pl.BlockSpec((tm, d), lambda i: (i, 0)), pl.BlockSpec((1, d), lambda i: (0, 0)),
                  pl.BlockSpec((d, LANES), lambda i: (0, 0)), pl.BlockSpec((1, LANES), lambda i: (0, 0))],
        out_specs=[pl.BlockSpec((tm, d), lambda i: (i, 0)), pl.BlockSpec((tm, LANES), lambda i: (i, 0))],
        out_shape=[jax.ShapeDtypeStruct((n, d), F32), jax.ShapeDtypeStruct((n, LANES), F32)],
        compiler_params=_cparams("parallel"),
        name="moe_router",
    )(h, gain.reshape(1, d), w_cat, b_cat)


def _gather_rows(src_hbm, dst_ref, sem, idx_ref, base, count, start):
    def body(r, c):
        cp = pltpu.make_async_copy(src_hbm.at[pl.ds(idx_ref[base + r], 1)], dst_ref.at[pl.ds(r, 1)], sem)
        if start:
            cp.start()
        else:
            cp.wait()
        return c

    lax.fori_loop(0, count, body, 0, unroll=8)


def _expert_body(tm, te_ref, nv_ref, tok_ref, cw_ref, wg_ref, wu_ref, wd_ref, x_hbm, o_ref, xbuf, sem):
    i = pl.program_id(0)
    nv = nv_ref[0]

    def rows_of(tile, start):
        slot = lax.rem(tile, 2)
        _gather_rows(x_hbm, xbuf.at[slot], sem.at[slot], tok_ref, tile * tm, tm, start)

    @pl.when(i == 0)
    def _():
        rows_of(0, True)

    @pl.when(i + 1 < nv)
    def _():
        rows_of(i + 1, True)

    @pl.when(i < nv)
    def _():
        rows_of(i, False)
        x = xbuf[lax.rem(i, 2)].astype(BF16)
        hg = jnp.dot(x, wg_ref[0], preferred_element_type=F32)
        hu = jnp.dot(x, wu_ref[0], preferred_element_type=F32)
        hid = (hg * jax.nn.sigmoid(hg)) * hu * cw_ref[...]
        o_ref[...] = jnp.dot(hid.astype(BF16), wd_ref[0], preferred_element_type=F32)

    @pl.when(i >= nv)
    def _():
        o_ref[...] = jnp.zeros(o_ref.shape, o_ref.dtype)


def _experts(x, row_token, w_sorted, tile_expert, n_valid, wg, wu, wd, tm):
    rows = row_token.shape[0]
    d = x.shape[1]
    n_tiles = rows // tm
    f = wg.shape[2]

    def row_map(i, te, nv, tok):
        return (jnp.minimum(i, nv[0] - 1), 0)

    def w_map(i, te, nv, tok):
        return (te[i], 0, 0)

    grid_spec = pltpu.PrefetchScalarGridSpec(
        num_scalar_prefetch=3,
        grid=(n_tiles,),
        in_specs=[pl.BlockSpec((tm, 1), row_map),
                  pl.BlockSpec((1, d, f), w_map), pl.BlockSpec((1, d, f), w_map), pl.BlockSpec((1, f, d), w_map),
                  pl.BlockSpec(memory_space=pl.ANY)],
        out_specs=pl.BlockSpec((tm, d), lambda i, te, nv, tok: (i, 0)),
        scratch_shapes=[pltpu.VMEM((2, tm, d), F32), pltpu.SemaphoreType.DMA((2,))],
    )
    return pl.pallas_call(
        functools.partial(_expert_body, tm),
        grid_spec=grid_spec,
        out_shape=jax.ShapeDtypeStruct((rows, d), F32),
        compiler_params=_cparams("arbitrary"),
        name="moe_experts",
    )(tile_expert, n_valid, row_token, w_sorted, wg, wu, wd, x)


def _combine_body(tm, n, slot_ref, h_ref, src_hbm, o_ref, buf, sem):
    i = pl.program_id(0)
    n_tiles = pl.num_programs(0)

    def rows_of(tile, start):
        slot = lax.rem(tile, 2)
        for k in range(2):
            _gather_rows(src_hbm, buf.at[slot, k], sem.at[slot], slot_ref, k * n + tile * tm, tm, start)

    @pl.when(i == 0)
    def _():
        rows_of(0, True)

    @pl.when(i + 1 < n_tiles)
    def _():
        rows_of(i + 1, True)

    rows_of(i, False)
    slot = lax.rem(i, 2)
    o_ref[...] = h_ref[...] + buf[slot, 0] + buf[slot, 1]


def _moe_combine(h, out_sorted, slot_kn):
    n, d = h.shape
    tm = _row_tile(n, 256)
    grid_spec = pltpu.PrefetchScalarGridSpec(
        num_scalar_prefetch=1,
        grid=(n // tm,),
        in_specs=[pl.BlockSpec((tm, d), lambda i, sl: (i, 0)), pl.BlockSpec(memory_space=pl.ANY)],
        out_specs=pl.BlockSpec((tm, d), lambda i, sl: (i, 0)),
        scratch_shapes=[pltpu.VMEM((2, 2, tm, d), F32), pltpu.SemaphoreType.DMA((2,))],
    )
    return pl.pallas_call(
        functools.partial(_combine_body, tm, n),
        grid_spec=grid_spec,
        out_shape=jax.ShapeDtypeStruct((n, d), F32),
        compiler_params=_cparams("arbitrary"),
        name="moe_combine",
    )(slot_kn, h, out_sorted)


def _moe(h, gain, w_router, b_router, wg, wu, wd):
    n, d = h.shape
    ut, route = _router(h, gain, w_router, b_router)
    expert = route[:, :2].astype(jnp.int32).reshape(-1)
    weight = route[:, 2:4].reshape(-1)
    n_asg = 2 * n
    tm = 256 if n_asg >= 8192 else 16
    n_tiles = n_asg // tm + N_EXPERTS
    onehot = (expert[:, None] == jnp.arange(N_EXPERTS, dtype=jnp.int32)[None, :]).astype(jnp.int32)
    running = jnp.cumsum(onehot, axis=0)
    counts = running[-1]
    rank = jnp.sum(onehot * running, axis=1) - 1
    tiles_per = (counts + tm - 1) // tm
    tile_end = jnp.cumsum(tiles_per)
    start_row = (tile_end - tiles_per) * tm
    slot = (jnp.sum(onehot * start_row[None, :], axis=1) + rank).astype(jnp.int32)
    row_token = jnp.zeros((n_tiles * tm,), jnp.int32).at[slot].set(jnp.arange(n_asg, dtype=jnp.int32) // 2)
    row_weight = jnp.zeros((n_tiles * tm,), F32).at[slot].set(weight)
    n_valid = tile_end[-1].astype(jnp.int32)
    tile_ids = jnp.minimum(jnp.arange(n_tiles, dtype=jnp.int32), n_valid - 1)
    tile_expert = jnp.sum((tile_end[None, :] <= tile_ids[:, None]).astype(jnp.int32), axis=1)
    out_sorted = _experts(ut, row_token, row_weight[:, None], tile_expert, n_valid.reshape(1), wg, wu, wd, tm)
    return _moe_combine(h, out_sorted, jnp.transpose(slot.reshape(n, 2)).reshape(-1))


def _rope_tables(positions):
    half = HEAD_DIM // 2
    inv = ROPE_THETA ** (-np.arange(half, dtype=np.float64) / half)
    ang = np.asarray(positions, np.float64)[:, None] * inv[None, :]
    cos = np.concatenate([np.cos(ang), np.cos(ang)], axis=-1)
    sin = np.concatenate([-np.sin(ang), np.sin(ang)], axis=-1)
    return jnp.asarray(cos, F32), jnp.asarray(sin, F32)


def _project(u, wts, p, b, t, q_start):
    n = u.shape[0]
    tm = _row_tile(n, 1024)
    tn = 256
    scale = HEAD_DIM ** -0.5
    gain_spec = pl.BlockSpec((1, HEAD_DIM), lambda i, j: (0, 0))

    def run(w, epilogue, extras, specs, dtype, name, tn_=tn):
        shape = jax.ShapeDtypeStruct((n, w.shape[1]), dtype)
        spec = pl.BlockSpec((tm, tn_), lambda i, j: (i, j))
        return _matmul(u, w, epilogue, extras, specs, shape, spec, tm, tn_, name)

    cos, sin = _rope_tables(q_start + (np.arange(max(tm, t)) % t))
    n_tab = cos.shape[0] // tm
    tab_spec = pl.BlockSpec((tm, HEAD_DIM), lambda i, j: (i % n_tab, 0))

    fq = run(wts["fq"], functools.partial(_ep_head_norm, scale), [p["fox_q_norm"]], [gain_spec], u.dtype, "proj_fq")
    fk = run(wts["fk"], functools.partial(_ep_head_norm, 1.0), [p["fox_k_norm"]], [gain_spec], F32, "proj_fk")
    fv = run(wts["fv"], _ep_identity, [], [], F32, "proj_fv")
    nq = run(wts["nq"], functools.partial(_ep_head_norm_rope, scale), [p["nsa_q_norm"], cos, sin],
             [gain_spec, tab_spec, tab_spec], u.dtype, "proj_nq")
    nk = run(wts["nk"], functools.partial(_ep_head_norm_rope, 1.0), [p["nsa_k_norm"], cos, sin],
             [gain_spec, tab_spec, tab_spec], F32, "proj_nk")
    nv = run(wts["nv"], _ep_identity, [], [], F32, "proj_nv")
    mg = run(wts["mg"], _ep_sigmoid, [], [], F32, "proj_mg")
    small = run(wts["small"], _ep_small, [p["small_bias"]], [pl.BlockSpec((1, LANES), lambda i, j: (0, 0))],
                F32, "proj_small", LANES)
    return fq, fk, fv, nq, nk, nv, mg, small


def _layer(x, p, wts, q_start, past):
    b, t, d = x.shape
    n = b * t
    g = NSA_KV_GROUPS
    kvw = g * HEAD_DIM
    x2d = x.reshape(n, d)
    dense = wts["f32"] if past is not None else wts
    u = _rmsnorm(x2d, p["norm_mix"], dense["fq"].dtype)
    fq, fk, fv, nq, nk, nv, mg, small = _project(u, dense, p, b, t, q_start)

    logf = small[:, :FOX_HEADS].reshape(b, t, FOX_HEADS)
    gate3 = small.reshape(b, t, LANES)
    fq3 = fq.reshape(b, t, -1)
    nq3 = nq.reshape(b, t, -1)
    fk3 = fk.reshape(b, t, -1)
    fv3 = fv.reshape(b, t, -1)
    nk3 = nk.reshape(b, t, 3 * kvw)
    nv3 = nv.reshape(b, t, 3 * kvw)
    ck, sk, wk = (nk3[:, :, i * kvw:(i + 1) * kvw] for i in range(3))
    cv, sv, wv = (nv3[:, :, i * kvw:(i + 1) * kvw] for i in range(3))

    if past is None:
        n_keys = t
        n_lp = t // LANES
        lf_pool = jnp.transpose(logf.reshape(b * n_lp, LANES, FOX_HEADS), (0, 2, 1))
        bias = _fox_key_bias(lf_pool, jnp.arange(b * n_lp, dtype=jnp.int32).reshape(b, n_lp), None)
        o_fox = _flash_attention("fox", fq3, fk3, fv3, 0, t, bias=bias)
        kc = _compress(ck, p["cmp_pos_k"], p["cmp_w1_k"], p["cmp_w2_k"])
        vc = _compress(cv, p["cmp_pos_v"], p["cmp_w1_v"], p["cmp_w2_v"])
        o_cmp, sel = _cmp_attention(nq3, kc, vc, gate3, t, q_start, n_keys, BF16)
        o_slc = _flash_attention("slc", nq3, nk3, nv3, g, t, sel=sel, gate=gate3, gate_col=1)
        o_win = _flash_attention("win", nq3, nk3, nv3, 2 * g, t, gate=gate3, gate_col=2)
        keep = min(WINDOW, t)
        win_k, win_v = wk[:, t - keep:], wv[:, t - keep:]
    else:
        table = past["table"]
        n_pages = table.shape[1]
        page = past["page"]
        past_len = n_pages * page
        n_keys = past_len + t
        new_page = jnp.pad(jnp.transpose(logf, (0, 2, 1)), ((0, 0), (0, 0), (0, page - t)))
        bias = _fox_key_bias(past["fox_logf"], table, new_page)[:, :, :n_keys]
        causal = jnp.where(jnp.arange(t)[None, :] <= jnp.arange(t)[:, None], 0.0, NEG_INF).astype(F32)
        bias_new = bias[:, :, None, past_len:] + causal[None, None]
        qf = jnp.transpose(fq3.reshape(b, t, FOX_HEADS, HEAD_DIM), (0, 2, 1, 3))
        o = _paged_attention(qf, past["fox_k"], past["fox_v"], table, bias[:, :, None, :past_len],
                             fk3, fv3, bias_new)
        o_fox = jnp.transpose(o, (0, 2, 1, 3)).reshape(b, t, -1)
        nc = (n_keys - CMP_BLOCK) // CMP_STRIDE + 1
        assert (nc + 1) * CMP_STRIDE <= past_len and page % CMP_STRIDE == 0
        kc = _compress_paged(past["cmp_k"], table, p["cmp_pos_k"], p["cmp_w1_k"], p["cmp_w2_k"])
        vc = _compress_paged(past["cmp_v"], table, p["cmp_pos_v"], p["cmp_w1_v"], p["cmp_w2_v"])
        o_cmp, sel = _cmp_attention(nq3, kc, vc, gate3, t, q_start, n_keys, F32)
        rows = HEADS_PER_GROUP * t
        qn = jnp.transpose(nq3.reshape(b, t, g, HEADS_PER_GROUP, HEAD_DIM), (0, 2, 3, 1, 4)).reshape(b, g, rows, HEAD_DIM)
        gates = small[:, FOX_HEADS:FOX_HEADS + NSA_HEADS * N_NSA_BRANCHES].reshape(b, t, g, HEADS_PER_GROUP, N_NSA_BRANCHES)
        gates = jnp.transpose(gates, (4, 0, 2, 3, 1)).reshape(N_NSA_BRANCHES, b, g, rows, 1)
        ns = -(-n_keys // SEL_BLOCK)
        key_sel = jnp.repeat(sel[..., :ns], SEL_BLOCK, axis=-1)[..., :n_keys]
        sel_bias = jnp.where(key_sel > 0.5, 0.0, NEG_INF).astype(F32)
        sel_new = jnp.tile(sel_bias[..., past_len:] + causal[None, None], (1, 1, HEADS_PER_GROUP, 1))
        o = _paged_attention(qn, past["slc_k"], past["slc_v"], table, sel_bias[..., :past_len],
                             sk, sv, sel_new, gate=gates[1])
        o_slc = _rows_to_tokens(o, b, t)
        win_len = past["win_k"].shape[1]
        assert win_len % page == 0
        k_start = q_start - win_len
        q_abs = q_start + jnp.arange(t)[:, None]
        k_abs = k_start + jnp.arange(win_len)[None, :]
        band = jnp.where((k_abs >= 0) & (k_abs > q_abs - WINDOW), 0.0, NEG_INF).astype(F32)
        band = jnp.broadcast_to(band[None, None], (b, g, t, win_len))
        band_new = jnp.where(jnp.arange(t)[None, :] > jnp.arange(t)[:, None] - WINDOW, causal, NEG_INF)
        band_new = jnp.broadcast_to(jnp.tile(band_new, (HEADS_PER_GROUP, 1))[None, None], (b, g, rows, t))
        o = _paged_attention(qn, past["win_k_pool"], past["win_v_pool"], past["win_table"], band, wk, wv,
                             band_new, gate=gates[2])
        o_win = _rows_to_tokens(o, b, t)
        wk_all = jnp.concatenate([past["win_k"], wk], axis=1)
        wv_all = jnp.concatenate([past["win_v"], wv], axis=1)
        keep = min(WINDOW, wk_all.shape[1])
        win_k, win_v = wk_all[:, -keep:], wv_all[:, -keep:]

    m = _merge(o_fox.reshape(n, -1), o_cmp.reshape(n, -1), o_slc.reshape(n, -1), o_win.reshape(n, -1),
               dense["up_fox"], dense["up_nsa"], mg, d)
    tm = _row_tile(n, 512)
    h = _matmul(m, dense["out"], _ep_residual, [x2d], [pl.BlockSpec((tm, 512), lambda i, j: (i, j))],
                jax.ShapeDtypeStruct((n, d), F32), pl.BlockSpec((tm, 512), lambda i, j: (i, j)), tm, 512, "out_proj")
    y = _moe(h, p["norm_ffn"], wts["router"], p["router_bias"], wts["gate_e"], wts["up_e"], wts["down_e"])

    def heads(a, hh):
        return a.reshape(b, -1, hh, HEAD_DIM)

    state = (heads(fk3, FOX_HEADS), heads(fv3, FOX_HEADS), logf, heads(ck, g), heads(cv, g), heads(sk, g),
             heads(sv, g), heads(win_k, g), heads(win_v, g))
    return y.reshape(b, t, d), state


def _rows_to_tokens(o, b, t):
    o = o.reshape(b, NSA_KV_GROUPS, HEADS_PER_GROUP, t, HEAD_DIM)
    return jnp.transpose(o, (0, 3, 1, 2, 4)).reshape(b, t, -1)


def _prepare(l, norm_mix, w_in, fox_f_bias, fox_q_norm, fox_k_norm, nsa_q_norm, nsa_k_norm, cmp_pos_k, cmp_w1_k,
             cmp_w2_k, cmp_pos_v, cmp_w1_v, cmp_w2_v, nsa_gate_bias, w_up_fox, w_up_nsa, w_out, norm_ffn,
             w_router_group, b_router_group, w_router_expert, b_router_expert, w_gate_e, w_up_e, w_down_e):
    d = w_in.shape[1]
    fw = FOX_HEADS * HEAD_DIM
    nw = NSA_HEADS * HEAD_DIM
    kvw = NSA_KV_GROUPS * HEAD_DIM
    sizes = (fw, fw, fw, FOX_HEADS, nw, kvw, kvw, kvw, kvw, kvw, kvw, NSA_HEADS * N_NSA_BRANCHES, 2 * d)
    offs = np.concatenate([[0], np.cumsum(sizes)])
    w = w_in[l]

    def seg(i):
        return w[:, offs[i]:offs[i + 1]]

    n_small = FOX_HEADS + NSA_HEADS * N_NSA_BRANCHES
    small_w = jnp.pad(jnp.concatenate([seg(3), seg(11)], axis=1), ((0, 0), (0, LANES - n_small)))
    small_b = jnp.pad(jnp.concatenate([fox_f_bias[l], nsa_gate_bias[l]]), (0, LANES - n_small)).reshape(1, LANES)
    n_route = N_GROUPS + N_EXPERTS
    router_w = jnp.pad(jnp.concatenate([w_router_group[l], w_router_expert[l]], axis=1), ((0, 0), (0, LANES - n_route)))
    router_b = jnp.pad(jnp.concatenate([b_router_group[l], b_router_expert[l]]), (0, LANES - n_route)).reshape(1, LANES)
    dense = {
        "fq": seg(0), "fk": seg(1), "fv": seg(2), "nq": seg(4),
        "nk": jnp.concatenate([seg(5), seg(7), seg(9)], axis=1),
        "nv": jnp.concatenate([seg(6), seg(8), seg(10)], axis=1),
        "mg": seg(12), "small": small_w,
        "up_fox": w_up_fox[l], "up_nsa": w_up_nsa[l], "out": w_out[l],
    }
    wts = {
        **{k: v.astype(BF16) for k, v in dense.items()},
        "f32": dense,
        "router": router_w,
        "gate_e": w_gate_e[l].astype(BF16), "up_e": w_up_e[l].astype(BF16), "down_e": w_down_e[l].astype(BF16),
    }
    p = {
        "norm_mix": norm_mix[l], "fox_q_norm": fox_q_norm[l].reshape(1, -1), "fox_k_norm": fox_k_norm[l].reshape(1, -1),
        "nsa_q_norm": nsa_q_norm[l].reshape(1, -1), "nsa_k_norm": nsa_k_norm[l].reshape(1, -1),
        "small_bias": small_b, "cmp_pos_k": cmp_pos_k[l], "cmp_w1_k": cmp_w1_k[l], "cmp_w2_k": cmp_w2_k[l],
        "cmp_pos_v": cmp_pos_v[l], "cmp_w1_v": cmp_w1_v[l], "cmp_w2_v": cmp_w2_v[l],
        "norm_ffn": norm_ffn[l], "router_bias": router_b,
    }
    return wts, p


def kernel(x_prompt, x_sample, cache_fox_k, cache_fox_v, cache_fox_logf, cache_cmp_k, cache_cmp_v, cache_slc_k, cache_slc_v, cache_win_k, cache_win_v, page_table, norm_mix, w_in, fox_f_bias, fox_q_norm, fox_k_norm, nsa_q_norm, nsa_k_norm, cmp_pos_k, cmp_w1_k, cmp_w2_k, cmp_pos_v, cmp_w1_v, cmp_w2_v, nsa_gate_bias, w_up_fox, w_up_nsa, w_out, norm_ffn, w_router_group, b_router_group, w_router_expert, b_router_expert, w_gate_e, w_up_e, w_down_e):
    depth = w_in.shape[0]
    n_phys, page = cache_fox_k.shape[1], cache_fox_k.shape[2]
    past_len = page_table.shape[1] * page
    h_p, h_s = x_prompt, x_sample
    p_states, s_states = [], []
    for l in range(depth):
        wts, p = _prepare(l, norm_mix, w_in, fox_f_bias, fox_q_norm, fox_k_norm, nsa_q_norm, nsa_k_norm, cmp_pos_k,
                          cmp_w1_k, cmp_w2_k, cmp_pos_v, cmp_w1_v, cmp_w2_v, nsa_gate_bias, w_up_fox, w_up_nsa,
                          w_out, norm_ffn, w_router_group, b_router_group, w_router_expert, b_router_expert,
                          w_gate_e, w_up_e, w_down_e)
        h_p, st_p = _layer(h_p, p, wts, 0, None)
        p_states.append(st_p)
        win_len = cache_win_k.shape[2]
        n_win = win_len // page
        dec_b = page_table.shape[0]

        def pool(c):
            return c.reshape(-1, page * c.shape[3], HEAD_DIM)

        past = {
            "table": page_table + l * n_phys, "page": page,
            "fox_k": pool(cache_fox_k), "fox_v": pool(cache_fox_v),
            "fox_logf": jnp.transpose(cache_fox_logf.reshape(-1, page, cache_fox_logf.shape[3]), (0, 2, 1)),
            "cmp_k": pool(cache_cmp_k), "cmp_v": pool(cache_cmp_v),
            "slc_k": pool(cache_slc_k), "slc_v": pool(cache_slc_v),
            "win_k_pool": pool(cache_win_k), "win_v_pool": pool(cache_win_v),
            "win_table": (l * dec_b + jnp.arange(dec_b, dtype=jnp.int32))[:, None] * n_win
                         + jnp.arange(n_win, dtype=jnp.int32)[None, :],
            "win_k": cache_win_k[l].reshape(dec_b, win_len, -1), "win_v": cache_win_v[l].reshape(dec_b, win_len, -1),
        }
        h_s, st_s = _layer(h_s, p, wts, past_len, past)
        s_states.append(st_s)
    p_out = [jnp.stack([st[i] for st in p_states], axis=0) for i in range(9)]
    s_out = [jnp.stack([st[i] for st in s_states], axis=0) for i in range(9)]
    return (h_p, h_s, *p_out, *s_out)
```

```python
import functools

import numpy as np
import jax
import jax.numpy as jnp
from jax import lax
from jax.experimental import pallas as pl
from jax.experimental.pallas import tpu as pltpu

F32 = jnp.float32
BF16 = jnp.bfloat16

HEAD_DIM = 128
FOX_HEADS = 8
NSA_HEADS = 8
NSA_KV_GROUPS = 2
HEADS_PER_GROUP = NSA_HEADS // NSA_KV_GROUPS
N_NSA_BRANCHES = 3
CMP_BLOCK = 32
CMP_STRIDE = 16
SEL_BLOCK = 64
N_SELECT = 16
N_LOCAL_SEL = 2
WINDOW = 512
N_GROUPS = 4
EXPERTS_PER_GROUP = 8
N_EXPERTS = N_GROUPS * EXPERTS_PER_GROUP
ROPE_THETA = 10000.0
NORM_EPS = 1e-6
NEG_INF = -1e30
FORCE_SCORE = 1e9
PAD_SCORE = -2e38
TAKEN_SCORE = -3e38

LANES = 128
SUBLANES = 8
VMEM_LIMIT_BYTES = 48 * 1024 * 1024

HIGHEST = lax.Precision.HIGHEST


def _cparams(*sem):
    return pltpu.CompilerParams(dimension_semantics=tuple(sem), vmem_limit_bytes=VMEM_LIMIT_BYTES)


def _row_tile(n, pref):
    t = min(n, pref)
    assert n % t == 0, (n, t)
    return t


def _rmsnorm_body(x_ref, g_ref, o_ref):
    x = x_ref[...]
    y = x * lax.rsqrt(jnp.mean(x * x, axis=-1, keepdims=True) + NORM_EPS)
    o_ref[...] = (y * g_ref[...]).astype(o_ref.dtype)


def _rmsnorm(x2d, gain, out_dtype):
    n, d = x2d.shape
    tm = _row_tile(n, 512)
    return pl.pallas_call(
        _rmsnorm_body,
        grid=(n // tm,),
        in_specs=[pl.BlockSpec((tm, d), lambda i: (i, 0)), pl.BlockSpec((1, d), lambda i: (0, 0))],
        out_specs=pl.BlockSpec((tm, d), lambda i: (i, 0)),
        out_shape=jax.ShapeDtypeStruct((n, d), out_dtype),
        compiler_params=_cparams("parallel"),
        name="rmsnorm",
    )(x2d, gain.reshape(1, d))


def _dot_precision(dtype):
    return HIGHEST if dtype == F32 else None


def _mm_body(epilogue, n_extra, a_ref, w_ref, *rest):
    acc = jnp.dot(a_ref[...], w_ref[...], preferred_element_type=F32, precision=_dot_precision(a_ref.dtype))
    epilogue(acc, rest[:n_extra], rest[n_extra:])


def _matmul(a, w, epilogue, extras, extra_specs, out_shapes, out_specs, tm, tn, name):
    n, k = a.shape
    c = w.shape[1]
    assert n % tm == 0 and c % tn == 0, (n, tm, c, tn)
    return pl.pallas_call(
        functools.partial(_mm_body, epilogue, len(extras)),
        grid=(n // tm, c // tn),
        in_specs=[pl.BlockSpec((tm, k), lambda i, j: (i, 0)), pl.BlockSpec((k, tn), lambda i, j: (0, j))]
        + list(extra_specs),
        out_specs=out_specs,
        out_shape=out_shapes,
        compiler_params=_cparams("parallel", "parallel"),
        name=name,
    )(a, w, *extras)


def _head_norm(x, gain):
    return x * lax.rsqrt(jnp.mean(x * x, axis=-1, keepdims=True) + NORM_EPS) * gain


def _rope_rot(x, cos, sin_signed):
    return x * cos + pltpu.roll(x, HEAD_DIM // 2, 1) * sin_signed


def _ep_head_norm(scale, acc, extras, outs):
    (gain_ref,) = extras
    (o_ref,) = outs
    for h in range(acc.shape[1] // HEAD_DIM):
        sl = slice(h * HEAD_DIM, (h + 1) * HEAD_DIM)
        o_ref[:, sl] = (_head_norm(acc[:, sl], gain_ref[...]) * scale).astype(o_ref.dtype)


def _ep_head_norm_rope(scale, acc, extras, outs):
    gain_ref, cos_ref, sin_ref = extras
    (o_ref,) = outs
    for h in range(acc.shape[1] // HEAD_DIM):
        sl = slice(h * HEAD_DIM, (h + 1) * HEAD_DIM)
        y = _rope_rot(_head_norm(acc[:, sl], gain_ref[...]), cos_ref[...], sin_ref[...])
        o_ref[:, sl] = (y * scale).astype(o_ref.dtype)


def _ep_identity(acc, extras, outs):
    (o_ref,) = outs
    o_ref[...] = acc.astype(o_ref.dtype)


def _ep_sigmoid(acc, extras, outs):
    (o_ref,) = outs
    o_ref[...] = jax.nn.sigmoid(acc).astype(o_ref.dtype)


def _ep_small(acc, extras, outs):
    (bias_ref,) = extras
    (o_ref,) = outs
    z = acc + bias_ref[...]
    col = lax.broadcasted_iota(jnp.int32, z.shape, 1)
    e = jnp.exp(-jnp.abs(z))
    log_sig = jnp.minimum(z, 0.0) - jnp.log(1.0 + e)
    sig = jnp.where(z >= 0, 1.0, e) / (1.0 + e)
    o_ref[...] = jnp.where(col < FOX_HEADS, log_sig, sig)


def _ep_residual(acc, extras, outs):
    (x_ref,) = extras
    (o_ref,) = outs
    o_ref[...] = x_ref[...] + acc


def _fox_bias_body(n_pages, has_new, *refs):
    if has_new:
        tbl_ref, new_ref, pool_ref, o_ref, buf, sem = refs
    else:
        tbl_ref, pool_ref, o_ref, buf, sem = refs
    bi = pl.program_id(0)

    def copy(p):
        return pltpu.make_async_copy(pool_ref.at[tbl_ref[bi * n_pages + p]], buf.at[p], sem)

    def start(p, c):
        copy(p).start()
        return c

    def wait(p, c):
        copy(p).wait()
        return c

    lax.fori_loop(0, n_pages, start, 0)
    lax.fori_loop(0, n_pages, wait, 0)
    if has_new:
        buf[n_pages] = new_ref[0]
    x = buf[...]
    p, h, w = x.shape
    within = x.reshape(p * h, w)
    lane = lax.broadcasted_iota(jnp.int32, within.shape, 1)
    s = 1
    while s < w:
        within = within + jnp.where(lane >= s, pltpu.roll(within, s, 1), 0.0)
        s *= 2
    within = within.reshape(p, h, w)
    total = jnp.broadcast_to(within[:, :, w - 1:w], x.shape)
    incl = total
    s = 1
    while s < p:
        incl = incl + jnp.concatenate([jnp.zeros((s, h, w), F32), incl[:p - s]], axis=0)
        s *= 2
    o_ref[0] = -(within + (incl - total))


def _fox_key_bias(pool, table, new_page):
    b, n_pages = table.shape
    h, w = pool.shape[1:]
    has_new = new_page is not None
    p_tot = n_pages + (1 if has_new else 0)
    in_specs = [pl.BlockSpec(memory_space=pl.ANY)]
    args = [pool]
    if has_new:
        in_specs.insert(0, pl.BlockSpec((1, h, w), lambda i, tbl: (i, 0, 0)))
        args.insert(0, new_page)
    grid_spec = pltpu.PrefetchScalarGridSpec(
        num_scalar_prefetch=1,
        grid=(b,),
        in_specs=in_specs,
        out_specs=pl.BlockSpec((1, p_tot, h, w), lambda i, tbl: (i, 0, 0, 0)),
        scratch_shapes=[pltpu.VMEM((p_tot, h, w), F32), pltpu.SemaphoreType.DMA(())],
    )
    out = pl.pallas_call(
        functools.partial(_fox_bias_body, n_pages, has_new),
        grid_spec=grid_spec,
        out_shape=jax.ShapeDtypeStruct((b, p_tot, h, w), F32),
        compiler_params=_cparams("arbitrary"),
        name="fox_key_bias",
    )(table.reshape(-1), *args)
    return jnp.transpose(out, (0, 2, 1, 3)).reshape(b, h, p_tot * w)


def _flash_body(mode, heads, tq, tk, n_kv, gate_col, *refs):
    if mode == "fox":
        q_ref, k_ref, v_ref, bias_ref, o_ref, kb_ref, vb_ref = refs
    elif mode == "slc":
        q_ref, k_ref, v_ref, sel_ref, expand_ref, gate_ref, o_ref, kb_ref, vb_ref, selx_ref = refs
    else:
        q_ref, k_ref, v_ref, gate_ref, o_ref, kb_ref, vb_ref = refs
    unit = pl.program_id(1)
    qi = pl.program_id(2)

    @pl.when(qi == 0)
    def _():
        kb_ref[...] = k_ref[0].astype(BF16)
        vb_ref[...] = v_ref[0].astype(BF16)

    if mode == "slc":
        sel = sel_ref[0, 0].astype(BF16)
        for j in range(n_kv):
            selx_ref[j] = jnp.dot(sel, expand_ref[:, j * tk:(j + 1) * tk], preferred_element_type=F32)

    q_pos = qi * tq + lax.broadcasted_iota(jnp.int32, (tq, tk), 0)
    col = lax.broadcasted_iota(jnp.int32, (tq, tk), 1)
    qs = [q_ref[0, :, r * HEAD_DIM:(r + 1) * HEAD_DIM] for r in range(heads)]

    def block(j, carry, diagonal):
        ks = pl.multiple_of(j * tk, tk)
        k_pos = ks + col
        ok = None
        if mode == "slc":
            ok = selx_ref[j] > 0.5
        if mode == "win":
            ok = (k_pos <= q_pos) & (k_pos > q_pos - WINDOW)
        elif diagonal:
            ok = (k_pos <= q_pos) if ok is None else ok & (k_pos <= q_pos)
        out = []
        for r in range(heads):
            m, l, acc = carry[r]
            c0 = r * HEAD_DIM if mode == "fox" else 0
            kb = kb_ref[pl.ds(ks, tk), c0:c0 + HEAD_DIM]
            vb = vb_ref[pl.ds(ks, tk), c0:c0 + HEAD_DIM]
            s = lax.dot_general(qs[r], kb, _NT, preferred_element_type=F32)
            if mode == "fox":
                s = s + bias_ref[0, r:r + 1, pl.ds(ks, tk)]
            if ok is not None:
                s = jnp.where(ok, s, NEG_INF)
            m_new = jnp.maximum(m, jnp.max(s, axis=-1, keepdims=True))
            alpha = jnp.exp(m - m_new)
            p = jnp.exp(s - m_new)
            l = alpha * l + jnp.sum(p, axis=-1, keepdims=True)
            acc = alpha * acc + jnp.dot(p.astype(BF16), vb, preferred_element_type=F32)
            out.append((m_new, l, acc))
        return tuple(out)

    init = tuple((jnp.full((tq, 1), NEG_INF, F32), jnp.zeros((tq, 1), F32), jnp.zeros((tq, HEAD_DIM), F32))
                 for _ in range(heads))
    if mode == "win":
        lo = jnp.maximum(qi - (-(-WINDOW // tk)), 0)
        state = lax.fori_loop(lo, qi + 1, functools.partial(block, diagonal=False), init)
    else:
        state = lax.fori_loop(0, qi, functools.partial(block, diagonal=False), init)
        state = block(qi, state, True)

    for r in range(heads):
        m, l, acc = state[r]
        w = 1.0 / l
        if mode != "fox":
            gate = gate_ref[0]
            lane = lax.broadcasted_iota(jnp.int32, gate.shape, 1)
            want = FOX_HEADS + (unit * heads + r) * N_NSA_BRANCHES + gate_col
            w = w * jnp.sum(jnp.where(lane == want, gate, 0.0), axis=-1, keepdims=True)
        o_ref[0, :, r * HEAD_DIM:(r + 1) * HEAD_DIM] = (acc * w).astype(o_ref.dtype)


def _flash_attention(mode, q, k, v, k_col0, t, *, bias=None, sel=None, gate=None, gate_col=0):
    b = q.shape[0]
    heads = HEADS_PER_GROUP
    qw = heads * HEAD_DIM
    units = q.shape[2] // qw
    kvw = qw if mode == "fox" else HEAD_DIM
    tq = tk = _row_tile(t, 256)
    n_kv = t // tk
    in_specs = [
        pl.BlockSpec((1, tq, qw), lambda bi, u, i: (bi, i, u)),
        pl.BlockSpec((1, t, kvw), lambda bi, u, i: (bi, 0, k_col0 + u)),
        pl.BlockSpec((1, t, kvw), lambda bi, u, i: (bi, 0, k_col0 + u)),
    ]
    args = [q, k, v]
    scratch = [pltpu.VMEM((t, kvw), BF16), pltpu.VMEM((t, kvw), BF16)]
    if mode == "fox":
        in_specs.append(pl.BlockSpec((1, heads, t), lambda bi, u, i: (bi * units + u, 0, 0)))
        args.append(bias.reshape(b * units, heads, t))
    else:
        if mode == "slc":
            nsp = sel.shape[-1]
            expand = (np.arange(nsp)[:, None] == (np.arange(t)[None, :] // SEL_BLOCK)).astype(np.float32)
            in_specs.append(pl.BlockSpec((1, 1, tq, nsp), lambda bi, u, i: (bi, u, i, 0)))
            in_specs.append(pl.BlockSpec((nsp, t), lambda bi, u, i: (0, 0)))
            args += [sel, jnp.asarray(expand, BF16)]
            scratch.append(pltpu.VMEM((n_kv, tq, tk), F32))
        in_specs.append(pl.BlockSpec((1, tq, LANES), lambda bi, u, i: (bi, i, 0)))
        args.append(gate)
    return pl.pallas_call(
        functools.partial(_flash_body, mode, heads, tq, tk, n_kv, gate_col),
        grid=(b, units, t // tq),
        in_specs=in_specs,
        out_specs=pl.BlockSpec((1, tq, qw), lambda bi, u, i: (bi, i, u)),
        out_shape=jax.ShapeDtypeStruct((b, t, units * qw), BF16),
        scratch_shapes=scratch,
        compiler_params=_cparams("parallel", "parallel", "arbitrary"),
        name="attn_" + mode,
    )(*args)


PAGED_BUFFER_BYTES = 2 * 1024 * 1024


def _pages_per_step(n_pages, page_bytes):
    pg = max(1, min(n_pages, PAGED_BUFFER_BYTES // page_bytes))
    while n_pages % pg:
        pg -= 1
    return pg


def _paged_body(units, rows, rows_b, n_pages, pg, page, use_gate, *refs):
    if use_gate:
        (tbl_ref, q_ref, bias_ref, kn_ref, vn_ref, biasn_ref, gate_ref, kpool, vpool, o_ref,
         kbuf, vbuf, sem, m_ref, l_ref, acc_ref) = refs
    else:
        (tbl_ref, q_ref, bias_ref, kn_ref, vn_ref, biasn_ref, kpool, vpool, o_ref,
         kbuf, vbuf, sem, m_ref, l_ref, acc_ref) = refs
        gate_ref = None
    bi = pl.program_id(0)
    s = pl.program_id(1)
    n_steps = n_pages // pg
    rpp = page * units
    n_keys = pg * page

    def copies(step, slot):
        out = []
        for k in range(pg):
            phys = tbl_ref[bi * n_pages + step * pg + k]
            out.append(pltpu.make_async_copy(kpool.at[phys], kbuf.at[slot, pl.ds(k * rpp, rpp)], sem.at[0, slot]))
            out.append(pltpu.make_async_copy(vpool.at[phys], vbuf.at[slot, pl.ds(k * rpp, rpp)], sem.at[1, slot]))
        return out

    @pl.when(s == 0)
    def _():
        m_ref[...] = jnp.full(m_ref.shape, NEG_INF, F32)
        l_ref[...] = jnp.zeros(l_ref.shape, F32)
        acc_ref[...] = jnp.zeros(acc_ref.shape, F32)
        for c in copies(0, 0):
            c.start()

    @pl.when(s + 1 < n_steps)
    def _():
        for c in copies(s + 1, lax.rem(s + 1, 2)):
            c.start()

    def attend(k, v, bias, own_blocks):
        k_hi, k_lo = _split_bf16(k)
        v_hi, v_lo = _split_bf16(v)
        sc = _dot_split(q_ref[0], k_hi, k_lo, _NT)
        nk = k.shape[0] // units
        if own_blocks:
            sc = jnp.concatenate([sc[u * rows:(u + 1) * rows, u * nk:(u + 1) * nk] for u in range(units)], axis=0)
        sc = sc + bias
        m = m_ref[...]
        m_new = jnp.maximum(m, jnp.max(sc, axis=-1, keepdims=True))
        alpha = jnp.exp(m - m_new)
        pr = jnp.exp(sc - m_new)
        l_ref[...] = alpha * l_ref[...] + jnp.sum(pr, axis=-1, keepdims=True)
        if own_blocks:
            zero = jnp.zeros((rows, nk), F32)
            pr = jnp.concatenate(
                [jnp.concatenate([pr[u * rows:(u + 1) * rows] if u2 == u else zero for u2 in range(units)], axis=1)
                 for u in range(units)], axis=0)
        acc_ref[...] = alpha * acc_ref[...] + _dot_split(pr, v_hi, v_lo, _NN)
        m_ref[...] = m_new

    def unit_bias(u):
        bb = bias_ref[0, u]
        if rows_b == 1:
            return jnp.broadcast_to(bb, (rows, n_keys))
        return jnp.concatenate([bb] * (rows // rows_b), axis=0)

    @pl.when(s < n_steps)
    def _():
        slot = lax.rem(s, 2)
        for c in copies(s, slot):
            c.wait()
        k = jnp.concatenate([kbuf[slot, pl.ds(u, n_keys, stride=units), :] for u in range(units)], axis=0)
        v = jnp.concatenate([vbuf[slot, pl.ds(u, n_keys, stride=units), :] for u in range(units)], axis=0)
        attend(k, v, jnp.concatenate([unit_bias(u) for u in range(units)], axis=0), True)

    @pl.when(s == n_steps)
    def _():
        k = jnp.concatenate([kn_ref[0, :, u * HEAD_DIM:(u + 1) * HEAD_DIM] for u in range(units)], axis=0)
        v = jnp.concatenate([vn_ref[0, :, u * HEAD_DIM:(u + 1) * HEAD_DIM] for u in range(units)], axis=0)
        attend(k, v, biasn_ref[0], False)
        w = 1.0 / l_ref[...]
        if use_gate:
            w = w * gate_ref[0]
        o_ref[0] = acc_ref[...] * w


def _paged_attention(q, k_pool, v_pool, table, bias, k_new, v_new, bias_new, gate=None):
    b, units, rows, _ = q.shape
    ur = units * rows
    n_pages = table.shape[1]
    rpp = k_pool.shape[1]
    page = rpp // units
    rows_b = bias.shape[2]
    n_new = k_new.shape[1]
    width = k_new.shape[2]
    pg = _pages_per_step(n_pages, rpp * HEAD_DIM * 4)
    n_steps = n_pages // pg
    last = n_steps - 1
    same_unit = jnp.arange(units)[:, None, None, None] == jnp.arange(units)[None, None, :, None]
    bias_new_full = jnp.where(same_unit[None], bias_new[:, :, :, None, :], NEG_INF).reshape(b, ur, units * n_new)
    in_specs = [
        pl.BlockSpec((1, ur, HEAD_DIM), lambda bi, p, tbl: (bi, 0, 0)),
        pl.BlockSpec((1, units, rows_b, pg * page), lambda bi, p, tbl: (bi, 0, 0, jnp.minimum(p, last))),
        pl.BlockSpec((1, n_new, width), lambda bi, p, tbl: (bi, 0, 0)),
        pl.BlockSpec((1, n_new, width), lambda bi, p, tbl: (bi, 0, 0)),
        pl.BlockSpec((1, ur, units * n_new), lambda bi, p, tbl: (bi, 0, 0)),
    ]
    args = [q.reshape(b, ur, HEAD_DIM), bias, k_new, v_new, bias_new_full]
    if gate is not None:
        in_specs.append(pl.BlockSpec((1, ur, 1), lambda bi, p, tbl: (bi, 0, 0)))
        args.append(gate.reshape(b, ur, 1))
    in_specs += [pl.BlockSpec(memory_space=pl.ANY), pl.BlockSpec(memory_space=pl.ANY)]
    args += [k_pool, v_pool]
    grid_spec = pltpu.PrefetchScalarGridSpec(
        num_scalar_prefetch=1,
        grid=(b, n_steps + 1),
        in_specs=in_specs,
        out_specs=pl.BlockSpec((1, ur, HEAD_DIM), lambda bi, p, tbl: (bi, 0, 0)),
        scratch_shapes=[pltpu.VMEM((2, pg * rpp, HEAD_DIM), F32), pltpu.VMEM((2, pg * rpp, HEAD_DIM), F32),
                        pltpu.SemaphoreType.DMA((2, 2)),
                        pltpu.VMEM((ur, 1), F32), pltpu.VMEM((ur, 1), F32), pltpu.VMEM((ur, HEAD_DIM), F32)],
    )
    out = pl.pallas_call(
        functools.partial(_paged_body, units, rows, rows_b, n_pages, pg, page, gate is not None),
        grid_spec=grid_spec,
        out_shape=jax.ShapeDtypeStruct((b, ur, HEAD_DIM), F32),
        compiler_params=_cparams("arbitrary", "arbitrary"),
        name="attn_paged",
    )(table.reshape(-1), *args)
    return out.reshape(b, units, rows, HEAD_DIM)


def _cmp_proj_body(x_ref, pa_ref, pb_ref, wa_ref, wb_ref, za_ref, zb_ref):
    x = x_ref[0]
    za_ref[0] = jnp.dot((x + pa_ref[...]).astype(BF16), wa_ref[...], preferred_element_type=F32)
    zb_ref[0] = jnp.dot((x + pb_ref[...]).astype(BF16), wb_ref[...], preferred_element_type=F32)


def _cmp_mlp_body(n_chunks, za_ref, zb_ref, w2_ref, o_ref):
    zb_next = pltpu.roll(zb_ref[0], n_chunks - 1, 0)
    hid = za_ref[0] + zb_next
    c = 0.7978845608028654
    hid = 0.5 * hid * (1.0 + jnp.tanh(c * (hid + 0.044715 * (hid * hid * hid))))
    w2 = w2_ref[...]
    hb = hid.astype(w2.dtype)
    for g in range(NSA_KV_GROUPS):
        sl = slice(g * HEAD_DIM, (g + 1) * HEAD_DIM)
        o_ref[0, :, sl] = jnp.dot(hb[:, sl], w2, preferred_element_type=F32, precision=_dot_precision(w2.dtype))


def _compress(rows, pos_emb, w1, w2):
    b, tc, gw = rows.shape
    g = NSA_KV_GROUPS
    n_chunks = tc // CMP_STRIDE
    cw = CMP_STRIDE * gw
    x = rows.reshape(b, n_chunks, cw)
    w1r = w1.reshape(2, CMP_STRIDE, HEAD_DIM, HEAD_DIM)
    eye = jnp.eye(g, dtype=w1.dtype)
    wcat = jnp.einsum("hldj,ge->hlgdej", w1r, eye).reshape(2, cw, g * HEAD_DIM).astype(BF16)
    pos = jnp.broadcast_to(pos_emb.reshape(2, CMP_STRIDE, 1, HEAD_DIM), (2, CMP_STRIDE, g, HEAD_DIM)).reshape(2, 1, cw)
    cm = _row_tile(n_chunks, 128)
    za, zb = pl.pallas_call(
        _cmp_proj_body,
        grid=(b, n_chunks // cm),
        in_specs=[
            pl.BlockSpec((1, cm, cw), lambda bi, i: (bi, i, 0)),
            pl.BlockSpec((1, cw), lambda bi, i: (0, 0)),
            pl.BlockSpec((1, cw), lambda bi, i: (0, 0)),
            pl.BlockSpec((cw, g * HEAD_DIM), lambda bi, i: (0, 0)),
            pl.BlockSpec((cw, g * HEAD_DIM), lambda bi, i: (0, 0)),
        ],
        out_specs=[pl.BlockSpec((1, cm, g * HEAD_DIM), lambda bi, i: (bi, i, 0))] * 2,
        out_shape=[jax.ShapeDtypeStruct((b, n_chunks, g * HEAD_DIM), F32)] * 2,
        compiler_params=_cparams("parallel", "parallel"),
        name="cmp_proj",
    )(x, pos[0], pos[1], wcat[0], wcat[1])
    return _cmp_mlp(za, zb, w2.astype(BF16))


def _cmp_mlp(za, zb, w2):
    b, n_chunks, gw = za.shape
    g = NSA_KV_GROUPS
    return pl.pallas_call(
        functools.partial(_cmp_mlp_body, n_chunks),
        grid=(b,),
        in_specs=[
            pl.BlockSpec((1, n_chunks, g * HEAD_DIM), lambda bi: (bi, 0, 0)),
            pl.BlockSpec((1, n_chunks, g * HEAD_DIM), lambda bi: (bi, 0, 0)),
            pl.BlockSpec((HEAD_DIM, HEAD_DIM), lambda bi: (0, 0)),
        ],
        out_specs=pl.BlockSpec((1, n_chunks, g * HEAD_DIM), lambda bi: (bi, 0, 0)),
        out_shape=jax.ShapeDtypeStruct((b, n_chunks, g * HEAD_DIM), F32),
        compiler_params=_cparams("parallel"),
        name="cmp_mlp",
    )(za, zb, w2)


def _split_bf16(x):
    hi = x.astype(BF16)
    return hi, (x - hi.astype(F32)).astype(BF16)


def _dot_split(a, b_hi, b_lo, dims):
    a_hi, a_lo = _split_bf16(a)
    n = a.shape[0]
    both = lax.dot_general(jnp.concatenate([a_hi, a_lo], axis=0), b_hi, dims, preferred_element_type=F32)
    return both[:n] + both[n:] + lax.dot_general(a_hi, b_lo, dims, preferred_element_type=F32)


_NT = (((1,), (1,)), ((), ()))
_NN = (((1,), (0,)), ((), ()))


def _cmp_paged_body(n_pages, pg, page, tbl_ref, pos_ref, whi_ref, wlo_ref, pool, za_ref, zb_ref, xbuf, sem):
    bi = pl.program_id(0)
    s = pl.program_id(1)
    n_steps = n_pages // pg
    g_n = NSA_KV_GROUPS
    rpp = page * g_n
    n_rows = pg * page // CMP_STRIDE

    def copies(step, slot):
        return [pltpu.make_async_copy(pool.at[tbl_ref[bi * n_pages + step * pg + k]],
                                      xbuf.at[slot, pl.ds(k * rpp, rpp)], sem.at[slot]) for k in range(pg)]

    @pl.when(s == 0)
    def _():
        for c in copies(0, 0):
            c.start()

    @pl.when(s + 1 < n_steps)
    def _():
        for c in copies(s + 1, lax.rem(s + 1, 2)):
            c.start()

    slot = lax.rem(s, 2)
    for c in copies(s, slot):
        c.wait()
    w_hi = whi_ref[...]
    w_lo = wlo_ref[...]
    pw = _dot_split(pos_ref[...], w_hi, w_lo, _NN)
    col = lax.broadcasted_iota(jnp.int32, (1, 2 * HEAD_DIM), 1)
    pos_term = jnp.where(col < HEAD_DIM, pw[0:1, :], pw[1:2, :])
    for g in range(g_n):
        x_cat = jnp.concatenate(
            [xbuf[slot, pl.ds(l * g_n + g, n_rows, stride=CMP_STRIDE * g_n), :] for l in range(CMP_STRIDE)], axis=1)
        z = _dot_split(x_cat, w_hi, w_lo, _NN) + pos_term
        za_ref[0, :, g * HEAD_DIM:(g + 1) * HEAD_DIM] = z[:, :HEAD_DIM]
        zb_ref[0, :, g * HEAD_DIM:(g + 1) * HEAD_DIM] = z[:, HEAD_DIM:]


def _compress_paged(pool, table, pos_emb, w1, w2):
    b, n_pages = table.shape
    rpp = pool.shape[1]
    page = rpp // NSA_KV_GROUPS
    gw = NSA_KV_GROUPS * HEAD_DIM
    pg = _pages_per_step(n_pages, rpp * HEAD_DIM * 4)
    n_steps = n_pages // pg
    n_rows = pg * page // CMP_STRIDE
    n_chunks = n_pages * page // CMP_STRIDE
    half = CMP_STRIDE * HEAD_DIM
    w_cat = jnp.concatenate([w1[:half], w1[half:]], axis=1)
    w_hi = w_cat.astype(BF16)
    w_lo = (w_cat - w_hi.astype(F32)).astype(BF16)
    pos_rows = jnp.pad(pos_emb.reshape(2, half), ((0, SUBLANES - 2), (0, 0)))
    grid_spec = pltpu.PrefetchScalarGridSpec(
        num_scalar_prefetch=1,
        grid=(b, n_steps),
        in_specs=[pl.BlockSpec((SUBLANES, half), lambda bi, i, tbl: (0, 0)),
                  pl.BlockSpec((half, 2 * HEAD_DIM), lambda bi, i, tbl: (0, 0)),
                  pl.BlockSpec((half, 2 * HEAD_DIM), lambda bi, i, tbl: (0, 0)),
                  pl.BlockSpec(memory_space=pl.ANY)],
        out_specs=[pl.BlockSpec((1, n_rows, gw), lambda bi, i, tbl: (bi, i, 0))] * 2,
        scratch_shapes=[pltpu.VMEM((2, pg * rpp, HEAD_DIM), F32), pltpu.SemaphoreType.DMA((2,))],
    )
    za, zb = pl.pallas_call(
        functools.partial(_cmp_paged_body, n_pages, pg, page),
        grid_spec=grid_spec,
        out_shape=[jax.ShapeDtypeStruct((b, n_chunks, gw), F32)] * 2,
        compiler_params=_cparams("arbitrary", "arbitrary"),
        name="cmp_proj_paged",
    )(table.reshape(-1), pos_rows, w_hi, w_lo, pool)
    return _cmp_mlp(za, zb, w2)


def _cmp_attn_body(tq, nc, ns, n_pick, q_start, q_ref, kc_ref, vc_ref, cover_ref, gate_ref, o_ref, sel_ref):
    unit = pl.program_id(1)
    qi = pl.program_id(2)
    cdt = q_ref.dtype
    prec = _dot_precision(cdt)
    kc = kc_ref[0].astype(cdt)
    vc = vc_ref[0].astype(cdt)
    ncp = kc.shape[0]
    q_pos = q_start + qi * tq + lax.broadcasted_iota(jnp.int32, (tq, ncp), 0)
    n_idx = lax.broadcasted_iota(jnp.int32, (tq, ncp), 1)
    visible = (n_idx * CMP_STRIDE + (CMP_BLOCK - 1) <= q_pos) & (n_idx < nc)
    gate = gate_ref[0]
    glane = lax.broadcasted_iota(jnp.int32, gate.shape, 1)
    p_grp = jnp.zeros((tq, ncp), F32)
    for r in range(HEADS_PER_GROUP):
        q = q_ref[0, :, r * HEAD_DIM:(r + 1) * HEAD_DIM]
        s = lax.dot_general(q, kc, _NT, preferred_element_type=F32, precision=prec)
        s = jnp.where(visible, s, NEG_INF)
        m = jnp.max(s, axis=-1, keepdims=True)
        e = jnp.where(visible, jnp.exp(s - m), 0.0)
        l = jnp.sum(e, axis=-1, keepdims=True)
        p = e / jnp.where(l > 0.0, l, 1.0)
        p_grp = p_grp + p
        want = FOX_HEADS + (unit * HEADS_PER_GROUP + r) * N_NSA_BRANCHES
        gcol = jnp.sum(jnp.where(glane == want, gate, 0.0), axis=-1, keepdims=True)
        o = jnp.dot(p.astype(cdt), vc, preferred_element_type=F32, precision=prec)
        o_ref[0, :, r * HEAD_DIM:(r + 1) * HEAD_DIM] = (o * gcol).astype(o_ref.dtype)

    imp = jnp.dot(p_grp, cover_ref[...], preferred_element_type=F32, precision=HIGHEST)
    nsp = imp.shape[1]
    j = lax.broadcasted_iota(jnp.int32, (tq, nsp), 1)
    cur = (q_start + qi * tq + lax.broadcasted_iota(jnp.int32, (tq, nsp), 0)) // SEL_BLOCK
    dist = cur - j
    forced = (j == 0) | ((dist >= 0) & (dist < N_LOCAL_SEL))
    score = jnp.where(forced, FORCE_SCORE, imp)
    score = jnp.where(dist >= 0, score, NEG_INF)
    score = jnp.where(j < ns, score, PAD_SCORE)
    sel = jnp.zeros((tq, nsp), F32)
    for _ in range(n_pick):
        mx = jnp.max(score, axis=-1, keepdims=True)
        first = jnp.min(jnp.where(score == mx, j, nsp), axis=-1, keepdims=True)
        hit = j == first
        sel = jnp.where(hit, 1.0, sel)
        score = jnp.where(hit, TAKEN_SCORE, score)
    sel_ref[0, 0] = sel


def _cmp_attention(nq, kc, vc, gate, t, q_start, n_keys, out_dtype):
    b = nq.shape[0]
    ncp = kc.shape[1]
    nc = (n_keys - CMP_BLOCK) // CMP_STRIDE + 1
    ns = -(-n_keys // SEL_BLOCK)
    nsp = -(-ns // LANES) * LANES
    n_pick = min(N_SELECT, ns)
    c0 = np.arange(ncp)[:, None] * CMP_STRIDE
    s0 = np.arange(nsp)[None, :] * SEL_BLOCK
    cover = np.clip(np.minimum(c0 + CMP_BLOCK, s0 + SEL_BLOCK) - np.maximum(c0, s0), 0, None).astype(np.float32) / CMP_BLOCK
    cover = cover * (np.arange(ncp)[:, None] < nc) * (np.arange(nsp)[None, :] < ns)
    tq = _row_tile(t, 256)
    qw = HEADS_PER_GROUP * HEAD_DIM
    return pl.pallas_call(
        functools.partial(_cmp_attn_body, tq, nc, ns, n_pick, q_start),
        grid=(b, NSA_KV_GROUPS, t // tq),
        in_specs=[
            pl.BlockSpec((1, tq, qw), lambda bi, u, i: (bi, i, u)),
            pl.BlockSpec((1, ncp, HEAD_DIM), lambda bi, u, i: (bi, 0, u)),
            pl.BlockSpec((1, ncp, HEAD_DIM), lambda bi, u, i: (bi, 0, u)),
            pl.BlockSpec((ncp, nsp), lambda bi, u, i: (0, 0)),
            pl.BlockSpec((1, tq, LANES), lambda bi, u, i: (bi, i, 0)),
        ],
        out_specs=[
            pl.BlockSpec((1, tq, qw), lambda bi, u, i: (bi, i, u)),
            pl.BlockSpec((1, 1, tq, nsp), lambda bi, u, i: (bi, u, i, 0)),
        ],
        out_shape=[
            jax.ShapeDtypeStruct((b, t, NSA_KV_GROUPS * qw), out_dtype),
            jax.ShapeDtypeStruct((b, NSA_KV_GROUPS, t, nsp), F32),
        ],
        compiler_params=_cparams("parallel", "parallel", "parallel"),
        name="attn_cmp_select",
    )(nq, kc, vc, jnp.asarray(cover), gate)


def _merge_body(a_ref, b1_ref, b2_ref, b3_ref, ua_ref, ub_ref, ga_ref, gb_ref, o_ref):
    a = a_ref[...]
    prec = _dot_precision(a.dtype)
    bsum = (b1_ref[...].astype(F32) + b2_ref[...].astype(F32) + b3_ref[...].astype(F32)).astype(a.dtype)
    ya = jnp.dot(a, ua_ref[...], preferred_element_type=F32, precision=prec)
    yb = jnp.dot(bsum, ub_ref[...], preferred_element_type=F32, precision=prec)
    o_ref[...] = (ga_ref[...] * ya + gb_ref[...] * yb).astype(o_ref.dtype)


def _merge(o_fox, o_cmp, o_slc, o_win, u_fox, u_nsa, mg, d_model):
    n, w = o_fox.shape
    tm = _row_tile(n, 512)
    tn = 512
    nj = d_model // tn
    row = pl.BlockSpec((tm, w), lambda i, j: (i, 0))
    return pl.pallas_call(
        _merge_body,
        grid=(n // tm, nj),
        in_specs=[row, row, row, row,
                  pl.BlockSpec((w, tn), lambda i, j: (0, j)),
                  pl.BlockSpec((w, tn), lambda i, j: (0, j)),
                  pl.BlockSpec((tm, tn), lambda i, j: (i, j)),
                  pl.BlockSpec((tm, tn), lambda i, j: (i, j + nj))],
        out_specs=pl.BlockSpec((tm, tn), lambda i, j: (i, j)),
        out_shape=jax.ShapeDtypeStruct((n, d_model), o_fox.dtype),
        compiler_params=_cparams("parallel", "parallel"),
        name="mixer_merge",
    )(o_fox, o_cmp, o_slc, o_win, u_fox, u_nsa, mg, mg)


def _router_body(h_ref, g_ref, w_ref, b_ref, u_ref, route_ref):
    x = h_ref[...]
    xn = x * lax.rsqrt(jnp.mean(x * x, axis=-1, keepdims=True) + NORM_EPS) * g_ref[...]
    u_ref[...] = xn.astype(u_ref.dtype)
    z = jnp.dot(xn, w_ref[...], preferred_element_type=F32, precision=HIGHEST) + b_ref[...]
    c = lax.broadcasted_iota(jnp.int32, z.shape, 1)
    big = z.shape[1]
    is_g = c < N_GROUPS
    gmax = jnp.max(jnp.where(is_g, z, -jnp.inf), axis=-1, keepdims=True)
    gsel = jnp.min(jnp.where(is_g & (z == gmax), c, big), axis=-1, keepdims=True)
    gsum = jnp.sum(jnp.where(is_g, jnp.exp(z - gmax), 0.0), axis=-1, keepdims=True)
    lo = N_GROUPS + gsel * EXPERTS_PER_GROUP
    in_grp = (c >= lo) & (c < lo + EXPERTS_PER_GROUP)
    z1 = jnp.max(jnp.where(in_grp, z, -jnp.inf), axis=-1, keepdims=True)
    c1 = jnp.min(jnp.where(in_grp & (z == z1), c, big), axis=-1, keepdims=True)
    rest = in_grp & (c != c1)
    z2 = jnp.max(jnp.where(rest, z, -jnp.inf), axis=-1, keepdims=True)
    c2 = jnp.min(jnp.where(rest & (z == z2), c, big), axis=-1, keepdims=True)
    e2 = jnp.exp(z2 - z1)
    p1 = 1.0 / (1.0 + e2)
    p2 = e2 / (1.0 + e2)
    gp = 1.0 / gsum
    out = jnp.where(c == 0, (c1 - N_GROUPS).astype(F32), 0.0)
    out = jnp.where(c == 1, (c2 - N_GROUPS).astype(F32), out)
    out = jnp.where(c == 2, gp * p1, out)
    out = jnp.where(c == 3, gp * p2, out)
    route_ref[...] = out


def _router(h, gain, w_cat, b_cat):
    n, d = h.shape
    tm = _row_tile(n, 512)
    return pl.pallas_call(
        _router_body,
        grid=(n // tm,),
        in_specs=[pl.BlockSpec((tm, d), lambda i: (i, 0)), pl.BlockSpec((1, d), lambda i: (0, 0)),
                  pl.BlockSpec((d, LANES), lambda i: (0, 0)), pl.BlockSpec((1, LANES), lambda i: (0, 0))],
        out_specs=[pl.BlockSpec((tm, d), lambda i: (i, 0)), pl.BlockSpec((tm, LANES), lambda i: (i, 0))],
        out_shape=[jax.ShapeDtypeStruct((n, d), F32), jax.ShapeDtypeStruct((n, LANES), F32)],
        compiler_params=_cparams("parallel"),
        name="moe_router",
    )(h, gain.reshape(1, d), w_cat, b_cat)


def _gather_rows(src_hbm, dst_ref, sem, idx_ref, base, count, start):
    def body(r, c):
        cp = pltpu.make_async_copy(src_hbm.at[pl.ds(idx_ref[base + r], 1)], dst_ref.at[pl.ds(r, 1)], sem)
        if start:
            cp.start()
        else:
            cp.wait()
        return c

    lax.fori_loop(0, count, body, 0, unroll=8)


def _expert_body(tm, te_ref, nv_ref, tok_ref, cw_ref, wg_ref, wu_ref, wd_ref, x_hbm, o_ref, xbuf, sem):
    i = pl.program_id(0)
    nv = nv_ref[0]

    def rows_of(tile, start):
        slot = lax.rem(tile, 2)
        _gather_rows(x_hbm, xbuf.at[slot], sem.at[slot], tok_ref, tile * tm, tm, start)

    @pl.when(i == 0)
    def _():
        rows_of(0, True)

    @pl.when(i + 1 < nv)
    def _():
        rows_of(i + 1, True)

    @pl.when(i < nv)
    def _():
        rows_of(i, False)
        x = xbuf[lax.rem(i, 2)].astype(BF16)
        hg = jnp.dot(x, wg_ref[0], preferred_element_type=F32)
        hu = jnp.dot(x, wu_ref[0], preferred_element_type=F32)
        hid = (hg * jax.nn.sigmoid(hg)) * hu * cw_ref[...]
        o_ref[...] = jnp.dot(hid.astype(BF16), wd_ref[0], preferred_element_type=F32)

    @pl.when(i >= nv)
    def _():
        o_ref[...] = jnp.zeros(o_ref.shape, o_ref.dtype)


def _experts(x, row_token, w_sorted, tile_expert, n_valid, wg, wu, wd, tm):
    rows = row_token.shape[0]
    d = x.shape[1]
    n_tiles = rows // tm
    f = wg.shape[2]

    def row_map(i, te, nv, tok):
        return (jnp.minimum(i, nv[0] - 1), 0)

    def w_map(i, te, nv, tok):
        return (te[i], 0, 0)

    grid_spec = pltpu.PrefetchScalarGridSpec(
        num_scalar_prefetch=3,
        grid=(n_tiles,),
        in_specs=[pl.BlockSpec((tm, 1), row_map),
                  pl.BlockSpec((1, d, f), w_map), pl.BlockSpec((1, d, f), w_map), pl.BlockSpec((1, f, d), w_map),
                  pl.BlockSpec(memory_space=pl.ANY)],
        out_specs=pl.BlockSpec((tm, d), lambda i, te, nv, tok: (i, 0)),
        scratch_shapes=[pltpu.VMEM((2, tm, d), F32), pltpu.SemaphoreType.DMA((2,))],
    )
    return pl.pallas_call(
        functools.partial(_expert_body, tm),
        grid_spec=grid_spec,
        out_shape=jax.ShapeDtypeStruct((rows, d), F32),
        compiler_params=_cparams("arbitrary"),
        name="moe_experts",
    )(tile_expert, n_valid, row_token, w_sorted, wg, wu, wd, x)


def _combine_body(tm, n, slot_ref, h_ref, src_hbm, o_ref, buf, sem):
    i = pl.program_id(0)
    n_tiles = pl.num_programs(0)

    def rows_of(tile, start):
        slot = lax.rem(tile, 2)
        for k in range(2):
            _gather_rows(src_hbm, buf.at[slot, k], sem.at[slot], slot_ref, k * n + tile * tm, tm, start)

    @pl.when(i == 0)
    def _():
        rows_of(0, True)

    @pl.when(i + 1 < n_tiles)
    def _():
        rows_of(i + 1, True)

    rows_of(i, False)
    slot = lax.rem(i, 2)
    o_ref[...] = h_ref[...] + buf[slot, 0] + buf[slot, 1]


def _moe_combine(h, out_sorted, slot_kn):
    n, d = h.shape
    tm = _row_tile(n, 256)
    grid_spec = pltpu.PrefetchScalarGridSpec(
        num_scalar_prefetch=1,
        grid=(n // tm,),
        in_specs=[pl.BlockSpec((tm, d), lambda i, sl: (i, 0)), pl.BlockSpec(memory_space=pl.ANY)],
        out_specs=pl.BlockSpec((tm, d), lambda i, sl: (i, 0)),
        scratch_shapes=[pltpu.VMEM((2, 2, tm, d), F32), pltpu.SemaphoreType.DMA((2,))],
    )
    return pl.pallas_call(
        functools.partial(_combine_body, tm, n),
        grid_spec=grid_spec,
        out_shape=jax.ShapeDtypeStruct((n, d), F32),
        compiler_params=_cparams("arbitrary"),
        name="moe_combine",
    )(slot_kn, h, out_sorted)


def _moe(h, gain, w_router, b_router, wg, wu, wd):
    n, d = h.shape
    ut, route = _router(h, gain, w_router, b_router)
    expert = route[:, :2].astype(jnp.int32).reshape(-1)
    weight = route[:, 2:4].reshape(-1)
    n_asg = 2 * n
    tm = 256 if n_asg >= 8192 else 16
    n_tiles = n_asg // tm + N_EXPERTS
    onehot = (expert[:, None] == jnp.arange(N_EXPERTS, dtype=jnp.int32)[None, :]).astype(jnp.int32)
    running = jnp.cumsum(onehot, axis=0)
    counts = running[-1]
    rank = jnp.sum(onehot * running, axis=1) - 1
    tiles_per = (counts + tm - 1) // tm
    tile_end = jnp.cumsum(tiles_per)
    start_row = (tile_end - tiles_per) * tm
    slot = (jnp.sum(onehot * start_row[None, :], axis=1) + rank).astype(jnp.int32)
    row_token = jnp.zeros((n_tiles * tm,), jnp.int32).at[slot].set(jnp.arange(n_asg, dtype=jnp.int32) // 2)
    row_weight = jnp.zeros((n_tiles * tm,), F32).at[slot].set(weight)
    n_valid = tile_end[-1].astype(jnp.int32)
    tile_ids = jnp.minimum(jnp.arange(n_tiles, dtype=jnp.int32), n_valid - 1)
    tile_expert = jnp.sum((tile_end[None, :] <= tile_ids[:, None]).astype(jnp.int32), axis=1)
    out_sorted = _experts(ut, row_token, row_weight[:, None], tile_expert, n_valid.reshape(1), wg, wu, wd, tm)
    return _moe_combine(h, out_sorted, jnp.transpose(slot.reshape(n, 2)).reshape(-1))


def _rope_tables(positions):
    half = HEAD_DIM // 2
    inv = ROPE_THETA ** (-np.arange(half, dtype=np.float64) / half)
    ang = np.asarray(positions, np.float64)[:, None] * inv[None, :]
    cos = np.concatenate([np.cos(ang), np.cos(ang)], axis=-1)
    sin = np.concatenate([-np.sin(ang), np.sin(ang)], axis=-1)
    return jnp.asarray(cos, F32), jnp.asarray(sin, F32)


def _project(u, wts, p, b, t, q_start):
    n = u.shape[0]
    tm = _row_tile(n, 1024)
    tn = 256
    scale = HEAD_DIM ** -0.5
    gain_spec = pl.BlockSpec((1, HEAD_DIM), lambda i, j: (0, 0))

    def run(w, epilogue, extras, specs, dtype, name, tn_=tn):
        shape = jax.ShapeDtypeStruct((n, w.shape[1]), dtype)
        spec = pl.BlockSpec((tm, tn_), lambda i, j: (i, j))
        return _matmul(u, w, epilogue, extras, specs, shape, spec, tm, tn_, name)

    cos, sin = _rope_tables(q_start + (np.arange(max(tm, t)) % t))
    n_tab = cos.shape[0] // tm
    tab_spec = pl.BlockSpec((tm, HEAD_DIM), lambda i, j: (i % n_tab, 0))

    fq = run(wts["fq"], functools.partial(_ep_head_norm, scale), [p["fox_q_norm"]], [gain_spec], u.dtype, "proj_fq")
    fk = run(wts["fk"], functools.partial(_ep_head_norm, 1.0), [p["fox_k_norm"]], [gain_spec], F32, "proj_fk")
    fv = run(wts["fv"], _ep_identity, [], [], F32, "proj_fv")
    nq = run(wts["nq"], functools.partial(_ep_head_norm_rope, scale), [p["nsa_q_norm"], cos, sin],
             [gain_spec, tab_spec, tab_spec], u.dtype, "proj_nq")
    nk = run(wts["nk"], functools.partial(_ep_head_norm_rope, 1.0), [p["nsa_k_norm"], cos, sin],
             [gain_spec, tab_spec, tab_spec], F32, "proj_nk")
    nv = run(wts["nv"], _ep_identity, [], [], F32, "proj_nv")
    mg = run(wts["mg"], _ep_sigmoid, [], [], F32, "proj_mg")
    small = run(wts["small"], _ep_small, [p["small_bias"]], [pl.BlockSpec((1, LANES), lambda i, j: (0, 0))],
                F32, "proj_small", LANES)
    return fq, fk, fv, nq, nk, nv, mg, small


def _layer(x, p, wts, q_start, past):
    b, t, d = x.shape
    n = b * t
    g = NSA_KV_GROUPS
    kvw = g * HEAD_DIM
    x2d = x.reshape(n, d)
    dense = wts["f32"] if past is not None else wts
    u = _rmsnorm(x2d, p["norm_mix"], dense["fq"].dtype)
    fq, fk, fv, nq, nk, nv, mg, small = _project(u, dense, p, b, t, q_start)

    logf = small[:, :FOX_HEADS].reshape(b, t, FOX_HEADS)
    gate3 = small.reshape(b, t, LANES)
    fq3 = fq.reshape(b, t, -1)
    nq3 = nq.reshape(b, t, -1)
    fk3 = fk.reshape(b, t, -1)
    fv3 = fv.reshape(b, t, -1)
    nk3 = nk.reshape(b, t, 3 * kvw)
    nv3 = nv.reshape(b, t, 3 * kvw)
    ck, sk, wk = (nk3[:, :, i * kvw:(i + 1) * kvw] for i in range(3))
    cv, sv, wv = (nv3[:, :, i * kvw:(i + 1) * kvw] for i in range(3))

    if past is None:
        n_keys = t
        n_lp = t // LANES
        lf_pool = jnp.transpose(logf.reshape(b * n_lp, LANES, FOX_HEADS), (0, 2, 1))
        bias = _fox_key_bias(lf_pool, jnp.arange(b * n_lp, dtype=jnp.int32).reshape(b, n_lp), None)
        o_fox = _flash_attention("fox", fq3, fk3, fv3, 0, t, bias=bias)
        kc = _compress(ck, p["cmp_pos_k"], p["cmp_w1_k"], p["cmp_w2_k"])
        vc = _compress(cv, p["cmp_pos_v"], p["cmp_w1_v"], p["cmp_w2_v"])
        o_cmp, sel = _cmp_attention(nq3, kc, vc, gate3, t, q_start, n_keys, BF16)
        o_slc = _flash_attention("slc", nq3, nk3, nv3, g, t, sel=sel, gate=gate3, gate_col=1)
        o_win = _flash_attention("win", nq3, nk3, nv3, 2 * g, t, gate=gate3, gate_col=2)
        keep = min(WINDOW, t)
        win_k, win_v = wk[:, t - keep:], wv[:, t - keep:]
    else:
        table = past["table"]
        n_pages = table.shape[1]
        page = past["page"]
        past_len = n_pages * page
        n_keys = past_len + t
        new_page = jnp.pad(jnp.transpose(logf, (0, 2, 1)), ((0, 0), (0, 0), (0, page - t)))
        bias = _fox_key_bias(past["fox_logf"], table, new_page)[:, :, :n_keys]
        causal = jnp.where(jnp.arange(t)[None, :] <= jnp.arange(t)[:, None], 0.0, NEG_INF).astype(F32)
        bias_new = bias[:, :, None, past_len:] + causal[None, None]
        qf = jnp.transpose(fq3.reshape(b, t, FOX_HEADS, HEAD_DIM), (0, 2, 1, 3))
        o = _paged_attention(qf, past["fox_k"], past["fox_v"], table, bias[:, :, None, :past_len],
                             fk3, fv3, bias_new)
        o_fox = jnp.transpose(o, (0, 2, 1, 3)).reshape(b, t, -1)
        nc = (n_keys - CMP_BLOCK) // CMP_STRIDE + 1
        assert (nc + 1) * CMP_STRIDE <= past_len and page % CMP_STRIDE == 0
        kc = _compress_paged(past["cmp_k"], table, p["cmp_pos_k"], p["cmp_w1_k"], p["cmp_w2_k"])
        vc = _compress_paged(past["cmp_v"], table, p["cmp_pos_v"], p["cmp_w1_v"], p["cmp_w2_v"])
        o_cmp, sel = _cmp_attention(nq3, kc, vc, gate3, t, q_start, n_keys, F32)
        rows = HEADS_PER_GROUP * t
        qn = jnp.transpose(nq3.reshape(b, t, g, HEADS_PER_GROUP, HEAD_DIM), (0, 2, 3, 1, 4)).reshape(b, g, rows, HEAD_DIM)
        gates = small[:, FOX_HEADS:FOX_HEADS + NSA_HEADS * N_NSA_BRANCHES].reshape(b, t, g, HEADS_PER_GROUP, N_NSA_BRANCHES)
        gates = jnp.transpose(gates, (4, 0, 2, 3, 1)).reshape(N_NSA_BRANCHES, b, g, rows, 1)
        ns = -(-n_keys // SEL_BLOCK)
        key_sel = jnp.repeat(sel[..., :ns], SEL_BLOCK, axis=-1)[..., :n_keys]
        sel_bias = jnp.where(key_sel > 0.5, 0.0, NEG_INF).astype(F32)
        sel_new = jnp.tile(sel_bias[..., past_len:] + causal[None, None], (1, 1, HEADS_PER_GROUP, 1))
        o = _paged_attention(qn, past["slc_k"], past["slc_v"], table, sel_bias[..., :past_len],
                             sk, sv, sel_new, gate=gates[1])
        o_slc = _rows_to_tokens(o, b, t)
        win_len = past["win_k"].shape[1]
        assert win_len % page == 0
        k_start = q_start - win_len
        q_abs = q_start + jnp.arange(t)[:, None]
        k_abs = k_start + jnp.arange(win_len)[None, :]
        band = jnp.where((k_abs >= 0) & (k_abs > q_abs - WINDOW), 0.0, NEG_INF).astype(F32)
        band = jnp.broadcast_to(band[None, None], (b, g, t, win_len))
        band_new = jnp.where(jnp.arange(t)[None, :] > jnp.arange(t)[:, None] - WINDOW, causal, NEG_INF)
        band_new = jnp.broadcast_to(jnp.tile(band_new, (HEADS_PER_GROUP, 1))[None, None], (b, g, rows, t))
        o = _paged_attention(qn, past["win_k_pool"], past["win_v_pool"], past["win_table"], band, wk, wv,
                             band_new, gate=gates[2])
        o_win = _rows_to_tokens(o, b, t)
        wk_all = jnp.concatenate([past["win_k"], wk], axis=1)
        wv_all = jnp.concatenate([past["win_v"], wv], axis=1)
        keep = min(WINDOW, wk_all.shape[1])
        win_k, win_v = wk_all[:, -keep:], wv_all[:, -keep:]

    m = _merge(o_fox.reshape(n, -1), o_cmp.reshape(n, -1), o_slc.reshape(n, -1), o_win.reshape(n, -1),
               dense["up_fox"], dense["up_nsa"], mg, d)
    tm = _row_tile(n, 512)
    h = _matmul(m, dense["out"], _ep_residual, [x2d], [pl.BlockSpec((tm, 512), lambda i, j: (i, j))],
                jax.ShapeDtypeStruct((n, d), F32), pl.BlockSpec((tm, 512), lambda i, j: (i, j)), tm, 512, "out_proj")
    y = _moe(h, p["norm_ffn"], wts["router"], p["router_bias"], wts["gate_e"], wts["up_e"], wts["down_e"])

    def heads(a, hh):
        return a.reshape(b, -1, hh, HEAD_DIM)

    state = (heads(fk3, FOX_HEADS), heads(fv3, FOX_HEADS), logf, heads(ck, g), heads(cv, g), heads(sk, g),
             heads(sv, g), heads(win_k, g), heads(win_v, g))
    return y.reshape(b, t, d), state


def _rows_to_tokens(o, b, t):
    o = o.reshape(b, NSA_KV_GROUPS, HEADS_PER_GROUP, t, HEAD_DIM)
    return jnp.transpose(o, (0, 3, 1, 2, 4)).reshape(b, t, -1)


def _prepare(l, norm_mix, w_in, fox_f_bias, fox_q_norm, fox_k_norm, nsa_q_norm, nsa_k_norm, cmp_pos_k, cmp_w1_k,
             cmp_w2_k, cmp_pos_v, cmp_w1_v, cmp_w2_v, nsa_gate_bias, w_up_fox, w_up_nsa, w_out, norm_ffn,
             w_router_group, b_router_group, w_router_expert, b_router_expert, w_gate_e, w_up_e, w_down_e):
    d = w_in.shape[1]
    fw = FOX_HEADS * HEAD_DIM
    nw = NSA_HEADS * HEAD_DIM
    kvw = NSA_KV_GROUPS * HEAD_DIM
    sizes = (fw, fw, fw, FOX_HEADS, nw, kvw, kvw, kvw, kvw, kvw, kvw, NSA_HEADS * N_NSA_BRANCHES, 2 * d)
    offs = np.concatenate([[0], np.cumsum(sizes)])
    w = w_in[l]

    def seg(i):
        return w[:, offs[i]:offs[i + 1]]

    n_small = FOX_HEADS + NSA_HEADS * N_NSA_BRANCHES
    small_w = jnp.pad(jnp.concatenate([seg(3), seg(11)], axis=1), ((0, 0), (0, LANES - n_small)))
    small_b = jnp.pad(jnp.concatenate([fox_f_bias[l], nsa_gate_bias[l]]), (0, LANES - n_small)).reshape(1, LANES)
    n_route = N_GROUPS + N_EXPERTS
    router_w = jnp.pad(jnp.concatenate([w_router_group[l], w_router_expert[l]], axis=1), ((0, 0), (0, LANES - n_route)))
    router_b = jnp.pad(jnp.concatenate([b_router_group[l], b_router_expert[l]]), (0, LANES - n_route)).reshape(1, LANES)
    dense = {
        "fq": seg(0), "fk": seg(1), "fv": seg(2), "nq": seg(4),
        "nk": jnp.concatenate([seg(5), seg(7), seg(9)], axis=1),
        "nv": jnp.concatenate([seg(6), seg(8), seg(10)], axis=1),
        "mg": seg(12), "small": small_w,
        "up_fox": w_up_fox[l], "up_nsa": w_up_nsa[l], "out": w_out[l],
    }
    wts = {
        **{k: v.astype(BF16) for k, v in dense.items()},
        "f32": dense,
        "router": router_w,
        "gate_e": w_gate_e[l].astype(BF16), "up_e": w_up_e[l].astype(BF16), "down_e": w_down_e[l].astype(BF16),
    }
    p = {
        "norm_mix": norm_mix[l], "fox_q_norm": fox_q_norm[l].reshape(1, -1), "fox_k_norm": fox_k_norm[l].reshape(1, -1),
        "nsa_q_norm": nsa_q_norm[l].reshape(1, -1), "nsa_k_norm": nsa_k_norm[l].reshape(1, -1),
        "small_bias": small_b, "cmp_pos_k": cmp_pos_k[l], "cmp_w1_k": cmp_w1_k[l], "cmp_w2_k": cmp_w2_k[l],
        "cmp_pos_v": cmp_pos_v[l], "cmp_w1_v": cmp_w1_v[l], "cmp_w2_v": cmp_w2_v[l],
        "norm_ffn": norm_ffn[l], "router_bias": router_b,
    }
    return wts, p


def kernel(x_prompt, x_sample, cache_fox_k, cache_fox_v, cache_fox_logf, cache_cmp_k, cache_cmp_v, cache_slc_k, cache_slc_v, cache_win_k, cache_win_v, page_table, norm_mix, w_in, fox_f_bias, fox_q_norm, fox_k_norm, nsa_q_norm, nsa_k_norm, cmp_pos_k, cmp_w1_k, cmp_w2_k, cmp_pos_v, cmp_w1_v, cmp_w2_v, nsa_gate_bias, w_up_fox, w_up_nsa, w_out, norm_ffn, w_router_group, b_router_group, w_router_expert, b_router_expert, w_gate_e, w_up_e, w_down_e):
    depth = w_in.shape[0]
    n_phys, page = cache_fox_k.shape[1], cache_fox_k.shape[2]
    past_len = page_table.shape[1] * page
    h_p, h_s = x_prompt, x_sample
    p_states, s_states = [], []
    for l in range(depth):
        wts, p = _prepare(l, norm_mix, w_in, fox_f_bias, fox_q_norm, fox_k_norm, nsa_q_norm, nsa_k_norm, cmp_pos_k,
                          cmp_w1_k, cmp_w2_k, cmp_pos_v, cmp_w1_v, cmp_w2_v, nsa_gate_bias, w_up_fox, w_up_nsa,
                          w_out, norm_ffn, w_router_group, b_router_group, w_router_expert, b_router_expert,
                          w_gate_e, w_up_e, w_down_e)
        h_p, st_p = _layer(h_p, p, wts, 0, None)
        p_states.append(st_p)
        win_len = cache_win_k.shape[2]
        n_win = win_len // page
        dec_b = page_table.shape[0]

        def pool(c):
            return c.reshape(-1, page * c.shape[3], HEAD_DIM)

        past = {
            "table": page_table + l * n_phys, "page": page,
            "fox_k": pool(cache_fox_k), "fox_v": pool(cache_fox_v),
            "fox_logf": jnp.transpose(cache_fox_logf.reshape(-1, page, cache_fox_logf.shape[3]), (0, 2, 1)),
            "cmp_k": pool(cache_cmp_k), "cmp_v": pool(cache_cmp_v),
            "slc_k": pool(cache_slc_k), "slc_v": pool(cache_slc_v),
            "win_k_pool": pool(cache_win_k), "win_v_pool": pool(cache_win_v),
            "win_table": (l * dec_b + jnp.arange(dec_b, dtype=jnp.int32))[:, None] * n_win
                         + jnp.arange(n_win, dtype=jnp.int32)[None, :],
            "win_k": cache_win_k[l].reshape(dec_b, win_len, -1), "win_v": cache_win_v[l].reshape(dec_b, win_len, -1),
        }
        h_s, st_s = _layer(h_s, p, wts, past_len, past)
        s_states.append(st_s)
    p_out = [jnp.stack([st[i] for st in p_states], axis=0) for i in range(9)]
    s_out = [jnp.stack([st[i] for st in s_states], axis=0) for i in range(9)]
    return (h_p, h_s, *p_out, *s_out)
```

```python
import functools

import numpy as np
import jax
import jax.numpy as jnp
from jax import lax
from jax.experimental import pallas as pl
from jax.experimental.pallas import tpu as pltpu

F32 = jnp.float32
BF16 = jnp.bfloat16

HEAD_DIM = 128
FOX_HEADS = 8
NSA_HEADS = 8
NSA_KV_GROUPS = 2
HEADS_PER_GROUP = NSA_HEADS // NSA_KV_GROUPS
N_NSA_BRANCHES = 3
CMP_BLOCK = 32
CMP_STRIDE = 16
SEL_BLOCK = 64
N_SELECT = 16
N_LOCAL_SEL = 2
WINDOW = 512
N_GROUPS = 4
EXPERTS_PER_GROUP = 8
N_EXPERTS = N_GROUPS * EXPERTS_PER_GROUP
ROPE_THETA = 10000.0
NORM_EPS = 1e-6
NEG_INF = -1e30
FORCE_SCORE = 1e9
PAD_SCORE = -2e38
TAKEN_SCORE = -3e38

LANES = 128
SUBLANES = 8
VMEM_LIMIT_BYTES = 48 * 1024 * 1024

HIGHEST = lax.Precision.HIGHEST


def _cparams(*sem):
    return pltpu.CompilerParams(dimension_semantics=tuple(sem), vmem_limit_bytes=VMEM_LIMIT_BYTES)


def _row_tile(n, pref):
    t = min(n, pref)
    assert n % t == 0, (n, t)
    return t


def _rmsnorm_body(x_ref, g_ref, o_ref):
    x = x_ref[...]
    y = x * lax.rsqrt(jnp.mean(x * x, axis=-1, keepdims=True) + NORM_EPS)
    o_ref[...] = (y * g_ref[...]).astype(o_ref.dtype)


def _rmsnorm(x2d, gain, out_dtype):
    n, d = x2d.shape
    tm = _row_tile(n, 512)
    return pl.pallas_call(
        _rmsnorm_body,
        grid=(n // tm,),
        in_specs=[pl.BlockSpec((tm, d), lambda i: (i, 0)), pl.BlockSpec((1, d), lambda i: (0, 0))],
        out_specs=pl.BlockSpec((tm, d), lambda i: (i, 0)),
        out_shape=jax.ShapeDtypeStruct((n, d), out_dtype),
        compiler_params=_cparams("parallel"),
        name="rmsnorm",
    )(x2d, gain.reshape(1, d))


def _dot_precision(dtype):
    return HIGHEST if dtype == F32 else None


def _mm_body(epilogue, n_extra, a_ref, w_ref, *rest):
    acc = jnp.dot(a_ref[...], w_ref[...], preferred_element_type=F32, precision=_dot_precision(a_ref.dtype))
    epilogue(acc, rest[:n_extra], rest[n_extra:])


def _matmul(a, w, epilogue, extras, extra_specs, out_shapes, out_specs, tm, tn, name):
    n, k = a.shape
    c = w.shape[1]
    assert n % tm == 0 and c % tn == 0, (n, tm, c, tn)
    return pl.pallas_call(
        functools.partial(_mm_body, epilogue, len(extras)),
        grid=(n // tm, c // tn),
        in_specs=[pl.BlockSpec((tm, k), lambda i, j: (i, 0)), pl.BlockSpec((k, tn), lambda i, j: (0, j))]
        + list(extra_specs),
        out_specs=out_specs,
        out_shape=out_shapes,
        compiler_params=_cparams("parallel", "parallel"),
        name=name,
    )(a, w, *extras)


def _head_norm(x, gain):
    return x * lax.rsqrt(jnp.mean(x * x, axis=-1, keepdims=True) + NORM_EPS) * gain


def _rope_rot(x, cos, sin_signed):
    return x * cos + pltpu.roll(x, HEAD_DIM // 2, 1) * sin_signed


def _ep_head_norm(scale, acc, extras, outs):
    (gain_ref,) = extras
    (o_ref,) = outs
    for h in range(acc.shape[1] // HEAD_DIM):
        sl = slice(h * HEAD_DIM, (h + 1) * HEAD_DIM)
        o_ref[:, sl] = (_head_norm(acc[:, sl], gain_ref[...]) * scale).astype(o_ref.dtype)


def _ep_head_norm_rope(scale, acc, extras, outs):
    gain_ref, cos_ref, sin_ref = extras
    (o_ref,) = outs
    for h in range(acc.shape[1] // HEAD_DIM):
        sl = slice(h * HEAD_DIM, (h + 1) * HEAD_DIM)
        y = _rope_rot(_head_norm(acc[:, sl], gain_ref[...]), cos_ref[...], sin_ref[...])
        o_ref[:, sl] = (y * scale).astype(o_ref.dtype)


def _ep_identity(acc, extras, outs):
    (o_ref,) = outs
    o_ref[...] = acc.astype(o_ref.dtype)


def _ep_sigmoid(acc, extras, outs):
    (o_ref,) = outs
    o_ref[...] = jax.nn.sigmoid(acc).astype(o_ref.dtype)


def _ep_small(acc, extras, outs):
    (bias_ref,) = extras
    (o_ref,) = outs
    z = acc + bias_ref[...]
    col = lax.broadcasted_iota(jnp.int32, z.shape, 1)
    e = jnp.exp(-jnp.abs(z))
    log_sig = jnp.minimum(z, 0.0) - jnp.log(1.0 + e)
    sig = jnp.where(z >= 0, 1.0, e) / (1.0 + e)
    o_ref[...] = jnp.where(col < FOX_HEADS, log_sig, sig)


def _ep_residual(acc, extras, outs):
    (x_ref,) = extras
    (o_ref,) = outs
    o_ref[...] = x_ref[...] + acc


def _fox_bias_body(n_pages, has_new, *refs):
    if has_new:
        tbl_ref, new_ref, pool_ref, o_ref, buf, sem = refs
    else:
        tbl_ref, pool_ref, o_ref, buf, sem = refs
    bi = pl.program_id(0)

    def copy(p):
        return pltpu.make_async_copy(pool_ref.at[tbl_ref[bi * n_pages + p]], buf.at[p], sem)

    def start(p, c):
        copy(p).start()
        return c

    def wait(p, c):
        copy(p).wait()
        return c

    lax.fori_loop(0, n_pages, start, 0)
    lax.fori_loop(0, n_pages, wait, 0)
    if has_new:
        buf[n_pages] = new_ref[0]
    x = buf[...]
    p, h, w = x.shape
    within = x.reshape(p * h, w)
    lane = lax.broadcasted_iota(jnp.int32, within.shape, 1)
    s = 1
    while s < w:
        within = within + jnp.where(lane >= s, pltpu.roll(within, s, 1), 0.0)
        s *= 2
    within = within.reshape(p, h, w)
    total = jnp.broadcast_to(within[:, :, w - 1:w], x.shape)
    incl = total
    s = 1
    while s < p:
        incl = incl + jnp.concatenate([jnp.zeros((s, h, w), F32), incl[:p - s]], axis=0)
        s *= 2
    o_ref[0] = -(within + (incl - total))


def _fox_key_bias(pool, table, new_page):
    b, n_pages = table.shape
    h, w = pool.shape[1:]
    has_new = new_page is not None
    p_tot = n_pages + (1 if has_new else 0)
    in_specs = [pl.BlockSpec(memory_space=pl.ANY)]
    args = [pool]
    if has_new:
        in_specs.insert(0, pl.BlockSpec((1, h, w), lambda i, tbl: (i, 0, 0)))
        args.insert(0, new_page)
    grid_spec = pltpu.PrefetchScalarGridSpec(
        num_scalar_prefetch=1,
        grid=(b,),
        in_specs=in_specs,
        out_specs=pl.BlockSpec((1, p_tot, h, w), lambda i, tbl: (i, 0, 0, 0)),
        scratch_shapes=[pltpu.VMEM((p_tot, h, w), F32), pltpu.SemaphoreType.DMA(())],
    )
    out = pl.pallas_call(
        functools.partial(_fox_bias_body, n_pages, has_new),
        grid_spec=grid_spec,
        out_shape=jax.ShapeDtypeStruct((b, p_tot, h, w), F32),
        compiler_params=_cparams("arbitrary"),
        name="fox_key_bias",
    )(table.reshape(-1), *args)
    return jnp.transpose(out, (0, 2, 1, 3)).reshape(b, h, p_tot * w)


def _flash_body(mode, heads, tq, tk, n_kv, gate_col, *refs):
    if mode == "fox":
        q_ref, k_ref, v_ref, bias_ref, o_ref, kb_ref, vb_ref = refs
    elif mode == "slc":
        q_ref, k_ref, v_ref, sel_ref, expand_ref, gate_ref, o_ref, kb_ref, vb_ref, selx_ref = refs
    else:
        q_ref, k_ref, v_ref, gate_ref, o_ref, kb_ref, vb_ref = refs
    unit = pl.program_id(1)
    qi = pl.program_id(2)

    @pl.when(qi == 0)
    def _():
        kb_ref[...] = k_ref[0].astype(BF16)
        vb_ref[...] = v_ref[0].astype(BF16)

    if mode == "slc":
        sel = sel_ref[0, 0].astype(BF16)
        for j in range(n_kv):
            selx_ref[j] = jnp.dot(sel, expand_ref[:, j * tk:(j + 1) * tk], preferred_element_type=F32)

    q_pos = qi * tq + lax.broadcasted_iota(jnp.int32, (tq, tk), 0)
    col = lax.broadcasted_iota(jnp.int32, (tq, tk), 1)
    qs = [q_ref[0, :, r * HEAD_DIM:(r + 1) * HEAD_DIM] for r in range(heads)]

    def block(j, carry, diagonal):
        ks = pl.multiple_of(j * tk, tk)
        k_pos = ks + col
        ok = None
        if mode == "slc":
            ok = selx_ref[j] > 0.5
        if mode == "win":
            ok = (k_pos <= q_pos) & (k_pos > q_pos - WINDOW)
        elif diagonal:
            ok = (k_pos <= q_pos) if ok is None else ok & (k_pos <= q_pos)
        out = []
        for r in range(heads):
            m, l, acc = carry[r]
            c0 = r * HEAD_DIM if mode == "fox" else 0
            kb = kb_ref[pl.ds(ks, tk), c0:c0 + HEAD_DIM]
            vb = vb_ref[pl.ds(ks, tk), c0:c0 + HEAD_DIM]
            s = lax.dot_general(qs[r], kb, _NT, preferred_element_type=F32)
            if mode == "fox":
                s = s + bias_ref[0, r:r + 1, pl.ds(ks, tk)]
            if ok is not None:
                s = jnp.where(ok, s, NEG_INF)
            m_new = jnp.maximum(m, jnp.max(s, axis=-1, keepdims=True))
            alpha = jnp.exp(m - m_new)
            p = jnp.exp(s - m_new)
            l = alpha * l + jnp.sum(p, axis=-1, keepdims=True)
            acc = alpha * acc + jnp.dot(p.astype(BF16), vb, preferred_element_type=F32)
            out.append((m_new, l, acc))
        return tuple(out)

    init = tuple((jnp.full((tq, 1), NEG_INF, F32), jnp.zeros((tq, 1), F32), jnp.zeros((tq, HEAD_DIM), F32))
                 for _ in range(heads))
    if mode == "win":
        lo = jnp.maximum(qi - (-(-WINDOW // tk)), 0)
        state = lax.fori_loop(lo, qi + 1, functools.partial(block, diagonal=False), init)
    else:
        state = lax.fori_loop(0, qi, functools.partial(block, diagonal=False), init)
        state = block(qi, state, True)

    for r in range(heads):
        m, l, acc = state[r]
        w = 1.0 / l
        if mode != "fox":
            gate = gate_ref[0]
            lane = lax.broadcasted_iota(jnp.int32, gate.shape, 1)
            want = FOX_HEADS + (unit * heads + r) * N_NSA_BRANCHES + gate_col
            w = w * jnp.sum(jnp.where(lane == want, gate, 0.0), axis=-1, keepdims=True)
        o_ref[0, :, r * HEAD_DIM:(r + 1) * HEAD_DIM] = (acc * w).astype(o_ref.dtype)


def _flash_attention(mode, q, k, v, k_col0, t, *, bias=None, sel=None, gate=None, gate_col=0):
    b = q.shape[0]
    heads = HEADS_PER_GROUP
    qw = heads * HEAD_DIM
    units = q.shape[2] // qw
    kvw = qw if mode == "fox" else HEAD_DIM
    tq = tk = _row_tile(t, 256)
    n_kv = t // tk
    in_specs = [
        pl.BlockSpec((1, tq, qw), lambda bi, u, i: (bi, i, u)),
        pl.BlockSpec((1, t, kvw), lambda bi, u, i: (bi, 0, k_col0 + u)),
        pl.BlockSpec((1, t, kvw), lambda bi, u, i: (bi, 0, k_col0 + u)),
    ]
    args = [q, k, v]
    scratch = [pltpu.VMEM((t, kvw), BF16), pltpu.VMEM((t, kvw), BF16)]
    if mode == "fox":
        in_specs.append(pl.BlockSpec((1, heads, t), lambda bi, u, i: (bi * units + u, 0, 0)))
        args.append(bias.reshape(b * units, heads, t))
    else:
        if mode == "slc":
            nsp = sel.shape[-1]
            expand = (np.arange(nsp)[:, None] == (np.arange(t)[None, :] // SEL_BLOCK)).astype(np.float32)
            in_specs.append(pl.BlockSpec((1, 1, tq, nsp), lambda bi, u, i: (bi, u, i, 0)))
            in_specs.append(pl.BlockSpec((nsp, t), lambda bi, u, i: (0, 0)))
            args += [sel, jnp.asarray(expand, BF16)]
            scratch.append(pltpu.VMEM((n_kv, tq, tk), F32))
        in_specs.append(pl.BlockSpec((1, tq, LANES), lambda bi, u, i: (bi, i, 0)))
        args.append(gate)
    return pl.pallas_call(
        functools.partial(_flash_body, mode, heads, tq, tk, n_kv, gate_col),
        grid=(b, units, t // tq),
        in_specs=in_specs,
        out_specs=pl.BlockSpec((1, tq, qw), lambda bi, u, i: (bi, i, u)),
        out_shape=jax.ShapeDtypeStruct((b, t, units * qw), BF16),
        scratch_shapes=scratch,
        compiler_params=_cparams("parallel", "parallel", "arbitrary"),
        name="attn_" + mode,
    )(*args)


PAGED_BUFFER_BYTES = 2 * 1024 * 1024
PAGED_COPY_BYTES = 128 * 1024


def _pages_per_step(n_pages, page_bytes):
    pg = max(1, min(n_pages, PAGED_BUFFER_BYTES // page_bytes))
    while n_pages % pg:
        pg -= 1
    return pg


def _paged_body(units, rows, rows_b, n_pages, pg, page, use_gate, *refs):
    if use_gate:
        (tbl_ref, q_ref, bias_ref, kn_ref, vn_ref, biasn_ref, gate_ref, kpool, vpool, o_ref,
         kbuf, vbuf, sem, m_ref, l_ref, acc_ref) = refs
    else:
        (tbl_ref, q_ref, bias_ref, kn_ref, vn_ref, biasn_ref, kpool, vpool, o_ref,
         kbuf, vbuf, sem, m_ref, l_ref, acc_ref) = refs
        gate_ref = None
    bi = pl.program_id(0)
    s = pl.program_id(1)
    n_steps = n_pages // pg
    rpp = page * units
    n_keys = pg * page

    n_chunks = max(1, (rpp * HEAD_DIM * 4) // PAGED_COPY_BYTES)
    crows = rpp // n_chunks

    def copies(step, slot):
        out = []
        for k in range(pg):
            phys = tbl_ref[bi * n_pages + step * pg + k]
            for c in range(n_chunks):
                src = pl.ds(c * crows, crows)
                dst = pl.ds(k * rpp + c * crows, crows)
                out.append(pltpu.make_async_copy(kpool.at[phys, src], kbuf.at[slot, dst], sem.at[0, slot]))
                out.append(pltpu.make_async_copy(vpool.at[phys, src], vbuf.at[slot, dst], sem.at[1, slot]))
        return out

    @pl.when(s == 0)
    def _():
        m_ref[...] = jnp.full(m_ref.shape, NEG_INF, F32)
        l_ref[...] = jnp.zeros(l_ref.shape, F32)
        acc_ref[...] = jnp.zeros(acc_ref.shape, F32)
        for c in copies(0, 0):
            c.start()

    @pl.when(s + 1 < n_steps)
    def _():
        for c in copies(s + 1, lax.rem(s + 1, 2)):
            c.start()

    def attend(k, v, bias, own_blocks):
        k_hi, k_lo = _split_bf16(k)
        v_hi, v_lo = _split_bf16(v)
        sc = _dot_split(q_ref[0], k_hi, k_lo, _NT)
        nk = k.shape[0] // units
        if own_blocks:
            sc = jnp.concatenate([sc[u * rows:(u + 1) * rows, u * nk:(u + 1) * nk] for u in range(units)], axis=0)
        sc = sc + bias
        m = m_ref[...]
        m_new = jnp.maximum(m, jnp.max(sc, axis=-1, keepdims=True))
        alpha = jnp.exp(m - m_new)
        pr = jnp.exp(sc - m_new)
        l_ref[...] = alpha * l_ref[...] + jnp.sum(pr, axis=-1, keepdims=True)
        if own_blocks:
            zero = jnp.zeros((rows, nk), F32)
            pr = jnp.concatenate(
                [jnp.concatenate([pr[u * rows:(u + 1) * rows] if u2 == u else zero for u2 in range(units)], axis=1)
                 for u in range(units)], axis=0)
        acc_ref[...] = alpha * acc_ref[...] + _dot_split(pr, v_hi, v_lo, _NN)
        m_ref[...] = m_new

    def unit_bias(u):
        bb = bias_ref[0, u]
        if rows_b == 1:
            return jnp.broadcast_to(bb, (rows, n_keys))
        return jnp.concatenate([bb] * (rows // rows_b), axis=0)

    @pl.when(s < n_steps)
    def _():
        slot = lax.rem(s, 2)
        for c in copies(s, slot):
            c.wait()
        k = jnp.concatenate([kbuf[slot, pl.ds(u, n_keys, stride=units), :] for u in range(units)], axis=0)
        v = jnp.concatenate([vbuf[slot, pl.ds(u, n_keys, stride=units), :] for u in range(units)], axis=0)
        attend(k, v, jnp.concatenate([unit_bias(u) for u in range(units)], axis=0), True)

    @pl.when(s == n_steps)
    def _():
        k = jnp.concatenate([kn_ref[0, :, u * HEAD_DIM:(u + 1) * HEAD_DIM] for u in range(units)], axis=0)
        v = jnp.concatenate([vn_ref[0, :, u * HEAD_DIM:(u + 1) * HEAD_DIM] for u in range(units)], axis=0)
        attend(k, v, biasn_ref[0], False)
        w = 1.0 / l_ref[...]
        if use_gate:
            w = w * gate_ref[0]
        o_ref[0] = acc_ref[...] * w


def _paged_attention(q, k_pool, v_pool, table, bias, k_new, v_new, bias_new, gate=None):
    b, units, rows, _ = q.shape
    ur = units * rows
    n_pages = table.shape[1]
    rpp = k_pool.shape[1]
    page = rpp // units
    rows_b = bias.shape[2]
    n_new = k_new.shape[1]
    width = k_new.shape[2]
    pg = _pages_per_step(n_pages, rpp * HEAD_DIM * 4)
    n_steps = n_pages // pg
    last = n_steps - 1
    same_unit = jnp.arange(units)[:, None, None, None] == jnp.arange(units)[None, None, :, None]
    bias_new_full = jnp.where(same_unit[None], bias_new[:, :, :, None, :], NEG_INF).reshape(b, ur, units * n_new)
    in_specs = [
        pl.BlockSpec((1, ur, HEAD_DIM), lambda bi, p, tbl: (bi, 0, 0)),
        pl.BlockSpec((1, units, rows_b, pg * page), lambda bi, p, tbl: (bi, 0, 0, jnp.minimum(p, last))),
        pl.BlockSpec((1, n_new, width), lambda bi, p, tbl: (bi, 0, 0)),
        pl.BlockSpec((1, n_new, width), lambda bi, p, tbl: (bi, 0, 0)),
        pl.BlockSpec((1, ur, units * n_new), lambda bi, p, tbl: (bi, 0, 0)),
    ]
    args = [q.reshape(b, ur, HEAD_DIM), bias, k_new, v_new, bias_new_full]
    if gate is not None:
        in_specs.append(pl.BlockSpec((1, ur, 1), lambda bi, p, tbl: (bi, 0, 0)))
        args.append(gate.reshape(b, ur, 1))
    in_specs += [pl.BlockSpec(memory_space=pl.ANY), pl.BlockSpec(memory_space=pl.ANY)]
    args += [k_pool, v_pool]
    grid_spec = pltpu.PrefetchScalarGridSpec(
        num_scalar_prefetch=1,
        grid=(b, n_steps + 1),
        in_specs=in_specs,
        out_specs=pl.BlockSpec((1, ur, HEAD_DIM), lambda bi, p, tbl: (bi, 0, 0)),
        scratch_shapes=[pltpu.VMEM((2, pg * rpp, HEAD_DIM), F32), pltpu.VMEM((2, pg * rpp, HEAD_DIM), F32),
                        pltpu.SemaphoreType.DMA((2, 2)),
                        pltpu.VMEM((ur, 1), F32), pltpu.VMEM((ur, 1), F32), pltpu.VMEM((ur, HEAD_DIM), F32)],
    )
    out = pl.pallas_call(
        functools.partial(_paged_body, units, rows, rows_b, n_pages, pg, page, gate is not None),
        grid_spec=grid_spec,
        out_shape=jax.ShapeDtypeStruct((b, ur, HEAD_DIM), F32),
        compiler_params=_cparams("arbitrary", "arbitrary"),
        name="attn_paged",
    )(table.reshape(-1), *args)
    return out.reshape(b, units, rows, HEAD_DIM)


def _cmp_proj_body(x_ref, pa_ref, pb_ref, wa_ref, wb_ref, za_ref, zb_ref):
    x = x_ref[0]
    za_ref[0] = jnp.dot((x + pa_ref[...]).astype(BF16), wa_ref[...], preferred_element_type=F32)
    zb_ref[0] = jnp.dot((x + pb_ref[...]).astype(BF16), wb_ref[...], preferred_element_type=F32)


def _cmp_mlp_body(n_chunks, za_ref, zb_ref, w2_ref, o_ref):
    zb_next = pltpu.roll(zb_ref[0], n_chunks - 1, 0)
    hid = za_ref[0] + zb_next
    c = 0.7978845608028654
    hid = 0.5 * hid * (1.0 + jnp.tanh(c * (hid + 0.044715 * (hid * hid * hid))))
    w2 = w2_ref[...]
    hb = hid.astype(w2.dtype)
    for g in range(NSA_KV_GROUPS):
        sl = slice(g * HEAD_DIM, (g + 1) * HEAD_DIM)
        o_ref[0, :, sl] = jnp.dot(hb[:, sl], w2, preferred_element_type=F32, precision=_dot_precision(w2.dtype))


def _compress(rows, pos_emb, w1, w2):
    b, tc, gw = rows.shape
    g = NSA_KV_GROUPS
    n_chunks = tc // CMP_STRIDE
    cw = CMP_STRIDE * gw
    x = rows.reshape(b, n_chunks, cw)
    w1r = w1.reshape(2, CMP_STRIDE, HEAD_DIM, HEAD_DIM)
    eye = jnp.eye(g, dtype=w1.dtype)
    wcat = jnp.einsum("hldj,ge->hlgdej", w1r, eye).reshape(2, cw, g * HEAD_DIM).astype(BF16)
    pos = jnp.broadcast_to(pos_emb.reshape(2, CMP_STRIDE, 1, HEAD_DIM), (2, CMP_STRIDE, g, HEAD_DIM)).reshape(2, 1, cw)
    cm = _row_tile(n_chunks, 128)
    za, zb = pl.pallas_call(
        _cmp_proj_body,
        grid=(b, n_chunks // cm),
        in_specs=[
            pl.BlockSpec((1, cm, cw), lambda bi, i: (bi, i, 0)),
            pl.BlockSpec((1, cw), lambda bi, i: (0, 0)),
            pl.BlockSpec((1, cw), lambda bi, i: (0, 0)),
            pl.BlockSpec((cw, g * HEAD_DIM), lambda bi, i: (0, 0)),
            pl.BlockSpec((cw, g * HEAD_DIM), lambda bi, i: (0, 0)),
        ],
        out_specs=[pl.BlockSpec((1, cm, g * HEAD_DIM), lambda bi, i: (bi, i, 0))] * 2,
        out_shape=[jax.ShapeDtypeStruct((b, n_chunks, g * HEAD_DIM), F32)] * 2,
        compiler_params=_cparams("parallel", "parallel"),
        name="cmp_proj",
    )(x, pos[0], pos[1], wcat[0], wcat[1])
    return _cmp_mlp(za, zb, w2.astype(BF16))


def _cmp_mlp(za, zb, w2):
    b, n_chunks, gw = za.shape
    g = NSA_KV_GROUPS
    return pl.pallas_call(
        functools.partial(_cmp_mlp_body, n_chunks),
        grid=(b,),
        in_specs=[
            pl.BlockSpec((1, n_chunks, g * HEAD_DIM), lambda bi: (bi, 0, 0)),
            pl.BlockSpec((1, n_chunks, g * HEAD_DIM), lambda bi: (bi, 0, 0)),
            pl.BlockSpec((HEAD_DIM, HEAD_DIM), lambda bi: (0, 0)),
        ],
        out_specs=pl.BlockSpec((1, n_chunks, g * HEAD_DIM), lambda bi: (bi, 0, 0)),
        out_shape=jax.ShapeDtypeStruct((b, n_chunks, g * HEAD_DIM), F32),
        compiler_params=_cparams("parallel"),
        name="cmp_mlp",
    )(za, zb, w2)


def _split_bf16(x):
    hi = x.astype(BF16)
    return hi, (x - hi.astype(F32)).astype(BF16)


def _dot_split(a, b_hi, b_lo, dims):
    a_hi, a_lo = _split_bf16(a)
    n = a.shape[0]
    both = lax.dot_general(jnp.concatenate([a_hi, a_lo], axis=0), b_hi, dims, preferred_element_type=F32)
    return both[:n] + both[n:] + lax.dot_general(a_hi, b_lo, dims, preferred_element_type=F32)


_NT = (((1,), (1,)), ((), ()))
_NN = (((1,), (0,)), ((), ()))


def _cmp_paged_body(n_pages, pg, page, tbl_ref, pos_ref, w_ref, pool, za_ref, zb_ref, xbuf, whi_ref, wlo_ref, sem):
    bi = pl.program_id(0)
    s = pl.program_id(1)
    n_steps = n_pages // pg
    g_n = NSA_KV_GROUPS
    rpp = page * g_n
    n_rows = pg * page // CMP_STRIDE

    def copies(step, slot):
        return [pltpu.make_async_copy(pool.at[tbl_ref[bi * n_pages + step * pg + k]],
                                      xbuf.at[slot, pl.ds(k * rpp, rpp)], sem.at[slot]) for k in range(pg)]

    @pl.when(s == 0)
    def _():
        for c in copies(0, 0):
            c.start()
        w_hi, w_lo = _split_bf16(w_ref[...])
        whi_ref[...] = w_hi
        wlo_ref[...] = w_lo

    @pl.when(s + 1 < n_steps)
    def _():
        for c in copies(s + 1, lax.rem(s + 1, 2)):
            c.start()

    slot = lax.rem(s, 2)
    for c in copies(s, slot):
        c.wait()
    w_hi = whi_ref[...]
    w_lo = wlo_ref[...]
    pw = _dot_split(pos_ref[...], w_hi, w_lo, _NN)
    col = lax.broadcasted_iota(jnp.int32, (1, 2 * HEAD_DIM), 1)
    pos_term = jnp.where(col < HEAD_DIM, pw[0:1, :], pw[1:2, :])
    for g in range(g_n):
        x_cat = jnp.concatenate(
            [xbuf[slot, pl.ds(l * g_n + g, n_rows, stride=CMP_STRIDE * g_n), :] for l in range(CMP_STRIDE)], axis=1)
        z = _dot_split(x_cat, w_hi, w_lo, _NN) + pos_term
        za_ref[0, :, g * HEAD_DIM:(g + 1) * HEAD_DIM] = z[:, :HEAD_DIM]
        zb_ref[0, :, g * HEAD_DIM:(g + 1) * HEAD_DIM] = z[:, HEAD_DIM:]


def _compress_paged(pool, table, pos_emb, w1, w2):
    b, n_pages = table.shape
    rpp = pool.shape[1]
    page = rpp // NSA_KV_GROUPS
    gw = NSA_KV_GROUPS * HEAD_DIM
    pg = _pages_per_step(n_pages, rpp * HEAD_DIM * 4)
    n_steps = n_pages // pg
    n_rows = pg * page // CMP_STRIDE
    n_chunks = n_pages * page // CMP_STRIDE
    half = CMP_STRIDE * HEAD_DIM
    w_cat = jnp.concatenate([w1[:half], w1[half:]], axis=1)
    pos_rows = jnp.pad(pos_emb.reshape(2, half), ((0, SUBLANES - 2), (0, 0)))
    grid_spec = pltpu.PrefetchScalarGridSpec(
        num_scalar_prefetch=1,
        grid=(b, n_steps),
        in_specs=[pl.BlockSpec((SUBLANES, half), lambda bi, i, tbl: (0, 0)),
                  pl.BlockSpec((half, 2 * HEAD_DIM), lambda bi, i, tbl: (0, 0)),
                  pl.BlockSpec(memory_space=pl.ANY)],
        out_specs=[pl.BlockSpec((1, n_rows, gw), lambda bi, i, tbl: (bi, i, 0))] * 2,
        scratch_shapes=[pltpu.VMEM((2, pg * rpp, HEAD_DIM), F32), pltpu.VMEM((half, 2 * HEAD_DIM), BF16),
                        pltpu.VMEM((half, 2 * HEAD_DIM), BF16), pltpu.SemaphoreType.DMA((2,))],
    )
    za, zb = pl.pallas_call(
        functools.partial(_cmp_paged_body, n_pages, pg, page),
        grid_spec=grid_spec,
        out_shape=[jax.ShapeDtypeStruct((b, n_chunks, gw), F32)] * 2,
        compiler_params=_cparams("arbitrary", "arbitrary"),
        name="cmp_proj_paged",
    )(table.reshape(-1), pos_rows, w_cat, pool)
    return _cmp_mlp(za, zb, w2)


def _cmp_attn_body(tq, nc, ns, n_pick, q_start, q_ref, kc_ref, vc_ref, cover_ref, gate_ref, o_ref, sel_ref):
    unit = pl.program_id(1)
    qi = pl.program_id(2)
    precise = q_ref.dtype == F32
    if precise:
        kc_hi, kc_lo = _split_bf16(kc_ref[0])
        vc_hi, vc_lo = _split_bf16(vc_ref[0])
    else:
        kc_hi = kc_ref[0].astype(BF16)
        vc_hi = vc_ref[0].astype(BF16)
    ncp = kc_hi.shape[0]
    q_pos = q_start + qi * tq + lax.broadcasted_iota(jnp.int32, (tq, ncp), 0)
    n_idx = lax.broadcasted_iota(jnp.int32, (tq, ncp), 1)
    visible = (n_idx * CMP_STRIDE + (CMP_BLOCK - 1) <= q_pos) & (n_idx < nc)
    gate = gate_ref[0]
    glane = lax.broadcasted_iota(jnp.int32, gate.shape, 1)
    p_grp = jnp.zeros((tq, ncp), F32)
    for r in range(HEADS_PER_GROUP):
        q = q_ref[0, :, r * HEAD_DIM:(r + 1) * HEAD_DIM]
        if precise:
            s = _dot_split(q, kc_hi, kc_lo, _NT)
        else:
            s = lax.dot_general(q, kc_hi, _NT, preferred_element_type=F32)
        s = jnp.where(visible, s, NEG_INF)
        m = jnp.max(s, axis=-1, keepdims=True)
        e = jnp.where(visible, jnp.exp(s - m), 0.0)
        l = jnp.sum(e, axis=-1, keepdims=True)
        p = e / jnp.where(l > 0.0, l, 1.0)
        p_grp = p_grp + p
        want = FOX_HEADS + (unit * HEADS_PER_GROUP + r) * N_NSA_BRANCHES
        gcol = jnp.sum(jnp.where(glane == want, gate, 0.0), axis=-1, keepdims=True)
        if precise:
            o = _dot_split(p, vc_hi, vc_lo, _NN)
        else:
            o = jnp.dot(p.astype(BF16), vc_hi, preferred_element_type=F32)
        o_ref[0, :, r * HEAD_DIM:(r + 1) * HEAD_DIM] = (o * gcol).astype(o_ref.dtype)

    imp = jnp.dot(p_grp, cover_ref[...], preferred_element_type=F32, precision=HIGHEST)
    nsp = imp.shape[1]
    j = lax.broadcasted_iota(jnp.int32, (tq, nsp), 1)
    cur = (q_start + qi * tq + lax.broadcasted_iota(jnp.int32, (tq, nsp), 0)) // SEL_BLOCK
    dist = cur - j
    forced = (j == 0) | ((dist >= 0) & (dist < N_LOCAL_SEL))
    score = jnp.where(forced, FORCE_SCORE, imp)
    score = jnp.where(dist >= 0, score, NEG_INF)
    score = jnp.where(j < ns, score, PAD_SCORE)
    sel = jnp.zeros((tq, nsp), F32)
    j_f = j.astype(F32)
    for _ in range(n_pick):
        mx = jnp.max(score, axis=-1, keepdims=True)
        first = jnp.min(jnp.where(score == mx, j_f, float(nsp)), axis=-1, keepdims=True)
        hit = j_f == first
        sel = jnp.where(hit, 1.0, sel)
        score = jnp.where(hit, TAKEN_SCORE, score)
    sel_ref[0, 0] = sel


def _cmp_attention(nq, kc, vc, gate, t, q_start, n_keys, out_dtype):
    b = nq.shape[0]
    ncp = kc.shape[1]
    nc = (n_keys - CMP_BLOCK) // CMP_STRIDE + 1
    ns = -(-n_keys // SEL_BLOCK)
    nsp = -(-ns // LANES) * LANES
    n_pick = min(N_SELECT, ns)
    c0 = np.arange(ncp)[:, None] * CMP_STRIDE
    s0 = np.arange(nsp)[None, :] * SEL_BLOCK
    cover = np.clip(np.minimum(c0 + CMP_BLOCK, s0 + SEL_BLOCK) - np.maximum(c0, s0), 0, None).astype(np.float32) / CMP_BLOCK
    cover = cover * (np.arange(ncp)[:, None] < nc) * (np.arange(nsp)[None, :] < ns)
    tq = _row_tile(t, 256)
    qw = HEADS_PER_GROUP * HEAD_DIM
    return pl.pallas_call(
        functools.partial(_cmp_attn_body, tq, nc, ns, n_pick, q_start),
        grid=(b, NSA_KV_GROUPS, t // tq),
        in_specs=[
            pl.BlockSpec((1, tq, qw), lambda bi, u, i: (bi, i, u)),
            pl.BlockSpec((1, ncp, HEAD_DIM), lambda bi, u, i: (bi, 0, u)),
            pl.BlockSpec((1, ncp, HEAD_DIM), lambda bi, u, i: (bi, 0, u)),
            pl.BlockSpec((ncp, nsp), lambda bi, u, i: (0, 0)),
            pl.BlockSpec((1, tq, LANES), lambda bi, u, i: (bi, i, 0)),
        ],
        out_specs=[
            pl.BlockSpec((1, tq, qw), lambda bi, u, i: (bi, i, u)),
            pl.BlockSpec((1, 1, tq, nsp), lambda bi, u, i: (bi, u, i, 0)),
        ],
        out_shape=[
            jax.ShapeDtypeStruct((b, t, NSA_KV_GROUPS * qw), out_dtype),
            jax.ShapeDtypeStruct((b, NSA_KV_GROUPS, t, nsp), F32),
        ],
        compiler_params=_cparams("parallel", "parallel", "parallel"),
        name="attn_cmp_select",
    )(nq, kc, vc, jnp.asarray(cover), gate)


def _merge_body(a_ref, b1_ref, b2_ref, b3_ref, ua_ref, ub_ref, ga_ref, gb_ref, o_ref):
    a = a_ref[...]
    prec = _dot_precision(a.dtype)
    bsum = (b1_ref[...].astype(F32) + b2_ref[...].astype(F32) + b3_ref[...].astype(F32)).astype(a.dtype)
    ya = jnp.dot(a, ua_ref[...], preferred_element_type=F32, precision=prec)
    yb = jnp.dot(bsum, ub_ref[...], preferred_element_type=F32, precision=prec)
    o_ref[...] = (ga_ref[...] * ya + gb_ref[...] * yb).astype(o_ref.dtype)


def _merge(o_fox, o_cmp, o_slc, o_win, u_fox, u_nsa, mg, d_model):
    n, w = o_fox.shape
    tm = _row_tile(n, 512)
    tn = 512
    nj = d_model // tn
    row = pl.BlockSpec((tm, w), lambda i, j: (i, 0))
    return pl.pallas_call(
        _merge_body,
        grid=(n // tm, nj),
        in_specs=[row, row, row, row,
                  pl.BlockSpec((w, tn), lambda i, j: (0, j)),
                  pl.BlockSpec((w, tn), lambda i, j: (0, j)),
                  pl.BlockSpec((tm, tn), lambda i, j: (i, j)),
                  pl.BlockSpec((tm, tn), lambda i, j: (i, j + nj))],
        out_specs=pl.BlockSpec((tm, tn), lambda i, j: (i, j)),
        out_shape=jax.ShapeDtypeStruct((n, d_model), o_fox.dtype),
        compiler_params=_cparams("parallel", "parallel"),
        name="mixer_merge",
    )(o_fox, o_cmp, o_slc, o_win, u_fox, u_nsa, mg, mg)


def _router_body(h_ref, g_ref, w_ref, b_ref, u_ref, route_ref):
    x = h_ref[...]
    xn = x * lax.rsqrt(jnp.mean(x * x, axis=-1, keepdims=True) + NORM_EPS) * g_ref[...]
    u_ref[...] = xn.astype(u_ref.dtype)
    z = jnp.dot(xn, w_ref[...], preferred_element_type=F32, precision=HIGHEST) + b_ref[...]
    c = lax.broadcasted_iota(jnp.int32, z.shape, 1)
    big = z.shape[1]
    is_g = c < N_GROUPS
    gmax = jnp.max(jnp.where(is_g, z, -jnp.inf), axis=-1, keepdims=True)
    gsel = jnp.min(jnp.where(is_g & (z == gmax), c, big), axis=-1, keepdims=True)
    gsum = jnp.sum(jnp.where(is_g, jnp.exp(z - gmax), 0.0), axis=-1, keepdims=True)
    lo = N_GROUPS + gsel * EXPERTS_PER_GROUP
    in_grp = (c >= lo) & (c < lo + EXPERTS_PER_GROUP)
    z1 = jnp.max(jnp.where(in_grp, z, -jnp.inf), axis=-1, keepdims=True)
    c1 = jnp.min(jnp.where(in_grp & (z == z1), c, big), axis=-1, keepdims=True)
    rest = in_grp & (c != c1)
    z2 = jnp.max(jnp.where(rest, z, -jnp.inf), axis=-1, keepdims=True)
    c2 = jnp.min(jnp.where(rest & (z == z2), c, big), axis=-1, keepdims=True)
    e2 = jnp.exp(z2 - z1)
    p1 = 1.0 / (1.0 + e2)
    p2 = e2 / (1.0 + e2)
    gp = 1.0 / gsum
    out = jnp.where(c == 0, (c1 - N_GROUPS).astype(F32), 0.0)
    out = jnp.where(c == 1, (c2 - N_GROUPS).astype(F32), out)
    out = jnp.where(c == 2, gp * p1, out)
    out = jnp.where(c == 3, gp * p2, out)
    route_ref[...] = out


def _router(h, gain, w_cat, b_cat):
    n, d = h.shape
    tm = _row_tile(n, 512)
    return pl.pallas_call(
        _router_body,
        grid=(n // tm,),
        in_specs=[pl.BlockSpec((tm, d), lambda i: (i, 0)), pl.BlockSpec((1, d), lambda i: (0, 0)),
                  pl.BlockSpec((d, LANES), lambda i: (0, 0)), pl.BlockSpec((1, LANES), lambda i: (0, 0))],
        out_specs=[pl.BlockSpec((tm, d), lambda i: (i, 0)), pl.BlockSpec((tm, LANES), lambda i: (i, 0))],
        out_shape=[jax.ShapeDtypeStruct((n, d), F32), jax.ShapeDtypeStruct((n, LANES), F32)],
        compiler_params=_cparams("parallel"),
        name="moe_router",
    )(h, gain.reshape(1, d), w_cat, b_cat)


def _gather_rows(src_hbm, dst_ref, sem, idx_ref, base, count, start):
    if not start:
        pltpu.make_async_copy(src_hbm.at[pl.ds(0, count)], dst_ref, sem).wait()
        return

    def body(r, c):
        pltpu.make_async_copy(src_hbm.at[pl.ds(idx_ref[base + r], 1)], dst_ref.at[pl.ds(r, 1)], sem).start()
        return c

    lax.fori_loop(0, count, body, 0, unroll=8)


def _expert_body(tm, te_ref, nv_ref, tok_ref, wg_ref, wu_ref, wd_ref, x_hbm, o_ref, xbuf, sem):
    i = pl.program_id(0)
    nv = nv_ref[0]

    def rows_of(tile, start):
        slot = lax.rem(tile, 2)
        _gather_rows(x_hbm, xbuf.at[slot], sem.at[slot], tok_ref, tile * tm, tm, start)

    @pl.when(i == 0)
    def _():
        rows_of(0, True)

    @pl.when(i + 1 < nv)
    def _():
        rows_of(i + 1, True)

    @pl.when(i < nv)
    def _():
        rows_of(i, False)
        x = xbuf[lax.rem(i, 2)].astype(BF16)
        hg = jnp.dot(x, wg_ref[0].astype(BF16), preferred_element_type=F32)
        hu = jnp.dot(x, wu_ref[0].astype(BF16), preferred_element_type=F32)
        hid = (hg * jax.nn.sigmoid(hg)) * hu
        o_ref[...] = jnp.dot(hid.astype(BF16), wd_ref[0].astype(BF16), preferred_element_type=F32)

    @pl.when(i >= nv)
    def _():
        o_ref[...] = jnp.zeros(o_ref.shape, o_ref.dtype)


def _experts(x, row_token, tile_expert, n_valid, wg, wu, wd, tm):
    rows = row_token.shape[0]
    d = x.shape[1]
    n_tiles = rows // tm
    f = wg.shape[2]

    def w_map(i, te, nv, tok):
        return (te[i], 0, 0)

    grid_spec = pltpu.PrefetchScalarGridSpec(
        num_scalar_prefetch=3,
        grid=(n_tiles,),
        in_specs=[pl.BlockSpec((1, d, f), w_map), pl.BlockSpec((1, d, f), w_map), pl.BlockSpec((1, f, d), w_map),
                  pl.BlockSpec(memory_space=pl.ANY)],
        out_specs=pl.BlockSpec((tm, d), lambda i, te, nv, tok: (i, 0)),
        scratch_shapes=[pltpu.VMEM((2, tm, d), F32), pltpu.SemaphoreType.DMA((2,))],
    )
    return pl.pallas_call(
        functools.partial(_expert_body, tm),
        grid_spec=grid_spec,
        out_shape=jax.ShapeDtypeStruct((rows, d), F32),
        compiler_params=_cparams("arbitrary"),
        name="moe_experts",
    )(tile_expert, n_valid, row_token, wg, wu, wd, x)


def _combine_body(tm, n, slot_ref, h_ref, route_ref, src_hbm, o_ref, buf, sem):
    i = pl.program_id(0)
    n_tiles = pl.num_programs(0)

    def rows_of(tile, start):
        slot = lax.rem(tile, 2)
        for k in range(2):
            _gather_rows(src_hbm, buf.at[slot, k], sem.at[slot], slot_ref, k * n + tile * tm, tm, start)

    @pl.when(i == 0)
    def _():
        rows_of(0, True)

    @pl.when(i + 1 < n_tiles)
    def _():
        rows_of(i + 1, True)

    rows_of(i, False)
    slot = lax.rem(i, 2)
    route = route_ref[...]
    o_ref[...] = h_ref[...] + route[:, 2:3] * buf[slot, 0] + route[:, 3:4] * buf[slot, 1]


def _moe_combine(h, route, out_sorted, slot_kn):
    n, d = h.shape
    tm = _row_tile(n, 256)
    grid_spec = pltpu.PrefetchScalarGridSpec(
        num_scalar_prefetch=1,
        grid=(n // tm,),
        in_specs=[pl.BlockSpec((tm, d), lambda i, sl: (i, 0)), pl.BlockSpec((tm, LANES), lambda i, sl: (i, 0)),
                  pl.BlockSpec(memory_space=pl.ANY)],
        out_specs=pl.BlockSpec((tm, d), lambda i, sl: (i, 0)),
        scratch_shapes=[pltpu.VMEM((2, 2, tm, d), F32), pltpu.SemaphoreType.DMA((2,))],
    )
    return pl.pallas_call(
        functools.partial(_combine_body, tm, n),
        grid_spec=grid_spec,
        out_shape=jax.ShapeDtypeStruct((n, d), F32),
        compiler_params=_cparams("arbitrary"),
        name="moe_combine",
    )(slot_kn, h, route, out_sorted)


def _moe(h, gain, w_router, b_router, wg, wu, wd):
    n, d = h.shape
    ut, route = _router(h, gain, w_router, b_router)
    expert = route[:, :2].astype(jnp.int32).reshape(-1)
    n_asg = 2 * n
    tm = 256 if n_asg >= 8192 else 16
    n_tiles = n_asg // tm + N_EXPERTS
    onehot = (expert[:, None] == jnp.arange(N_EXPERTS, dtype=jnp.int32)[None, :]).astype(jnp.int32)
    running = jnp.cumsum(onehot, axis=0)
    counts = running[-1]
    rank = jnp.sum(onehot * running, axis=1) - 1
    tiles_per = (counts + tm - 1) // tm
    tile_end = jnp.cumsum(tiles_per)
    start_row = (tile_end - tiles_per) * tm
    slot = (jnp.sum(onehot * start_row[None, :], axis=1) + rank).astype(jnp.int32)
    row_token = jnp.zeros((n_tiles * tm,), jnp.int32).at[slot].set(jnp.arange(n_asg, dtype=jnp.int32) // 2)
    n_valid = tile_end[-1].astype(jnp.int32)
    tile_ids = jnp.minimum(jnp.arange(n_tiles, dtype=jnp.int32), n_valid - 1)
    tile_expert = jnp.sum((tile_end[None, :] <= tile_ids[:, None]).astype(jnp.int32), axis=1)
    out_sorted = _experts(ut, row_token, tile_expert, n_valid.reshape(1), wg, wu, wd, tm)
    return _moe_combine(h, route, out_sorted, jnp.transpose(slot.reshape(n, 2)).reshape(-1))


def _rope_tables(positions):
    half = HEAD_DIM // 2
    inv = ROPE_THETA ** (-np.arange(half, dtype=np.float64) / half)
    ang = np.asarray(positions, np.float64)[:, None] * inv[None, :]
    cos = np.concatenate([np.cos(ang), np.cos(ang)], axis=-1)
    sin = np.concatenate([-np.sin(ang), np.sin(ang)], axis=-1)
    return jnp.asarray(cos, F32), jnp.asarray(sin, F32)


def _project(u, wts, p, b, t, q_start):
    n = u.shape[0]
    tm = _row_tile(n, 1024)
    tn = 256
    scale = HEAD_DIM ** -0.5
    gain_spec = pl.BlockSpec((1, HEAD_DIM), lambda i, j: (0, 0))

    def run(w, epilogue, extras, specs, dtype, name, tn_=tn):
        shape = jax.ShapeDtypeStruct((n, w.shape[1]), dtype)
        spec = pl.BlockSpec((tm, tn_), lambda i, j: (i, j))
        return _matmul(u, w, epilogue, extras, specs, shape, spec, tm, tn_, name)

    cos, sin = _rope_tables(q_start + (np.arange(max(tm, t)) % t))
    n_tab = cos.shape[0] // tm
    tab_spec = pl.BlockSpec((tm, HEAD_DIM), lambda i, j: (i % n_tab, 0))

    fq = run(wts["fq"], functools.partial(_ep_head_norm, scale), [p["fox_q_norm"]], [gain_spec], u.dtype, "proj_fq")
    fk = run(wts["fk"], functools.partial(_ep_head_norm, 1.0), [p["fox_k_norm"]], [gain_spec], F32, "proj_fk")
    fv = run(wts["fv"], _ep_identity, [], [], F32, "proj_fv")
    nq = run(wts["nq"], functools.partial(_ep_head_norm_rope, scale), [p["nsa_q_norm"], cos, sin],
             [gain_spec, tab_spec, tab_spec], u.dtype, "proj_nq")
    nk = run(wts["nk"], functools.partial(_ep_head_norm_rope, 1.0), [p["nsa_k_norm"], cos, sin],
             [gain_spec, tab_spec, tab_spec], F32, "proj_nk")
    nv = run(wts["nv"], _ep_identity, [], [], F32, "proj_nv")
    mg = run(wts["mg"], _ep_sigmoid, [], [], F32, "proj_mg")
    small = run(wts["small"], _ep_small, [p["small_bias"]], [pl.BlockSpec((1, LANES), lambda i, j: (0, 0))],
                F32, "proj_small", LANES)
    return fq, fk, fv, nq, nk, nv, mg, small


def _layer(x, p, wts, q_start, past):
    b, t, d = x.shape
    n = b * t
    g = NSA_KV_GROUPS
    kvw = g * HEAD_DIM
    x2d = x.reshape(n, d)
    dense = wts["f32"] if past is not None else wts
    u = _rmsnorm(x2d, p["norm_mix"], dense["fq"].dtype)
    fq, fk, fv, nq, nk, nv, mg, small = _project(u, dense, p, b, t, q_start)

    logf = small[:, :FOX_HEADS].reshape(b, t, FOX_HEADS)
    gate3 = small.reshape(b, t, LANES)
    fq3 = fq.reshape(b, t, -1)
    nq3 = nq.reshape(b, t, -1)
    fk3 = fk.reshape(b, t, -1)
    fv3 = fv.reshape(b, t, -1)
    nk3 = nk.reshape(b, t, 3 * kvw)
    nv3 = nv.reshape(b, t, 3 * kvw)
    ck, sk, wk = (nk3[:, :, i * kvw:(i + 1) * kvw] for i in range(3))
    cv, sv, wv = (nv3[:, :, i * kvw:(i + 1) * kvw] for i in range(3))

    if past is None:
        n_keys = t
        n_lp = t // LANES
        lf_pool = jnp.transpose(logf.reshape(b * n_lp, LANES, FOX_HEADS), (0, 2, 1))
        bias = _fox_key_bias(lf_pool, jnp.arange(b * n_lp, dtype=jnp.int32).reshape(b, n_lp), None)
        o_fox = _flash_attention("fox", fq3, fk3, fv3, 0, t, bias=bias)
        kc = _compress(ck, p["cmp_pos_k"], p["cmp_w1_k"], p["cmp_w2_k"])
        vc = _compress(cv, p["cmp_pos_v"], p["cmp_w1_v"], p["cmp_w2_v"])
        o_cmp, sel = _cmp_attention(nq3, kc, vc, gate3, t, q_start, n_keys, BF16)
        o_slc = _flash_attention("slc", nq3, nk3, nv3, g, t, sel=sel, gate=gate3, gate_col=1)
        o_win = _flash_attention("win", nq3, nk3, nv3, 2 * g, t, gate=gate3, gate_col=2)
        keep = min(WINDOW, t)
        win_k, win_v = wk[:, t - keep:], wv[:, t - keep:]
    else:
        table = past["table"]
        n_pages = table.shape[1]
        page = past["page"]
        past_len = n_pages * page
        n_keys = past_len + t
        new_page = jnp.pad(jnp.transpose(logf, (0, 2, 1)), ((0, 0), (0, 0), (0, page - t)))
        bias = _fox_key_bias(past["fox_logf"], table, new_page)[:, :, :n_keys]
        causal = jnp.where(jnp.arange(t)[None, :] <= jnp.arange(t)[:, None], 0.0, NEG_INF).astype(F32)
        bias_new = bias[:, :, None, past_len:] + causal[None, None]
        qf = jnp.transpose(fq3.reshape(b, t, FOX_HEADS, HEAD_DIM), (0, 2, 1, 3))
        o = _paged_attention(qf, past["fox_k"], past["fox_v"], table, bias[:, :, None, :past_len],
                             fk3, fv3, bias_new)
        o_fox = jnp.transpose(o, (0, 2, 1, 3)).reshape(b, t, -1)
        nc = (n_keys - CMP_BLOCK) // CMP_STRIDE + 1
        assert (nc + 1) * CMP_STRIDE <= past_len and page % CMP_STRIDE == 0
        kc = _compress_paged(past["cmp_k"], table, p["cmp_pos_k"], p["cmp_w1_k"], p["cmp_w2_k"])
        vc = _compress_paged(past["cmp_v"], table, p["cmp_pos_v"], p["cmp_w1_v"], p["cmp_w2_v"])
        o_cmp, sel = _cmp_attention(nq3, kc, vc, gate3, t, q_start, n_keys, F32)
        rows = HEADS_PER_GROUP * t
        qn = jnp.transpose(nq3.reshape(b, t, g, HEADS_PER_GROUP, HEAD_DIM), (0, 2, 3, 1, 4)).reshape(b, g, rows, HEAD_DIM)
        gates = small[:, FOX_HEADS:FOX_HEADS + NSA_HEADS * N_NSA_BRANCHES].reshape(b, t, g, HEADS_PER_GROUP, N_NSA_BRANCHES)
        gates = jnp.transpose(gates, (4, 0, 2, 3, 1)).reshape(N_NSA_BRANCHES, b, g, rows, 1)
        ns = -(-n_keys // SEL_BLOCK)
        key_sel = jnp.repeat(sel[..., :ns], SEL_BLOCK, axis=-1)[..., :n_keys]
        sel_bias = jnp.where(key_sel > 0.5, 0.0, NEG_INF).astype(F32)
        sel_new = jnp.tile(sel_bias[..., past_len:] + causal[None, None], (1, 1, HEADS_PER_GROUP, 1))
        o = _paged_attention(qn, past["slc_k"], past["slc_v"], table, sel_bias[..., :past_len],
                             sk, sv, sel_new, gate=gates[1])
        o_slc = _rows_to_tokens(o, b, t)
        win_len = past["win_k"].shape[1]
        assert win_len % page == 0
        k_start = q_start - win_len
        q_abs = q_start + jnp.arange(t)[:, None]
        k_abs = k_start + jnp.arange(win_len)[None, :]
        band = jnp.where((k_abs >= 0) & (k_abs > q_abs - WINDOW), 0.0, NEG_INF).astype(F32)
        band = jnp.broadcast_to(band[None, None], (b, g, t, win_len))
        band_new = jnp.where(jnp.arange(t)[None, :] > jnp.arange(t)[:, None] - WINDOW, causal, NEG_INF)
        band_new = jnp.broadcast_to(jnp.tile(band_new, (HEADS_PER_GROUP, 1))[None, None], (b, g, rows, t))
        o = _paged_attention(qn, past["win_k_pool"], past["win_v_pool"], past["win_table"], band, wk, wv,
                             band_new, gate=gates[2])
        o_win = _rows_to_tokens(o, b, t)
        wk_all = jnp.concatenate([past["win_k"], wk], axis=1)
        wv_all = jnp.concatenate([past["win_v"], wv], axis=1)
        keep = min(WINDOW, wk_all.shape[1])
        win_k, win_v = wk_all[:, -keep:], wv_all[:, -keep:]

    m = _merge(o_fox.reshape(n, -1), o_cmp.reshape(n, -1), o_slc.reshape(n, -1), o_win.reshape(n, -1),
               dense["up_fox"], dense["up_nsa"], mg, d)
    tm = _row_tile(n, 512)
    h = _matmul(m, dense["out"], _ep_residual, [x2d], [pl.BlockSpec((tm, 512), lambda i, j: (i, j))],
                jax.ShapeDtypeStruct((n, d), F32), pl.BlockSpec((tm, 512), lambda i, j: (i, j)), tm, 512, "out_proj")
    y = _moe(h, p["norm_ffn"], wts["router"], p["router_bias"], wts["gate_e"], wts["up_e"], wts["down_e"])

    def heads(a, hh):
        return a.reshape(b, -1, hh, HEAD_DIM)

    state = (heads(fk3, FOX_HEADS), heads(fv3, FOX_HEADS), logf, heads(ck, g), heads(cv, g), heads(sk, g),
             heads(sv, g), heads(win_k, g), heads(win_v, g))
    return y.reshape(b, t, d), state


def _rows_to_tokens(o, b, t):
    o = o.reshape(b, NSA_KV_GROUPS, HEADS_PER_GROUP, t, HEAD_DIM)
    return jnp.transpose(o, (0, 3, 1, 2, 4)).reshape(b, t, -1)


def _prepare(l, norm_mix, w_in, fox_f_bias, fox_q_norm, fox_k_norm, nsa_q_norm, nsa_k_norm, cmp_pos_k, cmp_w1_k,
             cmp_w2_k, cmp_pos_v, cmp_w1_v, cmp_w2_v, nsa_gate_bias, w_up_fox, w_up_nsa, w_out, norm_ffn,
             w_router_group, b_router_group, w_router_expert, b_router_expert, w_gate_e, w_up_e, w_down_e):
    d = w_in.shape[1]
    fw = FOX_HEADS * HEAD_DIM
    nw = NSA_HEADS * HEAD_DIM
    kvw = NSA_KV_GROUPS * HEAD_DIM
    sizes = (fw, fw, fw, FOX_HEADS, nw, kvw, kvw, kvw, kvw, kvw, kvw, NSA_HEADS * N_NSA_BRANCHES, 2 * d)
    offs = np.concatenate([[0], np.cumsum(sizes)])
    w = w_in[l]

    def seg(i):
        return w[:, offs[i]:offs[i + 1]]

    n_small = FOX_HEADS + NSA_HEADS * N_NSA_BRANCHES
    small_w = jnp.pad(jnp.concatenate([seg(3), seg(11)], axis=1), ((0, 0), (0, LANES - n_small)))
    small_b = jnp.pad(jnp.concatenate([fox_f_bias[l], nsa_gate_bias[l]]), (0, LANES - n_small)).reshape(1, LANES)
    n_route = N_GROUPS + N_EXPERTS
    router_w = jnp.pad(jnp.concatenate([w_router_group[l], w_router_expert[l]], axis=1), ((0, 0), (0, LANES - n_route)))
    router_b = jnp.pad(jnp.concatenate([b_router_group[l], b_router_expert[l]]), (0, LANES - n_route)).reshape(1, LANES)
    dense = {
        "fq": seg(0), "fk": seg(1), "fv": seg(2), "nq": seg(4),
        "nk": jnp.concatenate([seg(5), seg(7), seg(9)], axis=1),
        "nv": jnp.concatenate([seg(6), seg(8), seg(10)], axis=1),
        "mg": seg(12), "small": small_w,
        "up_fox": w_up_fox[l], "up_nsa": w_up_nsa[l], "out": w_out[l],
    }
    wts = {
        **{k: v.astype(BF16) for k, v in dense.items()},
        "f32": dense,
        "router": router_w,
        "gate_e": w_gate_e[l], "up_e": w_up_e[l], "down_e": w_down_e[l],
    }
    p = {
        "norm_mix": norm_mix[l], "fox_q_norm": fox_q_norm[l].reshape(1, -1), "fox_k_norm": fox_k_norm[l].reshape(1, -1),
        "nsa_q_norm": nsa_q_norm[l].reshape(1, -1), "nsa_k_norm": nsa_k_norm[l].reshape(1, -1),
        "small_bias": small_b, "cmp_pos_k": cmp_pos_k[l], "cmp_w1_k": cmp_w1_k[l], "cmp_w2_k": cmp_w2_k[l],
        "cmp_pos_v": cmp_pos_v[l], "cmp_w1_v": cmp_w1_v[l], "cmp_w2_v": cmp_w2_v[l],
        "norm_ffn": norm_ffn[l], "router_bias": router_b,
    }
    return wts, p


def kernel(x_prompt, x_sample, cache_fox_k, cache_fox_v, cache_fox_logf, cache_cmp_k, cache_cmp_v, cache_slc_k, cache_slc_v, cache_win_k, cache_win_v, page_table, norm_mix, w_in, fox_f_bias, fox_q_norm, fox_k_norm, nsa_q_norm, nsa_k_norm, cmp_pos_k, cmp_w1_k, cmp_w2_k, cmp_pos_v, cmp_w1_v, cmp_w2_v, nsa_gate_bias, w_up_fox, w_up_nsa, w_out, norm_ffn, w_router_group, b_router_group, w_router_expert, b_router_expert, w_gate_e, w_up_e, w_down_e):
    depth = w_in.shape[0]
    n_phys, page = cache_fox_k.shape[1], cache_fox_k.shape[2]
    past_len = page_table.shape[1] * page
    h_p, h_s = x_prompt, x_sample
    p_states, s_states = [], []
    for l in range(depth):
        wts, p = _prepare(l, norm_mix, w_in, fox_f_bias, fox_q_norm, fox_k_norm, nsa_q_norm, nsa_k_norm, cmp_pos_k,
                          cmp_w1_k, cmp_w2_k, cmp_pos_v, cmp_w1_v, cmp_w2_v, nsa_gate_bias, w_up_fox, w_up_nsa,
                          w_out, norm_ffn, w_router_group, b_router_group, w_router_expert, b_router_expert,
                          w_gate_e, w_up_e, w_down_e)
        h_p, st_p = _layer(h_p, p, wts, 0, None)
        p_states.append(st_p)
        win_len = cache_win_k.shape[2]
        n_win = win_len // page
        dec_b = page_table.shape[0]

        def pool(c):
            return c.reshape(-1, page * c.shape[3], HEAD_DIM)

        past = {
            "table": page_table + l * n_phys, "page": page,
            "fox_k": pool(cache_fox_k), "fox_v": pool(cache_fox_v),
            "fox_logf": jnp.transpose(cache_fox_logf.reshape(-1, page, cache_fox_logf.shape[3]), (0, 2, 1)),
            "cmp_k": pool(cache_cmp_k), "cmp_v": pool(cache_cmp_v),
            "slc_k": pool(cache_slc_k), "slc_v": pool(cache_slc_v),
            "win_k_pool": pool(cache_win_k), "win_v_pool": pool(cache_win_v),
            "win_table": (l * dec_b + jnp.arange(dec_b, dtype=jnp.int32))[:, None] * n_win
                         + jnp.arange(n_win, dtype=jnp.int32)[None, :],
            "win_k": cache_win_k[l].reshape(dec_b, win_len, -1), "win_v": cache_win_v[l].reshape(dec_b, win_len, -1),
        }
        h_s, st_s = _layer(h_s, p, wts, past_len, past)
        s_states.append(st_s)
    p_out = [jnp.stack([st[i] for st in p_states], axis=0) for i in range(9)]
    s_out = [jnp.stack([st[i] for st in s_states], axis=0) for i in range(9)]
    return (h_p, h_s, *p_out, *s_out)
```

```python
import functools

import numpy as np
import jax
import jax.numpy as jnp
from jax import lax
from jax.experimental import pallas as pl
from jax.experimental.pallas import tpu as pltpu

F32 = jnp.float32
BF16 = jnp.bfloat16

HEAD_DIM = 128
FOX_HEADS = 8
NSA_HEADS = 8
NSA_KV_GROUPS = 2
HEADS_PER_GROUP = NSA_HEADS // NSA_KV_GROUPS
N_NSA_BRANCHES = 3
CMP_BLOCK = 32
CMP_STRIDE = 16
SEL_BLOCK = 64
N_SELECT = 16
N_LOCAL_SEL = 2
WINDOW = 512
N_GROUPS = 4
EXPERTS_PER_GROUP = 8
N_EXPERTS = N_GROUPS * EXPERTS_PER_GROUP
ROPE_THETA = 10000.0
NORM_EPS = 1e-6
NEG_INF = -1e30
FORCE_SCORE = 1e9
PAD_SCORE = -2e38
TAKEN_SCORE = -3e38

LANES = 128
SUBLANES = 8
VMEM_LIMIT_BYTES = 48 * 1024 * 1024

HIGHEST = lax.Precision.HIGHEST


def _cparams(*sem):
    return pltpu.CompilerParams(dimension_semantics=tuple(sem), vmem_limit_bytes=VMEM_LIMIT_BYTES)


def _row_tile(n, pref):
    t = min(n, pref)
    assert n % t == 0, (n, t)
    return t


def _rmsnorm_body(x_ref, g_ref, o_ref):
    x = x_ref[...]
    y = x * lax.rsqrt(jnp.mean(x * x, axis=-1, keepdims=True) + NORM_EPS)
    o_ref[...] = (y * g_ref[...]).astype(o_ref.dtype)


def _rmsnorm(x2d, gain, out_dtype):
    n, d = x2d.shape
    tm = _row_tile(n, 512)
    return pl.pallas_call(
        _rmsnorm_body,
        grid=(n // tm,),
        in_specs=[pl.BlockSpec((tm, d), lambda i: (i, 0)), pl.BlockSpec((1, d), lambda i: (0, 0))],
        out_specs=pl.BlockSpec((tm, d), lambda i: (i, 0)),
        out_shape=jax.ShapeDtypeStruct((n, d), out_dtype),
        compiler_params=_cparams("parallel"),
        name="rmsnorm",
    )(x2d, gain.reshape(1, d))


def _dot_precision(dtype):
    return HIGHEST if dtype == F32 else None


def _mm_body(epilogue, n_extra, a_ref, w_ref, *rest):
    acc = jnp.dot(a_ref[...], w_ref[...], preferred_element_type=F32, precision=_dot_precision(a_ref.dtype))
    epilogue(acc, rest[:n_extra], rest[n_extra:])


def _matmul(a, w, epilogue, extras, extra_specs, out_shapes, out_specs, tm, tn, name):
    n, k = a.shape
    c = w.shape[1]
    assert n % tm == 0 and c % tn == 0, (n, tm, c, tn)
    return pl.pallas_call(
        functools.partial(_mm_body, epilogue, len(extras)),
        grid=(n // tm, c // tn),
        in_specs=[pl.BlockSpec((tm, k), lambda i, j: (i, 0)), pl.BlockSpec((k, tn), lambda i, j: (0, j))]
        + list(extra_specs),
        out_specs=out_specs,
        out_shape=out_shapes,
        compiler_params=_cparams("parallel", "parallel"),
        name=name,
    )(a, w, *extras)


def _head_norm(x, gain):
    return x * lax.rsqrt(jnp.mean(x * x, axis=-1, keepdims=True) + NORM_EPS) * gain


def _rope_rot(x, cos, sin_signed):
    return x * cos + pltpu.roll(x, HEAD_DIM // 2, 1) * sin_signed


def _ep_head_norm(scale, acc, extras, outs):
    (gain_ref,) = extras
    (o_ref,) = outs
    for h in range(acc.shape[1] // HEAD_DIM):
        sl = slice(h * HEAD_DIM, (h + 1) * HEAD_DIM)
        o_ref[:, sl] = (_head_norm(acc[:, sl], gain_ref[...]) * scale).astype(o_ref.dtype)


def _ep_head_norm_rope(scale, acc, extras, outs):
    gain_ref, cos_ref, sin_ref = extras
    (o_ref,) = outs
    for h in range(acc.shape[1] // HEAD_DIM):
        sl = slice(h * HEAD_DIM, (h + 1) * HEAD_DIM)
        y = _rope_rot(_head_norm(acc[:, sl], gain_ref[...]), cos_ref[...], sin_ref[...])
        o_ref[:, sl] = (y * scale).astype(o_ref.dtype)


def _ep_identity(acc, extras, outs):
    (o_ref,) = outs
    o_ref[...] = acc.astype(o_ref.dtype)


def _ep_sigmoid(acc, extras, outs):
    (o_ref,) = outs
    o_ref[...] = jax.nn.sigmoid(acc).astype(o_ref.dtype)


def _ep_small(acc, extras, outs):
    (bias_ref,) = extras
    (o_ref,) = outs
    z = acc + bias_ref[...]
    col = lax.broadcasted_iota(jnp.int32, z.shape, 1)
    e = jnp.exp(-jnp.abs(z))
    log_sig = jnp.minimum(z, 0.0) - jnp.log(1.0 + e)
    sig = jnp.where(z >= 0, 1.0, e) / (1.0 + e)
    o_ref[...] = jnp.where(col < FOX_HEADS, log_sig, sig)


def _ep_residual(acc, extras, outs):
    (x_ref,) = extras
    (o_ref,) = outs
    o_ref[...] = x_ref[...] + acc


def _fox_bias_body(n_pages, has_new, *refs):
    if has_new:
        tbl_ref, new_ref, pool_ref, o_ref, buf, sem = refs
    else:
        tbl_ref, pool_ref, o_ref, buf, sem = refs
    bi = pl.program_id(0)

    def copy(p):
        return pltpu.make_async_copy(pool_ref.at[tbl_ref[bi * n_pages + p]], buf.at[p], sem)

    def start(p, c):
        copy(p).start()
        return c

    def wait(p, c):
        copy(p).wait()
        return c

    lax.fori_loop(0, n_pages, start, 0)
    lax.fori_loop(0, n_pages, wait, 0)
    if has_new:
        buf[n_pages] = new_ref[0]
    x = buf[...]
    p, h, w = x.shape
    within = x.reshape(p * h, w)
    lane = lax.broadcasted_iota(jnp.int32, within.shape, 1)
    s = 1
    while s < w:
        within = within + jnp.where(lane >= s, pltpu.roll(within, s, 1), 0.0)
        s *= 2
    within = within.reshape(p, h, w)
    total = jnp.broadcast_to(within[:, :, w - 1:w], x.shape)
    incl = total
    s = 1
    while s < p:
        incl = incl + jnp.concatenate([jnp.zeros((s, h, w), F32), incl[:p - s]], axis=0)
        s *= 2
    o_ref[0] = -(within + (incl - total))


def _fox_key_bias(pool, table, new_page):
    b, n_pages = table.shape
    h, w = pool.shape[1:]
    has_new = new_page is not None
    p_tot = n_pages + (1 if has_new else 0)
    in_specs = [pl.BlockSpec(memory_space=pl.ANY)]
    args = [pool]
    if has_new:
        in_specs.insert(0, pl.BlockSpec((1, h, w), lambda i, tbl: (i, 0, 0)))
        args.insert(0, new_page)
    grid_spec = pltpu.PrefetchScalarGridSpec(
        num_scalar_prefetch=1,
        grid=(b,),
        in_specs=in_specs,
        out_specs=pl.BlockSpec((1, p_tot, h, w), lambda i, tbl: (i, 0, 0, 0)),
        scratch_shapes=[pltpu.VMEM((p_tot, h, w), F32), pltpu.SemaphoreType.DMA(())],
    )
    out = pl.pallas_call(
        functools.partial(_fox_bias_body, n_pages, has_new),
        grid_spec=grid_spec,
        out_shape=jax.ShapeDtypeStruct((b, p_tot, h, w), F32),
        compiler_params=_cparams("arbitrary"),
        name="fox_key_bias",
    )(table.reshape(-1), *args)
    return jnp.transpose(out, (0, 2, 1, 3)).reshape(b, h, p_tot * w)


FLASH_QUERY_BLOCK = 512
FLASH_KEY_BLOCK = 512


def _flash_body(mode, heads, tq, tk, n_kv, gate_col, *refs):
    if mode == "fox":
        q_ref, k_ref, v_ref, bias_ref, o_ref, kb_ref, vb_ref = refs
    elif mode == "slc":
        q_ref, k_ref, v_ref, sel_ref, expand_ref, gate_ref, o_ref, kb_ref, vb_ref, selx_ref = refs
    else:
        q_ref, k_ref, v_ref, gate_ref, o_ref, kb_ref, vb_ref = refs
    unit = pl.program_id(1)
    qi = pl.program_id(2)

    @pl.when(qi == 0)
    def _():
        kb_ref[...] = k_ref[0].astype(BF16)
        vb_ref[...] = v_ref[0].astype(BF16)

    if mode == "slc":
        sel = sel_ref[0, 0].astype(BF16)
        for j in range(n_kv):
            selx_ref[j] = jnp.dot(sel, expand_ref[:, j * tk:(j + 1) * tk], preferred_element_type=F32)

    q_pos = qi * tq + lax.broadcasted_iota(jnp.int32, (tq, tk), 0)
    col = lax.broadcasted_iota(jnp.int32, (tq, tk), 1)
    qs = [q_ref[0, :, r * HEAD_DIM:(r + 1) * HEAD_DIM] for r in range(heads)]

    def block(j, carry, diagonal):
        ks = pl.multiple_of(j * tk, tk)
        k_pos = ks + col
        ok = None
        if mode == "slc":
            ok = selx_ref[j] > 0.5
        if mode == "win":
            ok = (k_pos <= q_pos) & (k_pos > q_pos - WINDOW)
        elif diagonal:
            ok = (k_pos <= q_pos) if ok is None else ok & (k_pos <= q_pos)
        out = []
        for r in range(heads):
            m, l, acc = carry[r]
            c0 = r * HEAD_DIM if mode == "fox" else 0
            kb = kb_ref[pl.ds(ks, tk), c0:c0 + HEAD_DIM]
            vb = vb_ref[pl.ds(ks, tk), c0:c0 + HEAD_DIM]
            s = lax.dot_general(qs[r], kb, _NT, preferred_element_type=F32)
            if mode == "fox":
                s = s + bias_ref[0, r:r + 1, pl.ds(ks, tk)]
            if ok is not None:
                s = jnp.where(ok, s, NEG_INF)
            m_new = jnp.maximum(m, jnp.max(s, axis=-1, keepdims=True))
            alpha = jnp.exp(m - m_new)
            p = jnp.exp(s - m_new)
            l = alpha * l + jnp.sum(p, axis=-1, keepdims=True)
            acc = alpha * acc + jnp.dot(p.astype(BF16), vb, preferred_element_type=F32)
            out.append((m_new, l, acc))
        return tuple(out)

    init = tuple((jnp.full((tq, 1), NEG_INF, F32), jnp.zeros((tq, 1), F32), jnp.zeros((tq, HEAD_DIM), F32))
                 for _ in range(heads))
    first_diag = (qi * tq) // tk
    n_diag = -(-tq // tk)
    if mode == "win":
        lo = jnp.maximum(qi * tq - WINDOW + 1, 0) // tk
        state = lax.fori_loop(lo, first_diag + n_diag, functools.partial(block, diagonal=False), init)
    else:
        state = lax.fori_loop(0, first_diag, functools.partial(block, diagonal=False), init)
        for dj in range(n_diag):
            state = block(first_diag + dj, state, True)

    for r in range(heads):
        m, l, acc = state[r]
        w = 1.0 / l
        if mode != "fox":
            gate = gate_ref[0]
            lane = lax.broadcasted_iota(jnp.int32, gate.shape, 1)
            want = FOX_HEADS + (unit * heads + r) * N_NSA_BRANCHES + gate_col
            w = w * jnp.sum(jnp.where(lane == want, gate, 0.0), axis=-1, keepdims=True)
        o_ref[0, :, r * HEAD_DIM:(r + 1) * HEAD_DIM] = (acc * w).astype(o_ref.dtype)


def _flash_attention(mode, q, k, v, k_col0, t, *, bias=None, sel=None, gate=None, gate_col=0):
    b = q.shape[0]
    heads = HEADS_PER_GROUP
    qw = heads * HEAD_DIM
    units = q.shape[2] // qw
    kvw = qw if mode == "fox" else HEAD_DIM
    tq = _row_tile(t, FLASH_QUERY_BLOCK)
    tk = _row_tile(t, FLASH_KEY_BLOCK)
    n_kv = t // tk
    in_specs = [
        pl.BlockSpec((1, tq, qw), lambda bi, u, i: (bi, i, u)),
        pl.BlockSpec((1, t, kvw), lambda bi, u, i: (bi, 0, k_col0 + u)),
        pl.BlockSpec((1, t, kvw), lambda bi, u, i: (bi, 0, k_col0 + u)),
    ]
    args = [q, k, v]
    scratch = [pltpu.VMEM((t, kvw), BF16), pltpu.VMEM((t, kvw), BF16)]
    if mode == "fox":
        in_specs.append(pl.BlockSpec((1, heads, t), lambda bi, u, i: (bi * units + u, 0, 0)))
        args.append(bias.reshape(b * units, heads, t))
    else:
        if mode == "slc":
            nsp = sel.shape[-1]
            expand = (np.arange(nsp)[:, None] == (np.arange(t)[None, :] // SEL_BLOCK)).astype(np.float32)
            in_specs.append(pl.BlockSpec((1, 1, tq, nsp), lambda bi, u, i: (bi, u, i, 0)))
            in_specs.append(pl.BlockSpec((nsp, t), lambda bi, u, i: (0, 0)))
            args += [sel, jnp.asarray(expand, BF16)]
            scratch.append(pltpu.VMEM((n_kv, tq, tk), F32))
        in_specs.append(pl.BlockSpec((1, tq, LANES), lambda bi, u, i: (bi, i, 0)))
        args.append(gate)
    return pl.pallas_call(
        functools.partial(_flash_body, mode, heads, tq, tk, n_kv, gate_col),
        grid=(b, units, t // tq),
        in_specs=in_specs,
        out_specs=pl.BlockSpec((1, tq, qw), lambda bi, u, i: (bi, i, u)),
        out_shape=jax.ShapeDtypeStruct((b, t, units * qw), BF16),
        scratch_shapes=scratch,
        compiler_params=_cparams("parallel", "parallel", "arbitrary"),
        name="attn_" + mode,
    )(*args)


PAGED_BUFFER_BYTES = 2 * 1024 * 1024
PAGED_COPY_BYTES = 128 * 1024
CMP_BUFFER_BYTES = 4 * 1024 * 1024


def _pages_per_step(n_pages, page_bytes, buffer_bytes=PAGED_BUFFER_BYTES):
    pg = max(1, min(n_pages, buffer_bytes // page_bytes))
    while n_pages % pg:
        pg -= 1
    return pg


def _paged_body(units, rows, rows_b, n_pages, pg, page, use_gate, *refs):
    if use_gate:
        (tbl_ref, q_ref, bias_ref, kn_ref, vn_ref, biasn_ref, gate_ref, kpool, vpool, o_ref,
         kbuf, vbuf, sem, m_ref, l_ref, acc_ref) = refs
    else:
        (tbl_ref, q_ref, bias_ref, kn_ref, vn_ref, biasn_ref, kpool, vpool, o_ref,
         kbuf, vbuf, sem, m_ref, l_ref, acc_ref) = refs
        gate_ref = None
    bi = pl.program_id(0)
    s = pl.program_id(1)
    n_steps = n_pages // pg
    rpp = page * units
    n_keys = pg * page

    n_chunks = max(1, (rpp * HEAD_DIM * 4) // PAGED_COPY_BYTES)
    crows = rpp // n_chunks

    def copies(step, slot):
        out = []
        for k in range(pg):
            phys = tbl_ref[bi * n_pages + step * pg + k]
            for c in range(n_chunks):
                src = pl.ds(c * crows, crows)
                dst = pl.ds(k * rpp + c * crows, crows)
                out.append(pltpu.make_async_copy(kpool.at[phys, src], kbuf.at[slot, dst], sem.at[0, slot]))
                out.append(pltpu.make_async_copy(vpool.at[phys, src], vbuf.at[slot, dst], sem.at[1, slot]))
        return out

    @pl.when(s == 0)
    def _():
        m_ref[...] = jnp.full(m_ref.shape, NEG_INF, F32)
        l_ref[...] = jnp.zeros(l_ref.shape, F32)
        acc_ref[...] = jnp.zeros(acc_ref.shape, F32)
        for c in copies(0, 0):
            c.start()

    @pl.when(s + 1 < n_steps)
    def _():
        for c in copies(s + 1, lax.rem(s + 1, 2)):
            c.start()

    def attend(k, v, bias, own_blocks):
        k_hi, k_lo = _split_bf16(k)
        v_hi, v_lo = _split_bf16(v)
        sc = _dot_split(q_ref[0], k_hi, k_lo, _NT)
        nk = k.shape[0] // units
        if own_blocks:
            sc = jnp.concatenate([sc[u * rows:(u + 1) * rows, u * nk:(u + 1) * nk] for u in range(units)], axis=0)
        sc = sc + bias
        m = m_ref[...]
        m_new = jnp.maximum(m, jnp.max(sc, axis=-1, keepdims=True))
        alpha = jnp.exp(m - m_new)
        pr = jnp.exp(sc - m_new)
        l_ref[...] = alpha * l_ref[...] + jnp.sum(pr, axis=-1, keepdims=True)
        if own_blocks:
            zero = jnp.zeros((rows, nk), F32)
            pr = jnp.concatenate(
                [jnp.concatenate([pr[u * rows:(u + 1) * rows] if u2 == u else zero for u2 in range(units)], axis=1)
                 for u in range(units)], axis=0)
        acc_ref[...] = alpha * acc_ref[...] + _dot_split(pr, v_hi, v_lo, _NN)
        m_ref[...] = m_new

    def unit_bias(u):
        bb = bias_ref[0, u]
        if rows_b == 1:
            return jnp.broadcast_to(bb, (rows, n_keys))
        return jnp.concatenate([bb] * (rows // rows_b), axis=0)

    @pl.when(s < n_steps)
    def _():
        slot = lax.rem(s, 2)
        for c in copies(s, slot):
            c.wait()
        k = jnp.concatenate([kbuf[slot, pl.ds(u, n_keys, stride=units), :] for u in range(units)], axis=0)
        v = jnp.concatenate([vbuf[slot, pl.ds(u, n_keys, stride=units), :] for u in range(units)], axis=0)
        attend(k, v, jnp.concatenate([unit_bias(u) for u in range(units)], axis=0), True)

    @pl.when(s == n_steps)
    def _():
        k = jnp.concatenate([kn_ref[0, :, u * HEAD_DIM:(u + 1) * HEAD_DIM] for u in range(units)], axis=0)
        v = jnp.concatenate([vn_ref[0, :, u * HEAD_DIM:(u + 1) * HEAD_DIM] for u in range(units)], axis=0)
        attend(k, v, biasn_ref[0], False)
        w = 1.0 / l_ref[...]
        if use_gate:
            w = w * gate_ref[0]
        o_ref[0] = acc_ref[...] * w


def _paged_attention(q, k_pool, v_pool, table, bias, k_new, v_new, bias_new, gate=None):
    b, units, rows, _ = q.shape
    ur = units * rows
    n_pages = table.shape[1]
    rpp = k_pool.shape[1]
    page = rpp // units
    rows_b = bias.shape[2]
    n_new = k_new.shape[1]
    width = k_new.shape[2]
    pg = _pages_per_step(n_pages, rpp * HEAD_DIM * 4)
    n_steps = n_pages // pg
    last = n_steps - 1
    same_unit = jnp.arange(units)[:, None, None, None] == jnp.arange(units)[None, None, :, None]
    bias_new_full = jnp.where(same_unit[None], bias_new[:, :, :, None, :], NEG_INF).reshape(b, ur, units * n_new)
    in_specs = [
        pl.BlockSpec((1, ur, HEAD_DIM), lambda bi, p, tbl: (bi, 0, 0)),
        pl.BlockSpec((1, units, rows_b, pg * page), lambda bi, p, tbl: (bi, 0, 0, jnp.minimum(p, last))),
        pl.BlockSpec((1, n_new, width), lambda bi, p, tbl: (bi, 0, 0)),
        pl.BlockSpec((1, n_new, width), lambda bi, p, tbl: (bi, 0, 0)),
        pl.BlockSpec((1, ur, units * n_new), lambda bi, p, tbl: (bi, 0, 0)),
    ]
    args = [q.reshape(b, ur, HEAD_DIM), bias, k_new, v_new, bias_new_full]
    if gate is not None:
        in_specs.append(pl.BlockSpec((1, ur, 1), lambda bi, p, tbl: (bi, 0, 0)))
        args.append(gate.reshape(b, ur, 1))
    in_specs += [pl.BlockSpec(memory_space=pl.ANY), pl.BlockSpec(memory_space=pl.ANY)]
    args += [k_pool, v_pool]
    grid_spec = pltpu.PrefetchScalarGridSpec(
        num_scalar_prefetch=1,
        grid=(b, n_steps + 1),
        in_specs=in_specs,
        out_specs=pl.BlockSpec((1, ur, HEAD_DIM), lambda bi, p, tbl: (bi, 0, 0)),
        scratch_shapes=[pltpu.VMEM((2, pg * rpp, HEAD_DIM), F32), pltpu.VMEM((2, pg * rpp, HEAD_DIM), F32),
                        pltpu.SemaphoreType.DMA((2, 2)),
                        pltpu.VMEM((ur, 1), F32), pltpu.VMEM((ur, 1), F32), pltpu.VMEM((ur, HEAD_DIM), F32)],
    )
    out = pl.pallas_call(
        functools.partial(_paged_body, units, rows, rows_b, n_pages, pg, page, gate is not None),
        grid_spec=grid_spec,
        out_shape=jax.ShapeDtypeStruct((b, ur, HEAD_DIM), F32),
        compiler_params=_cparams("arbitrary", "arbitrary"),
        name="attn_paged",
    )(table.reshape(-1), *args)
    return out.reshape(b, units, rows, HEAD_DIM)


def _cmp_proj_body(x_ref, pa_ref, pb_ref, wa_ref, wb_ref, za_ref, zb_ref):
    x = x_ref[0]
    za_ref[0] = jnp.dot((x + pa_ref[...]).astype(BF16), wa_ref[...], preferred_element_type=F32)
    zb_ref[0] = jnp.dot((x + pb_ref[...]).astype(BF16), wb_ref[...], preferred_element_type=F32)


def _cmp_mlp_body(n_chunks, za_ref, zb_ref, w2_ref, o_ref):
    zb_next = pltpu.roll(zb_ref[0], n_chunks - 1, 0)
    hid = za_ref[0] + zb_next
    c = 0.7978845608028654
    hid = 0.5 * hid * (1.0 + jnp.tanh(c * (hid + 0.044715 * (hid * hid * hid))))
    w2 = w2_ref[...]
    hb = hid.astype(w2.dtype)
    for g in range(NSA_KV_GROUPS):
        sl = slice(g * HEAD_DIM, (g + 1) * HEAD_DIM)
        o_ref[0, :, sl] = jnp.dot(hb[:, sl], w2, preferred_element_type=F32, precision=_dot_precision(w2.dtype))


def _compress(rows, pos_emb, w1, w2):
    b, tc, gw = rows.shape
    g = NSA_KV_GROUPS
    n_chunks = tc // CMP_STRIDE
    cw = CMP_STRIDE * gw
    x = rows.reshape(b, n_chunks, cw)
    w1r = w1.reshape(2, CMP_STRIDE, HEAD_DIM, HEAD_DIM)
    eye = jnp.eye(g, dtype=w1.dtype)
    wcat = jnp.einsum("hldj,ge->hlgdej", w1r, eye).reshape(2, cw, g * HEAD_DIM).astype(BF16)
    pos = jnp.broadcast_to(pos_emb.reshape(2, CMP_STRIDE, 1, HEAD_DIM), (2, CMP_STRIDE, g, HEAD_DIM)).reshape(2, 1, cw)
    cm = _row_tile(n_chunks, 128)
    za, zb = pl.pallas_call(
        _cmp_proj_body,
        grid=(b, n_chunks // cm),
        in_specs=[
            pl.BlockSpec((1, cm, cw), lambda bi, i: (bi, i, 0)),
            pl.BlockSpec((1, cw), lambda bi, i: (0, 0)),
            pl.BlockSpec((1, cw), lambda bi, i: (0, 0)),
            pl.BlockSpec((cw, g * HEAD_DIM), lambda bi, i: (0, 0)),
            pl.BlockSpec((cw, g * HEAD_DIM), lambda bi, i: (0, 0)),
        ],
        out_specs=[pl.BlockSpec((1, cm, g * HEAD_DIM), lambda bi, i: (bi, i, 0))] * 2,
        out_shape=[jax.ShapeDtypeStruct((b, n_chunks, g * HEAD_DIM), F32)] * 2,
        compiler_params=_cparams("parallel", "parallel"),
        name="cmp_proj",
    )(x, pos[0], pos[1], wcat[0], wcat[1])
    return _cmp_mlp(za, zb, w2.astype(BF16))


def _cmp_mlp(za, zb, w2):
    b, n_chunks, gw = za.shape
    g = NSA_KV_GROUPS
    return pl.pallas_call(
        functools.partial(_cmp_mlp_body, n_chunks),
        grid=(b,),
        in_specs=[
            pl.BlockSpec((1, n_chunks, g * HEAD_DIM), lambda bi: (bi, 0, 0)),
            pl.BlockSpec((1, n_chunks, g * HEAD_DIM), lambda bi: (bi, 0, 0)),
            pl.BlockSpec((HEAD_DIM, HEAD_DIM), lambda bi: (0, 0)),
        ],
        out_specs=pl.BlockSpec((1, n_chunks, g * HEAD_DIM), lambda bi: (bi, 0, 0)),
        out_shape=jax.ShapeDtypeStruct((b, n_chunks, g * HEAD_DIM), F32),
        compiler_params=_cparams("parallel"),
        name="cmp_mlp",
    )(za, zb, w2)


def _split_bf16(x):
    hi = x.astype(BF16)
    return hi, (x - hi.astype(F32)).astype(BF16)


def _dot_split(a, b_hi, b_lo, dims):
    a_hi, a_lo = _split_bf16(a)
    n = a.shape[0]
    both = lax.dot_general(jnp.concatenate([a_hi, a_lo], axis=0), b_hi, dims, preferred_element_type=F32)
    return both[:n] + both[n:] + lax.dot_general(a_hi, b_lo, dims, preferred_element_type=F32)


_NT = (((1,), (1,)), ((), ()))
_NN = (((1,), (0,)), ((), ()))


def _cmp_paged_body(n_pages, pg, page, tbl_ref, pos_ref, w_ref, pool, za_ref, zb_ref, xbuf, whi_ref, wlo_ref, sem):
    bi = pl.program_id(0)
    s = pl.program_id(1)
    n_steps = n_pages // pg
    g_n = NSA_KV_GROUPS
    rpp = page * g_n
    n_rows = pg * page // CMP_STRIDE

    def copies(step, slot):
        return [pltpu.make_async_copy(pool.at[tbl_ref[bi * n_pages + step * pg + k]],
                                      xbuf.at[slot, pl.ds(k * rpp, rpp)], sem.at[slot]) for k in range(pg)]

    @pl.when(s == 0)
    def _():
        for c in copies(0, 0):
            c.start()
        w_hi, w_lo = _split_bf16(w_ref[...])
        whi_ref[...] = w_hi
        wlo_ref[...] = w_lo

    @pl.when(s + 1 < n_steps)
    def _():
        for c in copies(s + 1, lax.rem(s + 1, 2)):
            c.start()

    slot = lax.rem(s, 2)
    for c in copies(s, slot):
        c.wait()
    w_hi = whi_ref[...]
    w_lo = wlo_ref[...]
    pw = _dot_split(pos_ref[...], w_hi, w_lo, _NN)
    col = lax.broadcasted_iota(jnp.int32, (1, 2 * HEAD_DIM), 1)
    pos_term = jnp.where(col < HEAD_DIM, pw[0:1, :], pw[1:2, :])
    x_cat = jnp.concatenate(
        [jnp.concatenate([xbuf[slot, pl.ds(l * g_n + g, n_rows, stride=CMP_STRIDE * g_n), :]
                          for l in range(CMP_STRIDE)], axis=1) for g in range(g_n)], axis=0)
    z = _dot_split(x_cat, w_hi, w_lo, _NN) + pos_term
    for g in range(g_n):
        za_ref[0, :, g * HEAD_DIM:(g + 1) * HEAD_DIM] = z[g * n_rows:(g + 1) * n_rows, :HEAD_DIM]
        zb_ref[0, :, g * HEAD_DIM:(g + 1) * HEAD_DIM] = z[g * n_rows:(g + 1) * n_rows, HEAD_DIM:]


def _compress_paged(pool, table, pos_emb, w1, w2):
    b, n_pages = table.shape
    rpp = pool.shape[1]
    page = rpp // NSA_KV_GROUPS
    gw = NSA_KV_GROUPS * HEAD_DIM
    pg = _pages_per_step(n_pages, rpp * HEAD_DIM * 4, CMP_BUFFER_BYTES)
    n_steps = n_pages // pg
    n_rows = pg * page // CMP_STRIDE
    n_chunks = n_pages * page // CMP_STRIDE
    half = CMP_STRIDE * HEAD_DIM
    w_cat = jnp.concatenate([w1[:half], w1[half:]], axis=1)
    pos_rows = jnp.pad(pos_emb.reshape(2, half), ((0, SUBLANES - 2), (0, 0)))
    grid_spec = pltpu.PrefetchScalarGridSpec(
        num_scalar_prefetch=1,
        grid=(b, n_steps),
        in_specs=[pl.BlockSpec((SUBLANES, half), lambda bi, i, tbl: (0, 0)),
                  pl.BlockSpec((half, 2 * HEAD_DIM), lambda bi, i, tbl: (0, 0)),
                  pl.BlockSpec(memory_space=pl.ANY)],
        out_specs=[pl.BlockSpec((1, n_rows, gw), lambda bi, i, tbl: (bi, i, 0))] * 2,
        scratch_shapes=[pltpu.VMEM((2, pg * rpp, HEAD_DIM), F32), pltpu.VMEM((half, 2 * HEAD_DIM), BF16),
                        pltpu.VMEM((half, 2 * HEAD_DIM), BF16), pltpu.SemaphoreType.DMA((2,))],
    )
    za, zb = pl.pallas_call(
        functools.partial(_cmp_paged_body, n_pages, pg, page),
        grid_spec=grid_spec,
        out_shape=[jax.ShapeDtypeStruct((b, n_chunks, gw), F32)] * 2,
        compiler_params=_cparams("arbitrary", "arbitrary"),
        name="cmp_proj_paged",
    )(table.reshape(-1), pos_rows, w_cat, pool)
    return _cmp_mlp(za, zb, w2)


def _cmp_attn_body(tq, nc, ns, n_pick, q_start, q_ref, kc_ref, vc_ref, cover_ref, gate_ref, o_ref, sel_ref):
    unit = pl.program_id(1)
    qi = pl.program_id(2)
    precise = q_ref.dtype == F32
    if precise:
        kc_hi, kc_lo = _split_bf16(kc_ref[0])
        vc_hi, vc_lo = _split_bf16(vc_ref[0])
    else:
        kc_hi = kc_ref[0].astype(BF16)
        vc_hi = vc_ref[0].astype(BF16)
    ncp = kc_hi.shape[0]
    q_pos = q_start + qi * tq + lax.broadcasted_iota(jnp.int32, (tq, ncp), 0)
    n_idx = lax.broadcasted_iota(jnp.int32, (tq, ncp), 1)
    visible = (n_idx * CMP_STRIDE + (CMP_BLOCK - 1) <= q_pos) & (n_idx < nc)
    gate = gate_ref[0]
    glane = lax.broadcasted_iota(jnp.int32, gate.shape, 1)
    p_grp = jnp.zeros((tq, ncp), F32)
    for r in range(HEADS_PER_GROUP):
        q = q_ref[0, :, r * HEAD_DIM:(r + 1) * HEAD_DIM]
        if precise:
            s = _dot_split(q, kc_hi, kc_lo, _NT)
        else:
            s = lax.dot_general(q, kc_hi, _NT, preferred_element_type=F32)
        s = jnp.where(visible, s, NEG_INF)
        m = jnp.max(s, axis=-1, keepdims=True)
        e = jnp.where(visible, jnp.exp(s - m), 0.0)
        l = jnp.sum(e, axis=-1, keepdims=True)
        p = e / jnp.where(l > 0.0, l, 1.0)
        p_grp = p_grp + p
        want = FOX_HEADS + (unit * HEADS_PER_GROUP + r) * N_NSA_BRANCHES
        gcol = jnp.sum(jnp.where(glane == want, gate, 0.0), axis=-1, keepdims=True)
        if precise:
            o = _dot_split(p, vc_hi, vc_lo, _NN)
        else:
            o = jnp.dot(p.astype(BF16), vc_hi, preferred_element_type=F32)
        o_ref[0, :, r * HEAD_DIM:(r + 1) * HEAD_DIM] = (o * gcol).astype(o_ref.dtype)

    imp = jnp.dot(p_grp, cover_ref[...], preferred_element_type=F32, precision=HIGHEST)
    nsp = imp.shape[1]
    j = lax.broadcasted_iota(jnp.int32, (tq, nsp), 1)
    cur = (q_start + qi * tq + lax.broadcasted_iota(jnp.int32, (tq, nsp), 0)) // SEL_BLOCK
    dist = cur - j
    forced = (j == 0) | ((dist >= 0) & (dist < N_LOCAL_SEL))
    score = jnp.where(forced, FORCE_SCORE, imp)
    score = jnp.where(dist >= 0, score, NEG_INF)
    score = jnp.where(j < ns, score, PAD_SCORE)
    sel = jnp.zeros((tq, nsp), F32)
    j_f = j.astype(F32)
    for _ in range(n_pick):
        mx = jnp.max(score, axis=-1, keepdims=True)
        first = jnp.min(jnp.where(score == mx, j_f, float(nsp)), axis=-1, keepdims=True)
        hit = j_f == first
        sel = jnp.where(hit, 1.0, sel)
        score = jnp.where(hit, TAKEN_SCORE, score)
    sel_ref[0, 0] = sel


def _cmp_attention(nq, kc, vc, gate, t, q_start, n_keys, out_dtype):
    b = nq.shape[0]
    ncp = kc.shape[1]
    nc = (n_keys - CMP_BLOCK) // CMP_STRIDE + 1
    ns = -(-n_keys // SEL_BLOCK)
    nsp = -(-ns // LANES) * LANES
    n_pick = min(N_SELECT, ns)
    c0 = np.arange(ncp)[:, None] * CMP_STRIDE
    s0 = np.arange(nsp)[None, :] * SEL_BLOCK
    cover = np.clip(np.minimum(c0 + CMP_BLOCK, s0 + SEL_BLOCK) - np.maximum(c0, s0), 0, None).astype(np.float32) / CMP_BLOCK
    cover = cover * (np.arange(ncp)[:, None] < nc) * (np.arange(nsp)[None, :] < ns)
    tq = _row_tile(t, 256)
    qw = HEADS_PER_GROUP * HEAD_DIM
    return pl.pallas_call(
        functools.partial(_cmp_attn_body, tq, nc, ns, n_pick, q_start),
        grid=(b, NSA_KV_GROUPS, t // tq),
        in_specs=[
            pl.BlockSpec((1, tq, qw), lambda bi, u, i: (bi, i, u)),
            pl.BlockSpec((1, ncp, HEAD_DIM), lambda bi, u, i: (bi, 0, u)),
            pl.BlockSpec((1, ncp, HEAD_DIM), lambda bi, u, i: (bi, 0, u)),
            pl.BlockSpec((ncp, nsp), lambda bi, u, i: (0, 0)),
            pl.BlockSpec((1, tq, LANES), lambda bi, u, i: (bi, i, 0)),
        ],
        out_specs=[
            pl.BlockSpec((1, tq, qw), lambda bi, u, i: (bi, i, u)),
            pl.BlockSpec((1, 1, tq, nsp), lambda bi, u, i: (bi, u, i, 0)),
        ],
        out_shape=[
            jax.ShapeDtypeStruct((b, t, NSA_KV_GROUPS * qw), out_dtype),
            jax.ShapeDtypeStruct((b, NSA_KV_GROUPS, t, nsp), F32),
        ],
        compiler_params=_cparams("parallel", "parallel", "parallel"),
        name="attn_cmp_select",
    )(nq, kc, vc, jnp.asarray(cover), gate)


def _merge_body(a_ref, b1_ref, b2_ref, b3_ref, ua_ref, ub_ref, ga_ref, gb_ref, o_ref):
    a = a_ref[...]
    prec = _dot_precision(a.dtype)
    bsum = (b1_ref[...].astype(F32) + b2_ref[...].astype(F32) + b3_ref[...].astype(F32)).astype(a.dtype)
    ya = jnp.dot(a, ua_ref[...], preferred_element_type=F32, precision=prec)
    yb = jnp.dot(bsum, ub_ref[...], preferred_element_type=F32, precision=prec)
    o_ref[...] = (ga_ref[...] * ya + gb_ref[...] * yb).astype(o_ref.dtype)


def _merge(o_fox, o_cmp, o_slc, o_win, u_fox, u_nsa, mg, d_model):
    n, w = o_fox.shape
    tm = _row_tile(n, 512)
    tn = 512
    nj = d_model // tn
    row = pl.BlockSpec((tm, w), lambda i, j: (i, 0))
    return pl.pallas_call(
        _merge_body,
        grid=(n // tm, nj),
        in_specs=[row, row, row, row,
                  pl.BlockSpec((w, tn), lambda i, j: (0, j)),
                  pl.BlockSpec((w, tn), lambda i, j: (0, j)),
                  pl.BlockSpec((tm, tn), lambda i, j: (i, j)),
                  pl.BlockSpec((tm, tn), lambda i, j: (i, j + nj))],
        out_specs=pl.BlockSpec((tm, tn), lambda i, j: (i, j)),
        out_shape=jax.ShapeDtypeStruct((n, d_model), o_fox.dtype),
        compiler_params=_cparams("parallel", "parallel"),
        name="mixer_merge",
    )(o_fox, o_cmp, o_slc, o_win, u_fox, u_nsa, mg, mg)


def _router_body(h_ref, g_ref, w_ref, b_ref, u_ref, route_ref):
    x = h_ref[...]
    xn = x * lax.rsqrt(jnp.mean(x * x, axis=-1, keepdims=True) + NORM_EPS) * g_ref[...]
    u_ref[...] = xn.astype(u_ref.dtype)
    z = jnp.dot(xn, w_ref[...], preferred_element_type=F32, precision=HIGHEST) + b_ref[...]
    c = lax.broadcasted_iota(jnp.int32, z.shape, 1)
    big = z.shape[1]
    is_g = c < N_GROUPS
    gmax = jnp.max(jnp.where(is_g, z, -jnp.inf), axis=-1, keepdims=True)
    gsel = jnp.min(jnp.where(is_g & (z == gmax), c, big), axis=-1, keepdims=True)
    gsum = jnp.sum(jnp.where(is_g, jnp.exp(z - gmax), 0.0), axis=-1, keepdims=True)
    lo = N_GROUPS + gsel * EXPERTS_PER_GROUP
    in_grp = (c >= lo) & (c < lo + EXPERTS_PER_GROUP)
    z1 = jnp.max(jnp.where(in_grp, z, -jnp.inf), axis=-1, keepdims=True)
    c1 = jnp.min(jnp.where(in_grp & (z == z1), c, big), axis=-1, keepdims=True)
    rest = in_grp & (c != c1)
    z2 = jnp.max(jnp.where(rest, z, -jnp.inf), axis=-1, keepdims=True)
    c2 = jnp.min(jnp.where(rest & (z == z2), c, big), axis=-1, keepdims=True)
    e2 = jnp.exp(z2 - z1)
    p1 = 1.0 / (1.0 + e2)
    p2 = e2 / (1.0 + e2)
    gp = 1.0 / gsum
    out = jnp.where(c == 0, (c1 - N_GROUPS).astype(F32), 0.0)
    out = jnp.where(c == 1, (c2 - N_GROUPS).astype(F32), out)
    out = jnp.where(c == 2, gp * p1, out)
    out = jnp.where(c == 3, gp * p2, out)
    route_ref[...] = out


def _router(h, gain, w_cat, b_cat):
    n, d = h.shape
    tm = _row_tile(n, 512)
    return pl.pallas_call(
        _router_body,
        grid=(n // tm,),
        in_specs=[pl.BlockSpec((tm, d), lambda i: (i, 0)), pl.BlockSpec((1, d), lambda i: (0, 0)),
                  pl.BlockSpec((d, LANES), lambda i: (0, 0)), pl.BlockSpec((1, LANES), lambda i: (0, 0))],
        out_specs=[pl.BlockSpec((tm, d), lambda i: (i, 0)), pl.BlockSpec((tm, LANES), lambda i: (i, 0))],
        out_shape=[jax.ShapeDtypeStruct((n, d), F32), jax.ShapeDtypeStruct((n, LANES), F32)],
        compiler_params=_cparams("parallel"),
        name="moe_router",
    )(h, gain.reshape(1, d), w_cat, b_cat)


def _gather_rows(src_hbm, dst_ref, sem, idx_ref, base, count, start):
    if not start:
        pltpu.make_async_copy(src_hbm.at[pl.ds(0, count)], dst_ref, sem).wait()
        return

    def body(r, c):
        pltpu.make_async_copy(src_hbm.at[pl.ds(idx_ref[base + r], 1)], dst_ref.at[pl.ds(r, 1)], sem).start()
        return c

    lax.fori_loop(0, count, body, 0, unroll=8)


def _expert_body(tm, te_ref, nv_ref, tok_ref, wg_ref, wu_ref, wd_ref, x_hbm, o_ref, xbuf, sem):
    i = pl.program_id(0)
    nv = nv_ref[0]

    def rows_of(tile, start):
        slot = lax.rem(tile, 2)
        _gather_rows(x_hbm, xbuf.at[slot], sem.at[slot], tok_ref, tile * tm, tm, start)

    @pl.when(i == 0)
    def _():
        rows_of(0, True)

    @pl.when(i + 1 < nv)
    def _():
        rows_of(i + 1, True)

    @pl.when(i < nv)
    def _():
        rows_of(i, False)
        x = xbuf[lax.rem(i, 2)].astype(BF16)
        hg = jnp.dot(x, wg_ref[0].astype(BF16), preferred_element_type=F32)
        hu = jnp.dot(x, wu_ref[0].astype(BF16), preferred_element_type=F32)
        hid = (hg * jax.nn.sigmoid(hg)) * hu
        o_ref[...] = jnp.dot(hid.astype(BF16), wd_ref[0].astype(BF16), preferred_element_type=F32)

    @pl.when(i >= nv)
    def _():
        o_ref[...] = jnp.zeros(o_ref.shape, o_ref.dtype)


def _experts(x, row_token, tile_expert, n_valid, wg, wu, wd, tm):
    rows = row_token.shape[0]
    d = x.shape[1]
    n_tiles = rows // tm
    f = wg.shape[2]

    def w_map(i, te, nv, tok):
        return (te[i], 0, 0)

    grid_spec = pltpu.PrefetchScalarGridSpec(
        num_scalar_prefetch=3,
        grid=(n_tiles,),
        in_specs=[pl.BlockSpec((1, d, f), w_map), pl.BlockSpec((1, d, f), w_map), pl.BlockSpec((1, f, d), w_map),
                  pl.BlockSpec(memory_space=pl.ANY)],
        out_specs=pl.BlockSpec((tm, d), lambda i, te, nv, tok: (i, 0)),
        scratch_shapes=[pltpu.VMEM((2, tm, d), F32), pltpu.SemaphoreType.DMA((2,))],
    )
    return pl.pallas_call(
        functools.partial(_expert_body, tm),
        grid_spec=grid_spec,
        out_shape=jax.ShapeDtypeStruct((rows, d), F32),
        compiler_params=_cparams("arbitrary"),
        name="moe_experts",
    )(tile_expert, n_valid, row_token, wg, wu, wd, x)


def _combine_body(tm, n, slot_ref, h_ref, route_ref, src_hbm, o_ref, buf, sem):
    i = pl.program_id(0)
    n_tiles = pl.num_programs(0)

    def rows_of(tile, start):
        slot = lax.rem(tile, 2)
        for k in range(2):
            _gather_rows(src_hbm, buf.at[slot, k], sem.at[slot], slot_ref, k * n + tile * tm, tm, start)

    @pl.when(i == 0)
    def _():
        rows_of(0, True)

    @pl.when(i + 1 < n_tiles)
    def _():
        rows_of(i + 1, True)

    rows_of(i, False)
    slot = lax.rem(i, 2)
    route = route_ref[...]
    o_ref[...] = h_ref[...] + route[:, 2:3] * buf[slot, 0] + route[:, 3:4] * buf[slot, 1]


def _moe_combine(h, route, out_sorted, slot_kn):
    n, d = h.shape
    tm = _row_tile(n, 256)
    grid_spec = pltpu.PrefetchScalarGridSpec(
        num_scalar_prefetch=1,
        grid=(n // tm,),
        in_specs=[pl.BlockSpec((tm, d), lambda i, sl: (i, 0)), pl.BlockSpec((tm, LANES), lambda i, sl: (i, 0)),
                  pl.BlockSpec(memory_space=pl.ANY)],
        out_specs=pl.BlockSpec((tm, d), lambda i, sl: (i, 0)),
        scratch_shapes=[pltpu.VMEM((2, 2, tm, d), F32), pltpu.SemaphoreType.DMA((2,))],
    )
    return pl.pallas_call(
        functools.partial(_combine_body, tm, n),
        grid_spec=grid_spec,
        out_shape=jax.ShapeDtypeStruct((n, d), F32),
        compiler_params=_cparams("arbitrary"),
        name="moe_combine",
    )(slot_kn, h, route, out_sorted)


def _moe(h, gain, w_router, b_router, wg, wu, wd):
    n, d = h.shape
    ut, route = _router(h, gain, w_router, b_router)
    expert = route[:, :2].astype(jnp.int32).reshape(-1)
    n_asg = 2 * n
    tm = 256 if n_asg >= 8192 else 16
    n_tiles = n_asg // tm + N_EXPERTS
    onehot = (expert[:, None] == jnp.arange(N_EXPERTS, dtype=jnp.int32)[None, :]).astype(jnp.int32)
    running = jnp.cumsum(onehot, axis=0)
    counts = running[-1]
    rank = jnp.sum(onehot * running, axis=1) - 1
    tiles_per = (counts + tm - 1) // tm
    tile_end = jnp.cumsum(tiles_per)
    start_row = (tile_end - tiles_per) * tm
    slot = (jnp.sum(onehot * start_row[None, :], axis=1) + rank).astype(jnp.int32)
    row_token = jnp.zeros((n_tiles * tm,), jnp.int32).at[slot].set(jnp.arange(n_asg, dtype=jnp.int32) // 2)
    n_valid = tile_end[-1].astype(jnp.int32)
    tile_ids = jnp.minimum(jnp.arange(n_tiles, dtype=jnp.int32), n_valid - 1)
    tile_expert = jnp.sum((tile_end[None, :] <= tile_ids[:, None]).astype(jnp.int32), axis=1)
    out_sorted = _experts(ut, row_token, tile_expert, n_valid.reshape(1), wg, wu, wd, tm)
    return _moe_combine(h, route, out_sorted, jnp.transpose(slot.reshape(n, 2)).reshape(-1))


def _rope_tables(positions):
    half = HEAD_DIM // 2
    inv = ROPE_THETA ** (-np.arange(half, dtype=np.float64) / half)
    ang = np.asarray(positions, np.float64)[:, None] * inv[None, :]
    cos = np.concatenate([np.cos(ang), np.cos(ang)], axis=-1)
    sin = np.concatenate([-np.sin(ang), np.sin(ang)], axis=-1)
    return jnp.asarray(cos, F32), jnp.asarray(sin, F32)


def _project(u, wts, p, b, t, q_start):
    n = u.shape[0]
    tm = _row_tile(n, 1024)
    tn = 256
    scale = HEAD_DIM ** -0.5
    gain_spec = pl.BlockSpec((1, HEAD_DIM), lambda i, j: (0, 0))

    def run(w, epilogue, extras, specs, dtype, name, tn_=tn):
        shape = jax.ShapeDtypeStruct((n, w.shape[1]), dtype)
        spec = pl.BlockSpec((tm, tn_), lambda i, j: (i, j))
        return _matmul(u, w, epilogue, extras, specs, shape, spec, tm, tn_, name)

    cos, sin = _rope_tables(q_start + (np.arange(max(tm, t)) % t))
    n_tab = cos.shape[0] // tm
    tab_spec = pl.BlockSpec((tm, HEAD_DIM), lambda i, j: (i % n_tab, 0))

    fq = run(wts["fq"], functools.partial(_ep_head_norm, scale), [p["fox_q_norm"]], [gain_spec], u.dtype, "proj_fq")
    fk = run(wts["fk"], functools.partial(_ep_head_norm, 1.0), [p["fox_k_norm"]], [gain_spec], F32, "proj_fk")
    fv = run(wts["fv"], _ep_identity, [], [], F32, "proj_fv")
    nq = run(wts["nq"], functools.partial(_ep_head_norm_rope, scale), [p["nsa_q_norm"], cos, sin],
             [gain_spec, tab_spec, tab_spec], u.dtype, "proj_nq")
    nk = run(wts["nk"], functools.partial(_ep_head_norm_rope, 1.0), [p["nsa_k_norm"], cos, sin],
             [gain_spec, tab_spec, tab_spec], F32, "proj_nk")
    nv = run(wts["nv"], _ep_identity, [], [], F32, "proj_nv")
    mg = run(wts["mg"], _ep_sigmoid, [], [], F32, "proj_mg")
    small = run(wts["small"], _ep_small, [p["small_bias"]], [pl.BlockSpec((1, LANES), lambda i, j: (0, 0))],
                F32, "proj_small", LANES)
    return fq, fk, fv, nq, nk, nv, mg, small


def _layer(x, p, wts, q_start, past):
    b, t, d = x.shape
    n = b * t
    g = NSA_KV_GROUPS
    kvw = g * HEAD_DIM
    x2d = x.reshape(n, d)
    dense = wts["f32"] if past is not None else wts
    u = _rmsnorm(x2d, p["norm_mix"], dense["fq"].dtype)
    fq, fk, fv, nq, nk, nv, mg, small = _project(u, dense, p, b, t, q_start)

    logf = small[:, :FOX_HEADS].reshape(b, t, FOX_HEADS)
    gate3 = small.reshape(b, t, LANES)
    fq3 = fq.reshape(b, t, -1)
    nq3 = nq.reshape(b, t, -1)
    fk3 = fk.reshape(b, t, -1)
    fv3 = fv.reshape(b, t, -1)
    nk3 = nk.reshape(b, t, 3 * kvw)
    nv3 = nv.reshape(b, t, 3 * kvw)
    ck, sk, wk = (nk3[:, :, i * kvw:(i + 1) * kvw] for i in range(3))
    cv, sv, wv = (nv3[:, :, i * kvw:(i + 1) * kvw] for i in range(3))

    if past is None:
        n_keys = t
        n_lp = t // LANES
        lf_pool = jnp.transpose(logf.reshape(b * n_lp, LANES, FOX_HEADS), (0, 2, 1))
        bias = _fox_key_bias(lf_pool, jnp.arange(b * n_lp, dtype=jnp.int32).reshape(b, n_lp), None)
        o_fox = _flash_attention("fox", fq3, fk3, fv3, 0, t, bias=bias)
        kc = _compress(ck, p["cmp_pos_k"], p["cmp_w1_k"], p["cmp_w2_k"])
        vc = _compress(cv, p["cmp_pos_v"], p["cmp_w1_v"], p["cmp_w2_v"])
        o_cmp, sel = _cmp_attention(nq3, kc, vc, gate3, t, q_start, n_keys, BF16)
        o_slc = _flash_attention("slc", nq3, nk3, nv3, g, t, sel=sel, gate=gate3, gate_col=1)
        o_win = _flash_attention("win", nq3, nk3, nv3, 2 * g, t, gate=gate3, gate_col=2)
        keep = min(WINDOW, t)
        win_k, win_v = wk[:, t - keep:], wv[:, t - keep:]
    else:
        table = past["table"]
        n_pages = table.shape[1]
        page = past["page"]
        past_len = n_pages * page
        n_keys = past_len + t
        new_page = jnp.pad(jnp.transpose(logf, (0, 2, 1)), ((0, 0), (0, 0), (0, page - t)))
        bias = _fox_key_bias(past["fox_logf"], table, new_page)[:, :, :n_keys]
        causal = jnp.where(jnp.arange(t)[None, :] <= jnp.arange(t)[:, None], 0.0, NEG_INF).astype(F32)
        bias_new = bias[:, :, None, past_len:] + causal[None, None]
        qf = jnp.transpose(fq3.reshape(b, t, FOX_HEADS, HEAD_DIM), (0, 2, 1, 3))
        o = _paged_attention(qf, past["fox_k"], past["fox_v"], table, bias[:, :, None, :past_len],
                             fk3, fv3, bias_new)
        o_fox = jnp.transpose(o, (0, 2, 1, 3)).reshape(b, t, -1)
        nc = (n_keys - CMP_BLOCK) // CMP_STRIDE + 1
        assert (nc + 1) * CMP_STRIDE <= past_len and page % CMP_STRIDE == 0
        kc = _compress_paged(past["cmp_k"], table, p["cmp_pos_k"], p["cmp_w1_k"], p["cmp_w2_k"])
        vc = _compress_paged(past["cmp_v"], table, p["cmp_pos_v"], p["cmp_w1_v"], p["cmp_w2_v"])
        o_cmp, sel = _cmp_attention(nq3, kc, vc, gate3, t, q_start, n_keys, F32)
        rows = HEADS_PER_GROUP * t
        qn = jnp.transpose(nq3.reshape(b, t, g, HEADS_PER_GROUP, HEAD_DIM), (0, 2, 3, 1, 4)).reshape(b, g, rows, HEAD_DIM)
        gates = small[:, FOX_HEADS:FOX_HEADS + NSA_HEADS * N_NSA_BRANCHES].reshape(b, t, g, HEADS_PER_GROUP, N_NSA_BRANCHES)
        gates = jnp.transpose(gates, (4, 0, 2, 3, 1)).reshape(N_NSA_BRANCHES, b, g, rows, 1)
        ns = -(-n_keys // SEL_BLOCK)
        key_sel = jnp.repeat(sel[..., :ns], SEL_BLOCK, axis=-1)[..., :n_keys]
        sel_bias = jnp.where(key_sel > 0.5, 0.0, NEG_INF).astype(F32)
        sel_new = jnp.tile(sel_bias[..., past_len:] + causal[None, None], (1, 1, HEADS_PER_GROUP, 1))
        o = _paged_attention(qn, past["slc_k"], past["slc_v"], table, sel_bias[..., :past_len],
                             sk, sv, sel_new, gate=gates[1])
        o_slc = _rows_to_tokens(o, b, t)
        win_len = past["win_k"].shape[1]
        assert win_len % page == 0
        k_start = q_start - win_len
        q_abs = q_start + jnp.arange(t)[:, None]
        k_abs = k_start + jnp.arange(win_len)[None, :]
        band = jnp.where((k_abs >= 0) & (k_abs > q_abs - WINDOW), 0.0, NEG_INF).astype(F32)
        band = jnp.broadcast_to(band[None, None], (b, g, t, win_len))
        band_new = jnp.where(jnp.arange(t)[None, :] > jnp.arange(t)[:, None] - WINDOW, causal, NEG_INF)
        band_new = jnp.broadcast_to(jnp.tile(band_new, (HEADS_PER_GROUP, 1))[None, None], (b, g, rows, t))
        o = _paged_attention(qn, past["win_k_pool"], past["win_v_pool"], past["win_table"], band, wk, wv,
                             band_new, gate=gates[2])
        o_win = _rows_to_tokens(o, b, t)
        wk_all = jnp.concatenate([past["win_k"], wk], axis=1)
        wv_all = jnp.concatenate([past["win_v"], wv], axis=1)
        keep = min(WINDOW, wk_all.shape[1])
        win_k, win_v = wk_all[:, -keep:], wv_all[:, -keep:]

    m = _merge(o_fox.reshape(n, -1), o_cmp.reshape(n, -1), o_slc.reshape(n, -1), o_win.reshape(n, -1),
               dense["up_fox"], dense["up_nsa"], mg, d)
    tm = _row_tile(n, 512)
    h = _matmul(m, dense["out"], _ep_residual, [x2d], [pl.BlockSpec((tm, 512), lambda i, j: (i, j))],
                jax.ShapeDtypeStruct((n, d), F32), pl.BlockSpec((tm, 512), lambda i, j: (i, j)), tm, 512, "out_proj")
    y = _moe(h, p["norm_ffn"], wts["router"], p["router_bias"], wts["gate_e"], wts["up_e"], wts["down_e"])

    def heads(a, hh):
        return a.reshape(b, -1, hh, HEAD_DIM)

    state = (heads(fk3, FOX_HEADS), heads(fv3, FOX_HEADS), logf, heads(ck, g), heads(cv, g), heads(sk, g),
             heads(sv, g), heads(win_k, g), heads(win_v, g))
    return y.reshape(b, t, d), state


def _rows_to_tokens(o, b, t):
    o = o.reshape(b, NSA_KV_GROUPS, HEADS_PER_GROUP, t, HEAD_DIM)
    return jnp.transpose(o, (0, 3, 1, 2, 4)).reshape(b, t, -1)


def _prepare(l, norm_mix, w_in, fox_f_bias, fox_q_norm, fox_k_norm, nsa_q_norm, nsa_k_norm, cmp_pos_k, cmp_w1_k,
             cmp_w2_k, cmp_pos_v, cmp_w1_v, cmp_w2_v, nsa_gate_bias, w_up_fox, w_up_nsa, w_out, norm_ffn,
             w_router_group, b_router_group, w_router_expert, b_router_expert, w_gate_e, w_up_e, w_down_e):
    d = w_in.shape[1]
    fw = FOX_HEADS * HEAD_DIM
    nw = NSA_HEADS * HEAD_DIM
    kvw = NSA_KV_GROUPS * HEAD_DIM
    sizes = (fw, fw, fw, FOX_HEADS, nw, kvw, kvw, kvw, kvw, kvw, kvw, NSA_HEADS * N_NSA_BRANCHES, 2 * d)
    offs = np.concatenate([[0], np.cumsum(sizes)])
    w = w_in[l]

    def seg(i):
        return w[:, offs[i]:offs[i + 1]]

    n_small = FOX_HEADS + NSA_HEADS * N_NSA_BRANCHES
    small_w = jnp.pad(jnp.concatenate([seg(3), seg(11)], axis=1), ((0, 0), (0, LANES - n_small)))
    small_b = jnp.pad(jnp.concatenate([fox_f_bias[l], nsa_gate_bias[l]]), (0, LANES - n_small)).reshape(1, LANES)
    n_route = N_GROUPS + N_EXPERTS
    router_w = jnp.pad(jnp.concatenate([w_router_group[l], w_router_expert[l]], axis=1), ((0, 0), (0, LANES - n_route)))
    router_b = jnp.pad(jnp.concatenate([b_router_group[l], b_router_expert[l]]), (0, LANES - n_route)).reshape(1, LANES)
    dense = {
        "fq": seg(0), "fk": seg(1), "fv": seg(2), "nq": seg(4),
        "nk": jnp.concatenate([seg(5), seg(7), seg(9)], axis=1),
        "nv": jnp.concatenate([seg(6), seg(8), seg(10)], axis=1),
        "mg": seg(12), "small": small_w,
        "up_fox": w_up_fox[l], "up_nsa": w_up_nsa[l], "out": w_out[l],
    }
    wts = {
        **{k: v.astype(BF16) for k, v in dense.items()},
        "f32": dense,
        "router": router_w,
        "gate_e": w_gate_e[l], "up_e": w_up_e[l], "down_e": w_down_e[l],
    }
    p = {
        "norm_mix": norm_mix[l], "fox_q_norm": fox_q_norm[l].reshape(1, -1), "fox_k_norm": fox_k_norm[l].reshape(1, -1),
        "nsa_q_norm": nsa_q_norm[l].reshape(1, -1), "nsa_k_norm": nsa_k_norm[l].reshape(1, -1),
        "small_bias": small_b, "cmp_pos_k": cmp_pos_k[l], "cmp_w1_k": cmp_w1_k[l], "cmp_w2_k": cmp_w2_k[l],
        "cmp_pos_v": cmp_pos_v[l], "cmp_w1_v": cmp_w1_v[l], "cmp_w2_v": cmp_w2_v[l],
        "norm_ffn": norm_ffn[l], "router_bias": router_b,
    }
    return wts, p


def kernel(x_prompt, x_sample, cache_fox_k, cache_fox_v, cache_fox_logf, cache_cmp_k, cache_cmp_v, cache_slc_k, cache_slc_v, cache_win_k, cache_win_v, page_table, norm_mix, w_in, fox_f_bias, fox_q_norm, fox_k_norm, nsa_q_norm, nsa_k_norm, cmp_pos_k, cmp_w1_k, cmp_w2_k, cmp_pos_v, cmp_w1_v, cmp_w2_v, nsa_gate_bias, w_up_fox, w_up_nsa, w_out, norm_ffn, w_router_group, b_router_group, w_router_expert, b_router_expert, w_gate_e, w_up_e, w_down_e):
    depth = w_in.shape[0]
    n_phys, page = cache_fox_k.shape[1], cache_fox_k.shape[2]
    past_len = page_table.shape[1] * page
    h_p, h_s = x_prompt, x_sample
    p_states, s_states = [], []
    for l in range(depth):
        wts, p = _prepare(l, norm_mix, w_in, fox_f_bias, fox_q_norm, fox_k_norm, nsa_q_norm, nsa_k_norm, cmp_pos_k,
                          cmp_w1_k, cmp_w2_k, cmp_pos_v, cmp_w1_v, cmp_w2_v, nsa_gate_bias, w_up_fox, w_up_nsa,
                          w_out, norm_ffn, w_router_group, b_router_group, w_router_expert, b_router_expert,
                          w_gate_e, w_up_e, w_down_e)
        h_p, st_p = _layer(h_p, p, wts, 0, None)
        p_states.append(st_p)
        win_len = cache_win_k.shape[2]
        n_win = win_len // page
        dec_b = page_table.shape[0]

        def pool(c):
            return c.reshape(-1, page * c.shape[3], HEAD_DIM)

        past = {
            "table": page_table + l * n_phys, "page": page,
            "fox_k": pool(cache_fox_k), "fox_v": pool(cache_fox_v),
            "fox_logf": jnp.transpose(cache_fox_logf.reshape(-1, page, cache_fox_logf.shape[3]), (0, 2, 1)),
            "cmp_k": pool(cache_cmp_k), "cmp_v": pool(cache_cmp_v),
            "slc_k": pool(cache_slc_k), "slc_v": pool(cache_slc_v),
            "win_k_pool": pool(cache_win_k), "win_v_pool": pool(cache_win_v),
            "win_table": (l * dec_b + jnp.arange(dec_b, dtype=jnp.int32))[:, None] * n_win
                         + jnp.arange(n_win, dtype=jnp.int32)[None, :],
            "win_k": cache_win_k[l].reshape(dec_b, win_len, -1), "win_v": cache_win_v[l].reshape(dec_b, win_len, -1),
        }
        h_s, st_s = _layer(h_s, p, wts, past_len, past)
        s_states.append(st_s)
    p_out = [jnp.stack([st[i] for st in p_states], axis=0) for i in range(9)]
    s_out = [jnp.stack([st[i] for st in s_states], axis=0) for i in range(9)]
    return (h_p, h_s, *p_out, *s_out)
```

```python
import functools

import numpy as np
import jax
import jax.numpy as jnp
from jax import lax
from jax.experimental import pallas as pl
from jax.experimental.pallas import tpu as pltpu

F32 = jnp.float32
BF16 = jnp.bfloat16

HEAD_DIM = 128
FOX_HEADS = 8
NSA_HEADS = 8
NSA_KV_GROUPS = 2
HEADS_PER_GROUP = NSA_HEADS // NSA_KV_GROUPS
N_NSA_BRANCHES = 3
CMP_BLOCK = 32
CMP_STRIDE = 16
SEL_BLOCK = 64
N_SELECT = 16
N_LOCAL_SEL = 2
WINDOW = 512
N_GROUPS = 4
EXPERTS_PER_GROUP = 8
N_EXPERTS = N_GROUPS * EXPERTS_PER_GROUP
ROPE_THETA = 10000.0
NORM_EPS = 1e-6
NEG_INF = -1e30
FORCE_SCORE = 1e9
PAD_SCORE = -2e38
TAKEN_SCORE = -3e38

LANES = 128
SUBLANES = 8
VMEM_LIMIT_BYTES = 48 * 1024 * 1024

HIGHEST = lax.Precision.HIGHEST


def _cparams(*sem):
    return pltpu.CompilerParams(dimension_semantics=tuple(sem), vmem_limit_bytes=VMEM_LIMIT_BYTES)


def _row_tile(n, pref):
    t = min(n, pref)
    assert n % t == 0, (n, t)
    return t


def _rmsnorm_body(x_ref, g_ref, o_ref):
    x = x_ref[...]
    y = x * lax.rsqrt(jnp.mean(x * x, axis=-1, keepdims=True) + NORM_EPS)
    o_ref[...] = (y * g_ref[...]).astype(o_ref.dtype)


def _rmsnorm(x2d, gain, out_dtype):
    n, d = x2d.shape
    tm = _row_tile(n, 512)
    return pl.pallas_call(
        _rmsnorm_body,
        grid=(n // tm,),
        in_specs=[pl.BlockSpec((tm, d), lambda i: (i, 0)), pl.BlockSpec((1, d), lambda i: (0, 0))],
        out_specs=pl.BlockSpec((tm, d), lambda i: (i, 0)),
        out_shape=jax.ShapeDtypeStruct((n, d), out_dtype),
        compiler_params=_cparams("parallel"),
        name="rmsnorm",
    )(x2d, gain.reshape(1, d))


def _dot_precision(dtype):
    return HIGHEST if dtype == F32 else None


def _mm_body(epilogue, n_extra, a_ref, w_ref, *rest):
    acc = jnp.dot(a_ref[...], w_ref[...], preferred_element_type=F32, precision=_dot_precision(a_ref.dtype))
    epilogue(acc, rest[:n_extra], rest[n_extra:])


def _matmul(a, w, epilogue, extras, extra_specs, out_shapes, out_specs, tm, tn, name):
    n, k = a.shape
    c = w.shape[1]
    assert n % tm == 0 and c % tn == 0, (n, tm, c, tn)
    return pl.pallas_call(
        functools.partial(_mm_body, epilogue, len(extras)),
        grid=(n // tm, c // tn),
        in_specs=[pl.BlockSpec((tm, k), lambda i, j: (i, 0)), pl.BlockSpec((k, tn), lambda i, j: (0, j))]
        + list(extra_specs),
        out_specs=out_specs,
        out_shape=out_shapes,
        compiler_params=_cparams("parallel", "parallel"),
        name=name,
    )(a, w, *extras)


def _head_norm(x, gain):
    return x * lax.rsqrt(jnp.mean(x * x, axis=-1, keepdims=True) + NORM_EPS) * gain


def _rope_rot(x, cos, sin_signed):
    return x * cos + pltpu.roll(x, HEAD_DIM // 2, 1) * sin_signed


def _ep_head_norm(scale, acc, extras, outs):
    (gain_ref,) = extras
    (o_ref,) = outs
    for h in range(acc.shape[1] // HEAD_DIM):
        sl = slice(h * HEAD_DIM, (h + 1) * HEAD_DIM)
        o_ref[:, sl] = (_head_norm(acc[:, sl], gain_ref[...]) * scale).astype(o_ref.dtype)


def _ep_head_norm_rope(scale, acc, extras, outs):
    gain_ref, cos_ref, sin_ref = extras
    (o_ref,) = outs
    for h in range(acc.shape[1] // HEAD_DIM):
        sl = slice(h * HEAD_DIM, (h + 1) * HEAD_DIM)
        y = _rope_rot(_head_norm(acc[:, sl], gain_ref[...]), cos_ref[...], sin_ref[...])
        o_ref[:, sl] = (y * scale).astype(o_ref.dtype)


def _ep_identity(acc, extras, outs):
    (o_ref,) = outs
    o_ref[...] = acc.astype(o_ref.dtype)


def _ep_sigmoid(acc, extras, outs):
    (o_ref,) = outs
    o_ref[...] = jax.nn.sigmoid(acc).astype(o_ref.dtype)


def _ep_small(acc, extras, outs):
    (bias_ref,) = extras
    (o_ref,) = outs
    z = acc + bias_ref[...]
    col = lax.broadcasted_iota(jnp.int32, z.shape, 1)
    e = jnp.exp(-jnp.abs(z))
    log_sig = jnp.minimum(z, 0.0) - jnp.log(1.0 + e)
    sig = jnp.where(z >= 0, 1.0, e) / (1.0 + e)
    o_ref[...] = jnp.where(col < FOX_HEADS, log_sig, sig)


def _ep_residual(acc, extras, outs):
    (x_ref,) = extras
    (o_ref,) = outs
    o_ref[...] = x_ref[...] + acc


def _fox_bias_body(n_pages, has_new, *refs):
    if has_new:
        tbl_ref, new_ref, pool_ref, o_ref, buf, sem = refs
    else:
        tbl_ref, pool_ref, o_ref, buf, sem = refs
    bi = pl.program_id(0)

    def copy(p):
        return pltpu.make_async_copy(pool_ref.at[tbl_ref[bi * n_pages + p]], buf.at[p], sem)

    def start(p, c):
        copy(p).start()
        return c

    def wait(p, c):
        copy(p).wait()
        return c

    lax.fori_loop(0, n_pages, start, 0)
    lax.fori_loop(0, n_pages, wait, 0)
    if has_new:
        buf[n_pages] = new_ref[0]
    x = buf[...]
    p, h, w = x.shape
    within = x.reshape(p * h, w)
    lane = lax.broadcasted_iota(jnp.int32, within.shape, 1)
    s = 1
    while s < w:
        within = within + jnp.where(lane >= s, pltpu.roll(within, s, 1), 0.0)
        s *= 2
    within = within.reshape(p, h, w)
    total = jnp.broadcast_to(within[:, :, w - 1:w], x.shape)
    incl = total
    s = 1
    while s < p:
        incl = incl + jnp.concatenate([jnp.zeros((s, h, w), F32), incl[:p - s]], axis=0)
        s *= 2
    o_ref[0] = -(within + (incl - total))


def _fox_key_bias(pool, table, new_page):
    b, n_pages = table.shape
    h, w = pool.shape[1:]
    has_new = new_page is not None
    p_tot = n_pages + (1 if has_new else 0)
    in_specs = [pl.BlockSpec(memory_space=pl.ANY)]
    args = [pool]
    if has_new:
        in_specs.insert(0, pl.BlockSpec((1, h, w), lambda i, tbl: (i, 0, 0)))
        args.insert(0, new_page)
    grid_spec = pltpu.PrefetchScalarGridSpec(
        num_scalar_prefetch=1,
        grid=(b,),
        in_specs=in_specs,
        out_specs=pl.BlockSpec((1, p_tot, h, w), lambda i, tbl: (i, 0, 0, 0)),
        scratch_shapes=[pltpu.VMEM((p_tot, h, w), F32), pltpu.SemaphoreType.DMA(())],
    )
    out = pl.pallas_call(
        functools.partial(_fox_bias_body, n_pages, has_new),
        grid_spec=grid_spec,
        out_shape=jax.ShapeDtypeStruct((b, p_tot, h, w), F32),
        compiler_params=_cparams("arbitrary"),
        name="fox_key_bias",
    )(table.reshape(-1), *args)
    return jnp.transpose(out, (0, 2, 1, 3)).reshape(b, h, p_tot * w)


FLASH_QUERY_BLOCK = 512
FLASH_KEY_BLOCK = 512


def _flash_body(mode, heads, tq, tk, n_kv, gate_col, *refs):
    if mode == "fox":
        q_ref, k_ref, v_ref, bias_ref, o_ref, kb_ref, vb_ref = refs
    elif mode == "slc":
        q_ref, k_ref, v_ref, sel_ref, expand_ref, gate_ref, o_ref, kb_ref, vb_ref, selx_ref = refs
    else:
        q_ref, k_ref, v_ref, gate_ref, o_ref, kb_ref, vb_ref = refs
    unit = pl.program_id(1)
    qi = pl.program_id(2)

    @pl.when(qi == 0)
    def _():
        kb_ref[...] = k_ref[0].astype(BF16)
        vb_ref[...] = v_ref[0].astype(BF16)

    if mode == "slc":
        sel = sel_ref[0, 0].astype(BF16)
        for j in range(n_kv):
            selx_ref[j] = jnp.dot(sel, expand_ref[:, j * tk:(j + 1) * tk], preferred_element_type=F32)

    q_pos = qi * tq + lax.broadcasted_iota(jnp.int32, (tq, tk), 0)
    col = lax.broadcasted_iota(jnp.int32, (tq, tk), 1)
    qs = [q_ref[0, :, r * HEAD_DIM:(r + 1) * HEAD_DIM] for r in range(heads)]

    def block(j, carry, diagonal):
        ks = pl.multiple_of(j * tk, tk)
        k_pos = ks + col
        ok = None
        if mode == "slc":
            ok = selx_ref[j] > 0.5
        if mode == "win":
            ok = (k_pos <= q_pos) & (k_pos > q_pos - WINDOW)
        elif diagonal:
            ok = (k_pos <= q_pos) if ok is None else ok & (k_pos <= q_pos)
        out = []
        for r in range(heads):
            m, l, acc = carry[r]
            c0 = r * HEAD_DIM if mode == "fox" else 0
            kb = kb_ref[pl.ds(ks, tk), c0:c0 + HEAD_DIM]
            vb = vb_ref[pl.ds(ks, tk), c0:c0 + HEAD_DIM]
            s = lax.dot_general(qs[r], kb, _NT, preferred_element_type=F32)
            if mode == "fox":
                s = s + bias_ref[0, r:r + 1, pl.ds(ks, tk)]
            if ok is not None:
                s = jnp.where(ok, s, NEG_INF)
            m_new = jnp.maximum(m, jnp.max(s, axis=-1, keepdims=True))
            alpha = jnp.exp(m - m_new)
            p = jnp.exp(s - m_new)
            l = alpha * l + jnp.sum(p, axis=-1, keepdims=True)
            acc = alpha * acc + jnp.dot(p.astype(BF16), vb, preferred_element_type=F32)
            out.append((m_new, l, acc))
        return tuple(out)

    init = tuple((jnp.full((tq, 1), NEG_INF, F32), jnp.zeros((tq, 1), F32), jnp.zeros((tq, HEAD_DIM), F32))
                 for _ in range(heads))
    first_diag = (qi * tq) // tk
    n_diag = -(-tq // tk)
    if mode == "win":
        lo = jnp.maximum(qi * tq - WINDOW + 1, 0) // tk
        state = lax.fori_loop(lo, first_diag + n_diag, functools.partial(block, diagonal=False), init)
    else:
        state = lax.fori_loop(0, first_diag, functools.partial(block, diagonal=False), init)
        for dj in range(n_diag):
            state = block(first_diag + dj, state, True)

    for r in range(heads):
        m, l, acc = state[r]
        w = 1.0 / l
        if mode != "fox":
            gate = gate_ref[0]
            lane = lax.broadcasted_iota(jnp.int32, gate.shape, 1)
            want = FOX_HEADS + (unit * heads + r) * N_NSA_BRANCHES + gate_col
            w = w * jnp.sum(jnp.where(lane == want, gate, 0.0), axis=-1, keepdims=True)
        o_ref[0, :, r * HEAD_DIM:(r + 1) * HEAD_DIM] = (acc * w).astype(o_ref.dtype)


def _flash_attention(mode, q, k, v, k_col0, t, *, bias=None, sel=None, gate=None, gate_col=0):
    b = q.shape[0]
    heads = HEADS_PER_GROUP
    qw = heads * HEAD_DIM
    units = q.shape[2] // qw
    kvw = qw if mode == "fox" else HEAD_DIM
    tq = _row_tile(t, FLASH_QUERY_BLOCK)
    tk = _row_tile(t, FLASH_KEY_BLOCK)
    n_kv = t // tk
    in_specs = [
        pl.BlockSpec((1, tq, qw), lambda bi, u, i: (bi, i, u)),
        pl.BlockSpec((1, t, kvw), lambda bi, u, i: (bi, 0, k_col0 + u)),
        pl.BlockSpec((1, t, kvw), lambda bi, u, i: (bi, 0, k_col0 + u)),
    ]
    args = [q, k, v]
    scratch = [pltpu.VMEM((t, kvw), BF16), pltpu.VMEM((t, kvw), BF16)]
    if mode == "fox":
        in_specs.append(pl.BlockSpec((1, heads, t), lambda bi, u, i: (bi * units + u, 0, 0)))
        args.append(bias.reshape(b * units, heads, t))
    else:
        if mode == "slc":
            nsp = sel.shape[-1]
            expand = (np.arange(nsp)[:, None] == (np.arange(t)[None, :] // SEL_BLOCK)).astype(np.float32)
            in_specs.append(pl.BlockSpec((1, 1, tq, nsp), lambda bi, u, i: (bi, u, i, 0)))
            in_specs.append(pl.BlockSpec((nsp, t), lambda bi, u, i: (0, 0)))
            args += [sel, jnp.asarray(expand, BF16)]
            scratch.append(pltpu.VMEM((n_kv, tq, tk), F32))
        in_specs.append(pl.BlockSpec((1, tq, LANES), lambda bi, u, i: (bi, i, 0)))
        args.append(gate)
    return pl.pallas_call(
        functools.partial(_flash_body, mode, heads, tq, tk, n_kv, gate_col),
        grid=(b, units, t // tq),
        in_specs=in_specs,
        out_specs=pl.BlockSpec((1, tq, qw), lambda bi, u, i: (bi, i, u)),
        out_shape=jax.ShapeDtypeStruct((b, t, units * qw), BF16),
        scratch_shapes=scratch,
        compiler_params=_cparams("parallel", "parallel", "arbitrary"),
        name="attn_" + mode,
    )(*args)


PAGED_BUFFER_BYTES = 2 * 1024 * 1024
PAGED_COPY_BYTES = 128 * 1024
CMP_BUFFER_BYTES = 4 * 1024 * 1024


def _pages_per_step(n_pages, page_bytes, buffer_bytes=PAGED_BUFFER_BYTES):
    pg = max(1, min(n_pages, buffer_bytes // page_bytes))
    while n_pages % pg:
        pg -= 1
    return pg


def _paged_body(units, rows, rows_b, n_pages, pg, page, use_gate, *refs):
    if use_gate:
        (tbl_ref, ns_ref, q_ref, bias_ref, kn_ref, vn_ref, biasn_ref, gate_ref, kpool, vpool, o_ref,
         kbuf, vbuf, sem, m_ref, l_ref, acc_ref) = refs
    else:
        (tbl_ref, ns_ref, q_ref, bias_ref, kn_ref, vn_ref, biasn_ref, kpool, vpool, o_ref,
         kbuf, vbuf, sem, m_ref, l_ref, acc_ref) = refs
        gate_ref = None
    bi = pl.program_id(0)
    s = pl.program_id(1)
    n_steps = n_pages // pg
    rpp = page * units
    n_keys = pg * page

    n_chunks = max(1, (rpp * HEAD_DIM * 4) // PAGED_COPY_BYTES)
    crows = rpp // n_chunks

    def copies(step, slot):
        out = []
        for k in range(pg):
            phys = tbl_ref[bi * n_pages + step * pg + k]
            for c in range(n_chunks):
                src = pl.ds(c * crows, crows)
                dst = pl.ds(k * rpp + c * crows, crows)
                out.append(pltpu.make_async_copy(kpool.at[phys, src], kbuf.at[slot, dst], sem.at[0, slot]))
                out.append(pltpu.make_async_copy(vpool.at[phys, src], vbuf.at[slot, dst], sem.at[1, slot]))
        return out

    @pl.when(s == 0)
    def _():
        m_ref[...] = jnp.full(m_ref.shape, NEG_INF, F32)
        l_ref[...] = jnp.zeros(l_ref.shape, F32)
        acc_ref[...] = jnp.zeros(acc_ref.shape, F32)
        for c in copies(0, 0):
            c.start()

    live_steps = ns_ref[bi]

    @pl.when(s + 1 < live_steps)
    def _():
        for c in copies(s + 1, lax.rem(s + 1, 2)):
            c.start()

    def attend(k, v, bias, own_blocks):
        k_hi, k_lo = _split_bf16(k)
        v_hi, v_lo = _split_bf16(v)
        sc = _dot_split(q_ref[0], k_hi, k_lo, _NT)
        nk = k.shape[0] // units
        if own_blocks:
            sc = jnp.concatenate([sc[u * rows:(u + 1) * rows, u * nk:(u + 1) * nk] for u in range(units)], axis=0)
        sc = sc + bias
        m = m_ref[...]
        m_new = jnp.maximum(m, jnp.max(sc, axis=-1, keepdims=True))
        alpha = jnp.exp(m - m_new)
        pr = jnp.exp(sc - m_new)
        l_ref[...] = alpha * l_ref[...] + jnp.sum(pr, axis=-1, keepdims=True)
        if own_blocks:
            zero = jnp.zeros((rows, nk), F32)
            pr = jnp.concatenate(
                [jnp.concatenate([pr[u * rows:(u + 1) * rows] if u2 == u else zero for u2 in range(units)], axis=1)
                 for u in range(units)], axis=0)
        acc_ref[...] = alpha * acc_ref[...] + _dot_split(pr, v_hi, v_lo, _NN)
        m_ref[...] = m_new

    def unit_bias(u):
        bb = bias_ref[0, u]
        if rows_b == 1:
            return jnp.broadcast_to(bb, (rows, n_keys))
        return jnp.concatenate([bb] * (rows // rows_b), axis=0)

    @pl.when(s < live_steps)
    def _():
        slot = lax.rem(s, 2)
        for c in copies(s, slot):
            c.wait()
        k = jnp.concatenate([kbuf[slot, pl.ds(u, n_keys, stride=units), :] for u in range(units)], axis=0)
        v = jnp.concatenate([vbuf[slot, pl.ds(u, n_keys, stride=units), :] for u in range(units)], axis=0)
        attend(k, v, jnp.concatenate([unit_bias(u) for u in range(units)], axis=0), True)

    @pl.when(s == n_steps)
    def _():
        k = jnp.concatenate([kn_ref[0, :, u * HEAD_DIM:(u + 1) * HEAD_DIM] for u in range(units)], axis=0)
        v = jnp.concatenate([vn_ref[0, :, u * HEAD_DIM:(u + 1) * HEAD_DIM] for u in range(units)], axis=0)
        attend(k, v, biasn_ref[0], False)
        w = 1.0 / l_ref[...]
        if use_gate:
            w = w * gate_ref[0]
        o_ref[0] = acc_ref[...] * w


def _paged_attention(q, k_pool, v_pool, table, bias, k_new, v_new, bias_new, gate=None, n_live_pages=None):
    b, units, rows, _ = q.shape
    ur = units * rows
    n_pages = table.shape[1]
    rpp = k_pool.shape[1]
    page = rpp // units
    rows_b = bias.shape[2]
    n_new = k_new.shape[1]
    width = k_new.shape[2]
    pg = _pages_per_step(n_pages, rpp * HEAD_DIM * 4)
    n_steps = n_pages // pg
    last = n_steps - 1
    if n_live_pages is None:
        live_steps = jnp.full((b,), n_steps, jnp.int32)
    else:
        live_steps = jnp.clip((n_live_pages + pg - 1) // pg, 1, n_steps).astype(jnp.int32)
    same_unit = jnp.arange(units)[:, None, None, None] == jnp.arange(units)[None, None, :, None]
    bias_new_full = jnp.where(same_unit[None], bias_new[:, :, :, None, :], NEG_INF).reshape(b, ur, units * n_new)
    in_specs = [
        pl.BlockSpec((1, ur, HEAD_DIM), lambda bi, p, tbl, ns: (bi, 0, 0)),
        pl.BlockSpec((1, units, rows_b, pg * page), lambda bi, p, tbl, ns: (bi, 0, 0, jnp.minimum(p, last))),
        pl.BlockSpec((1, n_new, width), lambda bi, p, tbl, ns: (bi, 0, 0)),
        pl.BlockSpec((1, n_new, width), lambda bi, p, tbl, ns: (bi, 0, 0)),
        pl.BlockSpec((1, ur, units * n_new), lambda bi, p, tbl, ns: (bi, 0, 0)),
    ]
    args = [q.reshape(b, ur, HEAD_DIM), bias, k_new, v_new, bias_new_full]
    if gate is not None:
        in_specs.append(pl.BlockSpec((1, ur, 1), lambda bi, p, tbl, ns: (bi, 0, 0)))
        args.append(gate.reshape(b, ur, 1))
    in_specs += [pl.BlockSpec(memory_space=pl.ANY), pl.BlockSpec(memory_space=pl.ANY)]
    args += [k_pool, v_pool]
    grid_spec = pltpu.PrefetchScalarGridSpec(
        num_scalar_prefetch=2,
        grid=(b, n_steps + 1),
        in_specs=in_specs,
        out_specs=pl.BlockSpec((1, ur, HEAD_DIM), lambda bi, p, tbl, ns: (bi, 0, 0)),
        scratch_shapes=[pltpu.VMEM((2, pg * rpp, HEAD_DIM), F32), pltpu.VMEM((2, pg * rpp, HEAD_DIM), F32),
                        pltpu.SemaphoreType.DMA((2, 2)),
                        pltpu.VMEM((ur, 1), F32), pltpu.VMEM((ur, 1), F32), pltpu.VMEM((ur, HEAD_DIM), F32)],
    )
    out = pl.pallas_call(
        functools.partial(_paged_body, units, rows, rows_b, n_pages, pg, page, gate is not None),
        grid_spec=grid_spec,
        out_shape=jax.ShapeDtypeStruct((b, ur, HEAD_DIM), F32),
        compiler_params=_cparams("arbitrary", "arbitrary"),
        name="attn_paged",
    )(table.reshape(-1), live_steps, *args)
    return out.reshape(b, units, rows, HEAD_DIM)


def _cmp_proj_body(x_ref, pa_ref, pb_ref, wa_ref, wb_ref, za_ref, zb_ref):
    x = x_ref[0]
    za_ref[0] = jnp.dot((x + pa_ref[...]).astype(BF16), wa_ref[...], preferred_element_type=F32)
    zb_ref[0] = jnp.dot((x + pb_ref[...]).astype(BF16), wb_ref[...], preferred_element_type=F32)


def _cmp_mlp_body(n_chunks, za_ref, zb_ref, w2_ref, o_ref):
    zb_next = pltpu.roll(zb_ref[0], n_chunks - 1, 0)
    hid = za_ref[0] + zb_next
    c = 0.7978845608028654
    hid = 0.5 * hid * (1.0 + jnp.tanh(c * (hid + 0.044715 * (hid * hid * hid))))
    w2 = w2_ref[...]
    hb = hid.astype(w2.dtype)
    for g in range(NSA_KV_GROUPS):
        sl = slice(g * HEAD_DIM, (g + 1) * HEAD_DIM)
        o_ref[0, :, sl] = jnp.dot(hb[:, sl], w2, preferred_element_type=F32, precision=_dot_precision(w2.dtype))


def _compress(rows, pos_emb, w1, w2):
    b, tc, gw = rows.shape
    g = NSA_KV_GROUPS
    n_chunks = tc // CMP_STRIDE
    cw = CMP_STRIDE * gw
    x = rows.reshape(b, n_chunks, cw)
    w1r = w1.reshape(2, CMP_STRIDE, HEAD_DIM, HEAD_DIM)
    eye = jnp.eye(g, dtype=w1.dtype)
    wcat = jnp.einsum("hldj,ge->hlgdej", w1r, eye).reshape(2, cw, g * HEAD_DIM).astype(BF16)
    pos = jnp.broadcast_to(pos_emb.reshape(2, CMP_STRIDE, 1, HEAD_DIM), (2, CMP_STRIDE, g, HEAD_DIM)).reshape(2, 1, cw)
    cm = _row_tile(n_chunks, 128)
    za, zb = pl.pallas_call(
        _cmp_proj_body,
        grid=(b, n_chunks // cm),
        in_specs=[
            pl.BlockSpec((1, cm, cw), lambda bi, i: (bi, i, 0)),
            pl.BlockSpec((1, cw), lambda bi, i: (0, 0)),
            pl.BlockSpec((1, cw), lambda bi, i: (0, 0)),
            pl.BlockSpec((cw, g * HEAD_DIM), lambda bi, i: (0, 0)),
            pl.BlockSpec((cw, g * HEAD_DIM), lambda bi, i: (0, 0)),
        ],
        out_specs=[pl.BlockSpec((1, cm, g * HEAD_DIM), lambda bi, i: (bi, i, 0))] * 2,
        out_shape=[jax.ShapeDtypeStruct((b, n_chunks, g * HEAD_DIM), F32)] * 2,
        compiler_params=_cparams("parallel", "parallel"),
        name="cmp_proj",
    )(x, pos[0], pos[1], wcat[0], wcat[1])
    return _cmp_mlp(za, zb, w2.astype(BF16))


def _cmp_mlp(za, zb, w2):
    b, n_chunks, gw = za.shape
    g = NSA_KV_GROUPS
    return pl.pallas_call(
        functools.partial(_cmp_mlp_body, n_chunks),
        grid=(b,),
        in_specs=[
            pl.BlockSpec((1, n_chunks, g * HEAD_DIM), lambda bi: (bi, 0, 0)),
            pl.BlockSpec((1, n_chunks, g * HEAD_DIM), lambda bi: (bi, 0, 0)),
            pl.BlockSpec((HEAD_DIM, HEAD_DIM), lambda bi: (0, 0)),
        ],
        out_specs=pl.BlockSpec((1, n_chunks, g * HEAD_DIM), lambda bi: (bi, 0, 0)),
        out_shape=jax.ShapeDtypeStruct((b, n_chunks, g * HEAD_DIM), F32),
        compiler_params=_cparams("parallel"),
        name="cmp_mlp",
    )(za, zb, w2)


def _split_bf16(x):
    hi = x.astype(BF16)
    return hi, (x - hi.astype(F32)).astype(BF16)


def _dot_split(a, b_hi, b_lo, dims):
    a_hi, a_lo = _split_bf16(a)
    n = a.shape[0]
    both = lax.dot_general(jnp.concatenate([a_hi, a_lo], axis=0), b_hi, dims, preferred_element_type=F32)
    return both[:n] + both[n:] + lax.dot_general(a_hi, b_lo, dims, preferred_element_type=F32)


_NT = (((1,), (1,)), ((), ()))
_NN = (((1,), (0,)), ((), ()))


def _cmp_paged_body(n_pages, pg, page, tbl_ref, pos_ref, w_ref, pool, za_ref, zb_ref, xbuf, whi_ref, wlo_ref, sem):
    bi = pl.program_id(0)
    s = pl.program_id(1)
    n_steps = n_pages // pg
    g_n = NSA_KV_GROUPS
    rpp = page * g_n
    n_rows = pg * page // CMP_STRIDE

    def copies(step, slot):
        return [pltpu.make_async_copy(pool.at[tbl_ref[bi * n_pages + step * pg + k]],
                                      xbuf.at[slot, pl.ds(k * rpp, rpp)], sem.at[slot]) for k in range(pg)]

    @pl.when(s == 0)
    def _():
        for c in copies(0, 0):
            c.start()
        w_hi, w_lo = _split_bf16(w_ref[...])
        whi_ref[...] = w_hi
        wlo_ref[...] = w_lo

    @pl.when(s + 1 < n_steps)
    def _():
        for c in copies(s + 1, lax.rem(s + 1, 2)):
            c.start()

    slot = lax.rem(s, 2)
    for c in copies(s, slot):
        c.wait()
    w_hi = whi_ref[...]
    w_lo = wlo_ref[...]
    pw = _dot_split(pos_ref[...], w_hi, w_lo, _NN)
    col = lax.broadcasted_iota(jnp.int32, (1, 2 * HEAD_DIM), 1)
    pos_term = jnp.where(col < HEAD_DIM, pw[0:1, :], pw[1:2, :])
    x_cat = jnp.concatenate(
        [jnp.concatenate([xbuf[slot, pl.ds(l * g_n + g, n_rows, stride=CMP_STRIDE * g_n), :]
                          for l in range(CMP_STRIDE)], axis=1) for g in range(g_n)], axis=0)
    z = _dot_split(x_cat, w_hi, w_lo, _NN) + pos_term
    for g in range(g_n):
        za_ref[0, :, g * HEAD_DIM:(g + 1) * HEAD_DIM] = z[g * n_rows:(g + 1) * n_rows, :HEAD_DIM]
        zb_ref[0, :, g * HEAD_DIM:(g + 1) * HEAD_DIM] = z[g * n_rows:(g + 1) * n_rows, HEAD_DIM:]


def _compress_paged(pool, table, pos_emb, w1, w2):
    b, n_pages = table.shape
    rpp = pool.shape[1]
    page = rpp // NSA_KV_GROUPS
    gw = NSA_KV_GROUPS * HEAD_DIM
    pg = _pages_per_step(n_pages, rpp * HEAD_DIM * 4, CMP_BUFFER_BYTES)
    n_steps = n_pages // pg
    n_rows = pg * page // CMP_STRIDE
    n_chunks = n_pages * page // CMP_STRIDE
    half = CMP_STRIDE * HEAD_DIM
    w_cat = jnp.concatenate([w1[:half], w1[half:]], axis=1)
    pos_rows = jnp.pad(pos_emb.reshape(2, half), ((0, SUBLANES - 2), (0, 0)))
    grid_spec = pltpu.PrefetchScalarGridSpec(
        num_scalar_prefetch=1,
        grid=(b, n_steps),
        in_specs=[pl.BlockSpec((SUBLANES, half), lambda bi, i, tbl: (0, 0)),
                  pl.BlockSpec((half, 2 * HEAD_DIM), lambda bi, i, tbl: (0, 0)),
                  pl.BlockSpec(memory_space=pl.ANY)],
        out_specs=[pl.BlockSpec((1, n_rows, gw), lambda bi, i, tbl: (bi, i, 0))] * 2,
        scratch_shapes=[pltpu.VMEM((2, pg * rpp, HEAD_DIM), F32), pltpu.VMEM((half, 2 * HEAD_DIM), BF16),
                        pltpu.VMEM((half, 2 * HEAD_DIM), BF16), pltpu.SemaphoreType.DMA((2,))],
    )
    za, zb = pl.pallas_call(
        functools.partial(_cmp_paged_body, n_pages, pg, page),
        grid_spec=grid_spec,
        out_shape=[jax.ShapeDtypeStruct((b, n_chunks, gw), F32)] * 2,
        compiler_params=_cparams("arbitrary", "arbitrary"),
        name="cmp_proj_paged",
    )(table.reshape(-1), pos_rows, w_cat, pool)
    return _cmp_mlp(za, zb, w2)


def _cmp_attn_body(tq, nc, ns, n_pick, q_start, q_ref, kc_ref, vc_ref, cover_ref, gate_ref, o_ref, sel_ref):
    unit = pl.program_id(1)
    qi = pl.program_id(2)
    precise = q_ref.dtype == F32
    if precise:
        kc_hi, kc_lo = _split_bf16(kc_ref[0])
        vc_hi, vc_lo = _split_bf16(vc_ref[0])
    else:
        kc_hi = kc_ref[0].astype(BF16)
        vc_hi = vc_ref[0].astype(BF16)
    ncp = kc_hi.shape[0]
    q_pos = q_start + qi * tq + lax.broadcasted_iota(jnp.int32, (tq, ncp), 0)
    n_idx = lax.broadcasted_iota(jnp.int32, (tq, ncp), 1)
    visible = (n_idx * CMP_STRIDE + (CMP_BLOCK - 1) <= q_pos) & (n_idx < nc)
    gate = gate_ref[0]
    glane = lax.broadcasted_iota(jnp.int32, gate.shape, 1)
    p_grp = jnp.zeros((tq, ncp), F32)
    for r in range(HEADS_PER_GROUP):
        q = q_ref[0, :, r * HEAD_DIM:(r + 1) * HEAD_DIM]
        if precise:
            s = _dot_split(q, kc_hi, kc_lo, _NT)
        else:
            s = lax.dot_general(q, kc_hi, _NT, preferred_element_type=F32)
        s = jnp.where(visible, s, NEG_INF)
        m = jnp.max(s, axis=-1, keepdims=True)
        e = jnp.where(visible, jnp.exp(s - m), 0.0)
        l = jnp.sum(e, axis=-1, keepdims=True)
        p = e / jnp.where(l > 0.0, l, 1.0)
        p_grp = p_grp + p
        want = FOX_HEADS + (unit * HEADS_PER_GROUP + r) * N_NSA_BRANCHES
        gcol = jnp.sum(jnp.where(glane == want, gate, 0.0), axis=-1, keepdims=True)
        if precise:
            o = _dot_split(p, vc_hi, vc_lo, _NN)
        else:
            o = jnp.dot(p.astype(BF16), vc_hi, preferred_element_type=F32)
        o_ref[0, :, r * HEAD_DIM:(r + 1) * HEAD_DIM] = (o * gcol).astype(o_ref.dtype)

    imp = jnp.dot(p_grp, cover_ref[...], preferred_element_type=F32, precision=HIGHEST)
    nsp = imp.shape[1]
    j = lax.broadcasted_iota(jnp.int32, (tq, nsp), 1)
    cur = (q_start + qi * tq + lax.broadcasted_iota(jnp.int32, (tq, nsp), 0)) // SEL_BLOCK
    dist = cur - j
    forced = (j == 0) | ((dist >= 0) & (dist < N_LOCAL_SEL))
    score = jnp.where(forced, FORCE_SCORE, imp)
    score = jnp.where(dist >= 0, score, NEG_INF)
    score = jnp.where(j < ns, score, PAD_SCORE)
    sel = jnp.zeros((tq, nsp), F32)
    j_f = j.astype(F32)
    for _ in range(n_pick):
        mx = jnp.max(score, axis=-1, keepdims=True)
        first = jnp.min(jnp.where(score == mx, j_f, float(nsp)), axis=-1, keepdims=True)
        hit = j_f == first
        sel = jnp.where(hit, 1.0, sel)
        score = jnp.where(hit, TAKEN_SCORE, score)
    sel_ref[0, 0] = sel


def _cmp_attention(nq, kc, vc, gate, t, q_start, n_keys, out_dtype):
    b = nq.shape[0]
    ncp = kc.shape[1]
    nc = (n_keys - CMP_BLOCK) // CMP_STRIDE + 1
    ns = -(-n_keys // SEL_BLOCK)
    nsp = -(-ns // LANES) * LANES
    n_pick = min(N_SELECT, ns)
    c0 = np.arange(ncp)[:, None] * CMP_STRIDE
    s0 = np.arange(nsp)[None, :] * SEL_BLOCK
    cover = np.clip(np.minimum(c0 + CMP_BLOCK, s0 + SEL_BLOCK) - np.maximum(c0, s0), 0, None).astype(np.float32) / CMP_BLOCK
    cover = cover * (np.arange(ncp)[:, None] < nc) * (np.arange(nsp)[None, :] < ns)
    tq = _row_tile(t, 256)
    qw = HEADS_PER_GROUP * HEAD_DIM
    return pl.pallas_call(
        functools.partial(_cmp_attn_body, tq, nc, ns, n_pick, q_start),
        grid=(b, NSA_KV_GROUPS, t // tq),
        in_specs=[
            pl.BlockSpec((1, tq, qw), lambda bi, u, i: (bi, i, u)),
            pl.BlockSpec((1, ncp, HEAD_DIM), lambda bi, u, i: (bi, 0, u)),
            pl.BlockSpec((1, ncp, HEAD_DIM), lambda bi, u, i: (bi, 0, u)),
            pl.BlockSpec((ncp, nsp), lambda bi, u, i: (0, 0)),
            pl.BlockSpec((1, tq, LANES), lambda bi, u, i: (bi, i, 0)),
        ],
        out_specs=[
            pl.BlockSpec((1, tq, qw), lambda bi, u, i: (bi, i, u)),
            pl.BlockSpec((1, 1, tq, nsp), lambda bi, u, i: (bi, u, i, 0)),
        ],
        out_shape=[
            jax.ShapeDtypeStruct((b, t, NSA_KV_GROUPS * qw), out_dtype),
            jax.ShapeDtypeStruct((b, NSA_KV_GROUPS, t, nsp), F32),
        ],
        compiler_params=_cparams("parallel", "parallel", "parallel"),
        name="attn_cmp_select",
    )(nq, kc, vc, jnp.asarray(cover), gate)


def _merge_body(a_ref, b1_ref, b2_ref, b3_ref, ua_ref, ub_ref, ga_ref, gb_ref, o_ref):
    a = a_ref[...]
    prec = _dot_precision(a.dtype)
    bsum = (b1_ref[...].astype(F32) + b2_ref[...].astype(F32) + b3_ref[...].astype(F32)).astype(a.dtype)
    ya = jnp.dot(a, ua_ref[...], preferred_element_type=F32, precision=prec)
    yb = jnp.dot(bsum, ub_ref[...], preferred_element_type=F32, precision=prec)
    o_ref[...] = (ga_ref[...] * ya + gb_ref[...] * yb).astype(o_ref.dtype)


def _merge(o_fox, o_cmp, o_slc, o_win, u_fox, u_nsa, mg, d_model):
    n, w = o_fox.shape
    tm = _row_tile(n, 512)
    tn = 512
    nj = d_model // tn
    row = pl.BlockSpec((tm, w), lambda i, j: (i, 0))
    return pl.pallas_call(
        _merge_body,
        grid=(n // tm, nj),
        in_specs=[row, row, row, row,
                  pl.BlockSpec((w, tn), lambda i, j: (0, j)),
                  pl.BlockSpec((w, tn), lambda i, j: (0, j)),
                  pl.BlockSpec((tm, tn), lambda i, j: (i, j)),
                  pl.BlockSpec((tm, tn), lambda i, j: (i, j + nj))],
        out_specs=pl.BlockSpec((tm, tn), lambda i, j: (i, j)),
        out_shape=jax.ShapeDtypeStruct((n, d_model), o_fox.dtype),
        compiler_params=_cparams("parallel", "parallel"),
        name="mixer_merge",
    )(o_fox, o_cmp, o_slc, o_win, u_fox, u_nsa, mg, mg)


def _router_body(h_ref, g_ref, w_ref, b_ref, u_ref, route_ref):
    x = h_ref[...]
    xn = x * lax.rsqrt(jnp.mean(x * x, axis=-1, keepdims=True) + NORM_EPS) * g_ref[...]
    u_ref[...] = xn
    z = jnp.dot(xn, w_ref[...], preferred_element_type=F32, precision=HIGHEST) + b_ref[...]
    c = lax.broadcasted_iota(jnp.int32, z.shape, 1)
    big = z.shape[1]
    is_g = c < N_GROUPS
    gmax = jnp.max(jnp.where(is_g, z, -jnp.inf), axis=-1, keepdims=True)
    gsel = jnp.min(jnp.where(is_g & (z == gmax), c, big), axis=-1, keepdims=True)
    gsum = jnp.sum(jnp.where(is_g, jnp.exp(z - gmax), 0.0), axis=-1, keepdims=True)
    lo = N_GROUPS + gsel * EXPERTS_PER_GROUP
    in_grp = (c >= lo) & (c < lo + EXPERTS_PER_GROUP)
    z1 = jnp.max(jnp.where(in_grp, z, -jnp.inf), axis=-1, keepdims=True)
    c1 = jnp.min(jnp.where(in_grp & (z == z1), c, big), axis=-1, keepdims=True)
    rest = in_grp & (c != c1)
    z2 = jnp.max(jnp.where(rest, z, -jnp.inf), axis=-1, keepdims=True)
    c2 = jnp.min(jnp.where(rest & (z == z2), c, big), axis=-1, keepdims=True)
    e2 = jnp.exp(z2 - z1)
    p1 = 1.0 / (1.0 + e2)
    p2 = e2 / (1.0 + e2)
    gp = 1.0 / gsum
    out = jnp.where(c == 0, (c1 - N_GROUPS).astype(F32), 0.0)
    out = jnp.where(c == 1, (c2 - N_GROUPS).astype(F32), out)
    out = jnp.where(c == 2, gp * p1, out)
    out = jnp.where(c == 3, gp * p2, out)
    route_ref[...] = out


def _router(h, gain, w_cat, b_cat):
    n, d = h.shape
    tm = _row_tile(n, 512)
    return pl.pallas_call(
        _router_body,
        grid=(n // tm,),
        in_specs=[pl.BlockSpec((tm, d), lambda i: (i, 0)), pl.BlockSpec((1, d), lambda i: (0, 0)),
                  pl.BlockSpec((d, LANES), lambda i: (0, 0)), pl.BlockSpec((1, LANES), lambda i: (0, 0))],
        out_specs=[pl.BlockSpec((tm, d), lambda i: (i, 0)), pl.BlockSpec((tm, LANES), lambda i: (i, 0))],
        out_shape=[jax.ShapeDtypeStruct((n, d), F32), jax.ShapeDtypeStruct((n, LANES), F32)],
        compiler_params=_cparams("parallel"),
        name="moe_router",
    )(h, gain.reshape(1, d), w_cat, b_cat)


def _gather_rows(src_hbm, dst_ref, sem, idx_ref, base, count, start, inline=False):
    if not start:
        pltpu.make_async_copy(src_hbm.at[pl.ds(0, count)], dst_ref, sem).wait()
        return

    def issue(r):
        pltpu.make_async_copy(src_hbm.at[pl.ds(idx_ref[base + r], 1)], dst_ref.at[pl.ds(r, 1)], sem).start()

    if inline:
        for r in range(count):
            issue(r)
        return

    def body(r, c):
        issue(r)
        return c

    lax.fori_loop(0, count, body, 0, unroll=8)


def _expert_body(tm, te_ref, nv_ref, tok_ref, wg_ref, wu_ref, wd_ref, x_hbm, o_ref, xbuf, sem):
    i = pl.program_id(0)
    nv = nv_ref[0]

    def rows_of(tile, slot, start, inline=False):
        _gather_rows(x_hbm, xbuf.at[slot], sem.at[slot], tok_ref, tile * tm, tm, start, inline)

    @pl.when(i == 0)
    def _():
        rows_of(0, 0, True)

    @pl.when(i < nv)
    def _():
        slot = lax.rem(i, 2)
        rows_of(i, slot, False)
        rows_of(jnp.minimum(i + 1, nv - 1), 1 - slot, True, inline=True)
        x = xbuf[slot].astype(BF16)
        hg = jnp.dot(x, wg_ref[0].astype(BF16), preferred_element_type=F32)
        hu = jnp.dot(x, wu_ref[0].astype(BF16), preferred_element_type=F32)
        hid = (hg * jax.nn.sigmoid(hg)) * hu
        o_ref[...] = jnp.dot(hid.astype(BF16), wd_ref[0].astype(BF16), preferred_element_type=F32)

    @pl.when(i == nv - 1)
    def _():
        rows_of(i, 1 - lax.rem(i, 2), False)

    @pl.when(i >= nv)
    def _():
        o_ref[...] = jnp.zeros(o_ref.shape, o_ref.dtype)


def _experts(x, row_token, tile_expert, n_valid, wg, wu, wd, tm):
    rows = row_token.shape[0]
    d = x.shape[1]
    n_tiles = rows // tm
    f = wg.shape[2]

    def w_map(i, te, nv, tok):
        return (te[i], 0, 0)

    grid_spec = pltpu.PrefetchScalarGridSpec(
        num_scalar_prefetch=3,
        grid=(n_tiles,),
        in_specs=[pl.BlockSpec((1, d, f), w_map), pl.BlockSpec((1, d, f), w_map), pl.BlockSpec((1, f, d), w_map),
                  pl.BlockSpec(memory_space=pl.ANY)],
        out_specs=pl.BlockSpec((tm, d), lambda i, te, nv, tok: (i, 0)),
        scratch_shapes=[pltpu.VMEM((2, tm, d), F32), pltpu.SemaphoreType.DMA((2,))],
    )
    return pl.pallas_call(
        functools.partial(_expert_body, tm),
        grid_spec=grid_spec,
        out_shape=jax.ShapeDtypeStruct((rows, d), F32),
        compiler_params=_cparams("arbitrary"),
        name="moe_experts",
    )(tile_expert, n_valid, row_token, wg, wu, wd, x)


def _combine_body(tm, n, slot_ref, h_ref, route_ref, src_hbm, o_ref, buf, sem):
    i = pl.program_id(0)
    n_tiles = pl.num_programs(0)

    def rows_of(tile, start):
        slot = lax.rem(tile, 2)
        for k in range(2):
            _gather_rows(src_hbm, buf.at[slot, k], sem.at[slot], slot_ref, k * n + tile * tm, tm, start)

    @pl.when(i == 0)
    def _():
        rows_of(0, True)

    @pl.when(i + 1 < n_tiles)
    def _():
        rows_of(i + 1, True)

    rows_of(i, False)
    slot = lax.rem(i, 2)
    route = route_ref[...]
    o_ref[...] = h_ref[...] + route[:, 2:3] * buf[slot, 0] + route[:, 3:4] * buf[slot, 1]


def _moe_combine(h, route, out_sorted, slot_kn):
    n, d = h.shape
    tm = _row_tile(n, 256)
    grid_spec = pltpu.PrefetchScalarGridSpec(
        num_scalar_prefetch=1,
        grid=(n // tm,),
        in_specs=[pl.BlockSpec((tm, d), lambda i, sl: (i, 0)), pl.BlockSpec((tm, LANES), lambda i, sl: (i, 0)),
                  pl.BlockSpec(memory_space=pl.ANY)],
        out_specs=pl.BlockSpec((tm, d), lambda i, sl: (i, 0)),
        scratch_shapes=[pltpu.VMEM((2, 2, tm, d), F32), pltpu.SemaphoreType.DMA((2,))],
    )
    return pl.pallas_call(
        functools.partial(_combine_body, tm, n),
        grid_spec=grid_spec,
        out_shape=jax.ShapeDtypeStruct((n, d), F32),
        compiler_params=_cparams("arbitrary"),
        name="moe_combine",
    )(slot_kn, h, route, out_sorted)


def _moe(h, gain, w_router, b_router, wg, wu, wd):
    n, d = h.shape
    ut, route = _router(h, gain, w_router, b_router)
    expert = route[:, :2].astype(jnp.int32).reshape(-1)
    n_asg = 2 * n
    tm = 256 if n_asg >= 8192 else 16
    n_tiles = n_asg // tm + N_EXPERTS
    onehot = (expert[:, None] == jnp.arange(N_EXPERTS, dtype=jnp.int32)[None, :]).astype(jnp.int32)
    running = jnp.cumsum(onehot, axis=0)
    counts = running[-1]
    rank = jnp.sum(onehot * running, axis=1) - 1
    tiles_per = (counts + tm - 1) // tm
    tile_end = jnp.cumsum(tiles_per)
    start_row = (tile_end - tiles_per) * tm
    slot = (jnp.sum(onehot * start_row[None, :], axis=1) + rank).astype(jnp.int32)
    row_token = jnp.zeros((n_tiles * tm,), jnp.int32).at[slot].set(jnp.arange(n_asg, dtype=jnp.int32) // 2)
    n_valid = tile_end[-1].astype(jnp.int32)
    tile_ids = jnp.minimum(jnp.arange(n_tiles, dtype=jnp.int32), n_valid - 1)
    tile_expert = jnp.sum((tile_end[None, :] <= tile_ids[:, None]).astype(jnp.int32), axis=1)
    out_sorted = _experts(ut, row_token, tile_expert, n_valid.reshape(1), wg, wu, wd, tm)
    return _moe_combine(h, route, out_sorted, jnp.transpose(slot.reshape(n, 2)).reshape(-1))


def _rope_tables(positions):
    half = HEAD_DIM // 2
    inv = ROPE_THETA ** (-np.arange(half, dtype=np.float64) / half)
    ang = np.asarray(positions, np.float64)[:, None] * inv[None, :]
    cos = np.concatenate([np.cos(ang), np.cos(ang)], axis=-1)
    sin = np.concatenate([-np.sin(ang), np.sin(ang)], axis=-1)
    return jnp.asarray(cos, F32), jnp.asarray(sin, F32)


def _project(u, wts, p, b, t, q_start):
    n = u.shape[0]
    tm = _row_tile(n, 1024)
    tn = 256
    scale = HEAD_DIM ** -0.5
    gain_spec = pl.BlockSpec((1, HEAD_DIM), lambda i, j: (0, 0))

    def run(w, epilogue, extras, specs, dtype, name, tn_=tn):
        shape = jax.ShapeDtypeStruct((n, w.shape[1]), dtype)
        spec = pl.BlockSpec((tm, tn_), lambda i, j: (i, j))
        return _matmul(u, w, epilogue, extras, specs, shape, spec, tm, tn_, name)

    cos, sin = _rope_tables(q_start + (np.arange(max(tm, t)) % t))
    n_tab = cos.shape[0] // tm
    tab_spec = pl.BlockSpec((tm, HEAD_DIM), lambda i, j: (i % n_tab, 0))

    fq = run(wts["fq"], functools.partial(_ep_head_norm, scale), [p["fox_q_norm"]], [gain_spec], u.dtype, "proj_fq")
    fk = run(wts["fk"], functools.partial(_ep_head_norm, 1.0), [p["fox_k_norm"]], [gain_spec], F32, "proj_fk")
    fv = run(wts["fv"], _ep_identity, [], [], F32, "proj_fv")
    nq = run(wts["nq"], functools.partial(_ep_head_norm_rope, scale), [p["nsa_q_norm"], cos, sin],
             [gain_spec, tab_spec, tab_spec], u.dtype, "proj_nq")
    nk = run(wts["nk"], functools.partial(_ep_head_norm_rope, 1.0), [p["nsa_k_norm"], cos, sin],
             [gain_spec, tab_spec, tab_spec], F32, "proj_nk")
    nv = run(wts["nv"], _ep_identity, [], [], F32, "proj_nv")
    mg = run(wts["mg"], _ep_sigmoid, [], [], F32, "proj_mg")
    small = run(wts["small"], _ep_small, [p["small_bias"]], [pl.BlockSpec((1, LANES), lambda i, j: (0, 0))],
                F32, "proj_small", LANES)
    return fq, fk, fv, nq, nk, nv, mg, small


def _layer(x, p, wts, q_start, past):
    b, t, d = x.shape
    n = b * t
    g = NSA_KV_GROUPS
    kvw = g * HEAD_DIM
    x2d = x.reshape(n, d)
    dense = wts["f32"] if past is not None else wts
    u = _rmsnorm(x2d, p["norm_mix"], dense["fq"].dtype)
    fq, fk, fv, nq, nk, nv, mg, small = _project(u, dense, p, b, t, q_start)

    logf = small[:, :FOX_HEADS].reshape(b, t, FOX_HEADS)
    gate3 = small.reshape(b, t, LANES)
    fq3 = fq.reshape(b, t, -1)
    nq3 = nq.reshape(b, t, -1)
    fk3 = fk.reshape(b, t, -1)
    fv3 = fv.reshape(b, t, -1)
    nk3 = nk.reshape(b, t, 3 * kvw)
    nv3 = nv.reshape(b, t, 3 * kvw)
    ck, sk, wk = (nk3[:, :, i * kvw:(i + 1) * kvw] for i in range(3))
    cv, sv, wv = (nv3[:, :, i * kvw:(i + 1) * kvw] for i in range(3))

    if past is None:
        n_keys = t
        n_lp = t // LANES
        lf_pool = jnp.transpose(logf.reshape(b * n_lp, LANES, FOX_HEADS), (0, 2, 1))
        bias = _fox_key_bias(lf_pool, jnp.arange(b * n_lp, dtype=jnp.int32).reshape(b, n_lp), None)
        o_fox = _flash_attention("fox", fq3, fk3, fv3, 0, t, bias=bias)
        kc = _compress(ck, p["cmp_pos_k"], p["cmp_w1_k"], p["cmp_w2_k"])
        vc = _compress(cv, p["cmp_pos_v"], p["cmp_w1_v"], p["cmp_w2_v"])
        o_cmp, sel = _cmp_attention(nq3, kc, vc, gate3, t, q_start, n_keys, BF16)
        o_slc = _flash_attention("slc", nq3, nk3, nv3, g, t, sel=sel, gate=gate3, gate_col=1)
        o_win = _flash_attention("win", nq3, nk3, nv3, 2 * g, t, gate=gate3, gate_col=2)
        keep = min(WINDOW, t)
        win_k, win_v = wk[:, t - keep:], wv[:, t - keep:]
    else:
        table = past["table"]
        n_pages = table.shape[1]
        page = past["page"]
        past_len = n_pages * page
        n_keys = past_len + t
        new_page = jnp.pad(jnp.transpose(logf, (0, 2, 1)), ((0, 0), (0, 0), (0, page - t)))
        bias = _fox_key_bias(past["fox_logf"], table, new_page)[:, :, :n_keys]
        causal = jnp.where(jnp.arange(t)[None, :] <= jnp.arange(t)[:, None], 0.0, NEG_INF).astype(F32)
        bias_new = bias[:, :, None, past_len:] + causal[None, None]
        qf = jnp.transpose(fq3.reshape(b, t, FOX_HEADS, HEAD_DIM), (0, 2, 1, 3))
        o = _paged_attention(qf, past["fox_k"], past["fox_v"], table, bias[:, :, None, :past_len],
                             fk3, fv3, bias_new)
        o_fox = jnp.transpose(o, (0, 2, 1, 3)).reshape(b, t, -1)
        nc = (n_keys - CMP_BLOCK) // CMP_STRIDE + 1
        assert (nc + 1) * CMP_STRIDE <= past_len and page % CMP_STRIDE == 0
        kc = _compress_paged(past["cmp_k"], table, p["cmp_pos_k"], p["cmp_w1_k"], p["cmp_w2_k"])
        vc = _compress_paged(past["cmp_v"], table, p["cmp_pos_v"], p["cmp_w1_v"], p["cmp_w2_v"])
        o_cmp, sel = _cmp_attention(nq3, kc, vc, gate3, t, q_start, n_keys, F32)
        rows = HEADS_PER_GROUP * t
        qn = jnp.transpose(nq3.reshape(b, t, g, HEADS_PER_GROUP, HEAD_DIM), (0, 2, 3, 1, 4)).reshape(b, g, rows, HEAD_DIM)
        gates = small[:, FOX_HEADS:FOX_HEADS + NSA_HEADS * N_NSA_BRANCHES].reshape(b, t, g, HEADS_PER_GROUP, N_NSA_BRANCHES)
        gates = jnp.transpose(gates, (4, 0, 2, 3, 1)).reshape(N_NSA_BRANCHES, b, g, rows, 1)
        ns = -(-n_keys // SEL_BLOCK)
        key_sel = jnp.repeat(sel[..., :ns], SEL_BLOCK, axis=-1)[..., :n_keys]
        sel_bias = jnp.where(key_sel > 0.5, 0.0, NEG_INF).astype(F32)
        sel_new = jnp.tile(sel_bias[..., past_len:] + causal[None, None], (1, 1, HEADS_PER_GROUP, 1))
        blocks_per_page = page // SEL_BLOCK
        live = (sel[..., :n_pages * blocks_per_page] > 0.5).any(axis=(1, 2)).reshape(b, n_pages, blocks_per_page).any(-1)
        n_live = jnp.sum(live, axis=-1).astype(jnp.int32)
        dest = jnp.where(live, jnp.cumsum(live, axis=-1) - 1, n_pages)
        page_ids = jnp.broadcast_to(jnp.arange(n_pages, dtype=jnp.int32)[None], (b, n_pages))
        order = jnp.zeros((b, n_pages + 1), jnp.int32).at[jnp.arange(b)[:, None], dest].set(page_ids)[:, :n_pages]
        is_live = jnp.arange(n_pages)[None, :] < n_live[:, None]
        order = jnp.where(is_live, order, order[:, :1])
        page_bias = sel_bias[..., :past_len].reshape(b, g, t, n_pages, page)
        page_bias = jnp.take_along_axis(page_bias, order[:, None, None, :, None], axis=3)
        page_bias = jnp.where(is_live[:, None, None, :, None], page_bias, NEG_INF).reshape(b, g, t, past_len)
        o = _paged_attention(qn, past["slc_k"], past["slc_v"], jnp.take_along_axis(table, order, axis=1), page_bias,
                             sk, sv, sel_new, gate=gates[1], n_live_pages=n_live)
        o_slc = _rows_to_tokens(o, b, t)
        win_len = past["win_k"].shape[1]
        assert win_len % page == 0
        k_start = q_start - win_len
        q_abs = q_start + jnp.arange(t)[:, None]
        k_abs = k_start + jnp.arange(win_len)[None, :]
        band = jnp.where((k_abs >= 0) & (k_abs > q_abs - WINDOW), 0.0, NEG_INF).astype(F32)
        band = jnp.broadcast_to(band[None, None], (b, g, t, win_len))
        band_new = jnp.where(jnp.arange(t)[None, :] > jnp.arange(t)[:, None] - WINDOW, causal, NEG_INF)
        band_new = jnp.broadcast_to(jnp.tile(band_new, (HEADS_PER_GROUP, 1))[None, None], (b, g, rows, t))
        o = _paged_attention(qn, past["win_k_pool"], past["win_v_pool"], past["win_table"], band, wk, wv,
                             band_new, gate=gates[2])
        o_win = _rows_to_tokens(o, b, t)
        wk_all = jnp.concatenate([past["win_k"], wk], axis=1)
        wv_all = jnp.concatenate([past["win_v"], wv], axis=1)
        keep = min(WINDOW, wk_all.shape[1])
        win_k, win_v = wk_all[:, -keep:], wv_all[:, -keep:]

    m = _merge(o_fox.reshape(n, -1), o_cmp.reshape(n, -1), o_slc.reshape(n, -1), o_win.reshape(n, -1),
               dense["up_fox"], dense["up_nsa"], mg, d)
    tm = _row_tile(n, 512)
    h = _matmul(m, dense["out"], _ep_residual, [x2d], [pl.BlockSpec((tm, 512), lambda i, j: (i, j))],
                jax.ShapeDtypeStruct((n, d), F32), pl.BlockSpec((tm, 512), lambda i, j: (i, j)), tm, 512, "out_proj")
    y = _moe(h, p["norm_ffn"], wts["router"], p["router_bias"], wts["gate_e"], wts["up_e"], wts["down_e"])

    def heads(a, hh):
        return a.reshape(b, -1, hh, HEAD_DIM)

    state = (heads(fk3, FOX_HEADS), heads(fv3, FOX_HEADS), logf, heads(ck, g), heads(cv, g), heads(sk, g),
             heads(sv, g), heads(win_k, g), heads(win_v, g))
    return y.reshape(b, t, d), state


def _rows_to_tokens(o, b, t):
    o = o.reshape(b, NSA_KV_GROUPS, HEADS_PER_GROUP, t, HEAD_DIM)
    return jnp.transpose(o, (0, 3, 1, 2, 4)).reshape(b, t, -1)


def _prepare(l, norm_mix, w_in, fox_f_bias, fox_q_norm, fox_k_norm, nsa_q_norm, nsa_k_norm, cmp_pos_k, cmp_w1_k,
             cmp_w2_k, cmp_pos_v, cmp_w1_v, cmp_w2_v, nsa_gate_bias, w_up_fox, w_up_nsa, w_out, norm_ffn,
             w_router_group, b_router_group, w_router_expert, b_router_expert, w_gate_e, w_up_e, w_down_e):
    d = w_in.shape[1]
    fw = FOX_HEADS * HEAD_DIM
    nw = NSA_HEADS * HEAD_DIM
    kvw = NSA_KV_GROUPS * HEAD_DIM
    sizes = (fw, fw, fw, FOX_HEADS, nw, kvw, kvw, kvw, kvw, kvw, kvw, NSA_HEADS * N_NSA_BRANCHES, 2 * d)
    offs = np.concatenate([[0], np.cumsum(sizes)])
    w = w_in[l]

    def seg(i):
        return w[:, offs[i]:offs[i + 1]]

    n_small = FOX_HEADS + NSA_HEADS * N_NSA_BRANCHES
    small_w = jnp.pad(jnp.concatenate([seg(3), seg(11)], axis=1), ((0, 0), (0, LANES - n_small)))
    small_b = jnp.pad(jnp.concatenate([fox_f_bias[l], nsa_gate_bias[l]]), (0, LANES - n_small)).reshape(1, LANES)
    n_route = N_GROUPS + N_EXPERTS
    router_w = jnp.pad(jnp.concatenate([w_router_group[l], w_router_expert[l]], axis=1), ((0, 0), (0, LANES - n_route)))
    router_b = jnp.pad(jnp.concatenate([b_router_group[l], b_router_expert[l]]), (0, LANES - n_route)).reshape(1, LANES)
    dense = {
        "fq": seg(0), "fk": seg(1), "fv": seg(2), "nq": seg(4),
        "nk": jnp.concatenate([seg(5), seg(7), seg(9)], axis=1),
        "nv": jnp.concatenate([seg(6), seg(8), seg(10)], axis=1),
        "mg": seg(12), "small": small_w,
        "up_fox": w_up_fox[l], "up_nsa": w_up_nsa[l], "out": w_out[l],
    }
    wts = {
        **{k: v.astype(BF16) for k, v in dense.items()},
        "f32": dense,
        "router": router_w,
        "gate_e": w_gate_e[l], "up_e": w_up_e[l], "down_e": w_down_e[l],
    }
    p = {
        "norm_mix": norm_mix[l], "fox_q_norm": fox_q_norm[l].reshape(1, -1), "fox_k_norm": fox_k_norm[l].reshape(1, -1),
        "nsa_q_norm": nsa_q_norm[l].reshape(1, -1), "nsa_k_norm": nsa_k_norm[l].reshape(1, -1),
        "small_bias": small_b, "cmp_pos_k": cmp_pos_k[l], "cmp_w1_k": cmp_w1_k[l], "cmp_w2_k": cmp_w2_k[l],
        "cmp_pos_v": cmp_pos_v[l], "cmp_w1_v": cmp_w1_v[l], "cmp_w2_v": cmp_w2_v[l],
        "norm_ffn": norm_ffn[l], "router_bias": router_b,
    }
    return wts, p


def kernel(x_prompt, x_sample, cache_fox_k, cache_fox_v, cache_fox_logf, cache_cmp_k, cache_cmp_v, cache_slc_k, cache_slc_v, cache_win_k, cache_win_v, page_table, norm_mix, w_in, fox_f_bias, fox_q_norm, fox_k_norm, nsa_q_norm, nsa_k_norm, cmp_pos_k, cmp_w1_k, cmp_w2_k, cmp_pos_v, cmp_w1_v, cmp_w2_v, nsa_gate_bias, w_up_fox, w_up_nsa, w_out, norm_ffn, w_router_group, b_router_group, w_router_expert, b_router_expert, w_gate_e, w_up_e, w_down_e):
    depth = w_in.shape[0]
    n_phys, page = cache_fox_k.shape[1], cache_fox_k.shape[2]
    past_len = page_table.shape[1] * page
    h_p, h_s = x_prompt, x_sample
    p_states, s_states = [], []
    for l in range(depth):
        wts, p = _prepare(l, norm_mix, w_in, fox_f_bias, fox_q_norm, fox_k_norm, nsa_q_norm, nsa_k_norm, cmp_pos_k,
                          cmp_w1_k, cmp_w2_k, cmp_pos_v, cmp_w1_v, cmp_w2_v, nsa_gate_bias, w_up_fox, w_up_nsa,
                          w_out, norm_ffn, w_router_group, b_router_group, w_router_expert, b_router_expert,
                          w_gate_e, w_up_e, w_down_e)
        h_p, st_p = _layer(h_p, p, wts, 0, None)
        p_states.append(st_p)
        win_len = cache_win_k.shape[2]
        n_win = win_len // page
        dec_b = page_table.shape[0]

        def pool(c):
            return c.reshape(-1, page * c.shape[3], HEAD_DIM)

        past = {
            "table": page_table + l * n_phys, "page": page,
            "fox_k": pool(cache_fox_k), "fox_v": pool(cache_fox_v),
            "fox_logf": jnp.transpose(cache_fox_logf.reshape(-1, page, cache_fox_logf.shape[3]), (0, 2, 1)),
            "cmp_k": pool(cache_cmp_k), "cmp_v": pool(cache_cmp_v),
            "slc_k": pool(cache_slc_k), "slc_v": pool(cache_slc_v),
            "win_k_pool": pool(cache_win_k), "win_v_pool": pool(cache_win_v),
            "win_table": (l * dec_b + jnp.arange(dec_b, dtype=jnp.int32))[:, None] * n_win
                         + jnp.arange(n_win, dtype=jnp.int32)[None, :],
            "win_k": cache_win_k[l].reshape(dec_b, win_len, -1), "win_v": cache_win_v[l].reshape(dec_b, win_len, -1),
        }
        h_s, st_s = _layer(h_s, p, wts, past_len, past)
        s_states.append(st_s)
    p_out = [jnp.stack([st[i] for st in p_states], axis=0) for i in range(9)]
    s_out = [jnp.stack([st[i] for st in s_states], axis=0) for i in range(9)]
    return (h_p, h_s, *p_out, *s_out)
```

```python
import functools

import numpy as np
import jax
import jax.numpy as jnp
from jax import lax
from jax.experimental import pallas as pl
from jax.experimental.pallas import tpu as pltpu

F32 = jnp.float32
BF16 = jnp.bfloat16

HEAD_DIM = 128
FOX_HEADS = 8
NSA_HEADS = 8
NSA_KV_GROUPS = 2
HEADS_PER_GROUP = NSA_HEADS // NSA_KV_GROUPS
N_NSA_BRANCHES = 3
CMP_BLOCK = 32
CMP_STRIDE = 16
SEL_BLOCK = 64
N_SELECT = 16
N_LOCAL_SEL = 2
WINDOW = 512
N_GROUPS = 4
EXPERTS_PER_GROUP = 8
N_EXPERTS = N_GROUPS * EXPERTS_PER_GROUP
ROPE_THETA = 10000.0
NORM_EPS = 1e-6
NEG_INF = -1e30
FORCE_SCORE = 1e9
PAD_SCORE = -2e38
TAKEN_SCORE = -3e38

LANES = 128
SUBLANES = 8
VMEM_LIMIT_BYTES = 48 * 1024 * 1024

HIGHEST = lax.Precision.HIGHEST


def _cparams(*sem):
    return pltpu.CompilerParams(dimension_semantics=tuple(sem), vmem_limit_bytes=VMEM_LIMIT_BYTES)


def _row_tile(n, pref):
    t = min(n, pref)
    assert n % t == 0, (n, t)
    return t


def _rmsnorm_body(x_ref, g_ref, o_ref):
    x = x_ref[...]
    y = x * lax.rsqrt(jnp.mean(x * x, axis=-1, keepdims=True) + NORM_EPS)
    o_ref[...] = (y * g_ref[...]).astype(o_ref.dtype)


def _rmsnorm(x2d, gain, out_dtype):
    n, d = x2d.shape
    tm = _row_tile(n, 512)
    return pl.pallas_call(
        _rmsnorm_body,
        grid=(n // tm,),
        in_specs=[pl.BlockSpec((tm, d), lambda i: (i, 0)), pl.BlockSpec((1, d), lambda i: (0, 0))],
        out_specs=pl.BlockSpec((tm, d), lambda i: (i, 0)),
        out_shape=jax.ShapeDtypeStruct((n, d), out_dtype),
        compiler_params=_cparams("parallel"),
        name="rmsnorm",
    )(x2d, gain.reshape(1, d))


def _dot_precision(dtype):
    return HIGHEST if dtype == F32 else None


def _mm_body(epilogue, n_extra, a_ref, w_ref, *rest):
    acc = jnp.dot(a_ref[...], w_ref[...], preferred_element_type=F32, precision=_dot_precision(a_ref.dtype))
    epilogue(acc, rest[:n_extra], rest[n_extra:])


def _matmul(a, w, epilogue, extras, extra_specs, out_shapes, out_specs, tm, tn, name):
    n, k = a.shape
    c = w.shape[1]
    assert n % tm == 0 and c % tn == 0, (n, tm, c, tn)
    return pl.pallas_call(
        functools.partial(_mm_body, epilogue, len(extras)),
        grid=(n // tm, c // tn),
        in_specs=[pl.BlockSpec((tm, k), lambda i, j: (i, 0)), pl.BlockSpec((k, tn), lambda i, j: (0, j))]
        + list(extra_specs),
        out_specs=out_specs,
        out_shape=out_shapes,
        compiler_params=_cparams("parallel", "parallel"),
        name=name,
    )(a, w, *extras)


def _head_norm(x, gain):
    return x * lax.rsqrt(jnp.mean(x * x, axis=-1, keepdims=True) + NORM_EPS) * gain


def _rope_rot(x, cos, sin_signed):
    return x * cos + pltpu.roll(x, HEAD_DIM // 2, 1) * sin_signed


def _ep_head_norm(scale, acc, extras, outs):
    (gain_ref,) = extras
    (o_ref,) = outs
    for h in range(acc.shape[1] // HEAD_DIM):
        sl = slice(h * HEAD_DIM, (h + 1) * HEAD_DIM)
        o_ref[:, sl] = (_head_norm(acc[:, sl], gain_ref[...]) * scale).astype(o_ref.dtype)


def _ep_head_norm_rope(scale, acc, extras, outs):
    gain_ref, cos_ref, sin_ref = extras
    (o_ref,) = outs
    for h in range(acc.shape[1] // HEAD_DIM):
        sl = slice(h * HEAD_DIM, (h + 1) * HEAD_DIM)
        y = _rope_rot(_head_norm(acc[:, sl], gain_ref[...]), cos_ref[...], sin_ref[...])
        o_ref[:, sl] = (y * scale).astype(o_ref.dtype)


def _ep_identity(acc, extras, outs):
    (o_ref,) = outs
    o_ref[...] = acc.astype(o_ref.dtype)


def _ep_sigmoid(acc, extras, outs):
    (o_ref,) = outs
    o_ref[...] = jax.nn.sigmoid(acc).astype(o_ref.dtype)


def _ep_small(acc, extras, outs):
    (bias_ref,) = extras
    (o_ref,) = outs
    z = acc + bias_ref[...]
    col = lax.broadcasted_iota(jnp.int32, z.shape, 1)
    e = jnp.exp(-jnp.abs(z))
    log_sig = jnp.minimum(z, 0.0) - jnp.log(1.0 + e)
    sig = jnp.where(z >= 0, 1.0, e) / (1.0 + e)
    o_ref[...] = jnp.where(col < FOX_HEADS, log_sig, sig)


def _ep_residual(acc, extras, outs):
    (x_ref,) = extras
    (o_ref,) = outs
    o_ref[...] = x_ref[...] + acc


def _fox_bias_body(n_pages, has_new, *refs):
    if has_new:
        tbl_ref, new_ref, pool_ref, o_ref, buf, sem = refs
    else:
        tbl_ref, pool_ref, o_ref, buf, sem = refs
    bi = pl.program_id(0)

    def copy(p):
        return pltpu.make_async_copy(pool_ref.at[tbl_ref[bi * n_pages + p]], buf.at[p], sem)

    def start(p, c):
        copy(p).start()
        return c

    def wait(p, c):
        copy(p).wait()
        return c

    lax.fori_loop(0, n_pages, start, 0)
    lax.fori_loop(0, n_pages, wait, 0)
    if has_new:
        buf[n_pages] = new_ref[0]
    x = buf[...]
    p, h, w = x.shape
    within = x.reshape(p * h, w)
    lane = lax.broadcasted_iota(jnp.int32, within.shape, 1)
    s = 1
    while s < w:
        within = within + jnp.where(lane >= s, pltpu.roll(within, s, 1), 0.0)
        s *= 2
    within = within.reshape(p, h, w)
    total = jnp.broadcast_to(within[:, :, w - 1:w], x.shape)
    incl = total
    s = 1
    while s < p:
        incl = incl + jnp.concatenate([jnp.zeros((s, h, w), F32), incl[:p - s]], axis=0)
        s *= 2
    o_ref[0] = -(within + (incl - total))


def _fox_key_bias(pool, table, new_page):
    b, n_pages = table.shape
    h, w = pool.shape[1:]
    has_new = new_page is not None
    p_tot = n_pages + (1 if has_new else 0)
    in_specs = [pl.BlockSpec(memory_space=pl.ANY)]
    args = [pool]
    if has_new:
        in_specs.insert(0, pl.BlockSpec((1, h, w), lambda i, tbl: (i, 0, 0)))
        args.insert(0, new_page)
    grid_spec = pltpu.PrefetchScalarGridSpec(
        num_scalar_prefetch=1,
        grid=(b,),
        in_specs=in_specs,
        out_specs=pl.BlockSpec((1, p_tot, h, w), lambda i, tbl: (i, 0, 0, 0)),
        scratch_shapes=[pltpu.VMEM((p_tot, h, w), F32), pltpu.SemaphoreType.DMA(())],
    )
    out = pl.pallas_call(
        functools.partial(_fox_bias_body, n_pages, has_new),
        grid_spec=grid_spec,
        out_shape=jax.ShapeDtypeStruct((b, p_tot, h, w), F32),
        compiler_params=_cparams("arbitrary"),
        name="fox_key_bias",
    )(table.reshape(-1), *args)
    return jnp.transpose(out, (0, 2, 1, 3)).reshape(b, h, p_tot * w)


FLASH_QUERY_BLOCK = 512
FLASH_KEY_BLOCK = 512


def _flash_body(mode, heads, tq, tk, n_kv, gate_col, *refs):
    if mode == "fox":
        q_ref, k_ref, v_ref, bias_ref, o_ref, kb_ref, vb_ref = refs
    elif mode == "slc":
        q_ref, k_ref, v_ref, sel_ref, expand_ref, gate_ref, o_ref, kb_ref, vb_ref, selx_ref = refs
    else:
        q_ref, k_ref, v_ref, gate_ref, o_ref, kb_ref, vb_ref = refs
    unit = pl.program_id(1)
    qi = pl.program_id(2)

    @pl.when(qi == 0)
    def _():
        kb_ref[...] = k_ref[0].astype(BF16)
        vb_ref[...] = v_ref[0].astype(BF16)

    if mode == "slc":
        sel = sel_ref[0, 0].astype(BF16)
        for j in range(n_kv):
            selx_ref[j] = jnp.dot(sel, expand_ref[:, j * tk:(j + 1) * tk], preferred_element_type=F32)

    q_pos = qi * tq + lax.broadcasted_iota(jnp.int32, (tq, tk), 0)
    col = lax.broadcasted_iota(jnp.int32, (tq, tk), 1)
    qs = [q_ref[0, :, r * HEAD_DIM:(r + 1) * HEAD_DIM] for r in range(heads)]

    def block(j, carry, diagonal):
        ks = pl.multiple_of(j * tk, tk)
        k_pos = ks + col
        ok = None
        if mode == "slc":
            ok = selx_ref[j] > 0.5
        if mode == "win":
            ok = (k_pos <= q_pos) & (k_pos > q_pos - WINDOW)
        elif diagonal:
            ok = (k_pos <= q_pos) if ok is None else ok & (k_pos <= q_pos)
        out = []
        for r in range(heads):
            m, l, acc = carry[r]
            c0 = r * HEAD_DIM if mode == "fox" else 0
            kb = kb_ref[pl.ds(ks, tk), c0:c0 + HEAD_DIM]
            vb = vb_ref[pl.ds(ks, tk), c0:c0 + HEAD_DIM]
            s = lax.dot_general(qs[r], kb, _NT, preferred_element_type=F32)
            if mode == "fox":
                s = s + bias_ref[0, r:r + 1, pl.ds(ks, tk)]
            if ok is not None:
                s = jnp.where(ok, s, NEG_INF)
            m_new = jnp.maximum(m, jnp.max(s, axis=-1, keepdims=True))
            alpha = jnp.exp(m - m_new)
            p = jnp.exp(s - m_new)
            l = alpha * l + jnp.sum(p, axis=-1, keepdims=True)
            acc = alpha * acc + jnp.dot(p.astype(BF16), vb, preferred_element_type=F32)
            out.append((m_new, l, acc))
        return tuple(out)

    init = tuple((jnp.full((tq, 1), NEG_INF, F32), jnp.zeros((tq, 1), F32), jnp.zeros((tq, HEAD_DIM), F32))
                 for _ in range(heads))
    first_diag = (qi * tq) // tk
    n_diag = -(-tq // tk)
    if mode == "win":
        lo = jnp.maximum(qi * tq - WINDOW + 1, 0) // tk
        state = lax.fori_loop(lo, first_diag + n_diag, functools.partial(block, diagonal=False), init)
    else:
        state = lax.fori_loop(0, first_diag, functools.partial(block, diagonal=False), init)
        for dj in range(n_diag):
            state = block(first_diag + dj, state, True)

    for r in range(heads):
        m, l, acc = state[r]
        w = 1.0 / l
        if mode != "fox":
            gate = gate_ref[0]
            lane = lax.broadcasted_iota(jnp.int32, gate.shape, 1)
            want = FOX_HEADS + (unit * heads + r) * N_NSA_BRANCHES + gate_col
            w = w * jnp.sum(jnp.where(lane == want, gate, 0.0), axis=-1, keepdims=True)
        o_ref[0, :, r * HEAD_DIM:(r + 1) * HEAD_DIM] = (acc * w).astype(o_ref.dtype)


def _flash_attention(mode, q, k, v, k_col0, t, *, bias=None, sel=None, gate=None, gate_col=0):
    b = q.shape[0]
    heads = HEADS_PER_GROUP
    qw = heads * HEAD_DIM
    units = q.shape[2] // qw
    kvw = qw if mode == "fox" else HEAD_DIM
    tq = _row_tile(t, FLASH_QUERY_BLOCK)
    tk = _row_tile(t, FLASH_KEY_BLOCK)
    n_kv = t // tk
    in_specs = [
        pl.BlockSpec((1, tq, qw), lambda bi, u, i: (bi, i, u)),
        pl.BlockSpec((1, t, kvw), lambda bi, u, i: (bi, 0, k_col0 + u)),
        pl.BlockSpec((1, t, kvw), lambda bi, u, i: (bi, 0, k_col0 + u)),
    ]
    args = [q, k, v]
    scratch = [pltpu.VMEM((t, kvw), BF16), pltpu.VMEM((t, kvw), BF16)]
    if mode == "fox":
        in_specs.append(pl.BlockSpec((1, heads, t), lambda bi, u, i: (bi * units + u, 0, 0)))
        args.append(bias.reshape(b * units, heads, t))
    else:
        if mode == "slc":
            nsp = sel.shape[-1]
            expand = (np.arange(nsp)[:, None] == (np.arange(t)[None, :] // SEL_BLOCK)).astype(np.float32)
            in_specs.append(pl.BlockSpec((1, 1, tq, nsp), lambda bi, u, i: (bi, u, i, 0)))
            in_specs.append(pl.BlockSpec((nsp, t), lambda bi, u, i: (0, 0)))
            args += [sel, jnp.asarray(expand, BF16)]
            scratch.append(pltpu.VMEM((n_kv, tq, tk), F32))
        in_specs.append(pl.BlockSpec((1, tq, LANES), lambda bi, u, i: (bi, i, 0)))
        args.append(gate)
    return pl.pallas_call(
        functools.partial(_flash_body, mode, heads, tq, tk, n_kv, gate_col),
        grid=(b, units, t // tq),
        in_specs=in_specs,
        out_specs=pl.BlockSpec((1, tq, qw), lambda bi, u, i: (bi, i, u)),
        out_shape=jax.ShapeDtypeStruct((b, t, units * qw), BF16),
        scratch_shapes=scratch,
        compiler_params=_cparams("parallel", "parallel", "arbitrary"),
        name="attn_" + mode,
    )(*args)


PAGED_BUFFER_BYTES = 2 * 1024 * 1024
PAGED_COPY_BYTES = 128 * 1024
CMP_BUFFER_BYTES = 4 * 1024 * 1024


def _pages_per_step(n_pages, page_bytes, buffer_bytes=PAGED_BUFFER_BYTES):
    pg = max(1, min(n_pages, buffer_bytes // page_bytes))
    while n_pages % pg:
        pg -= 1
    return pg


def _paged_body(units, rows, rows_b, n_pages, pg, page, use_gate, *refs):
    if use_gate:
        (tbl_ref, q_ref, bias_ref, kn_ref, vn_ref, biasn_ref, gate_ref, kpool, vpool, o_ref,
         kbuf, vbuf, sem, m_ref, l_ref, acc_ref) = refs
    else:
        (tbl_ref, q_ref, bias_ref, kn_ref, vn_ref, biasn_ref, kpool, vpool, o_ref,
         kbuf, vbuf, sem, m_ref, l_ref, acc_ref) = refs
        gate_ref = None
    bi = pl.program_id(0)
    s = pl.program_id(1)
    n_steps = n_pages // pg
    rpp = page * units
    n_keys = pg * page

    n_chunks = max(1, (rpp * HEAD_DIM * 4) // PAGED_COPY_BYTES)
    crows = rpp // n_chunks

    def copies(step, slot):
        out = []
        for k in range(pg):
            phys = tbl_ref[bi * n_pages + step * pg + k]
            for c in range(n_chunks):
                src = pl.ds(c * crows, crows)
                dst = pl.ds(k * rpp + c * crows, crows)
                out.append(pltpu.make_async_copy(kpool.at[phys, src], kbuf.at[slot, dst], sem.at[0, slot]))
                out.append(pltpu.make_async_copy(vpool.at[phys, src], vbuf.at[slot, dst], sem.at[1, slot]))
        return out

    @pl.when(s == 0)
    def _():
        m_ref[...] = jnp.full(m_ref.shape, NEG_INF, F32)
        l_ref[...] = jnp.zeros(l_ref.shape, F32)
        acc_ref[...] = jnp.zeros(acc_ref.shape, F32)
        for c in copies(0, 0):
            c.start()

    @pl.when(s + 1 < n_steps)
    def _():
        for c in copies(s + 1, lax.rem(s + 1, 2)):
            c.start()

    def attend(k, v, bias, own_blocks):
        k_hi, k_lo = _split_bf16(k)
        v_hi, v_lo = _split_bf16(v)
        sc = _dot_split(q_ref[0], k_hi, k_lo, _NT)
        nk = k.shape[0] // units
        if own_blocks:
            sc = jnp.concatenate([sc[u * rows:(u + 1) * rows, u * nk:(u + 1) * nk] for u in range(units)], axis=0)
        sc = sc + bias
        m = m_ref[...]
        m_new = jnp.maximum(m, jnp.max(sc, axis=-1, keepdims=True))
        alpha = jnp.exp(m - m_new)
        pr = jnp.exp(sc - m_new)
        l_ref[...] = alpha * l_ref[...] + jnp.sum(pr, axis=-1, keepdims=True)
        if own_blocks:
            zero = jnp.zeros((rows, nk), F32)
            pr = jnp.concatenate(
                [jnp.concatenate([pr[u * rows:(u + 1) * rows] if u2 == u else zero for u2 in range(units)], axis=1)
                 for u in range(units)], axis=0)
        acc_ref[...] = alpha * acc_ref[...] + _dot_split(pr, v_hi, v_lo, _NN)
        m_ref[...] = m_new

    def unit_bias(u):
        bb = bias_ref[0, u]
        if rows_b == 1:
            return jnp.broadcast_to(bb, (rows, n_keys))
        return jnp.concatenate([bb] * (rows // rows_b), axis=0)

    @pl.when(s < n_steps)
    def _():
        slot = lax.rem(s, 2)
        for c in copies(s, slot):
            c.wait()
        k = jnp.concatenate([kbuf[slot, pl.ds(u, n_keys, stride=units), :] for u in range(units)], axis=0)
        v = jnp.concatenate([vbuf[slot, pl.ds(u, n_keys, stride=units), :] for u in range(units)], axis=0)
        attend(k, v, jnp.concatenate([unit_bias(u) for u in range(units)], axis=0), True)

    @pl.when(s == n_steps)
    def _():
        k = jnp.concatenate([kn_ref[0, :, u * HEAD_DIM:(u + 1) * HEAD_DIM] for u in range(units)], axis=0)
        v = jnp.concatenate([vn_ref[0, :, u * HEAD_DIM:(u + 1) * HEAD_DIM] for u in range(units)], axis=0)
        attend(k, v, biasn_ref[0], False)
        w = 1.0 / l_ref[...]
        if use_gate:
            w = w * gate_ref[0]
        o_ref[0] = acc_ref[...] * w


def _paged_attention(q, k_pool, v_pool, table, bias, k_new, v_new, bias_new, gate=None):
    b, units, rows, _ = q.shape
    ur = units * rows
    n_pages = table.shape[1]
    rpp = k_pool.shape[1]
    page = rpp // units
    rows_b = bias.shape[2]
    n_new = k_new.shape[1]
    width = k_new.shape[2]
    pg = _pages_per_step(n_pages, rpp * HEAD_DIM * 4)
    n_steps = n_pages // pg
    last = n_steps - 1
    same_unit = jnp.arange(units)[:, None, None, None] == jnp.arange(units)[None, None, :, None]
    bias_new_full = jnp.where(same_unit[None], bias_new[:, :, :, None, :], NEG_INF).reshape(b, ur, units * n_new)
    in_specs = [
        pl.BlockSpec((1, ur, HEAD_DIM), lambda bi, p, tbl: (bi, 0, 0)),
        pl.BlockSpec((1, units, rows_b, pg * page), lambda bi, p, tbl: (bi, 0, 0, jnp.minimum(p, last))),
        pl.BlockSpec((1, n_new, width), lambda bi, p, tbl: (bi, 0, 0)),
        pl.BlockSpec((1, n_new, width), lambda bi, p, tbl: (bi, 0, 0)),
        pl.BlockSpec((1, ur, units * n_new), lambda bi, p, tbl: (bi, 0, 0)),
    ]
    args = [q.reshape(b, ur, HEAD_DIM), bias, k_new, v_new, bias_new_full]
    if gate is not None:
        in_specs.append(pl.BlockSpec((1, ur, 1), lambda bi, p, tbl: (bi, 0, 0)))
        args.append(gate.reshape(b, ur, 1))
    in_specs += [pl.BlockSpec(memory_space=pl.ANY), pl.BlockSpec(memory_space=pl.ANY)]
    args += [k_pool, v_pool]
    grid_spec = pltpu.PrefetchScalarGridSpec(
        num_scalar_prefetch=1,
        grid=(b, n_steps + 1),
        in_specs=in_specs,
        out_specs=pl.BlockSpec((1, ur, HEAD_DIM), lambda bi, p, tbl: (bi, 0, 0)),
        scratch_shapes=[pltpu.VMEM((2, pg * rpp, HEAD_DIM), F32), pltpu.VMEM((2, pg * rpp, HEAD_DIM), F32),
                        pltpu.SemaphoreType.DMA((2, 2)),
                        pltpu.VMEM((ur, 1), F32), pltpu.VMEM((ur, 1), F32), pltpu.VMEM((ur, HEAD_DIM), F32)],
    )
    out = pl.pallas_call(
        functools.partial(_paged_body, units, rows, rows_b, n_pages, pg, page, gate is not None),
        grid_spec=grid_spec,
        out_shape=jax.ShapeDtypeStruct((b, ur, HEAD_DIM), F32),
        compiler_params=_cparams("arbitrary", "arbitrary"),
        name="attn_paged",
    )(table.reshape(-1), *args)
    return out.reshape(b, units, rows, HEAD_DIM)


def _cmp_proj_body(x_ref, pa_ref, pb_ref, wa_ref, wb_ref, za_ref, zb_ref):
    x = x_ref[0]
    za_ref[0] = jnp.dot((x + pa_ref[...]).astype(BF16), wa_ref[...], preferred_element_type=F32)
    zb_ref[0] = jnp.dot((x + pb_ref[...]).astype(BF16), wb_ref[...], preferred_element_type=F32)


def _cmp_mlp_body(n_chunks, za_ref, zb_ref, w2_ref, o_ref):
    zb_next = pltpu.roll(zb_ref[0], n_chunks - 1, 0)
    hid = za_ref[0] + zb_next
    c = 0.7978845608028654
    hid = 0.5 * hid * (1.0 + jnp.tanh(c * (hid + 0.044715 * (hid * hid * hid))))
    w2 = w2_ref[...]
    hb = hid.astype(w2.dtype)
    for g in range(NSA_KV_GROUPS):
        sl = slice(g * HEAD_DIM, (g + 1) * HEAD_DIM)
        o_ref[0, :, sl] = jnp.dot(hb[:, sl], w2, preferred_element_type=F32, precision=_dot_precision(w2.dtype))


def _compress(rows, pos_emb, w1, w2):
    b, tc, gw = rows.shape
    g = NSA_KV_GROUPS
    n_chunks = tc // CMP_STRIDE
    cw = CMP_STRIDE * gw
    x = rows.reshape(b, n_chunks, cw)
    w1r = w1.reshape(2, CMP_STRIDE, HEAD_DIM, HEAD_DIM)
    eye = jnp.eye(g, dtype=w1.dtype)
    wcat = jnp.einsum("hldj,ge->hlgdej", w1r, eye).reshape(2, cw, g * HEAD_DIM).astype(BF16)
    pos = jnp.broadcast_to(pos_emb.reshape(2, CMP_STRIDE, 1, HEAD_DIM), (2, CMP_STRIDE, g, HEAD_DIM)).reshape(2, 1, cw)
    cm = _row_tile(n_chunks, 128)
    za, zb = pl.pallas_call(
        _cmp_proj_body,
        grid=(b, n_chunks // cm),
        in_specs=[
            pl.BlockSpec((1, cm, cw), lambda bi, i: (bi, i, 0)),
            pl.BlockSpec((1, cw), lambda bi, i: (0, 0)),
            pl.BlockSpec((1, cw), lambda bi, i: (0, 0)),
            pl.BlockSpec((cw, g * HEAD_DIM), lambda bi, i: (0, 0)),
            pl.BlockSpec((cw, g * HEAD_DIM), lambda bi, i: (0, 0)),
        ],
        out_specs=[pl.BlockSpec((1, cm, g * HEAD_DIM), lambda bi, i: (bi, i, 0))] * 2,
        out_shape=[jax.ShapeDtypeStruct((b, n_chunks, g * HEAD_DIM), F32)] * 2,
        compiler_params=_cparams("parallel", "parallel"),
        name="cmp_proj",
    )(x, pos[0], pos[1], wcat[0], wcat[1])
    return _cmp_mlp(za, zb, w2.astype(BF16))


def _cmp_mlp(za, zb, w2):
    b, n_chunks, gw = za.shape
    g = NSA_KV_GROUPS
    return pl.pallas_call(
        functools.partial(_cmp_mlp_body, n_chunks),
        grid=(b,),
        in_specs=[
            pl.BlockSpec((1, n_chunks, g * HEAD_DIM), lambda bi: (bi, 0, 0)),
            pl.BlockSpec((1, n_chunks, g * HEAD_DIM), lambda bi: (bi, 0, 0)),
            pl.BlockSpec((HEAD_DIM, HEAD_DIM), lambda bi: (0, 0)),
        ],
        out_specs=pl.BlockSpec((1, n_chunks, g * HEAD_DIM), lambda bi: (bi, 0, 0)),
        out_shape=jax.ShapeDtypeStruct((b, n_chunks, g * HEAD_DIM), F32),
        compiler_params=_cparams("parallel"),
        name="cmp_mlp",
    )(za, zb, w2)


def _split_bf16(x):
    hi = x.astype(BF16)
    return hi, (x - hi.astype(F32)).astype(BF16)


def _dot_split(a, b_hi, b_lo, dims):
    a_hi, a_lo = _split_bf16(a)
    n = a.shape[0]
    both = lax.dot_general(jnp.concatenate([a_hi, a_lo], axis=0), b_hi, dims, preferred_element_type=F32)
    return both[:n] + both[n:] + lax.dot_general(a_hi, b_lo, dims, preferred_element_type=F32)


_NT = (((1,), (1,)), ((), ()))
_NN = (((1,), (0,)), ((), ()))


def _cmp_paged_body(n_pages, pg, page, tbl_ref, pos_ref, w_ref, pool, za_ref, zb_ref, xbuf, whi_ref, wlo_ref, sem):
    bi = pl.program_id(0)
    s = pl.program_id(1)
    n_steps = n_pages // pg
    g_n = NSA_KV_GROUPS
    rpp = page * g_n
    n_rows = pg * page // CMP_STRIDE

    def copies(step, slot):
        return [pltpu.make_async_copy(pool.at[tbl_ref[bi * n_pages + step * pg + k]],
                                      xbuf.at[slot, pl.ds(k * rpp, rpp)], sem.at[slot]) for k in range(pg)]

    @pl.when(s == 0)
    def _():
        for c in copies(0, 0):
            c.start()
        w_hi, w_lo = _split_bf16(w_ref[...])
        whi_ref[...] = w_hi
        wlo_ref[...] = w_lo

    @pl.when(s + 1 < n_steps)
    def _():
        for c in copies(s + 1, lax.rem(s + 1, 2)):
            c.start()

    slot = lax.rem(s, 2)
    for c in copies(s, slot):
        c.wait()
    w_hi = whi_ref[...]
    w_lo = wlo_ref[...]
    pw = _dot_split(pos_ref[...], w_hi, w_lo, _NN)
    col = lax.broadcasted_iota(jnp.int32, (1, 2 * HEAD_DIM), 1)
    pos_term = jnp.where(col < HEAD_DIM, pw[0:1, :], pw[1:2, :])
    x_cat = jnp.concatenate(
        [jnp.concatenate([xbuf[slot, pl.ds(l * g_n + g, n_rows, stride=CMP_STRIDE * g_n), :]
                          for l in range(CMP_STRIDE)], axis=1) for g in range(g_n)], axis=0)
    z = _dot_split(x_cat, w_hi, w_lo, _NN) + pos_term
    for g in range(g_n):
        za_ref[0, :, g * HEAD_DIM:(g + 1) * HEAD_DIM] = z[g * n_rows:(g + 1) * n_rows, :HEAD_DIM]
        zb_ref[0, :, g * HEAD_DIM:(g + 1) * HEAD_DIM] = z[g * n_rows:(g + 1) * n_rows, HEAD_DIM:]


def _compress_paged(pool, table, pos_emb, w1, w2):
    b, n_pages = table.shape
    rpp = pool.shape[1]
    page = rpp // NSA_KV_GROUPS
    gw = NSA_KV_GROUPS * HEAD_DIM
    pg = _pages_per_step(n_pages, rpp * HEAD_DIM * 4, CMP_BUFFER_BYTES)
    n_steps = n_pages // pg
    n_rows = pg * page // CMP_STRIDE
    n_chunks = n_pages * page // CMP_STRIDE
    half = CMP_STRIDE * HEAD_DIM
    w_cat = jnp.concatenate([w1[:half], w1[half:]], axis=1)
    pos_rows = jnp.pad(pos_emb.reshape(2, half), ((0, SUBLANES - 2), (0, 0)))
    grid_spec = pltpu.PrefetchScalarGridSpec(
        num_scalar_prefetch=1,
        grid=(b, n_steps),
        in_specs=[pl.BlockSpec((SUBLANES, half), lambda bi, i, tbl: (0, 0)),
                  pl.BlockSpec((half, 2 * HEAD_DIM), lambda bi, i, tbl: (0, 0)),
                  pl.BlockSpec(memory_space=pl.ANY)],
        out_specs=[pl.BlockSpec((1, n_rows, gw), lambda bi, i, tbl: (bi, i, 0))] * 2,
        scratch_shapes=[pltpu.VMEM((2, pg * rpp, HEAD_DIM), F32), pltpu.VMEM((half, 2 * HEAD_DIM), BF16),
                        pltpu.VMEM((half, 2 * HEAD_DIM), BF16), pltpu.SemaphoreType.DMA((2,))],
    )
    za, zb = pl.pallas_call(
        functools.partial(_cmp_paged_body, n_pages, pg, page),
        grid_spec=grid_spec,
        out_shape=[jax.ShapeDtypeStruct((b, n_chunks, gw), F32)] * 2,
        compiler_params=_cparams("arbitrary", "arbitrary"),
        name="cmp_proj_paged",
    )(table.reshape(-1), pos_rows, w_cat, pool)
    return _cmp_mlp(za, zb, w2)


def _cmp_attn_body(tq, nc, ns, n_pick, q_start, q_ref, kc_ref, vc_ref, cover_ref, gate_ref, o_ref, sel_ref):
    unit = pl.program_id(1)
    qi = pl.program_id(2)
    precise = q_ref.dtype == F32
    if precise:
        kc_hi, kc_lo = _split_bf16(kc_ref[0])
        vc_hi, vc_lo = _split_bf16(vc_ref[0])
    else:
        kc_hi = kc_ref[0].astype(BF16)
        vc_hi = vc_ref[0].astype(BF16)
    ncp = kc_hi.shape[0]
    q_pos = q_start + qi * tq + lax.broadcasted_iota(jnp.int32, (tq, ncp), 0)
    n_idx = lax.broadcasted_iota(jnp.int32, (tq, ncp), 1)
    visible = (n_idx * CMP_STRIDE + (CMP_BLOCK - 1) <= q_pos) & (n_idx < nc)
    gate = gate_ref[0]
    glane = lax.broadcasted_iota(jnp.int32, gate.shape, 1)
    p_grp = jnp.zeros((tq, ncp), F32)
    for r in range(HEADS_PER_GROUP):
        q = q_ref[0, :, r * HEAD_DIM:(r + 1) * HEAD_DIM]
        if precise:
            s = _dot_split(q, kc_hi, kc_lo, _NT)
        else:
            s = lax.dot_general(q, kc_hi, _NT, preferred_element_type=F32)
        s = jnp.where(visible, s, NEG_INF)
        m = jnp.max(s, axis=-1, keepdims=True)
        e = jnp.where(visible, jnp.exp(s - m), 0.0)
        l = jnp.sum(e, axis=-1, keepdims=True)
        p = e / jnp.where(l > 0.0, l, 1.0)
        p_grp = p_grp + p
        want = FOX_HEADS + (unit * HEADS_PER_GROUP + r) * N_NSA_BRANCHES
        gcol = jnp.sum(jnp.where(glane == want, gate, 0.0), axis=-1, keepdims=True)
        if precise:
            o = _dot_split(p, vc_hi, vc_lo, _NN)
        else:
            o = jnp.dot(p.astype(BF16), vc_hi, preferred_element_type=F32)
        o_ref[0, :, r * HEAD_DIM:(r + 1) * HEAD_DIM] = (o * gcol).astype(o_ref.dtype)

    imp = jnp.dot(p_grp, cover_ref[...], preferred_element_type=F32, precision=HIGHEST)
    nsp = imp.shape[1]
    j = lax.broadcasted_iota(jnp.int32, (tq, nsp), 1)
    cur = (q_start + qi * tq + lax.broadcasted_iota(jnp.int32, (tq, nsp), 0)) // SEL_BLOCK
    dist = cur - j
    forced = (j == 0) | ((dist >= 0) & (dist < N_LOCAL_SEL))
    score = jnp.where(forced, FORCE_SCORE, imp)
    score = jnp.where(dist >= 0, score, NEG_INF)
    score = jnp.where(j < ns, score, PAD_SCORE)
    sel = jnp.zeros((tq, nsp), F32)
    j_f = j.astype(F32)
    for _ in range(n_pick):
        mx = jnp.max(score, axis=-1, keepdims=True)
        first = jnp.min(jnp.where(score == mx, j_f, float(nsp)), axis=-1, keepdims=True)
        hit = j_f == first
        sel = jnp.where(hit, 1.0, sel)
        score = jnp.where(hit, TAKEN_SCORE, score)
    sel_ref[0, 0] = sel


def _cmp_attention(nq, kc, vc, gate, t, q_start, n_keys, out_dtype):
    b = nq.shape[0]
    ncp = kc.shape[1]
    nc = (n_keys - CMP_BLOCK) // CMP_STRIDE + 1
    ns = -(-n_keys // SEL_BLOCK)
    nsp = -(-ns // LANES) * LANES
    n_pick = min(N_SELECT, ns)
    c0 = np.arange(ncp)[:, None] * CMP_STRIDE
    s0 = np.arange(nsp)[None, :] * SEL_BLOCK
    cover = np.clip(np.minimum(c0 + CMP_BLOCK, s0 + SEL_BLOCK) - np.maximum(c0, s0), 0, None).astype(np.float32) / CMP_BLOCK
    cover = cover * (np.arange(ncp)[:, None] < nc) * (np.arange(nsp)[None, :] < ns)
    tq = _row_tile(t, 256)
    qw = HEADS_PER_GROUP * HEAD_DIM
    return pl.pallas_call(
        functools.partial(_cmp_attn_body, tq, nc, ns, n_pick, q_start),
        grid=(b, NSA_KV_GROUPS, t // tq),
        in_specs=[
            pl.BlockSpec((1, tq, qw), lambda bi, u, i: (bi, i, u)),
            pl.BlockSpec((1, ncp, HEAD_DIM), lambda bi, u, i: (bi, 0, u)),
            pl.BlockSpec((1, ncp, HEAD_DIM), lambda bi, u, i: (bi, 0, u)),
            pl.BlockSpec((ncp, nsp), lambda bi, u, i: (0, 0)),
            pl.BlockSpec((1, tq, LANES), lambda bi, u, i: (bi, i, 0)),
        ],
        out_specs=[
            pl.BlockSpec((1, tq, qw), lambda bi, u, i: (bi, i, u)),
            pl.BlockSpec((1, 1, tq, nsp), lambda bi, u, i: (bi, u, i, 0)),
        ],
        out_shape=[
            jax.ShapeDtypeStruct((b, t, NSA_KV_GROUPS * qw), out_dtype),
            jax.ShapeDtypeStruct((b, NSA_KV_GROUPS, t, nsp), F32),
        ],
        compiler_params=_cparams("parallel", "parallel", "parallel"),
        name="attn_cmp_select",
    )(nq, kc, vc, jnp.asarray(cover), gate)


def _merge_body(a_ref, b1_ref, b2_ref, b3_ref, ua_ref, ub_ref, ga_ref, gb_ref, o_ref):
    a = a_ref[...]
    prec = _dot_precision(a.dtype)
    bsum = (b1_ref[...].astype(F32) + b2_ref[...].astype(F32) + b3_ref[...].astype(F32)).astype(a.dtype)
    ya = jnp.dot(a, ua_ref[...], preferred_element_type=F32, precision=prec)
    yb = jnp.dot(bsum, ub_ref[...], preferred_element_type=F32, precision=prec)
    o_ref[...] = (ga_ref[...] * ya + gb_ref[...] * yb).astype(o_ref.dtype)


def _merge(o_fox, o_cmp, o_slc, o_win, u_fox, u_nsa, mg, d_model):
    n, w = o_fox.shape
    tm = _row_tile(n, MERGE_ROW_TILE)
    tn = 512
    nj = d_model // tn
    row = pl.BlockSpec((tm, w), lambda i, j: (i, 0))
    return pl.pallas_call(
        _merge_body,
        grid=(n // tm, nj),
        in_specs=[row, row, row, row,
                  pl.BlockSpec((w, tn), lambda i, j: (0, j)),
                  pl.BlockSpec((w, tn), lambda i, j: (0, j)),
                  pl.BlockSpec((tm, tn), lambda i, j: (i, j)),
                  pl.BlockSpec((tm, tn), lambda i, j: (i, j + nj))],
        out_specs=pl.BlockSpec((tm, tn), lambda i, j: (i, j)),
        out_shape=jax.ShapeDtypeStruct((n, d_model), o_fox.dtype),
        compiler_params=_cparams("parallel", "parallel"),
        name="mixer_merge",
    )(o_fox, o_cmp, o_slc, o_win, u_fox, u_nsa, mg, mg)


def _router_body(h_ref, g_ref, w_ref, b_ref, u_ref, route_ref):
    x = h_ref[...]
    xn = x * lax.rsqrt(jnp.mean(x * x, axis=-1, keepdims=True) + NORM_EPS) * g_ref[...]
    u_ref[...] = xn.astype(u_ref.dtype)
    z = jnp.dot(xn, w_ref[...], preferred_element_type=F32, precision=HIGHEST) + b_ref[...]
    c = lax.broadcasted_iota(jnp.int32, z.shape, 1)
    big = z.shape[1]
    is_g = c < N_GROUPS
    gmax = jnp.max(jnp.where(is_g, z, -jnp.inf), axis=-1, keepdims=True)
    gsel = jnp.min(jnp.where(is_g & (z == gmax), c, big), axis=-1, keepdims=True)
    gsum = jnp.sum(jnp.where(is_g, jnp.exp(z - gmax), 0.0), axis=-1, keepdims=True)
    lo = N_GROUPS + gsel * EXPERTS_PER_GROUP
    in_grp = (c >= lo) & (c < lo + EXPERTS_PER_GROUP)
    z1 = jnp.max(jnp.where(in_grp, z, -jnp.inf), axis=-1, keepdims=True)
    c1 = jnp.min(jnp.where(in_grp & (z == z1), c, big), axis=-1, keepdims=True)
    rest = in_grp & (c != c1)
    z2 = jnp.max(jnp.where(rest, z, -jnp.inf), axis=-1, keepdims=True)
    c2 = jnp.min(jnp.where(rest & (z == z2), c, big), axis=-1, keepdims=True)
    e2 = jnp.exp(z2 - z1)
    p1 = 1.0 / (1.0 + e2)
    p2 = e2 / (1.0 + e2)
    gp = 1.0 / gsum
    out = jnp.where(c == 0, (c1 - N_GROUPS).astype(F32), 0.0)
    out = jnp.where(c == 1, (c2 - N_GROUPS).astype(F32), out)
    out = jnp.where(c == 2, gp * p1, out)
    out = jnp.where(c == 3, gp * p2, out)
    route_ref[...] = out


def _router(h, gain, w_cat, b_cat):
    n, d = h.shape
    tm = _row_tile(n, 512)
    return pl.pallas_call(
        _router_body,
        grid=(n // tm,),
        in_specs=[pl.BlockSpec((tm, d), lambda i: (i, 0)), pl.BlockSpec((1, d), lambda i: (0, 0)),
                  pl.BlockSpec((d, LANES), lambda i: (0, 0)), pl.BlockSpec((1, LANES), lambda i: (0, 0))],
        out_specs=[pl.BlockSpec((tm, d), lambda i: (i, 0)), pl.BlockSpec((tm, LANES), lambda i: (i, 0))],
        out_shape=[jax.ShapeDtypeStruct((n, d), F32), jax.ShapeDtypeStruct((n, LANES), F32)],
        compiler_params=_cparams("parallel"),
        name="moe_router",
    )(h, gain.reshape(1, d), w_cat, b_cat)


def _gather_rows(src_hbm, dst_ref, sem, idx_ref, base, count, start):
    if not start:
        pltpu.make_async_copy(src_hbm.at[pl.ds(0, count)], dst_ref, sem).wait()
        return

    def body(r, c):
        pltpu.make_async_copy(src_hbm.at[pl.ds(idx_ref[base + r], 1)], dst_ref.at[pl.ds(r, 1)], sem).start()
        return c

    lax.fori_loop(0, count, body, 0, unroll=8)


def _expert_body(tm, te_ref, nv_ref, tok_ref, wg_ref, wu_ref, wd_ref, x_hbm, o_ref, xbuf, sem):
    i = pl.program_id(0)
    nv = nv_ref[0]

    def rows_of(tile, start):
        slot = lax.rem(tile, 2)
        _gather_rows(x_hbm, xbuf.at[slot], sem.at[slot], tok_ref, tile * tm, tm, start)

    @pl.when(i == 0)
    def _():
        rows_of(0, True)

    @pl.when(i + 1 < nv)
    def _():
        rows_of(i + 1, True)

    @pl.when(i < nv)
    def _():
        rows_of(i, False)
        x = xbuf[lax.rem(i, 2)].astype(BF16)
        hg = jnp.dot(x, wg_ref[0].astype(BF16), preferred_element_type=F32)
        hu = jnp.dot(x, wu_ref[0].astype(BF16), preferred_element_type=F32)
        hid = (hg * jax.nn.sigmoid(hg)) * hu
        o_ref[...] = jnp.dot(hid.astype(BF16), wd_ref[0].astype(BF16), preferred_element_type=F32)

    @pl.when(i >= nv)
    def _():
        o_ref[...] = jnp.zeros(o_ref.shape, o_ref.dtype)


def _experts(x, row_token, tile_expert, n_valid, wg, wu, wd, tm):
    rows = row_token.shape[0]
    d = x.shape[1]
    n_tiles = rows // tm
    f = wg.shape[2]

    def w_map(i, te, nv, tok):
        return (te[i], 0, 0)

    grid_spec = pltpu.PrefetchScalarGridSpec(
        num_scalar_prefetch=3,
        grid=(n_tiles,),
        in_specs=[pl.BlockSpec((1, d, f), w_map), pl.BlockSpec((1, d, f), w_map), pl.BlockSpec((1, f, d), w_map),
                  pl.BlockSpec(memory_space=pl.ANY)],
        out_specs=pl.BlockSpec((tm, d), lambda i, te, nv, tok: (i, 0)),
        scratch_shapes=[pltpu.VMEM((2, tm, d), F32), pltpu.SemaphoreType.DMA((2,))],
    )
    return pl.pallas_call(
        functools.partial(_expert_body, tm),
        grid_spec=grid_spec,
        out_shape=jax.ShapeDtypeStruct((rows, d), F32),
        compiler_params=_cparams("arbitrary"),
        name="moe_experts",
    )(tile_expert, n_valid, row_token, wg, wu, wd, x)


def _combine_body(tm, n, slot_ref, h_ref, route_ref, src_hbm, o_ref, buf, sem):
    i = pl.program_id(0)
    n_tiles = pl.num_programs(0)

    def rows_of(tile, start):
        slot = lax.rem(tile, 2)
        for k in range(2):
            _gather_rows(src_hbm, buf.at[slot, k], sem.at[slot], slot_ref, k * n + tile * tm, tm, start)

    @pl.when(i == 0)
    def _():
        rows_of(0, True)

    @pl.when(i + 1 < n_tiles)
    def _():
        rows_of(i + 1, True)

    rows_of(i, False)
    slot = lax.rem(i, 2)
    route = route_ref[...]
    o_ref[...] = h_ref[...] + route[:, 2:3] * buf[slot, 0] + route[:, 3:4] * buf[slot, 1]


def _moe_combine(h, route, out_sorted, slot_kn):
    n, d = h.shape
    tm = _row_tile(n, 256)
    grid_spec = pltpu.PrefetchScalarGridSpec(
        num_scalar_prefetch=1,
        grid=(n // tm,),
        in_specs=[pl.BlockSpec((tm, d), lambda i, sl: (i, 0)), pl.BlockSpec((tm, LANES), lambda i, sl: (i, 0)),
                  pl.BlockSpec(memory_space=pl.ANY)],
        out_specs=pl.BlockSpec((tm, d), lambda i, sl: (i, 0)),
        scratch_shapes=[pltpu.VMEM((2, 2, tm, d), F32), pltpu.SemaphoreType.DMA((2,))],
    )
    return pl.pallas_call(
        functools.partial(_combine_body, tm, n),
        grid_spec=grid_spec,
        out_shape=jax.ShapeDtypeStruct((n, d), F32),
        compiler_params=_cparams("arbitrary"),
        name="moe_combine",
    )(slot_kn, h, route, out_sorted)


def _moe(h, gain, w_router, b_router, wg, wu, wd):
    n, d = h.shape
    ut, route = _router(h, gain, w_router, b_router)
    expert = route[:, :2].astype(jnp.int32).reshape(-1)
    n_asg = 2 * n
    tm = 256 if n_asg >= 8192 else 16
    n_tiles = n_asg // tm + N_EXPERTS
    onehot = (expert[:, None] == jnp.arange(N_EXPERTS, dtype=jnp.int32)[None, :]).astype(jnp.int32)
    running = jnp.cumsum(onehot, axis=0)
    counts = running[-1]
    rank = jnp.sum(onehot * running, axis=1) - 1
    tiles_per = (counts + tm - 1) // tm
    tile_end = jnp.cumsum(tiles_per)
    start_row = (tile_end - tiles_per) * tm
    slot = (jnp.sum(onehot * start_row[None, :], axis=1) + rank).astype(jnp.int32)
    row_token = jnp.zeros((n_tiles * tm,), jnp.int32).at[slot].set(jnp.arange(n_asg, dtype=jnp.int32) // 2)
    n_valid = tile_end[-1].astype(jnp.int32)
    tile_ids = jnp.minimum(jnp.arange(n_tiles, dtype=jnp.int32), n_valid - 1)
    tile_expert = jnp.sum((tile_end[None, :] <= tile_ids[:, None]).astype(jnp.int32), axis=1)
    out_sorted = _experts(ut, row_token, tile_expert, n_valid.reshape(1), wg, wu, wd, tm)
    return _moe_combine(h, route, out_sorted, jnp.transpose(slot.reshape(n, 2)).reshape(-1))


def _rope_tables(positions):
    half = HEAD_DIM // 2
    inv = ROPE_THETA ** (-np.arange(half, dtype=np.float64) / half)
    ang = np.asarray(positions, np.float64)[:, None] * inv[None, :]
    cos = np.concatenate([np.cos(ang), np.cos(ang)], axis=-1)
    sin = np.concatenate([-np.sin(ang), np.sin(ang)], axis=-1)
    return jnp.asarray(cos, F32), jnp.asarray(sin, F32)


PROJ_ROW_TILE = 2048
MERGE_ROW_TILE = 1024


def _project(u, wts, p, b, t, q_start):
    n = u.shape[0]
    tm = _row_tile(n, PROJ_ROW_TILE)
    tn = 512
    tn_kv = 3 * HEAD_DIM
    scale = HEAD_DIM ** -0.5
    gain_spec = pl.BlockSpec((1, HEAD_DIM), lambda i, j: (0, 0))

    def run(w, epilogue, extras, specs, dtype, name, tn_=tn):
        shape = jax.ShapeDtypeStruct((n, w.shape[1]), dtype)
        spec = pl.BlockSpec((tm, tn_), lambda i, j: (i, j))
        return _matmul(u, w, epilogue, extras, specs, shape, spec, tm, tn_, name)

    cos, sin = _rope_tables(q_start + (np.arange(max(tm, t)) % t))
    n_tab = cos.shape[0] // tm
    tab_spec = pl.BlockSpec((tm, HEAD_DIM), lambda i, j: (i % n_tab, 0))

    fq = run(wts["fq"], functools.partial(_ep_head_norm, scale), [p["fox_q_norm"]], [gain_spec], u.dtype, "proj_fq")
    fk = run(wts["fk"], functools.partial(_ep_head_norm, 1.0), [p["fox_k_norm"]], [gain_spec], F32, "proj_fk")
    fv = run(wts["fv"], _ep_identity, [], [], F32, "proj_fv")
    nq = run(wts["nq"], functools.partial(_ep_head_norm_rope, scale), [p["nsa_q_norm"], cos, sin],
             [gain_spec, tab_spec, tab_spec], u.dtype, "proj_nq")
    nk = run(wts["nk"], functools.partial(_ep_head_norm_rope, 1.0), [p["nsa_k_norm"], cos, sin],
             [gain_spec, tab_spec, tab_spec], F32, "proj_nk", tn_kv)
    nv = run(wts["nv"], _ep_identity, [], [], F32, "proj_nv", tn_kv)
    mg = run(wts["mg"], _ep_sigmoid, [], [], F32, "proj_mg")
    small = run(wts["small"], _ep_small, [p["small_bias"]], [pl.BlockSpec((1, LANES), lambda i, j: (0, 0))],
                F32, "proj_small", LANES)
    return fq, fk, fv, nq, nk, nv, mg, small


def _layer(x, p, wts, q_start, past):
    b, t, d = x.shape
    n = b * t
    g = NSA_KV_GROUPS
    kvw = g * HEAD_DIM
    x2d = x.reshape(n, d)
    dense = wts["f32"] if past is not None else wts
    u = _rmsnorm(x2d, p["norm_mix"], dense["fq"].dtype)
    fq, fk, fv, nq, nk, nv, mg, small = _project(u, dense, p, b, t, q_start)

    logf = small[:, :FOX_HEADS].reshape(b, t, FOX_HEADS)
    gate3 = small.reshape(b, t, LANES)
    fq3 = fq.reshape(b, t, -1)
    nq3 = nq.reshape(b, t, -1)
    fk3 = fk.reshape(b, t, -1)
    fv3 = fv.reshape(b, t, -1)
    nk3 = nk.reshape(b, t, 3 * kvw)
    nv3 = nv.reshape(b, t, 3 * kvw)
    ck, sk, wk = (nk3[:, :, i * kvw:(i + 1) * kvw] for i in range(3))
    cv, sv, wv = (nv3[:, :, i * kvw:(i + 1) * kvw] for i in range(3))

    if past is None:
        n_keys = t
        n_lp = t // LANES
        lf_pool = jnp.transpose(logf.reshape(b * n_lp, LANES, FOX_HEADS), (0, 2, 1))
        bias = _fox_key_bias(lf_pool, jnp.arange(b * n_lp, dtype=jnp.int32).reshape(b, n_lp), None)
        o_fox = _flash_attention("fox", fq3, fk3, fv3, 0, t, bias=bias)
        kc = _compress(ck, p["cmp_pos_k"], p["cmp_w1_k"], p["cmp_w2_k"])
        vc = _compress(cv, p["cmp_pos_v"], p["cmp_w1_v"], p["cmp_w2_v"])
        o_cmp, sel = _cmp_attention(nq3, kc, vc, gate3, t, q_start, n_keys, BF16)
        o_slc = _flash_attention("slc", nq3, nk3, nv3, g, t, sel=sel, gate=gate3, gate_col=1)
        o_win = _flash_attention("win", nq3, nk3, nv3, 2 * g, t, gate=gate3, gate_col=2)
        keep = min(WINDOW, t)
        win_k, win_v = wk[:, t - keep:], wv[:, t - keep:]
    else:
        table = past["table"]
        n_pages = table.shape[1]
        page = past["page"]
        past_len = n_pages * page
        n_keys = past_len + t
        new_page = jnp.pad(jnp.transpose(logf, (0, 2, 1)), ((0, 0), (0, 0), (0, page - t)))
        bias = _fox_key_bias(past["fox_logf"], table, new_page)[:, :, :n_keys]
        causal = jnp.where(jnp.arange(t)[None, :] <= jnp.arange(t)[:, None], 0.0, NEG_INF).astype(F32)
        bias_new = bias[:, :, None, past_len:] + causal[None, None]
        qf = jnp.transpose(fq3.reshape(b, t, FOX_HEADS, HEAD_DIM), (0, 2, 1, 3))
        o = _paged_attention(qf, past["fox_k"], past["fox_v"], table, bias[:, :, None, :past_len],
                             fk3, fv3, bias_new)
        o_fox = jnp.transpose(o, (0, 2, 1, 3)).reshape(b, t, -1)
        nc = (n_keys - CMP_BLOCK) // CMP_STRIDE + 1
        assert (nc + 1) * CMP_STRIDE <= past_len and page % CMP_STRIDE == 0
        kc = _compress_paged(past["cmp_k"], table, p["cmp_pos_k"], p["cmp_w1_k"], p["cmp_w2_k"])
        vc = _compress_paged(past["cmp_v"], table, p["cmp_pos_v"], p["cmp_w1_v"], p["cmp_w2_v"])
        o_cmp, sel = _cmp_attention(nq3, kc, vc, gate3, t, q_start, n_keys, F32)
        rows = HEADS_PER_GROUP * t
        qn = jnp.transpose(nq3.reshape(b, t, g, HEADS_PER_GROUP, HEAD_DIM), (0, 2, 3, 1, 4)).reshape(b, g, rows, HEAD_DIM)
        gates = small[:, FOX_HEADS:FOX_HEADS + NSA_HEADS * N_NSA_BRANCHES].reshape(b, t, g, HEADS_PER_GROUP, N_NSA_BRANCHES)
        gates = jnp.transpose(gates, (4, 0, 2, 3, 1)).reshape(N_NSA_BRANCHES, b, g, rows, 1)
        ns = -(-n_keys // SEL_BLOCK)
        key_sel = jnp.repeat(sel[..., :ns], SEL_BLOCK, axis=-1)[..., :n_keys]
        sel_bias = jnp.where(key_sel > 0.5, 0.0, NEG_INF).astype(F32)
        sel_new = jnp.tile(sel_bias[..., past_len:] + causal[None, None], (1, 1, HEADS_PER_GROUP, 1))
        o = _paged_attention(qn, past["slc_k"], past["slc_v"], table, sel_bias[..., :past_len],
                             sk, sv, sel_new, gate=gates[1])
        o_slc = _rows_to_tokens(o, b, t)
        win_len = past["win_k"].shape[1]
        assert win_len % page == 0
        k_start = q_start - win_len
        q_abs = q_start + jnp.arange(t)[:, None]
        k_abs = k_start + jnp.arange(win_len)[None, :]
        band = jnp.where((k_abs >= 0) & (k_abs > q_abs - WINDOW), 0.0, NEG_INF).astype(F32)
        band = jnp.broadcast_to(band[None, None], (b, g, t, win_len))
        band_new = jnp.where(jnp.arange(t)[None, :] > jnp.arange(t)[:, None] - WINDOW, causal, NEG_INF)
        band_new = jnp.broadcast_to(jnp.tile(band_new, (HEADS_PER_GROUP, 1))[None, None], (b, g, rows, t))
        o = _paged_attention(qn, past["win_k_pool"], past["win_v_pool"], past["win_table"], band, wk, wv,
                             band_new, gate=gates[2])
        o_win = _rows_to_tokens(o, b, t)
        wk_all = jnp.concatenate([past["win_k"], wk], axis=1)
        wv_all = jnp.concatenate([past["win_v"], wv], axis=1)
        keep = min(WINDOW, wk_all.shape[1])
        win_k, win_v = wk_all[:, -keep:], wv_all[:, -keep:]

    m = _merge(o_fox.reshape(n, -1), o_cmp.reshape(n, -1), o_slc.reshape(n, -1), o_win.reshape(n, -1),
               dense["up_fox"], dense["up_nsa"], mg, d)
    tm = _row_tile(n, MERGE_ROW_TILE)
    h = _matmul(m, dense["out"], _ep_residual, [x2d], [pl.BlockSpec((tm, 512), lambda i, j: (i, j))],
                jax.ShapeDtypeStruct((n, d), F32), pl.BlockSpec((tm, 512), lambda i, j: (i, j)), tm, 512, "out_proj")
    y = _moe(h, p["norm_ffn"], wts["router"], p["router_bias"], wts["gate_e"], wts["up_e"], wts["down_e"])

    def heads(a, hh):
        return a.reshape(b, -1, hh, HEAD_DIM)

    state = (heads(fk3, FOX_HEADS), heads(fv3, FOX_HEADS), logf, heads(ck, g), heads(cv, g), heads(sk, g),
             heads(sv, g), heads(win_k, g), heads(win_v, g))
    return y.reshape(b, t, d), state


def _rows_to_tokens(o, b, t):
    o = o.reshape(b, NSA_KV_GROUPS, HEADS_PER_GROUP, t, HEAD_DIM)
    return jnp.transpose(o, (0, 3, 1, 2, 4)).reshape(b, t, -1)


def _prepare(l, norm_mix, w_in, fox_f_bias, fox_q_norm, fox_k_norm, nsa_q_norm, nsa_k_norm, cmp_pos_k, cmp_w1_k,
             cmp_w2_k, cmp_pos_v, cmp_w1_v, cmp_w2_v, nsa_gate_bias, w_up_fox, w_up_nsa, w_out, norm_ffn,
             w_router_group, b_router_group, w_router_expert, b_router_expert, w_gate_e, w_up_e, w_down_e):
    d = w_in.shape[1]
    fw = FOX_HEADS * HEAD_DIM
    nw = NSA_HEADS * HEAD_DIM
    kvw = NSA_KV_GROUPS * HEAD_DIM
    sizes = (fw, fw, fw, FOX_HEADS, nw, kvw, kvw, kvw, kvw, kvw, kvw, NSA_HEADS * N_NSA_BRANCHES, 2 * d)
    offs = np.concatenate([[0], np.cumsum(sizes)])
    w = w_in[l]

    def seg(i):
        return w[:, offs[i]:offs[i + 1]]

    n_small = FOX_HEADS + NSA_HEADS * N_NSA_BRANCHES
    small_w = jnp.pad(jnp.concatenate([seg(3), seg(11)], axis=1), ((0, 0), (0, LANES - n_small)))
    small_b = jnp.pad(jnp.concatenate([fox_f_bias[l], nsa_gate_bias[l]]), (0, LANES - n_small)).reshape(1, LANES)
    n_route = N_GROUPS + N_EXPERTS
    router_w = jnp.pad(jnp.concatenate([w_router_group[l], w_router_expert[l]], axis=1), ((0, 0), (0, LANES - n_route)))
    router_b = jnp.pad(jnp.concatenate([b_router_group[l], b_router_expert[l]]), (0, LANES - n_route)).reshape(1, LANES)
    dense = {
        "fq": seg(0), "fk": seg(1), "fv": seg(2), "nq": seg(4),
        "nk": jnp.concatenate([seg(5), seg(7), seg(9)], axis=1),
        "nv": jnp.concatenate([seg(6), seg(8), seg(10)], axis=1),
        "mg": seg(12), "small": small_w,
        "up_fox": w_up_fox[l], "up_nsa": w_up_nsa[l], "out": w_out[l],
    }
    wts = {
        **{k: v.astype(BF16) for k, v in dense.items()},
        "f32": dense,
        "router": router_w,
        "gate_e": w_gate_e[l], "up_e": w_up_e[l], "down_e": w_down_e[l],
    }
    p = {
        "norm_mix": norm_mix[l], "fox_q_norm": fox_q_norm[l].reshape(1, -1), "fox_k_norm": fox_k_norm[l].reshape(1, -1),
        "nsa_q_norm": nsa_q_norm[l].reshape(1, -1), "nsa_k_norm": nsa_k_norm[l].reshape(1, -1),
        "small_bias": small_b, "cmp_pos_k": cmp_pos_k[l], "cmp_w1_k": cmp_w1_k[l], "cmp_w2_k": cmp_w2_k[l],
        "cmp_pos_v": cmp_pos_v[l], "cmp_w1_v": cmp_w1_v[l], "cmp_w2_v": cmp_w2_v[l],
        "norm_ffn": norm_ffn[l], "router_bias": router_b,
    }
    return wts, p


def kernel(x_prompt, x_sample, cache_fox_k, cache_fox_v, cache_fox_logf, cache_cmp_k, cache_cmp_v, cache_slc_k, cache_slc_v, cache_win_k, cache_win_v, page_table, norm_mix, w_in, fox_f_bias, fox_q_norm, fox_k_norm, nsa_q_norm, nsa_k_norm, cmp_pos_k, cmp_w1_k, cmp_w2_k, cmp_pos_v, cmp_w1_v, cmp_w2_v, nsa_gate_bias, w_up_fox, w_up_nsa, w_out, norm_ffn, w_router_group, b_router_group, w_router_expert, b_router_expert, w_gate_e, w_up_e, w_down_e):
    depth = w_in.shape[0]
    n_phys, page = cache_fox_k.shape[1], cache_fox_k.shape[2]
    past_len = page_table.shape[1] * page
    h_p, h_s = x_prompt, x_sample
    p_states, s_states = [], []
    for l in range(depth):
        wts, p = _prepare(l, norm_mix, w_in, fox_f_bias, fox_q_norm, fox_k_norm, nsa_q_norm, nsa_k_norm, cmp_pos_k,
                          cmp_w1_k, cmp_w2_k, cmp_pos_v, cmp_w1_v, cmp_w2_v, nsa_gate_bias, w_up_fox, w_up_nsa,
                          w_out, norm_ffn, w_router_group, b_router_group, w_router_expert, b_router_expert,
                          w_gate_e, w_up_e, w_down_e)
        h_p, st_p = _layer(h_p, p, wts, 0, None)
        p_states.append(st_p)
        win_len = cache_win_k.shape[2]
        n_win = win_len // page
        dec_b = page_table.shape[0]

        def pool(c):
            return c.reshape(-1, page * c.shape[3], HEAD_DIM)

        past = {
            "table": page_table + l * n_phys, "page": page,
            "fox_k": pool(cache_fox_k), "fox_v": pool(cache_fox_v),
            "fox_logf": jnp.transpose(cache_fox_logf.reshape(-1, page, cache_fox_logf.shape[3]), (0, 2, 1)),
            "cmp_k": pool(cache_cmp_k), "cmp_v": pool(cache_cmp_v),
            "slc_k": pool(cache_slc_k), "slc_v": pool(cache_slc_v),
            "win_k_pool": pool(cache_win_k), "win_v_pool": pool(cache_win_v),
            "win_table": (l * dec_b + jnp.arange(dec_b, dtype=jnp.int32))[:, None] * n_win
                         + jnp.arange(n_win, dtype=jnp.int32)[None, :],
            "win_k": cache_win_k[l].reshape(dec_b, win_len, -1), "win_v": cache_win_v[l].reshape(dec_b, win_len, -1),
        }
        h_s, st_s = _layer(h_s, p, wts, past_len, past)
        s_states.append(st_s)
    p_out = [jnp.stack([st[i] for st in p_states], axis=0) for i in range(9)]
    s_out = [jnp.stack([st[i] for st in s_states], axis=0) for i in range(9)]
    return (h_p, h_s, *p_out, *s_out)
```

```python
import functools

import numpy as np
import jax
import jax.numpy as jnp
from jax import lax
from jax.experimental import pallas as pl
from jax.experimental.pallas import tpu as pltpu

F32 = jnp.float32
BF16 = jnp.bfloat16

HEAD_DIM = 128
FOX_HEADS = 8
NSA_HEADS = 8
NSA_KV_GROUPS = 2
HEADS_PER_GROUP = NSA_HEADS // NSA_KV_GROUPS
N_NSA_BRANCHES = 3
CMP_BLOCK = 32
CMP_STRIDE = 16
SEL_BLOCK = 64
N_SELECT = 16
N_LOCAL_SEL = 2
WINDOW = 512
N_GROUPS = 4
EXPERTS_PER_GROUP = 8
N_EXPERTS = N_GROUPS * EXPERTS_PER_GROUP
ROPE_THETA = 10000.0
NORM_EPS = 1e-6
NEG_INF = -1e30
FORCE_SCORE = 1e9
PAD_SCORE = -2e38
TAKEN_SCORE = -3e38

LANES = 128
SUBLANES = 8
VMEM_LIMIT_BYTES = 48 * 1024 * 1024

HIGHEST = lax.Precision.HIGHEST


def _cparams(*sem):
    return pltpu.CompilerParams(dimension_semantics=tuple(sem), vmem_limit_bytes=VMEM_LIMIT_BYTES)


def _row_tile(n, pref):
    t = min(n, pref)
    assert n % t == 0, (n, t)
    return t


def _rmsnorm_body(x_ref, g_ref, o_ref):
    x = x_ref[...]
    y = x * lax.rsqrt(jnp.mean(x * x, axis=-1, keepdims=True) + NORM_EPS)
    o_ref[...] = (y * g_ref[...]).astype(o_ref.dtype)


def _rmsnorm(x2d, gain, out_dtype):
    n, d = x2d.shape
    tm = _row_tile(n, 512)
    return pl.pallas_call(
        _rmsnorm_body,
        grid=(n // tm,),
        in_specs=[pl.BlockSpec((tm, d), lambda i: (i, 0)), pl.BlockSpec((1, d), lambda i: (0, 0))],
        out_specs=pl.BlockSpec((tm, d), lambda i: (i, 0)),
        out_shape=jax.ShapeDtypeStruct((n, d), out_dtype),
        compiler_params=_cparams("parallel"),
        name="rmsnorm",
    )(x2d, gain.reshape(1, d))


def _dot_precision(dtype):
    return HIGHEST if dtype == F32 else None


def _mm_body(epilogue, n_extra, a_ref, w_ref, *rest):
    acc = jnp.dot(a_ref[...], w_ref[...], preferred_element_type=F32, precision=_dot_precision(a_ref.dtype))
    epilogue(acc, rest[:n_extra], rest[n_extra:])


def _matmul(a, w, epilogue, extras, extra_specs, out_shapes, out_specs, tm, tn, name):
    n, k = a.shape
    c = w.shape[1]
    assert n % tm == 0 and c % tn == 0, (n, tm, c, tn)
    return pl.pallas_call(
        functools.partial(_mm_body, epilogue, len(extras)),
        grid=(n // tm, c // tn),
        in_specs=[pl.BlockSpec((tm, k), lambda i, j: (i, 0)), pl.BlockSpec((k, tn), lambda i, j: (0, j))]
        + list(extra_specs),
        out_specs=out_specs,
        out_shape=out_shapes,
        compiler_params=_cparams("parallel", "parallel"),
        name=name,
    )(a, w, *extras)


def _head_norm(x, gain):
    return x * lax.rsqrt(jnp.mean(x * x, axis=-1, keepdims=True) + NORM_EPS) * gain


def _rope_rot(x, cos, sin_signed):
    return x * cos + pltpu.roll(x, HEAD_DIM // 2, 1) * sin_signed


def _ep_head_norm(scale, acc, extras, outs):
    (gain_ref,) = extras
    (o_ref,) = outs
    for h in range(acc.shape[1] // HEAD_DIM):
        sl = slice(h * HEAD_DIM, (h + 1) * HEAD_DIM)
        o_ref[:, sl] = (_head_norm(acc[:, sl], gain_ref[...]) * scale).astype(o_ref.dtype)


def _ep_head_norm_rope(scale, acc, extras, outs):
    gain_ref, cos_ref, sin_ref = extras
    (o_ref,) = outs
    for h in range(acc.shape[1] // HEAD_DIM):
        sl = slice(h * HEAD_DIM, (h + 1) * HEAD_DIM)
        y = _rope_rot(_head_norm(acc[:, sl], gain_ref[...]), cos_ref[...], sin_ref[...])
        o_ref[:, sl] = (y * scale).astype(o_ref.dtype)


def _ep_identity(acc, extras, outs):
    (o_ref,) = outs
    o_ref[...] = acc.astype(o_ref.dtype)


def _ep_sigmoid(acc, extras, outs):
    (o_ref,) = outs
    o_ref[...] = jax.nn.sigmoid(acc).astype(o_ref.dtype)


def _ep_small(acc, extras, outs):
    (bias_ref,) = extras
    (o_ref,) = outs
    z = acc + bias_ref[...]
    col = lax.broadcasted_iota(jnp.int32, z.shape, 1)
    e = jnp.exp(-jnp.abs(z))
    log_sig = jnp.minimum(z, 0.0) - jnp.log(1.0 + e)
    sig = jnp.where(z >= 0, 1.0, e) / (1.0 + e)
    o_ref[...] = jnp.where(col < FOX_HEADS, log_sig, sig)


def _ep_residual(acc, extras, outs):
    (x_ref,) = extras
    (o_ref,) = outs
    o_ref[...] = x_ref[...] + acc


def _fox_bias_body(n_pages, has_new, *refs):
    if has_new:
        tbl_ref, new_ref, pool_ref, o_ref, buf, sem = refs
    else:
        tbl_ref, pool_ref, o_ref, buf, sem = refs
    bi = pl.program_id(0)

    def copy(p):
        return pltpu.make_async_copy(pool_ref.at[tbl_ref[bi * n_pages + p]], buf.at[p], sem)

    def start(p, c):
        copy(p).start()
        return c

    def wait(p, c):
        copy(p).wait()
        return c

    lax.fori_loop(0, n_pages, start, 0)
    lax.fori_loop(0, n_pages, wait, 0)
    if has_new:
        buf[n_pages] = new_ref[0]
    x = buf[...]
    p, h, w = x.shape
    within = x.reshape(p * h, w)
    lane = lax.broadcasted_iota(jnp.int32, within.shape, 1)
    s = 1
    while s < w:
        within = within + jnp.where(lane >= s, pltpu.roll(within, s, 1), 0.0)
        s *= 2
    within = within.reshape(p, h, w)
    total = jnp.broadcast_to(within[:, :, w - 1:w], x.shape)
    incl = total
    s = 1
    while s < p:
        incl = incl + jnp.concatenate([jnp.zeros((s, h, w), F32), incl[:p - s]], axis=0)
        s *= 2
    o_ref[0] = -(within + (incl - total))


def _fox_key_bias(pool, table, new_page):
    b, n_pages = table.shape
    h, w = pool.shape[1:]
    has_new = new_page is not None
    p_tot = n_pages + (1 if has_new else 0)
    in_specs = [pl.BlockSpec(memory_space=pl.ANY)]
    args = [pool]
    if has_new:
        in_specs.insert(0, pl.BlockSpec((1, h, w), lambda i, tbl: (i, 0, 0)))
        args.insert(0, new_page)
    grid_spec = pltpu.PrefetchScalarGridSpec(
        num_scalar_prefetch=1,
        grid=(b,),
        in_specs=in_specs,
        out_specs=pl.BlockSpec((1, p_tot, h, w), lambda i, tbl: (i, 0, 0, 0)),
        scratch_shapes=[pltpu.VMEM((p_tot, h, w), F32), pltpu.SemaphoreType.DMA(())],
    )
    out = pl.pallas_call(
        functools.partial(_fox_bias_body, n_pages, has_new),
        grid_spec=grid_spec,
        out_shape=jax.ShapeDtypeStruct((b, p_tot, h, w), F32),
        compiler_params=_cparams("arbitrary"),
        name="fox_key_bias",
    )(table.reshape(-1), *args)
    return jnp.transpose(out, (0, 2, 1, 3)).reshape(b, h, p_tot * w)


FLASH_QUERY_BLOCK = 512
FLASH_KEY_BLOCK = 512


def _flash_body(mode, heads, tq, tk, n_kv, gate_col, *refs):
    if mode == "fox":
        q_ref, k_ref, v_ref, bias_ref, o_ref, kb_ref, vb_ref = refs
    elif mode == "slc":
        q_ref, k_ref, v_ref, sel_ref, expand_ref, gate_ref, o_ref, kb_ref, vb_ref, selx_ref = refs
    else:
        q_ref, k_ref, v_ref, gate_ref, o_ref, kb_ref, vb_ref = refs
    unit = pl.program_id(1)
    qi = pl.program_id(2)

    @pl.when(qi == 0)
    def _():
        kb_ref[...] = k_ref[0].astype(BF16)
        vb_ref[...] = v_ref[0].astype(BF16)

    if mode == "slc":
        sel = sel_ref[0, 0].astype(BF16)
        for j in range(n_kv):
            selx_ref[j] = jnp.dot(sel, expand_ref[:, j * tk:(j + 1) * tk], preferred_element_type=F32)

    q_pos = qi * tq + lax.broadcasted_iota(jnp.int32, (tq, tk), 0)
    col = lax.broadcasted_iota(jnp.int32, (tq, tk), 1)
    qs = [q_ref[0, :, r * HEAD_DIM:(r + 1) * HEAD_DIM] for r in range(heads)]

    def block(j, carry, diagonal):
        ks = pl.multiple_of(j * tk, tk)
        k_pos = ks + col
        ok = None
        if mode == "slc":
            ok = selx_ref[j] > 0.5
        if mode == "win":
            ok = (k_pos <= q_pos) & (k_pos > q_pos - WINDOW)
        elif diagonal:
            ok = (k_pos <= q_pos) if ok is None else ok & (k_pos <= q_pos)
        out = []
        for r in range(heads):
            m, l, acc = carry[r]
            c0 = r * HEAD_DIM if mode == "fox" else 0
            kb = kb_ref[pl.ds(ks, tk), c0:c0 + HEAD_DIM]
            vb = vb_ref[pl.ds(ks, tk), c0:c0 + HEAD_DIM]
            s = lax.dot_general(qs[r], kb, _NT, preferred_element_type=F32)
            if mode == "fox":
                s = s + bias_ref[0, r:r + 1, pl.ds(ks, tk)]
            if ok is not None:
                s = jnp.where(ok, s, NEG_INF)
            m_new = jnp.maximum(m, jnp.max(s, axis=-1, keepdims=True))
            alpha = jnp.exp(m - m_new)
            p = jnp.exp(s - m_new)
            l = alpha * l + jnp.sum(p, axis=-1, keepdims=True)
            acc = alpha * acc + jnp.dot(p.astype(BF16), vb, preferred_element_type=F32)
            out.append((m_new, l, acc))
        return tuple(out)

    init = tuple((jnp.full((tq, 1), NEG_INF, F32), jnp.zeros((tq, 1), F32), jnp.zeros((tq, HEAD_DIM), F32))
                 for _ in range(heads))
    first_diag = (qi * tq) // tk
    n_diag = -(-tq // tk)
    if mode == "win":
        lo = jnp.maximum(qi * tq - WINDOW + 1, 0) // tk
        state = lax.fori_loop(lo, first_diag + n_diag, functools.partial(block, diagonal=False), init)
    else:
        state = lax.fori_loop(0, first_diag, functools.partial(block, diagonal=False), init)
        for dj in range(n_diag):
            state = block(first_diag + dj, state, True)

    for r in range(heads):
        m, l, acc = state[r]
        w = 1.0 / l
        if mode != "fox":
            gate = gate_ref[0]
            lane = lax.broadcasted_iota(jnp.int32, gate.shape, 1)
            want = FOX_HEADS + (unit * heads + r) * N_NSA_BRANCHES + gate_col
            w = w * jnp.sum(jnp.where(lane == want, gate, 0.0), axis=-1, keepdims=True)
        o_ref[0, :, r * HEAD_DIM:(r + 1) * HEAD_DIM] = (acc * w).astype(o_ref.dtype)


def _flash_attention(mode, q, k, v, k_col0, t, *, bias=None, sel=None, gate=None, gate_col=0):
    b = q.shape[0]
    heads = HEADS_PER_GROUP
    qw = heads * HEAD_DIM
    units = q.shape[2] // qw
    kvw = qw if mode == "fox" else HEAD_DIM
    tq = _row_tile(t, FLASH_QUERY_BLOCK)
    tk = _row_tile(t, FLASH_KEY_BLOCK)
    n_kv = t // tk
    in_specs = [
        pl.BlockSpec((1, tq, qw), lambda bi, u, i: (bi, i, u)),
        pl.BlockSpec((1, t, kvw), lambda bi, u, i: (bi, 0, k_col0 + u)),
        pl.BlockSpec((1, t, kvw), lambda bi, u, i: (bi, 0, k_col0 + u)),
    ]
    args = [q, k, v]
    scratch = [pltpu.VMEM((t, kvw), BF16), pltpu.VMEM((t, kvw), BF16)]
    if mode == "fox":
        in_specs.append(pl.BlockSpec((1, heads, t), lambda bi, u, i: (bi * units + u, 0, 0)))
        args.append(bias.reshape(b * units, heads, t))
    else:
        if mode == "slc":
            nsp = sel.shape[-1]
            expand = (np.arange(nsp)[:, None] == (np.arange(t)[None, :] // SEL_BLOCK)).astype(np.float32)
            in_specs.append(pl.BlockSpec((1, 1, tq, nsp), lambda bi, u, i: (bi, u, i, 0)))
            in_specs.append(pl.BlockSpec((nsp, t), lambda bi, u, i: (0, 0)))
            args += [sel, jnp.asarray(expand, BF16)]
            scratch.append(pltpu.VMEM((n_kv, tq, tk), F32))
        in_specs.append(pl.BlockSpec((1, tq, LANES), lambda bi, u, i: (bi, i, 0)))
        args.append(gate)
    return pl.pallas_call(
        functools.partial(_flash_body, mode, heads, tq, tk, n_kv, gate_col),
        grid=(b, units, t // tq),
        in_specs=in_specs,
        out_specs=pl.BlockSpec((1, tq, qw), lambda bi, u, i: (bi, i, u)),
        out_shape=jax.ShapeDtypeStruct((b, t, units * qw), BF16),
        scratch_shapes=scratch,
        compiler_params=_cparams("parallel", "parallel", "arbitrary"),
        name="attn_" + mode,
    )(*args)


PAGED_BUFFER_BYTES = 2 * 1024 * 1024
PAGED_COPY_BYTES = 128 * 1024
CMP_BUFFER_BYTES = 4 * 1024 * 1024


def _pages_per_step(n_pages, page_bytes, buffer_bytes=PAGED_BUFFER_BYTES):
    pg = max(1, min(n_pages, buffer_bytes // page_bytes))
    while n_pages % pg:
        pg -= 1
    return pg


def _paged_body(units, rows, rows_b, n_pages, pg, page, use_gate, *refs):
    if use_gate:
        (tbl_ref, q_ref, bias_ref, kn_ref, vn_ref, biasn_ref, gate_ref, kpool, vpool, o_ref,
         kbuf, vbuf, sem, m_ref, l_ref, acc_ref) = refs
    else:
        (tbl_ref, q_ref, bias_ref, kn_ref, vn_ref, biasn_ref, kpool, vpool, o_ref,
         kbuf, vbuf, sem, m_ref, l_ref, acc_ref) = refs
        gate_ref = None
    bi = pl.program_id(0)
    s = pl.program_id(1)
    n_steps = n_pages // pg
    rpp = page * units
    n_keys = pg * page

    n_chunks = max(1, (rpp * HEAD_DIM * 4) // PAGED_COPY_BYTES)
    crows = rpp // n_chunks

    def copies(row, step, slot):
        out = []
        for k in range(pg):
            phys = tbl_ref[row * n_pages + step * pg + k]
            for c in range(n_chunks):
                src = pl.ds(c * crows, crows)
                dst = pl.ds(k * rpp + c * crows, crows)
                out.append(pltpu.make_async_copy(kpool.at[phys, src], kbuf.at[slot, dst], sem.at[0, slot]))
                out.append(pltpu.make_async_copy(vpool.at[phys, src], vbuf.at[slot, dst], sem.at[1, slot]))
        return out

    @pl.when(s == 0)
    def _():
        m_ref[...] = jnp.full(m_ref.shape, NEG_INF, F32)
        l_ref[...] = jnp.zeros(l_ref.shape, F32)
        acc_ref[...] = jnp.zeros(acc_ref.shape, F32)

    gstep = bi * n_steps + s

    @pl.when((bi == 0) & (s == 0))
    def _():
        for c in copies(0, 0, 0):
            c.start()

    @pl.when(s + 1 < n_steps)
    def _():
        for c in copies(bi, s + 1, lax.rem(gstep + 1, 2)):
            c.start()

    @pl.when((s + 1 == n_steps) & (bi + 1 < pl.num_programs(0)))
    def _():
        for c in copies(bi + 1, 0, lax.rem(gstep + 1, 2)):
            c.start()

    def attend(k, v, bias, own_blocks):
        k_hi, k_lo = _split_bf16(k)
        v_hi, v_lo = _split_bf16(v)
        sc = _dot_split(q_ref[0], k_hi, k_lo, _NT)
        nk = k.shape[0] // units
        if own_blocks:
            sc = jnp.concatenate([sc[u * rows:(u + 1) * rows, u * nk:(u + 1) * nk] for u in range(units)], axis=0)
        sc = sc + bias
        m = m_ref[...]
        m_new = jnp.maximum(m, jnp.max(sc, axis=-1, keepdims=True))
        alpha = jnp.exp(m - m_new)
        pr = jnp.exp(sc - m_new)
        l_ref[...] = alpha * l_ref[...] + jnp.sum(pr, axis=-1, keepdims=True)
        if own_blocks:
            zero = jnp.zeros((rows, nk), F32)
            pr = jnp.concatenate(
                [jnp.concatenate([pr[u * rows:(u + 1) * rows] if u2 == u else zero for u2 in range(units)], axis=1)
                 for u in range(units)], axis=0)
        acc_ref[...] = alpha * acc_ref[...] + _dot_split(pr, v_hi, v_lo, _NN)
        m_ref[...] = m_new

    def unit_bias(u):
        bb = bias_ref[0, u]
        if rows_b == 1:
            return jnp.broadcast_to(bb, (rows, n_keys))
        return jnp.concatenate([bb] * (rows // rows_b), axis=0)

    @pl.when(s < n_steps)
    def _():
        slot = lax.rem(gstep, 2)
        for c in copies(bi, s, slot):
            c.wait()
        k = jnp.concatenate([kbuf[slot, pl.ds(u, n_keys, stride=units), :] for u in range(units)], axis=0)
        v = jnp.concatenate([vbuf[slot, pl.ds(u, n_keys, stride=units), :] for u in range(units)], axis=0)
        attend(k, v, jnp.concatenate([unit_bias(u) for u in range(units)], axis=0), True)

    @pl.when(s == n_steps)
    def _():
        k = jnp.concatenate([kn_ref[0, :, u * HEAD_DIM:(u + 1) * HEAD_DIM] for u in range(units)], axis=0)
        v = jnp.concatenate([vn_ref[0, :, u * HEAD_DIM:(u + 1) * HEAD_DIM] for u in range(units)], axis=0)
        attend(k, v, biasn_ref[0], False)
        w = 1.0 / l_ref[...]
        if use_gate:
            w = w * gate_ref[0]
        o_ref[0] = acc_ref[...] * w


def _paged_attention(q, k_pool, v_pool, table, bias, k_new, v_new, bias_new, gate=None):
    b, units, rows, _ = q.shape
    ur = units * rows
    n_pages = table.shape[1]
    rpp = k_pool.shape[1]
    page = rpp // units
    rows_b = bias.shape[2]
    n_new = k_new.shape[1]
    width = k_new.shape[2]
    pg = _pages_per_step(n_pages, rpp * HEAD_DIM * 4)
    n_steps = n_pages // pg
    last = n_steps - 1
    same_unit = jnp.arange(units)[:, None, None, None] == jnp.arange(units)[None, None, :, None]
    bias_new_full = jnp.where(same_unit[None], bias_new[:, :, :, None, :], NEG_INF).reshape(b, ur, units * n_new)
    in_specs = [
        pl.BlockSpec((1, ur, HEAD_DIM), lambda bi, p, tbl: (bi, 0, 0)),
        pl.BlockSpec((1, units, rows_b, pg * page), lambda bi, p, tbl: (bi, 0, 0, jnp.minimum(p, last))),
        pl.BlockSpec((1, n_new, width), lambda bi, p, tbl: (bi, 0, 0)),
        pl.BlockSpec((1, n_new, width), lambda bi, p, tbl: (bi, 0, 0)),
        pl.BlockSpec((1, ur, units * n_new), lambda bi, p, tbl: (bi, 0, 0)),
    ]
    args = [q.reshape(b, ur, HEAD_DIM), bias, k_new, v_new, bias_new_full]
    if gate is not None:
        in_specs.append(pl.BlockSpec((1, ur, 1), lambda bi, p, tbl: (bi, 0, 0)))
        args.append(gate.reshape(b, ur, 1))
    in_specs += [pl.BlockSpec(memory_space=pl.ANY), pl.BlockSpec(memory_space=pl.ANY)]
    args += [k_pool, v_pool]
    grid_spec = pltpu.PrefetchScalarGridSpec(
        num_scalar_prefetch=1,
        grid=(b, n_steps + 1),
        in_specs=in_specs,
        out_specs=pl.BlockSpec((1, ur, HEAD_DIM), lambda bi, p, tbl: (bi, 0, 0)),
        scratch_shapes=[pltpu.VMEM((2, pg * rpp, HEAD_DIM), F32), pltpu.VMEM((2, pg * rpp, HEAD_DIM), F32),
                        pltpu.SemaphoreType.DMA((2, 2)),
                        pltpu.VMEM((ur, 1), F32), pltpu.VMEM((ur, 1), F32), pltpu.VMEM((ur, HEAD_DIM), F32)],
    )
    out = pl.pallas_call(
        functools.partial(_paged_body, units, rows, rows_b, n_pages, pg, page, gate is not None),
        grid_spec=grid_spec,
        out_shape=jax.ShapeDtypeStruct((b, ur, HEAD_DIM), F32),
        compiler_params=_cparams("arbitrary", "arbitrary"),
        name="attn_paged",
    )(table.reshape(-1), *args)
    return out.reshape(b, units, rows, HEAD_DIM)


def _cmp_proj_body(x_ref, pa_ref, pb_ref, wa_ref, wb_ref, za_ref, zb_ref):
    x = x_ref[0]
    za_ref[0] = jnp.dot((x + pa_ref[...]).astype(BF16), wa_ref[...], preferred_element_type=F32)
    zb_ref[0] = jnp.dot((x + pb_ref[...]).astype(BF16), wb_ref[...], preferred_element_type=F32)


def _cmp_mlp_body(n_chunks, za_ref, zb_ref, w2_ref, o_ref):
    zb_next = pltpu.roll(zb_ref[0], n_chunks - 1, 0)
    hid = za_ref[0] + zb_next
    c = 0.7978845608028654
    hid = 0.5 * hid * (1.0 + jnp.tanh(c * (hid + 0.044715 * (hid * hid * hid))))
    w2 = w2_ref[...]
    hb = hid.astype(w2.dtype)
    for g in range(NSA_KV_GROUPS):
        sl = slice(g * HEAD_DIM, (g + 1) * HEAD_DIM)
        o_ref[0, :, sl] = jnp.dot(hb[:, sl], w2, preferred_element_type=F32, precision=_dot_precision(w2.dtype))


def _compress(rows, pos_emb, w1, w2):
    b, tc, gw = rows.shape
    g = NSA_KV_GROUPS
    n_chunks = tc // CMP_STRIDE
    cw = CMP_STRIDE * gw
    x = rows.reshape(b, n_chunks, cw)
    w1r = w1.reshape(2, CMP_STRIDE, HEAD_DIM, HEAD_DIM)
    eye = jnp.eye(g, dtype=w1.dtype)
    wcat = jnp.einsum("hldj,ge->hlgdej", w1r, eye).reshape(2, cw, g * HEAD_DIM).astype(BF16)
    pos = jnp.broadcast_to(pos_emb.reshape(2, CMP_STRIDE, 1, HEAD_DIM), (2, CMP_STRIDE, g, HEAD_DIM)).reshape(2, 1, cw)
    cm = _row_tile(n_chunks, 128)
    za, zb = pl.pallas_call(
        _cmp_proj_body,
        grid=(b, n_chunks // cm),
        in_specs=[
            pl.BlockSpec((1, cm, cw), lambda bi, i: (bi, i, 0)),
            pl.BlockSpec((1, cw), lambda bi, i: (0, 0)),
            pl.BlockSpec((1, cw), lambda bi, i: (0, 0)),
            pl.BlockSpec((cw, g * HEAD_DIM), lambda bi, i: (0, 0)),
            pl.BlockSpec((cw, g * HEAD_DIM), lambda bi, i: (0, 0)),
        ],
        out_specs=[pl.BlockSpec((1, cm, g * HEAD_DIM), lambda bi, i: (bi, i, 0))] * 2,
        out_shape=[jax.ShapeDtypeStruct((b, n_chunks, g * HEAD_DIM), F32)] * 2,
        compiler_params=_cparams("parallel", "parallel"),
        name="cmp_proj",
    )(x, pos[0], pos[1], wcat[0], wcat[1])
    return _cmp_mlp(za, zb, w2.astype(BF16))


def _cmp_mlp(za, zb, w2):
    b, n_chunks, gw = za.shape
    g = NSA_KV_GROUPS
    return pl.pallas_call(
        functools.partial(_cmp_mlp_body, n_chunks),
        grid=(b,),
        in_specs=[
            pl.BlockSpec((1, n_chunks, g * HEAD_DIM), lambda bi: (bi, 0, 0)),
            pl.BlockSpec((1, n_chunks, g * HEAD_DIM), lambda bi: (bi, 0, 0)),
            pl.BlockSpec((HEAD_DIM, HEAD_DIM), lambda bi: (0, 0)),
        ],
        out_specs=pl.BlockSpec((1, n_chunks, g * HEAD_DIM), lambda bi: (bi, 0, 0)),
        out_shape=jax.ShapeDtypeStruct((b, n_chunks, g * HEAD_DIM), F32),
        compiler_params=_cparams("parallel"),
        name="cmp_mlp",
    )(za, zb, w2)


def _split_bf16(x):
    hi = x.astype(BF16)
    return hi, (x - hi.astype(F32)).astype(BF16)


def _dot_split(a, b_hi, b_lo, dims):
    a_hi, a_lo = _split_bf16(a)
    n = a.shape[0]
    both = lax.dot_general(jnp.concatenate([a_hi, a_lo], axis=0), b_hi, dims, preferred_element_type=F32)
    return both[:n] + both[n:] + lax.dot_general(a_hi, b_lo, dims, preferred_element_type=F32)


_NT = (((1,), (1,)), ((), ()))
_NN = (((1,), (0,)), ((), ()))


def _cmp_paged_body(n_pages, pg, page, tbl_ref, pos_ref, w_ref, pool, za_ref, zb_ref, xbuf, whi_ref, wlo_ref, sem):
    bi = pl.program_id(0)
    s = pl.program_id(1)
    n_steps = n_pages // pg
    g_n = NSA_KV_GROUPS
    rpp = page * g_n
    n_rows = pg * page // CMP_STRIDE

    def copies(row, step, slot):
        return [pltpu.make_async_copy(pool.at[tbl_ref[row * n_pages + step * pg + k]],
                                      xbuf.at[slot, pl.ds(k * rpp, rpp)], sem.at[slot]) for k in range(pg)]

    gstep = bi * n_steps + s

    @pl.when((bi == 0) & (s == 0))
    def _():
        for c in copies(0, 0, 0):
            c.start()

    @pl.when(s == 0)
    def _():
        w_hi, w_lo = _split_bf16(w_ref[...])
        whi_ref[...] = w_hi
        wlo_ref[...] = w_lo

    @pl.when(s + 1 < n_steps)
    def _():
        for c in copies(bi, s + 1, lax.rem(gstep + 1, 2)):
            c.start()

    @pl.when((s + 1 == n_steps) & (bi + 1 < pl.num_programs(0)))
    def _():
        for c in copies(bi + 1, 0, lax.rem(gstep + 1, 2)):
            c.start()

    slot = lax.rem(gstep, 2)
    for c in copies(bi, s, slot):
        c.wait()
    w_hi = whi_ref[...]
    w_lo = wlo_ref[...]
    pw = _dot_split(pos_ref[...], w_hi, w_lo, _NN)
    col = lax.broadcasted_iota(jnp.int32, (1, 2 * HEAD_DIM), 1)
    pos_term = jnp.where(col < HEAD_DIM, pw[0:1, :], pw[1:2, :])
    x_cat = jnp.concatenate(
        [jnp.concatenate([xbuf[slot, pl.ds(l * g_n + g, n_rows, stride=CMP_STRIDE * g_n), :]
                          for l in range(CMP_STRIDE)], axis=1) for g in range(g_n)], axis=0)
    z = _dot_split(x_cat, w_hi, w_lo, _NN) + pos_term
    for g in range(g_n):
        za_ref[0, :, g * HEAD_DIM:(g + 1) * HEAD_DIM] = z[g * n_rows:(g + 1) * n_rows, :HEAD_DIM]
        zb_ref[0, :, g * HEAD_DIM:(g + 1) * HEAD_DIM] = z[g * n_rows:(g + 1) * n_rows, HEAD_DIM:]


def _compress_paged(pool, table, pos_emb, w1, w2):
    b, n_pages = table.shape
    rpp = pool.shape[1]
    page = rpp // NSA_KV_GROUPS
    gw = NSA_KV_GROUPS * HEAD_DIM
    pg = _pages_per_step(n_pages, rpp * HEAD_DIM * 4, CMP_BUFFER_BYTES)
    n_steps = n_pages // pg
    n_rows = pg * page // CMP_STRIDE
    n_chunks = n_pages * page // CMP_STRIDE
    half = CMP_STRIDE * HEAD_DIM
    w_cat = jnp.concatenate([w1[:half], w1[half:]], axis=1)
    pos_rows = jnp.pad(pos_emb.reshape(2, half), ((0, SUBLANES - 2), (0, 0)))
    grid_spec = pltpu.PrefetchScalarGridSpec(
        num_scalar_prefetch=1,
        grid=(b, n_steps),
        in_specs=[pl.BlockSpec((SUBLANES, half), lambda bi, i, tbl: (0, 0)),
                  pl.BlockSpec((half, 2 * HEAD_DIM), lambda bi, i, tbl: (0, 0)),
                  pl.BlockSpec(memory_space=pl.ANY)],
        out_specs=[pl.BlockSpec((1, n_rows, gw), lambda bi, i, tbl: (bi, i, 0))] * 2,
        scratch_shapes=[pltpu.VMEM((2, pg * rpp, HEAD_DIM), F32), pltpu.VMEM((half, 2 * HEAD_DIM), BF16),
                        pltpu.VMEM((half, 2 * HEAD_DIM), BF16), pltpu.SemaphoreType.DMA((2,))],
    )
    za, zb = pl.pallas_call(
        functools.partial(_cmp_paged_body, n_pages, pg, page),
        grid_spec=grid_spec,
        out_shape=[jax.ShapeDtypeStruct((b, n_chunks, gw), F32)] * 2,
        compiler_params=_cparams("arbitrary", "arbitrary"),
        name="cmp_proj_paged",
    )(table.reshape(-1), pos_rows, w_cat, pool)
    return _cmp_mlp(za, zb, w2)


def _cmp_attn_body(tq, nc, ns, n_pick, q_start, q_ref, kc_ref, vc_ref, cover_ref, gate_ref, o_ref, sel_ref):
    unit = pl.program_id(1)
    qi = pl.program_id(2)
    precise = q_ref.dtype == F32
    if precise:
        kc_hi, kc_lo = _split_bf16(kc_ref[0])
        vc_hi, vc_lo = _split_bf16(vc_ref[0])
    else:
        kc_hi = kc_ref[0].astype(BF16)
        vc_hi = vc_ref[0].astype(BF16)
    ncp = kc_hi.shape[0]
    q_pos = q_start + qi * tq + lax.broadcasted_iota(jnp.int32, (tq, ncp), 0)
    n_idx = lax.broadcasted_iota(jnp.int32, (tq, ncp), 1)
    visible = (n_idx * CMP_STRIDE + (CMP_BLOCK - 1) <= q_pos) & (n_idx < nc)
    gate = gate_ref[0]
    glane = lax.broadcasted_iota(jnp.int32, gate.shape, 1)
    p_grp = jnp.zeros((tq, ncp), F32)
    for r in range(HEADS_PER_GROUP):
        q = q_ref[0, :, r * HEAD_DIM:(r + 1) * HEAD_DIM]
        if precise:
            s = _dot_split(q, kc_hi, kc_lo, _NT)
        else:
            s = lax.dot_general(q, kc_hi, _NT, preferred_element_type=F32)
        s = jnp.where(visible, s, NEG_INF)
        m = jnp.max(s, axis=-1, keepdims=True)
        e = jnp.where(visible, jnp.exp(s - m), 0.0)
        l = jnp.sum(e, axis=-1, keepdims=True)
        p = e / jnp.where(l > 0.0, l, 1.0)
        p_grp = p_grp + p
        want = FOX_HEADS + (unit * HEADS_PER_GROUP + r) * N_NSA_BRANCHES
        gcol = jnp.sum(jnp.where(glane == want, gate, 0.0), axis=-1, keepdims=True)
        if precise:
            o = _dot_split(p, vc_hi, vc_lo, _NN)
        else:
            o = jnp.dot(p.astype(BF16), vc_hi, preferred_element_type=F32)
        o_ref[0, :, r * HEAD_DIM:(r + 1) * HEAD_DIM] = (o * gcol).astype(o_ref.dtype)

    imp = jnp.dot(p_grp, cover_ref[...], preferred_element_type=F32, precision=HIGHEST)
    nsp = imp.shape[1]
    j = lax.broadcasted_iota(jnp.int32, (tq, nsp), 1)
    cur = (q_start + qi * tq + lax.broadcasted_iota(jnp.int32, (tq, nsp), 0)) // SEL_BLOCK
    dist = cur - j
    forced = (j == 0) | ((dist >= 0) & (dist < N_LOCAL_SEL))
    score = jnp.where(forced, FORCE_SCORE, imp)
    score = jnp.where(dist >= 0, score, NEG_INF)
    score = jnp.where(j < ns, score, PAD_SCORE)
    sel = jnp.zeros((tq, nsp), F32)
    j_f = j.astype(F32)
    for _ in range(n_pick):
        mx = jnp.max(score, axis=-1, keepdims=True)
        first = jnp.min(jnp.where(score == mx, j_f, float(nsp)), axis=-1, keepdims=True)
        hit = j_f == first
        sel = jnp.where(hit, 1.0, sel)
        score = jnp.where(hit, TAKEN_SCORE, score)
    sel_ref[0, 0] = sel


def _cmp_attention(nq, kc, vc, gate, t, q_start, n_keys, out_dtype):
    b = nq.shape[0]
    ncp = kc.shape[1]
    nc = (n_keys - CMP_BLOCK) // CMP_STRIDE + 1
    ns = -(-n_keys // SEL_BLOCK)
    nsp = -(-ns // LANES) * LANES
    n_pick = min(N_SELECT, ns)
    c0 = np.arange(ncp)[:, None] * CMP_STRIDE
    s0 = np.arange(nsp)[None, :] * SEL_BLOCK
    cover = np.clip(np.minimum(c0 + CMP_BLOCK, s0 + SEL_BLOCK) - np.maximum(c0, s0), 0, None).astype(np.float32) / CMP_BLOCK
    cover = cover * (np.arange(ncp)[:, None] < nc) * (np.arange(nsp)[None, :] < ns)
    tq = _row_tile(t, 256)
    qw = HEADS_PER_GROUP * HEAD_DIM
    return pl.pallas_call(
        functools.partial(_cmp_attn_body, tq, nc, ns, n_pick, q_start),
        grid=(b, NSA_KV_GROUPS, t // tq),
        in_specs=[
            pl.BlockSpec((1, tq, qw), lambda bi, u, i: (bi, i, u)),
            pl.BlockSpec((1, ncp, HEAD_DIM), lambda bi, u, i: (bi, 0, u)),
            pl.BlockSpec((1, ncp, HEAD_DIM), lambda bi, u, i: (bi, 0, u)),
            pl.BlockSpec((ncp, nsp), lambda bi, u, i: (0, 0)),
            pl.BlockSpec((1, tq, LANES), lambda bi, u, i: (bi, i, 0)),
        ],
        out_specs=[
            pl.BlockSpec((1, tq, qw), lambda bi, u, i: (bi, i, u)),
            pl.BlockSpec((1, 1, tq, nsp), lambda bi, u, i: (bi, u, i, 0)),
        ],
        out_shape=[
            jax.ShapeDtypeStruct((b, t, NSA_KV_GROUPS * qw), out_dtype),
            jax.ShapeDtypeStruct((b, NSA_KV_GROUPS, t, nsp), F32),
        ],
        compiler_params=_cparams("parallel", "parallel", "parallel"),
        name="attn_cmp_select",
    )(nq, kc, vc, jnp.asarray(cover), gate)


def _merge_body(a_ref, b1_ref, b2_ref, b3_ref, ua_ref, ub_ref, ga_ref, gb_ref, o_ref):
    a = a_ref[...]
    prec = _dot_precision(a.dtype)
    bsum = (b1_ref[...].astype(F32) + b2_ref[...].astype(F32) + b3_ref[...].astype(F32)).astype(a.dtype)
    ya = jnp.dot(a, ua_ref[...], preferred_element_type=F32, precision=prec)
    yb = jnp.dot(bsum, ub_ref[...], preferred_element_type=F32, precision=prec)
    o_ref[...] = (ga_ref[...] * ya + gb_ref[...] * yb).astype(o_ref.dtype)


def _merge(o_fox, o_cmp, o_slc, o_win, u_fox, u_nsa, mg, d_model):
    n, w = o_fox.shape
    tm = _row_tile(n, MERGE_ROW_TILE)
    tn = 512
    nj = d_model // tn
    row = pl.BlockSpec((tm, w), lambda i, j: (i, 0))
    return pl.pallas_call(
        _merge_body,
        grid=(n // tm, nj),
        in_specs=[row, row, row, row,
                  pl.BlockSpec((w, tn), lambda i, j: (0, j)),
                  pl.BlockSpec((w, tn), lambda i, j: (0, j)),
                  pl.BlockSpec((tm, tn), lambda i, j: (i, j)),
                  pl.BlockSpec((tm, tn), lambda i, j: (i, j + nj))],
        out_specs=pl.BlockSpec((tm, tn), lambda i, j: (i, j)),
        out_shape=jax.ShapeDtypeStruct((n, d_model), o_fox.dtype),
        compiler_params=_cparams("parallel", "parallel"),
        name="mixer_merge",
    )(o_fox, o_cmp, o_slc, o_win, u_fox, u_nsa, mg, mg)


def _router_body(h_ref, g_ref, w_ref, b_ref, u_ref, route_ref):
    x = h_ref[...]
    xn = x * lax.rsqrt(jnp.mean(x * x, axis=-1, keepdims=True) + NORM_EPS) * g_ref[...]
    u_ref[...] = xn.astype(u_ref.dtype)
    z = jnp.dot(xn, w_ref[...], preferred_element_type=F32, precision=HIGHEST) + b_ref[...]
    c = lax.broadcasted_iota(jnp.int32, z.shape, 1)
    big = z.shape[1]
    is_g = c < N_GROUPS
    gmax = jnp.max(jnp.where(is_g, z, -jnp.inf), axis=-1, keepdims=True)
    gsel = jnp.min(jnp.where(is_g & (z == gmax), c, big), axis=-1, keepdims=True)
    gsum = jnp.sum(jnp.where(is_g, jnp.exp(z - gmax), 0.0), axis=-1, keepdims=True)
    lo = N_GROUPS + gsel * EXPERTS_PER_GROUP
    in_grp = (c >= lo) & (c < lo + EXPERTS_PER_GROUP)
    z1 = jnp.max(jnp.where(in_grp, z, -jnp.inf), axis=-1, keepdims=True)
    c1 = jnp.min(jnp.where(in_grp & (z == z1), c, big), axis=-1, keepdims=True)
    rest = in_grp & (c != c1)
    z2 = jnp.max(jnp.where(rest, z, -jnp.inf), axis=-1, keepdims=True)
    c2 = jnp.min(jnp.where(rest & (z == z2), c, big), axis=-1, keepdims=True)
    e2 = jnp.exp(z2 - z1)
    p1 = 1.0 / (1.0 + e2)
    p2 = e2 / (1.0 + e2)
    gp = 1.0 / gsum
    out = jnp.where(c == 0, (c1 - N_GROUPS).astype(F32), 0.0)
    out = jnp.where(c == 1, (c2 - N_GROUPS).astype(F32), out)
    out = jnp.where(c == 2, gp * p1, out)
    out = jnp.where(c == 3, gp * p2, out)
    route_ref[...] = out


def _router(h, gain, w_cat, b_cat):
    n, d = h.shape
    tm = _row_tile(n, 512)
    return pl.pallas_call(
        _router_body,
        grid=(n // tm,),
        in_specs=[pl.BlockSpec((tm, d), lambda i: (i, 0)), pl.BlockSpec((1, d), lambda i: (0, 0)),
                  pl.BlockSpec((d, LANES), lambda i: (0, 0)), pl.BlockSpec((1, LANES), lambda i: (0, 0))],
        out_specs=[pl.BlockSpec((tm, d), lambda i: (i, 0)), pl.BlockSpec((tm, LANES), lambda i: (i, 0))],
        out_shape=[jax.ShapeDtypeStruct((n, d), F32), jax.ShapeDtypeStruct((n, LANES), F32)],
        compiler_params=_cparams("parallel"),
        name="moe_router",
    )(h, gain.reshape(1, d), w_cat, b_cat)


def _gather_rows(src_hbm, dst_ref, sem, idx_ref, base, count, start):
    if not start:
        pltpu.make_async_copy(src_hbm.at[pl.ds(0, count)], dst_ref, sem).wait()
        return

    def body(r, c):
        pltpu.make_async_copy(src_hbm.at[pl.ds(idx_ref[base + r], 1)], dst_ref.at[pl.ds(r, 1)], sem).start()
        return c

    lax.fori_loop(0, count, body, 0, unroll=8)


def _expert_body(tm, te_ref, nv_ref, tok_ref, wg_ref, wu_ref, wd_ref, x_hbm, o_ref, xbuf, sem):
    i = pl.program_id(0)
    nv = nv_ref[0]

    def rows_of(tile, start):
        slot = lax.rem(tile, 2)
        _gather_rows(x_hbm, xbuf.at[slot], sem.at[slot], tok_ref, tile * tm, tm, start)

    @pl.when(i == 0)
    def _():
        rows_of(0, True)

    @pl.when(i + 1 < nv)
    def _():
        rows_of(i + 1, True)

    @pl.when(i < nv)
    def _():
        rows_of(i, False)
        x = xbuf[lax.rem(i, 2)].astype(BF16)
        hg = jnp.dot(x, wg_ref[0].astype(BF16), preferred_element_type=F32)
        hu = jnp.dot(x, wu_ref[0].astype(BF16), preferred_element_type=F32)
        hid = (hg * jax.nn.sigmoid(hg)) * hu
        o_ref[...] = jnp.dot(hid.astype(BF16), wd_ref[0].astype(BF16), preferred_element_type=F32)

    @pl.when(i >= nv)
    def _():
        o_ref[...] = jnp.zeros(o_ref.shape, o_ref.dtype)


def _experts(x, row_token, tile_expert, n_valid, wg, wu, wd, tm):
    rows = row_token.shape[0]
    d = x.shape[1]
    n_tiles = rows // tm
    f = wg.shape[2]

    def w_map(i, te, nv, tok):
        return (te[i], 0, 0)

    grid_spec = pltpu.PrefetchScalarGridSpec(
        num_scalar_prefetch=3,
        grid=(n_tiles,),
        in_specs=[pl.BlockSpec((1, d, f), w_map), pl.BlockSpec((1, d, f), w_map), pl.BlockSpec((1, f, d), w_map),
                  pl.BlockSpec(memory_space=pl.ANY)],
        out_specs=pl.BlockSpec((tm, d), lambda i, te, nv, tok: (i, 0)),
        scratch_shapes=[pltpu.VMEM((2, tm, d), F32), pltpu.SemaphoreType.DMA((2,))],
    )
    return pl.pallas_call(
        functools.partial(_expert_body, tm),
        grid_spec=grid_spec,
        out_shape=jax.ShapeDtypeStruct((rows, d), F32),
        compiler_params=_cparams("arbitrary"),
        name="moe_experts",
    )(tile_expert, n_valid, row_token, wg, wu, wd, x)


def _combine_body(tm, n, slot_ref, h_ref, route_ref, src_hbm, o_ref, buf, sem):
    i = pl.program_id(0)
    n_tiles = pl.num_programs(0)

    def rows_of(tile, start):
        slot = lax.rem(tile, 2)
        for k in range(2):
            _gather_rows(src_hbm, buf.at[slot, k], sem.at[slot], slot_ref, k * n + tile * tm, tm, start)

    @pl.when(i == 0)
    def _():
        rows_of(0, True)

    @pl.when(i + 1 < n_tiles)
    def _():
        rows_of(i + 1, True)

    rows_of(i, False)
    slot = lax.rem(i, 2)
    route = route_ref[...]
    o_ref[...] = h_ref[...] + route[:, 2:3] * buf[slot, 0] + route[:, 3:4] * buf[slot, 1]


def _moe_combine(h, route, out_sorted, slot_kn):
    n, d = h.shape
    tm = _row_tile(n, 256)
    grid_spec = pltpu.PrefetchScalarGridSpec(
        num_scalar_prefetch=1,
        grid=(n // tm,),
        in_specs=[pl.BlockSpec((tm, d), lambda i, sl: (i, 0)), pl.BlockSpec((tm, LANES), lambda i, sl: (i, 0)),
                  pl.BlockSpec(memory_space=pl.ANY)],
        out_specs=pl.BlockSpec((tm, d), lambda i, sl: (i, 0)),
        scratch_shapes=[pltpu.VMEM((2, 2, tm, d), F32), pltpu.SemaphoreType.DMA((2,))],
    )
    return pl.pallas_call(
        functools.partial(_combine_body, tm, n),
        grid_spec=grid_spec,
        out_shape=jax.ShapeDtypeStruct((n, d), F32),
        compiler_params=_cparams("arbitrary"),
        name="moe_combine",
    )(slot_kn, h, route, out_sorted)


def _moe(h, gain, w_router, b_router, wg, wu, wd):
    n, d = h.shape
    ut, route = _router(h, gain, w_router, b_router)
    expert = route[:, :2].astype(jnp.int32).reshape(-1)
    n_asg = 2 * n
    tm = 256 if n_asg >= 8192 else 16
    n_tiles = n_asg // tm + N_EXPERTS
    onehot = (expert[:, None] == jnp.arange(N_EXPERTS, dtype=jnp.int32)[None, :]).astype(jnp.int32)
    running = jnp.cumsum(onehot, axis=0)
    counts = running[-1]
    rank = jnp.sum(onehot * running, axis=1) - 1
    tiles_per = (counts + tm - 1) // tm
    tile_end = jnp.cumsum(tiles_per)
    start_row = (tile_end - tiles_per) * tm
    slot = (jnp.sum(onehot * start_row[None, :], axis=1) + rank).astype(jnp.int32)
    row_token = jnp.zeros((n_tiles * tm,), jnp.int32).at[slot].set(jnp.arange(n_asg, dtype=jnp.int32) // 2)
    n_valid = tile_end[-1].astype(jnp.int32)
    tile_ids = jnp.minimum(jnp.arange(n_tiles, dtype=jnp.int32), n_valid - 1)
    tile_expert = jnp.sum((tile_end[None, :] <= tile_ids[:, None]).astype(jnp.int32), axis=1)
    out_sorted = _experts(ut, row_token, tile_expert, n_valid.reshape(1), wg, wu, wd, tm)
    return _moe_combine(h, route, out_sorted, jnp.transpose(slot.reshape(n, 2)).reshape(-1))


def _rope_tables(positions):
    half = HEAD_DIM // 2
    inv = ROPE_THETA ** (-np.arange(half, dtype=np.float64) / half)
    ang = np.asarray(positions, np.float64)[:, None] * inv[None, :]
    cos = np.concatenate([np.cos(ang), np.cos(ang)], axis=-1)
    sin = np.concatenate([-np.sin(ang), np.sin(ang)], axis=-1)
    return jnp.asarray(cos, F32), jnp.asarray(sin, F32)


PROJ_ROW_TILE = 2048
MERGE_ROW_TILE = 1024


def _project(u, wts, p, b, t, q_start):
    n = u.shape[0]
    tm = _row_tile(n, PROJ_ROW_TILE)
    tn = 512
    tn_kv = 3 * HEAD_DIM
    scale = HEAD_DIM ** -0.5
    gain_spec = pl.BlockSpec((1, HEAD_DIM), lambda i, j: (0, 0))

    def run(w, epilogue, extras, specs, dtype, name, tn_=tn):
        shape = jax.ShapeDtypeStruct((n, w.shape[1]), dtype)
        spec = pl.BlockSpec((tm, tn_), lambda i, j: (i, j))
        return _matmul(u, w, epilogue, extras, specs, shape, spec, tm, tn_, name)

    cos, sin = _rope_tables(q_start + (np.arange(max(tm, t)) % t))
    n_tab = cos.shape[0] // tm
    tab_spec = pl.BlockSpec((tm, HEAD_DIM), lambda i, j: (i % n_tab, 0))

    fq = run(wts["fq"], functools.partial(_ep_head_norm, scale), [p["fox_q_norm"]], [gain_spec], u.dtype, "proj_fq")
    fk = run(wts["fk"], functools.partial(_ep_head_norm, 1.0), [p["fox_k_norm"]], [gain_spec], F32, "proj_fk")
    fv = run(wts["fv"], _ep_identity, [], [], F32, "proj_fv")
    nq = run(wts["nq"], functools.partial(_ep_head_norm_rope, scale), [p["nsa_q_norm"], cos, sin],
             [gain_spec, tab_spec, tab_spec], u.dtype, "proj_nq")
    nk = run(wts["nk"], functools.partial(_ep_head_norm_rope, 1.0), [p["nsa_k_norm"], cos, sin],
             [gain_spec, tab_spec, tab_spec], F32, "proj_nk", tn_kv)
    nv = run(wts["nv"], _ep_identity, [], [], F32, "proj_nv", tn_kv)
    mg = run(wts["mg"], _ep_sigmoid, [], [], F32, "proj_mg")
    small = run(wts["small"], _ep_small, [p["small_bias"]], [pl.BlockSpec((1, LANES), lambda i, j: (0, 0))],
                F32, "proj_small", LANES)
    return fq, fk, fv, nq, nk, nv, mg, small


def _layer(x, p, wts, q_start, past):
    b, t, d = x.shape
    n = b * t
    g = NSA_KV_GROUPS
    kvw = g * HEAD_DIM
    x2d = x.reshape(n, d)
    dense = wts["f32"] if past is not None else wts
    u = _rmsnorm(x2d, p["norm_mix"], dense["fq"].dtype)
    fq, fk, fv, nq, nk, nv, mg, small = _project(u, dense, p, b, t, q_start)

    logf = small[:, :FOX_HEADS].reshape(b, t, FOX_HEADS)
    gate3 = small.reshape(b, t, LANES)
    fq3 = fq.reshape(b, t, -1)
    nq3 = nq.reshape(b, t, -1)
    fk3 = fk.reshape(b, t, -1)
    fv3 = fv.reshape(b, t, -1)
    nk3 = nk.reshape(b, t, 3 * kvw)
    nv3 = nv.reshape(b, t, 3 * kvw)
    ck, sk, wk = (nk3[:, :, i * kvw:(i + 1) * kvw] for i in range(3))
    cv, sv, wv = (nv3[:, :, i * kvw:(i + 1) * kvw] for i in range(3))

    if past is None:
        n_keys = t
        n_lp = t // LANES
        lf_pool = jnp.transpose(logf.reshape(b * n_lp, LANES, FOX_HEADS), (0, 2, 1))
        bias = _fox_key_bias(lf_pool, jnp.arange(b * n_lp, dtype=jnp.int32).reshape(b, n_lp), None)
        o_fox = _flash_attention("fox", fq3, fk3, fv3, 0, t, bias=bias)
        kc = _compress(ck, p["cmp_pos_k"], p["cmp_w1_k"], p["cmp_w2_k"])
        vc = _compress(cv, p["cmp_pos_v"], p["cmp_w1_v"], p["cmp_w2_v"])
        o_cmp, sel = _cmp_attention(nq3, kc, vc, gate3, t, q_start, n_keys, BF16)
        o_slc = _flash_attention("slc", nq3, nk3, nv3, g, t, sel=sel, gate=gate3, gate_col=1)
        o_win = _flash_attention("win", nq3, nk3, nv3, 2 * g, t, gate=gate3, gate_col=2)
        keep = min(WINDOW, t)
        win_k, win_v = wk[:, t - keep:], wv[:, t - keep:]
    else:
        table = past["table"]
        n_pages = table.shape[1]
        page = past["page"]
        past_len = n_pages * page
        n_keys = past_len + t
        new_page = jnp.pad(jnp.transpose(logf, (0, 2, 1)), ((0, 0), (0, 0), (0, page - t)))
        bias = _fox_key_bias(past["fox_logf"], table, new_page)[:, :, :n_keys]
        causal = jnp.where(jnp.arange(t)[None, :] <= jnp.arange(t)[:, None], 0.0, NEG_INF).astype(F32)
        bias_new = bias[:, :, None, past_len:] + causal[None, None]
        qf = jnp.transpose(fq3.reshape(b, t, FOX_HEADS, HEAD_DIM), (0, 2, 1, 3))
        o = _paged_attention(qf, past["fox_k"], past["fox_v"], table, bias[:, :, None, :past_len],
                             fk3, fv3, bias_new)
        o_fox = jnp.transpose(o, (0, 2, 1, 3)).reshape(b, t, -1)
        nc = (n_keys - CMP_BLOCK) // CMP_STRIDE + 1
        assert (nc + 1) * CMP_STRIDE <= past_len and page % CMP_STRIDE == 0
        kc = _compress_paged(past["cmp_k"], table, p["cmp_pos_k"], p["cmp_w1_k"], p["cmp_w2_k"])
        vc = _compress_paged(past["cmp_v"], table, p["cmp_pos_v"], p["cmp_w1_v"], p["cmp_w2_v"])
        o_cmp, sel = _cmp_attention(nq3, kc, vc, gate3, t, q_start, n_keys, F32)
        rows = HEADS_PER_GROUP * t
        qn = jnp.transpose(nq3.reshape(b, t, g, HEADS_PER_GROUP, HEAD_DIM), (0, 2, 3, 1, 4)).reshape(b, g, rows, HEAD_DIM)
        gates = small[:, FOX_HEADS:FOX_HEADS + NSA_HEADS * N_NSA_BRANCHES].reshape(b, t, g, HEADS_PER_GROUP, N_NSA_BRANCHES)
        gates = jnp.transpose(gates, (4, 0, 2, 3, 1)).reshape(N_NSA_BRANCHES, b, g, rows, 1)
        ns = -(-n_keys // SEL_BLOCK)
        key_sel = jnp.repeat(sel[..., :ns], SEL_BLOCK, axis=-1)[..., :n_keys]
        sel_bias = jnp.where(key_sel > 0.5, 0.0, NEG_INF).astype(F32)
        sel_new = jnp.tile(sel_bias[..., past_len:] + causal[None, None], (1, 1, HEADS_PER_GROUP, 1))
        o = _paged_attention(qn, past["slc_k"], past["slc_v"], table, sel_bias[..., :past_len],
                             sk, sv, sel_new, gate=gates[1])
        o_slc = _rows_to_tokens(o, b, t)
        win_len = past["win_k"].shape[1]
        assert win_len % page == 0
        k_start = q_start - win_len
        q_abs = q_start + jnp.arange(t)[:, None]
        k_abs = k_start + jnp.arange(win_len)[None, :]
        band = jnp.where((k_abs >= 0) & (k_abs > q_abs - WINDOW), 0.0, NEG_INF).astype(F32)
        band = jnp.broadcast_to(band[None, None], (b, g, t, win_len))
        band_new = jnp.where(jnp.arange(t)[None, :] > jnp.arange(t)[:, None] - WINDOW, causal, NEG_INF)
        band_new = jnp.broadcast_to(jnp.tile(band_new, (HEADS_PER_GROUP, 1))[None, None], (b, g, rows, t))
        o = _paged_attention(qn, past["win_k_pool"], past["win_v_pool"], past["win_table"], band, wk, wv,
                             band_new, gate=gates[2])
        o_win = _rows_to_tokens(o, b, t)
        wk_all = jnp.concatenate([past["win_k"], wk], axis=1)
        wv_all = jnp.concatenate([past["win_v"], wv], axis=1)
        keep = min(WINDOW, wk_all.shape[1])
        win_k, win_v = wk_all[:, -keep:], wv_all[:, -keep:]

    m = _merge(o_fox.reshape(n, -1), o_cmp.reshape(n, -1), o_slc.reshape(n, -1), o_win.reshape(n, -1),
               dense["up_fox"], dense["up_nsa"], mg, d)
    tm = _row_tile(n, MERGE_ROW_TILE)
    h = _matmul(m, dense["out"], _ep_residual, [x2d], [pl.BlockSpec((tm, 512), lambda i, j: (i, j))],
                jax.ShapeDtypeStruct((n, d), F32), pl.BlockSpec((tm, 512), lambda i, j: (i, j)), tm, 512, "out_proj")
    y = _moe(h, p["norm_ffn"], wts["router"], p["router_bias"], wts["gate_e"], wts["up_e"], wts["down_e"])

    def heads(a, hh):
        return a.reshape(b, -1, hh, HEAD_DIM)

    state = (heads(fk3, FOX_HEADS), heads(fv3, FOX_HEADS), logf, heads(ck, g), heads(cv, g), heads(sk, g),
             heads(sv, g), heads(win_k, g), heads(win_v, g))
    return y.reshape(b, t, d), state


def _rows_to_tokens(o, b, t):
    o = o.reshape(b, NSA_KV_GROUPS, HEADS_PER_GROUP, t, HEAD_DIM)
    return jnp.transpose(o, (0, 3, 1, 2, 4)).reshape(b, t, -1)


def _prepare(l, norm_mix, w_in, fox_f_bias, fox_q_norm, fox_k_norm, nsa_q_norm, nsa_k_norm, cmp_pos_k, cmp_w1_k,
             cmp_w2_k, cmp_pos_v, cmp_w1_v, cmp_w2_v, nsa_gate_bias, w_up_fox, w_up_nsa, w_out, norm_ffn,
             w_router_group, b_router_group, w_router_expert, b_router_expert, w_gate_e, w_up_e, w_down_e):
    d = w_in.shape[1]
    fw = FOX_HEADS * HEAD_DIM
    nw = NSA_HEADS * HEAD_DIM
    kvw = NSA_KV_GROUPS * HEAD_DIM
    sizes = (fw, fw, fw, FOX_HEADS, nw, kvw, kvw, kvw, kvw, kvw, kvw, NSA_HEADS * N_NSA_BRANCHES, 2 * d)
    offs = np.concatenate([[0], np.cumsum(sizes)])
    w = w_in[l]

    def seg(i):
        return w[:, offs[i]:offs[i + 1]]

    n_small = FOX_HEADS + NSA_HEADS * N_NSA_BRANCHES
    small_w = jnp.pad(jnp.concatenate([seg(3), seg(11)], axis=1), ((0, 0), (0, LANES - n_small)))
    small_b = jnp.pad(jnp.concatenate([fox_f_bias[l], nsa_gate_bias[l]]), (0, LANES - n_small)).reshape(1, LANES)
    n_route = N_GROUPS + N_EXPERTS
    router_w = jnp.pad(jnp.concatenate([w_router_group[l], w_router_expert[l]], axis=1), ((0, 0), (0, LANES - n_route)))
    router_b = jnp.pad(jnp.concatenate([b_router_group[l], b_router_expert[l]]), (0, LANES - n_route)).reshape(1, LANES)
    dense = {
        "fq": seg(0), "fk": seg(1), "fv": seg(2), "nq": seg(4),
        "nk": jnp.concatenate([seg(5), seg(7), seg(9)], axis=1),
        "nv": jnp.concatenate([seg(6), seg(8), seg(10)], axis=1),
        "mg": seg(12), "small": small_w,
        "up_fox": w_up_fox[l], "up_nsa": w_up_nsa[l], "out": w_out[l],
    }
    wts = {
        **{k: v.astype(BF16) for k, v in dense.items()},
        "f32": dense,
        "router": router_w,
        "gate_e": w_gate_e[l], "up_e": w_up_e[l], "down_e": w_down_e[l],
    }
    p = {
        "norm_mix": norm_mix[l], "fox_q_norm": fox_q_norm[l].reshape(1, -1), "fox_k_norm": fox_k_norm[l].reshape(1, -1),
        "nsa_q_norm": nsa_q_norm[l].reshape(1, -1), "nsa_k_norm": nsa_k_norm[l].reshape(1, -1),
        "small_bias": small_b, "cmp_pos_k": cmp_pos_k[l], "cmp_w1_k": cmp_w1_k[l], "cmp_w2_k": cmp_w2_k[l],
        "cmp_pos_v": cmp_pos_v[l], "cmp_w1_v": cmp_w1_v[l], "cmp_w2_v": cmp_w2_v[l],
        "norm_ffn": norm_ffn[l], "router_bias": router_b,
    }
    return wts, p


def kernel(x_prompt, x_sample, cache_fox_k, cache_fox_v, cache_fox_logf, cache_cmp_k, cache_cmp_v, cache_slc_k, cache_slc_v, cache_win_k, cache_win_v, page_table, norm_mix, w_in, fox_f_bias, fox_q_norm, fox_k_norm, nsa_q_norm, nsa_k_norm, cmp_pos_k, cmp_w1_k, cmp_w2_k, cmp_pos_v, cmp_w1_v, cmp_w2_v, nsa_gate_bias, w_up_fox, w_up_nsa, w_out, norm_ffn, w_router_group, b_router_group, w_router_expert, b_router_expert, w_gate_e, w_up_e, w_down_e):
    depth = w_in.shape[0]
    n_phys, page = cache_fox_k.shape[1], cache_fox_k.shape[2]
    past_len = page_table.shape[1] * page
    h_p, h_s = x_prompt, x_sample
    p_states, s_states = [], []
    for l in range(depth):
        wts, p = _prepare(l, norm_mix, w_in, fox_f_bias, fox_q_norm, fox_k_norm, nsa_q_norm, nsa_k_norm, cmp_pos_k,
                          cmp_w1_k, cmp_w2_k, cmp_pos_v, cmp_w1_v, cmp_w2_v, nsa_gate_bias, w_up_fox, w_up_nsa,
                          w_out, norm_ffn, w_router_group, b_router_group, w_router_expert, b_router_expert,
                          w_gate_e, w_up_e, w_down_e)
        h_p, st_p = _layer(h_p, p, wts, 0, None)
        p_states.append(st_p)
        win_len = cache_win_k.shape[2]
        n_win = win_len // page
        dec_b = page_table.shape[0]

        def pool(c):
            return c.reshape(-1, page * c.shape[3], HEAD_DIM)

        past = {
            "table": page_table + l * n_phys, "page": page,
            "fox_k": pool(cache_fox_k), "fox_v": pool(cache_fox_v),
            "fox_logf": jnp.transpose(cache_fox_logf.reshape(-1, page, cache_fox_logf.shape[3]), (0, 2, 1)),
            "cmp_k": pool(cache_cmp_k), "cmp_v": pool(cache_cmp_v),
            "slc_k": pool(cache_slc_k), "slc_v": pool(cache_slc_v),
            "win_k_pool": pool(cache_win_k), "win_v_pool": pool(cache_win_v),
            "win_table": (l * dec_b + jnp.arange(dec_b, dtype=jnp.int32))[:, None] * n_win
                         + jnp.arange(n_win, dtype=jnp.int32)[None, :],
            "win_k": cache_win_k[l].reshape(dec_b, win_len, -1), "win_v": cache_win_v[l].reshape(dec_b, win_len, -1),
        }
        h_s, st_s = _layer(h_s, p, wts, past_len, past)
        s_states.append(st_s)
    p_out = [jnp.stack([st[i] for st in p_states], axis=0) for i in range(9)]
    s_out = [jnp.stack([st[i] for st in s_states], axis=0) for i in range(9)]
    return (h_p, h_s, *p_out, *s_out)
```

```python
import functools

import numpy as np
import jax
import jax.numpy as jnp
from jax import lax
from jax.experimental import pallas as pl
from jax.experimental.pallas import tpu as pltpu

F32 = jnp.float32
BF16 = jnp.bfloat16

HEAD_DIM = 128
FOX_HEADS = 8
NSA_HEADS = 8
NSA_KV_GROUPS = 2
HEADS_PER_GROUP = NSA_HEADS // NSA_KV_GROUPS
N_NSA_BRANCHES = 3
CMP_BLOCK = 32
CMP_STRIDE = 16
SEL_BLOCK = 64
N_SELECT = 16
N_LOCAL_SEL = 2
WINDOW = 512
N_GROUPS = 4
EXPERTS_PER_GROUP = 8
N_EXPERTS = N_GROUPS * EXPERTS_PER_GROUP
ROPE_THETA = 10000.0
NORM_EPS = 1e-6
NEG_INF = -1e30
FORCE_SCORE = 1e9
PAD_SCORE = -2e38
TAKEN_SCORE = -3e38

LANES = 128
SUBLANES = 8
VMEM_LIMIT_BYTES = 48 * 1024 * 1024

HIGHEST = lax.Precision.HIGHEST


def _cparams(*sem):
    return pltpu.CompilerParams(dimension_semantics=tuple(sem), vmem_limit_bytes=VMEM_LIMIT_BYTES)


def _row_tile(n, pref):
    t = min(n, pref)
    assert n % t == 0, (n, t)
    return t


def _rmsnorm_body(x_ref, g_ref, o_ref):
    x = x_ref[...]
    y = x * lax.rsqrt(jnp.mean(x * x, axis=-1, keepdims=True) + NORM_EPS)
    o_ref[...] = (y * g_ref[...]).astype(o_ref.dtype)


def _rmsnorm(x2d, gain, out_dtype):
    n, d = x2d.shape
    tm = _row_tile(n, 512)
    return pl.pallas_call(
        _rmsnorm_body,
        grid=(n // tm,),
        in_specs=[pl.BlockSpec((tm, d), lambda i: (i, 0)), pl.BlockSpec((1, d), lambda i: (0, 0))],
        out_specs=pl.BlockSpec((tm, d), lambda i: (i, 0)),
        out_shape=jax.ShapeDtypeStruct((n, d), out_dtype),
        compiler_params=_cparams("parallel"),
        name="rmsnorm",
    )(x2d, gain.reshape(1, d))


def _dot_precision(dtype):
    return HIGHEST if dtype == F32 else None


def _mm_body(epilogue, n_extra, a_ref, w_ref, *rest):
    acc = jnp.dot(a_ref[...], w_ref[...], preferred_element_type=F32, precision=_dot_precision(a_ref.dtype))
    epilogue(acc, rest[:n_extra], rest[n_extra:])


def _matmul(a, w, epilogue, extras, extra_specs, out_shapes, out_specs, tm, tn, name):
    n, k = a.shape
    c = w.shape[1]
    assert n % tm == 0 and c % tn == 0, (n, tm, c, tn)
    return pl.pallas_call(
        functools.partial(_mm_body, epilogue, len(extras)),
        grid=(n // tm, c // tn),
        in_specs=[pl.BlockSpec((tm, k), lambda i, j: (i, 0)), pl.BlockSpec((k, tn), lambda i, j: (0, j))]
        + list(extra_specs),
        out_specs=out_specs,
        out_shape=out_shapes,
        compiler_params=_cparams("parallel", "parallel"),
        name=name,
    )(a, w, *extras)


def _head_norm(x, gain):
    return x * lax.rsqrt(jnp.mean(x * x, axis=-1, keepdims=True) + NORM_EPS) * gain


def _rope_rot(x, cos, sin_signed):
    return x * cos + pltpu.roll(x, HEAD_DIM // 2, 1) * sin_signed


def _ep_head_norm(scale, acc, extras, outs):
    (gain_ref,) = extras
    (o_ref,) = outs
    for h in range(acc.shape[1] // HEAD_DIM):
        sl = slice(h * HEAD_DIM, (h + 1) * HEAD_DIM)
        o_ref[:, sl] = (_head_norm(acc[:, sl], gain_ref[...]) * scale).astype(o_ref.dtype)


def _ep_head_norm_rope(scale, acc, extras, outs):
    gain_ref, cos_ref, sin_ref = extras
    (o_ref,) = outs
    for h in range(acc.shape[1] // HEAD_DIM):
        sl = slice(h * HEAD_DIM, (h + 1) * HEAD_DIM)
        y = _rope_rot(_head_norm(acc[:, sl], gain_ref[...]), cos_ref[...], sin_ref[...])
        o_ref[:, sl] = (y * scale).astype(o_ref.dtype)


def _ep_identity(acc, extras, outs):
    (o_ref,) = outs
    o_ref[...] = acc.astype(o_ref.dtype)


def _ep_sigmoid(acc, extras, outs):
    (o_ref,) = outs
    o_ref[...] = jax.nn.sigmoid(acc).astype(o_ref.dtype)


def _ep_small(acc, extras, outs):
    (bias_ref,) = extras
    (o_ref,) = outs
    z = acc + bias_ref[...]
    col = lax.broadcasted_iota(jnp.int32, z.shape, 1)
    e = jnp.exp(-jnp.abs(z))
    log_sig = jnp.minimum(z, 0.0) - jnp.log(1.0 + e)
    sig = jnp.where(z >= 0, 1.0, e) / (1.0 + e)
    o_ref[...] = jnp.where(col < FOX_HEADS, log_sig, sig)


def _ep_residual(acc, extras, outs):
    (x_ref,) = extras
    (o_ref,) = outs
    o_ref[...] = x_ref[...] + acc


def _fox_bias_body(n_pages, has_new, *refs):
    if has_new:
        tbl_ref, new_ref, pool_ref, o_ref, buf, sem = refs
    else:
        tbl_ref, pool_ref, o_ref, buf, sem = refs
    bi = pl.program_id(0)

    def copy(p):
        return pltpu.make_async_copy(pool_ref.at[tbl_ref[bi * n_pages + p]], buf.at[p], sem)

    def start(p, c):
        copy(p).start()
        return c

    def wait(p, c):
        copy(p).wait()
        return c

    lax.fori_loop(0, n_pages, start, 0)
    lax.fori_loop(0, n_pages, wait, 0)
    if has_new:
        buf[n_pages] = new_ref[0]
    x = buf[...]
    p, h, w = x.shape
    within = x.reshape(p * h, w)
    lane = lax.broadcasted_iota(jnp.int32, within.shape, 1)
    s = 1
    while s < w:
        within = within + jnp.where(lane >= s, pltpu.roll(within, s, 1), 0.0)
        s *= 2
    within = within.reshape(p, h, w)
    total = jnp.broadcast_to(within[:, :, w - 1:w], x.shape)
    incl = total
    s = 1
    while s < p:
        incl = incl + jnp.concatenate([jnp.zeros((s, h, w), F32), incl[:p - s]], axis=0)
        s *= 2
    o_ref[0] = -(within + (incl - total))


def _fox_key_bias(pool, table, new_page):
    b, n_pages = table.shape
    h, w = pool.shape[1:]
    has_new = new_page is not None
    p_tot = n_pages + (1 if has_new else 0)
    in_specs = [pl.BlockSpec(memory_space=pl.ANY)]
    args = [pool]
    if has_new:
        in_specs.insert(0, pl.BlockSpec((1, h, w), lambda i, tbl: (i, 0, 0)))
        args.insert(0, new_page)
    grid_spec = pltpu.PrefetchScalarGridSpec(
        num_scalar_prefetch=1,
        grid=(b,),
        in_specs=in_specs,
        out_specs=pl.BlockSpec((1, p_tot, h, w), lambda i, tbl: (i, 0, 0, 0)),
        scratch_shapes=[pltpu.VMEM((p_tot, h, w), F32), pltpu.SemaphoreType.DMA(())],
    )
    out = pl.pallas_call(
        functools.partial(_fox_bias_body, n_pages, has_new),
        grid_spec=grid_spec,
        out_shape=jax.ShapeDtypeStruct((b, p_tot, h, w), F32),
        compiler_params=_cparams("arbitrary"),
        name="fox_key_bias",
    )(table.reshape(-1), *args)
    return jnp.transpose(out, (0, 2, 1, 3)).reshape(b, h, p_tot * w)


FLASH_QUERY_BLOCK = 512
FLASH_KEY_BLOCK = 512


def _flash_body(mode, heads, tq, tk, n_kv, gate_col, *refs):
    if mode == "fox":
        q_ref, k_ref, v_ref, bias_ref, o_ref, kb_ref, vb_ref = refs
    elif mode == "slc":
        q_ref, k_ref, v_ref, sel_ref, expand_ref, gate_ref, o_ref, kb_ref, vb_ref, selx_ref = refs
    else:
        q_ref, k_ref, v_ref, gate_ref, o_ref, kb_ref, vb_ref = refs
    unit = pl.program_id(1)
    qi = pl.program_id(2)

    @pl.when(qi == 0)
    def _():
        kb_ref[...] = k_ref[0].astype(BF16)
        vb_ref[...] = v_ref[0].astype(BF16)

    if mode == "slc":
        sel = sel_ref[0, 0].astype(BF16)
        for j in range(n_kv):
            selx_ref[j] = jnp.dot(sel, expand_ref[:, j * tk:(j + 1) * tk], preferred_element_type=F32)

    q_pos = qi * tq + lax.broadcasted_iota(jnp.int32, (tq, tk), 0)
    col = lax.broadcasted_iota(jnp.int32, (tq, tk), 1)
    qs = [q_ref[0, :, r * HEAD_DIM:(r + 1) * HEAD_DIM] for r in range(heads)]

    def block(j, carry, diagonal):
        ks = pl.multiple_of(j * tk, tk)
        k_pos = ks + col
        ok = None
        if mode == "slc":
            ok = selx_ref[j] > 0.5
        if mode == "win":
            ok = (k_pos <= q_pos) & (k_pos > q_pos - WINDOW)
        elif diagonal:
            ok = (k_pos <= q_pos) if ok is None else ok & (k_pos <= q_pos)
        out = []
        for r in range(heads):
            m, l, acc = carry[r]
            c0 = r * HEAD_DIM if mode == "fox" else 0
            kb = kb_ref[pl.ds(ks, tk), c0:c0 + HEAD_DIM]
            vb = vb_ref[pl.ds(ks, tk), c0:c0 + HEAD_DIM]
            s = lax.dot_general(qs[r], kb, _NT, preferred_element_type=F32)
            if mode == "fox":
                s = s + bias_ref[0, r:r + 1, pl.ds(ks, tk)]
            if ok is not None:
                s = jnp.where(ok, s, NEG_INF)
            m_new = jnp.maximum(m, jnp.max(s, axis=-1, keepdims=True))
            alpha = jnp.exp(m - m_new)
            p = jnp.exp(s - m_new)
            l = alpha * l + jnp.sum(p, axis=-1, keepdims=True)
            acc = alpha * acc + jnp.dot(p.astype(BF16), vb, preferred_element_type=F32)
            out.append((m_new, l, acc))
        return tuple(out)

    init = tuple((jnp.full((tq, 1), NEG_INF, F32), jnp.zeros((tq, 1), F32), jnp.zeros((tq, HEAD_DIM), F32))
                 for _ in range(heads))
    first_diag = (qi * tq) // tk
    n_diag = -(-tq // tk)
    if mode == "win":
        lo = jnp.maximum(qi * tq - WINDOW + 1, 0) // tk
        state = lax.fori_loop(lo, first_diag + n_diag, functools.partial(block, diagonal=False), init)
    else:
        state = lax.fori_loop(0, first_diag, functools.partial(block, diagonal=False), init)
        for dj in range(n_diag):
            state = block(first_diag + dj, state, True)

    for r in range(heads):
        m, l, acc = state[r]
        w = 1.0 / l
        if mode != "fox":
            gate = gate_ref[0]
            lane = lax.broadcasted_iota(jnp.int32, gate.shape, 1)
            want = FOX_HEADS + (unit * heads + r) * N_NSA_BRANCHES + gate_col
            w = w * jnp.sum(jnp.where(lane == want, gate, 0.0), axis=-1, keepdims=True)
        o_ref[0, :, r * HEAD_DIM:(r + 1) * HEAD_DIM] = (acc * w).astype(o_ref.dtype)


def _flash_attention(mode, q, k, v, k_col0, t, *, bias=None, sel=None, gate=None, gate_col=0):
    b = q.shape[0]
    heads = HEADS_PER_GROUP
    qw = heads * HEAD_DIM
    units = q.shape[2] // qw
    kvw = qw if mode == "fox" else HEAD_DIM
    tq = _row_tile(t, FLASH_QUERY_BLOCK)
    tk = _row_tile(t, FLASH_KEY_BLOCK)
    n_kv = t // tk
    in_specs = [
        pl.BlockSpec((1, tq, qw), lambda bi, u, i: (bi, i, u)),
        pl.BlockSpec((1, t, kvw), lambda bi, u, i: (bi, 0, k_col0 + u)),
        pl.BlockSpec((1, t, kvw), lambda bi, u, i: (bi, 0, k_col0 + u)),
    ]
    args = [q, k, v]
    scratch = [pltpu.VMEM((t, kvw), BF16), pltpu.VMEM((t, kvw), BF16)]
    if mode == "fox":
        in_specs.append(pl.BlockSpec((1, heads, t), lambda bi, u, i: (bi * units + u, 0, 0)))
        args.append(bias.reshape(b * units, heads, t))
    else:
        if mode == "slc":
            nsp = sel.shape[-1]
            expand = (np.arange(nsp)[:, None] == (np.arange(t)[None, :] // SEL_BLOCK)).astype(np.float32)
            in_specs.append(pl.BlockSpec((1, 1, tq, nsp), lambda bi, u, i: (bi, u, i, 0)))
            in_specs.append(pl.BlockSpec((nsp, t), lambda bi, u, i: (0, 0)))
            args += [sel, jnp.asarray(expand, BF16)]
            scratch.append(pltpu.VMEM((n_kv, tq, tk), F32))
        in_specs.append(pl.BlockSpec((1, tq, LANES), lambda bi, u, i: (bi, i, 0)))
        args.append(gate)
    return pl.pallas_call(
        functools.partial(_flash_body, mode, heads, tq, tk, n_kv, gate_col),
        grid=(b, units, t // tq),
        in_specs=in_specs,
        out_specs=pl.BlockSpec((1, tq, qw), lambda bi, u, i: (bi, i, u)),
        out_shape=jax.ShapeDtypeStruct((b, t, units * qw), BF16),
        scratch_shapes=scratch,
        compiler_params=_cparams("parallel", "parallel", "arbitrary"),
        name="attn_" + mode,
    )(*args)


PAGED_BUFFER_BYTES = 2 * 1024 * 1024
PAGED_COPY_BYTES = 128 * 1024
CMP_BUFFER_BYTES = 4 * 1024 * 1024


def _pages_per_step(n_pages, page_bytes, buffer_bytes=PAGED_BUFFER_BYTES):
    pg = max(1, min(n_pages, buffer_bytes // page_bytes))
    while n_pages % pg:
        pg -= 1
    return pg


def _paged_body(units, rows, rows_b, n_pages, pg, page, use_gate, *refs):
    if use_gate:
        (tbl_ref, q_ref, bias_ref, kn_ref, vn_ref, biasn_ref, gate_ref, kpool, vpool, o_ref,
         kbuf, vbuf, sem, m_ref, l_ref, acc_ref) = refs
    else:
        (tbl_ref, q_ref, bias_ref, kn_ref, vn_ref, biasn_ref, kpool, vpool, o_ref,
         kbuf, vbuf, sem, m_ref, l_ref, acc_ref) = refs
        gate_ref = None
    bi = pl.program_id(0)
    s = pl.program_id(1)
    n_steps = n_pages // pg
    rpp = page * units
    n_keys = pg * page

    n_chunks = max(1, (rpp * HEAD_DIM * 4) // PAGED_COPY_BYTES)
    crows = rpp // n_chunks

    def copies(row, step, slot):
        out = []
        for k in range(pg):
            phys = tbl_ref[row * n_pages + step * pg + k]
            for c in range(n_chunks):
                src = pl.ds(c * crows, crows)
                dst = pl.ds(k * rpp + c * crows, crows)
                out.append(pltpu.make_async_copy(kpool.at[phys, src], kbuf.at[slot, dst], sem.at[0, slot]))
                out.append(pltpu.make_async_copy(vpool.at[phys, src], vbuf.at[slot, dst], sem.at[1, slot]))
        return out

    @pl.when(s == 0)
    def _():
        m_ref[...] = jnp.full(m_ref.shape, NEG_INF, F32)
        l_ref[...] = jnp.zeros(l_ref.shape, F32)
        acc_ref[...] = jnp.zeros(acc_ref.shape, F32)

    gstep = bi * n_steps + s

    @pl.when((bi == 0) & (s == 0))
    def _():
        for c in copies(0, 0, 0):
            c.start()

    @pl.when(s + 1 < n_steps)
    def _():
        for c in copies(bi, s + 1, lax.rem(gstep + 1, 2)):
            c.start()

    @pl.when((s + 1 == n_steps) & (bi + 1 < pl.num_programs(0)))
    def _():
        for c in copies(bi + 1, 0, lax.rem(gstep + 1, 2)):
            c.start()

    def attend(k, v, bias, own_blocks):
        k_hi, k_lo = _split_bf16(k)
        v_hi, v_lo = _split_bf16(v)
        sc = _dot_split(q_ref[0], k_hi, k_lo, _NT)
        nk = k.shape[0] // units
        if own_blocks:
            sc = jnp.concatenate([sc[u * rows:(u + 1) * rows, u * nk:(u + 1) * nk] for u in range(units)], axis=0)
        sc = sc + bias
        m = m_ref[...]
        m_new = jnp.maximum(m, jnp.max(sc, axis=-1, keepdims=True))
        alpha = jnp.exp(m - m_new)
        pr = jnp.exp(sc - m_new)
        l_ref[...] = alpha * l_ref[...] + jnp.sum(pr, axis=-1, keepdims=True)
        if own_blocks:
            zero = jnp.zeros((rows, nk), F32)
            pr = jnp.concatenate(
                [jnp.concatenate([pr[u * rows:(u + 1) * rows] if u2 == u else zero for u2 in range(units)], axis=1)
                 for u in range(units)], axis=0)
        acc_ref[...] = alpha * acc_ref[...] + _dot_split(pr, v_hi, v_lo, _NN)
        m_ref[...] = m_new

    def unit_bias(u):
        bb = bias_ref[0, u]
        if rows_b == 1:
            return jnp.broadcast_to(bb, (rows, n_keys))
        return jnp.concatenate([bb] * (rows // rows_b), axis=0)

    @pl.when(s < n_steps)
    def _():
        slot = lax.rem(gstep, 2)
        for c in copies(bi, s, slot):
            c.wait()
        k = jnp.concatenate([kbuf[slot, pl.ds(u, n_keys, stride=units), :] for u in range(units)], axis=0)
        v = jnp.concatenate([vbuf[slot, pl.ds(u, n_keys, stride=units), :] for u in range(units)], axis=0)
        attend(k, v, jnp.concatenate([unit_bias(u) for u in range(units)], axis=0), True)

    @pl.when(s == n_steps)
    def _():
        k = jnp.concatenate([kn_ref[0, :, u * HEAD_DIM:(u + 1) * HEAD_DIM] for u in range(units)], axis=0)
        v = jnp.concatenate([vn_ref[0, :, u * HEAD_DIM:(u + 1) * HEAD_DIM] for u in range(units)], axis=0)
        attend(k, v, biasn_ref[0], False)
        w = 1.0 / l_ref[...]
        if use_gate:
            w = w * gate_ref[0]
        o_ref[0] = acc_ref[...] * w


def _paged_attention(q, k_pool, v_pool, table, bias, k_new, v_new, bias_new, gate=None):
    b, units, rows, _ = q.shape
    ur = units * rows
    n_pages = table.shape[1]
    rpp = k_pool.shape[1]
    page = rpp // units
    rows_b = bias.shape[2]
    n_new = k_new.shape[1]
    width = k_new.shape[2]
    pg = _pages_per_step(n_pages, rpp * HEAD_DIM * 4)
    n_steps = n_pages // pg
    last = n_steps - 1
    same_unit = jnp.arange(units)[:, None, None, None] == jnp.arange(units)[None, None, :, None]
    bias_new_full = jnp.where(same_unit[None], bias_new[:, :, :, None, :], NEG_INF).reshape(b, ur, units * n_new)
    in_specs = [
        pl.BlockSpec((1, ur, HEAD_DIM), lambda bi, p, tbl: (bi, 0, 0)),
        pl.BlockSpec((1, units, rows_b, pg * page), lambda bi, p, tbl: (bi, 0, 0, jnp.minimum(p, last))),
        pl.BlockSpec((1, n_new, width), lambda bi, p, tbl: (bi, 0, 0)),
        pl.BlockSpec((1, n_new, width), lambda bi, p, tbl: (bi, 0, 0)),
        pl.BlockSpec((1, ur, units * n_new), lambda bi, p, tbl: (bi, 0, 0)),
    ]
    args = [q.reshape(b, ur, HEAD_DIM), bias, k_new, v_new, bias_new_full]
    if gate is not None:
        in_specs.append(pl.BlockSpec((1, ur, 1), lambda bi, p, tbl: (bi, 0, 0)))
        args.append(gate.reshape(b, ur, 1))
    in_specs += [pl.BlockSpec(memory_space=pl.ANY), pl.BlockSpec(memory_space=pl.ANY)]
    args += [k_pool, v_pool]
    grid_spec = pltpu.PrefetchScalarGridSpec(
        num_scalar_prefetch=1,
        grid=(b, n_steps + 1),
        in_specs=in_specs,
        out_specs=pl.BlockSpec((1, ur, HEAD_DIM), lambda bi, p, tbl: (bi, 0, 0)),
        scratch_shapes=[pltpu.VMEM((2, pg * rpp, HEAD_DIM), F32), pltpu.VMEM((2, pg * rpp, HEAD_DIM), F32),
                        pltpu.SemaphoreType.DMA((2, 2)),
                        pltpu.VMEM((ur, 1), F32), pltpu.VMEM((ur, 1), F32), pltpu.VMEM((ur, HEAD_DIM), F32)],
    )
    out = pl.pallas_call(
        functools.partial(_paged_body, units, rows, rows_b, n_pages, pg, page, gate is not None),
        grid_spec=grid_spec,
        out_shape=jax.ShapeDtypeStruct((b, ur, HEAD_DIM), F32),
        compiler_params=_cparams("arbitrary", "arbitrary"),
        name="attn_paged",
    )(table.reshape(-1), *args)
    return out.reshape(b, units, rows, HEAD_DIM)


def _cmp_proj_body(x_ref, pa_ref, pb_ref, wa_ref, wb_ref, za_ref, zb_ref):
    x = x_ref[0]
    za_ref[0] = jnp.dot((x + pa_ref[...]).astype(BF16), wa_ref[...], preferred_element_type=F32)
    zb_ref[0] = jnp.dot((x + pb_ref[...]).astype(BF16), wb_ref[...], preferred_element_type=F32)


def _cmp_mlp_body(n_chunks, za_ref, zb_ref, w2_ref, o_ref):
    zb_next = pltpu.roll(zb_ref[0], n_chunks - 1, 0)
    hid = za_ref[0] + zb_next
    c = 0.7978845608028654
    hid = 0.5 * hid * (1.0 + jnp.tanh(c * (hid + 0.044715 * (hid * hid * hid))))
    w2 = w2_ref[...]
    hb = hid.astype(w2.dtype)
    for g in range(NSA_KV_GROUPS):
        sl = slice(g * HEAD_DIM, (g + 1) * HEAD_DIM)
        o_ref[0, :, sl] = jnp.dot(hb[:, sl], w2, preferred_element_type=F32, precision=_dot_precision(w2.dtype))


def _compress(rows, pos_emb, w1, w2):
    b, tc, gw = rows.shape
    g = NSA_KV_GROUPS
    n_chunks = tc // CMP_STRIDE
    cw = CMP_STRIDE * gw
    x = rows.reshape(b, n_chunks, cw)
    w1r = w1.reshape(2, CMP_STRIDE, HEAD_DIM, HEAD_DIM)
    eye = jnp.eye(g, dtype=w1.dtype)
    wcat = jnp.einsum("hldj,ge->hlgdej", w1r, eye).reshape(2, cw, g * HEAD_DIM).astype(BF16)
    pos = jnp.broadcast_to(pos_emb.reshape(2, CMP_STRIDE, 1, HEAD_DIM), (2, CMP_STRIDE, g, HEAD_DIM)).reshape(2, 1, cw)
    cm = _row_tile(n_chunks, 128)
    za, zb = pl.pallas_call(
        _cmp_proj_body,
        grid=(b, n_chunks // cm),
        in_specs=[
            pl.BlockSpec((1, cm, cw), lambda bi, i: (bi, i, 0)),
            pl.BlockSpec((1, cw), lambda bi, i: (0, 0)),
            pl.BlockSpec((1, cw), lambda bi, i: (0, 0)),
            pl.BlockSpec((cw, g * HEAD_DIM), lambda bi, i: (0, 0)),
            pl.BlockSpec((cw, g * HEAD_DIM), lambda bi, i: (0, 0)),
        ],
        out_specs=[pl.BlockSpec((1, cm, g * HEAD_DIM), lambda bi, i: (bi, i, 0))] * 2,
        out_shape=[jax.ShapeDtypeStruct((b, n_chunks, g * HEAD_DIM), F32)] * 2,
        compiler_params=_cparams("parallel", "parallel"),
        name="cmp_proj",
    )(x, pos[0], pos[1], wcat[0], wcat[1])
    return _cmp_mlp(za, zb, w2.astype(BF16))


def _cmp_mlp(za, zb, w2):
    b, n_chunks, gw = za.shape
    g = NSA_KV_GROUPS
    return pl.pallas_call(
        functools.partial(_cmp_mlp_body, n_chunks),
        grid=(b,),
        in_specs=[
            pl.BlockSpec((1, n_chunks, g * HEAD_DIM), lambda bi: (bi, 0, 0)),
            pl.BlockSpec((1, n_chunks, g * HEAD_DIM), lambda bi: (bi, 0, 0)),
            pl.BlockSpec((HEAD_DIM, HEAD_DIM), lambda bi: (0, 0)),
        ],
        out_specs=pl.BlockSpec((1, n_chunks, g * HEAD_DIM), lambda bi: (bi, 0, 0)),
        out_shape=jax.ShapeDtypeStruct((b, n_chunks, g * HEAD_DIM), F32),
        compiler_params=_cparams("parallel"),
        name="cmp_mlp",
    )(za, zb, w2)


def _split_bf16(x):
    hi = x.astype(BF16)
    return hi, (x - hi.astype(F32)).astype(BF16)


def _dot_split(a, b_hi, b_lo, dims):
    a_hi, a_lo = _split_bf16(a)
    n = a.shape[0]
    both = lax.dot_general(jnp.concatenate([a_hi, a_lo], axis=0), b_hi, dims, preferred_element_type=F32)
    return both[:n] + both[n:] + lax.dot_general(a_hi, b_lo, dims, preferred_element_type=F32)


_NT = (((1,), (1,)), ((), ()))
_NN = (((1,), (0,)), ((), ()))


def _cmp_paged_body(n_pages, pg, page, tbl_ref, pos_ref, w_ref, pool, za_ref, zb_ref, xbuf, whi_ref, wlo_ref, sem):
    bi = pl.program_id(0)
    s = pl.program_id(1)
    n_steps = n_pages // pg
    g_n = NSA_KV_GROUPS
    rpp = page * g_n
    n_rows = pg * page // CMP_STRIDE

    def copies(row, step, slot):
        return [pltpu.make_async_copy(pool.at[tbl_ref[row * n_pages + step * pg + k]],
                                      xbuf.at[slot, pl.ds(k * rpp, rpp)], sem.at[slot]) for k in range(pg)]

    gstep = bi * n_steps + s

    @pl.when((bi == 0) & (s == 0))
    def _():
        for c in copies(0, 0, 0):
            c.start()

    @pl.when(s == 0)
    def _():
        w_hi, w_lo = _split_bf16(w_ref[...])
        whi_ref[...] = w_hi
        wlo_ref[...] = w_lo

    @pl.when(s + 1 < n_steps)
    def _():
        for c in copies(bi, s + 1, lax.rem(gstep + 1, 2)):
            c.start()

    @pl.when((s + 1 == n_steps) & (bi + 1 < pl.num_programs(0)))
    def _():
        for c in copies(bi + 1, 0, lax.rem(gstep + 1, 2)):
            c.start()

    slot = lax.rem(gstep, 2)
    for c in copies(bi, s, slot):
        c.wait()
    w_hi = whi_ref[...]
    w_lo = wlo_ref[...]
    pw = _dot_split(pos_ref[...], w_hi, w_lo, _NN)
    col = lax.broadcasted_iota(jnp.int32, (1, 2 * HEAD_DIM), 1)
    pos_term = jnp.where(col < HEAD_DIM, pw[0:1, :], pw[1:2, :])
    x_cat = jnp.concatenate(
        [jnp.concatenate([xbuf[slot, pl.ds(l * g_n + g, n_rows, stride=CMP_STRIDE * g_n), :]
                          for l in range(CMP_STRIDE)], axis=1) for g in range(g_n)], axis=0)
    z = _dot_split(x_cat, w_hi, w_lo, _NN) + pos_term
    for g in range(g_n):
        za_ref[0, :, g * HEAD_DIM:(g + 1) * HEAD_DIM] = z[g * n_rows:(g + 1) * n_rows, :HEAD_DIM]
        zb_ref[0, :, g * HEAD_DIM:(g + 1) * HEAD_DIM] = z[g * n_rows:(g + 1) * n_rows, HEAD_DIM:]


def _compress_paged(pool, table, pos_emb, w1, w2):
    b, n_pages = table.shape
    rpp = pool.shape[1]
    page = rpp // NSA_KV_GROUPS
    gw = NSA_KV_GROUPS * HEAD_DIM
    pg = _pages_per_step(n_pages, rpp * HEAD_DIM * 4, CMP_BUFFER_BYTES)
    n_steps = n_pages // pg
    n_rows = pg * page // CMP_STRIDE
    n_chunks = n_pages * page // CMP_STRIDE
    half = CMP_STRIDE * HEAD_DIM
    w_cat = jnp.concatenate([w1[:half], w1[half:]], axis=1)
    pos_rows = jnp.pad(pos_emb.reshape(2, half), ((0, SUBLANES - 2), (0, 0)))
    grid_spec = pltpu.PrefetchScalarGridSpec(
        num_scalar_prefetch=1,
        grid=(b, n_steps),
        in_specs=[pl.BlockSpec((SUBLANES, half), lambda bi, i, tbl: (0, 0)),
                  pl.BlockSpec((half, 2 * HEAD_DIM), lambda bi, i, tbl: (0, 0)),
                  pl.BlockSpec(memory_space=pl.ANY)],
        out_specs=[pl.BlockSpec((1, n_rows, gw), lambda bi, i, tbl: (bi, i, 0))] * 2,
        scratch_shapes=[pltpu.VMEM((2, pg * rpp, HEAD_DIM), F32), pltpu.VMEM((half, 2 * HEAD_DIM), BF16),
                        pltpu.VMEM((half, 2 * HEAD_DIM), BF16), pltpu.SemaphoreType.DMA((2,))],
    )
    za, zb = pl.pallas_call(
        functools.partial(_cmp_paged_body, n_pages, pg, page),
        grid_spec=grid_spec,
        out_shape=[jax.ShapeDtypeStruct((b, n_chunks, gw), F32)] * 2,
        compiler_params=_cparams("arbitrary", "arbitrary"),
        name="cmp_proj_paged",
    )(table.reshape(-1), pos_rows, w_cat, pool)
    return _cmp_mlp(za, zb, w2)


def _cmp_attn_body(tq, nc, ns, n_pick, q_start, q_ref, kc_ref, vc_ref, cover_ref, gate_ref, o_ref, sel_ref):
    unit = pl.program_id(1)
    qi = pl.program_id(2)
    precise = q_ref.dtype == F32
    if precise:
        kc_hi, kc_lo = _split_bf16(kc_ref[0])
        vc_hi, vc_lo = _split_bf16(vc_ref[0])
    else:
        kc_hi = kc_ref[0].astype(BF16)
        vc_hi = vc_ref[0].astype(BF16)
    ncp = kc_hi.shape[0]
    q_pos = q_start + qi * tq + lax.broadcasted_iota(jnp.int32, (tq, ncp), 0)
    n_idx = lax.broadcasted_iota(jnp.int32, (tq, ncp), 1)
    visible = (n_idx * CMP_STRIDE + (CMP_BLOCK - 1) <= q_pos) & (n_idx < nc)
    gate = gate_ref[0]
    glane = lax.broadcasted_iota(jnp.int32, gate.shape, 1)
    p_grp = jnp.zeros((tq, ncp), F32)
    for r in range(HEADS_PER_GROUP):
        q = q_ref[0, :, r * HEAD_DIM:(r + 1) * HEAD_DIM]
        if precise:
            s = _dot_split(q, kc_hi, kc_lo, _NT)
        else:
            s = lax.dot_general(q, kc_hi, _NT, preferred_element_type=F32)
        s = jnp.where(visible, s, NEG_INF)
        m = jnp.max(s, axis=-1, keepdims=True)
        e = jnp.where(visible, jnp.exp(s - m), 0.0)
        l = jnp.sum(e, axis=-1, keepdims=True)
        p = e / jnp.where(l > 0.0, l, 1.0)
        p_grp = p_grp + p
        want = FOX_HEADS + (unit * HEADS_PER_GROUP + r) * N_NSA_BRANCHES
        gcol = jnp.sum(jnp.where(glane == want, gate, 0.0), axis=-1, keepdims=True)
        if precise:
            o = _dot_split(p, vc_hi, vc_lo, _NN)
        else:
            o = jnp.dot(p.astype(BF16), vc_hi, preferred_element_type=F32)
        o_ref[0, :, r * HEAD_DIM:(r + 1) * HEAD_DIM] = (o * gcol).astype(o_ref.dtype)

    imp = jnp.dot(p_grp, cover_ref[...], preferred_element_type=F32, precision=HIGHEST)
    nsp = imp.shape[1]
    j = lax.broadcasted_iota(jnp.int32, (tq, nsp), 1)
    cur = (q_start + qi * tq + lax.broadcasted_iota(jnp.int32, (tq, nsp), 0)) // SEL_BLOCK
    dist = cur - j
    forced = (j == 0) | ((dist >= 0) & (dist < N_LOCAL_SEL))
    score = jnp.where(forced, FORCE_SCORE, imp)
    score = jnp.where(dist >= 0, score, NEG_INF)
    score = jnp.where(j < ns, score, PAD_SCORE)
    sel = jnp.zeros((tq, nsp), F32)
    j_f = j.astype(F32)
    for _ in range(n_pick):
        mx = jnp.max(score, axis=-1, keepdims=True)
        first = jnp.min(jnp.where(score == mx, j_f, float(nsp)), axis=-1, keepdims=True)
        hit = j_f == first
        sel = jnp.where(hit, 1.0, sel)
        score = jnp.where(hit, TAKEN_SCORE, score)
    sel_ref[0, 0] = sel


def _cmp_attention(nq, kc, vc, gate, t, q_start, n_keys, out_dtype):
    b = nq.shape[0]
    ncp = kc.shape[1]
    nc = (n_keys - CMP_BLOCK) // CMP_STRIDE + 1
    ns = -(-n_keys // SEL_BLOCK)
    nsp = -(-ns // LANES) * LANES
    n_pick = min(N_SELECT, ns)
    c0 = np.arange(ncp)[:, None] * CMP_STRIDE
    s0 = np.arange(nsp)[None, :] * SEL_BLOCK
    cover = np.clip(np.minimum(c0 + CMP_BLOCK, s0 + SEL_BLOCK) - np.maximum(c0, s0), 0, None).astype(np.float32) / CMP_BLOCK
    cover = cover * (np.arange(ncp)[:, None] < nc) * (np.arange(nsp)[None, :] < ns)
    tq = _row_tile(t, 512)
    qw = HEADS_PER_GROUP * HEAD_DIM
    return pl.pallas_call(
        functools.partial(_cmp_attn_body, tq, nc, ns, n_pick, q_start),
        grid=(b, NSA_KV_GROUPS, t // tq),
        in_specs=[
            pl.BlockSpec((1, tq, qw), lambda bi, u, i: (bi, i, u)),
            pl.BlockSpec((1, ncp, HEAD_DIM), lambda bi, u, i: (bi, 0, u)),
            pl.BlockSpec((1, ncp, HEAD_DIM), lambda bi, u, i: (bi, 0, u)),
            pl.BlockSpec((ncp, nsp), lambda bi, u, i: (0, 0)),
            pl.BlockSpec((1, tq, LANES), lambda bi, u, i: (bi, i, 0)),
        ],
        out_specs=[
            pl.BlockSpec((1, tq, qw), lambda bi, u, i: (bi, i, u)),
            pl.BlockSpec((1, 1, tq, nsp), lambda bi, u, i: (bi, u, i, 0)),
        ],
        out_shape=[
            jax.ShapeDtypeStruct((b, t, NSA_KV_GROUPS * qw), out_dtype),
            jax.ShapeDtypeStruct((b, NSA_KV_GROUPS, t, nsp), F32),
        ],
        compiler_params=_cparams("parallel", "parallel", "parallel"),
        name="attn_cmp_select",
    )(nq, kc, vc, jnp.asarray(cover), gate)


def _merge_body(a_ref, b1_ref, b2_ref, b3_ref, ua_ref, ub_ref, ga_ref, gb_ref, o_ref):
    a = a_ref[...]
    prec = _dot_precision(a.dtype)
    bsum = (b1_ref[...].astype(F32) + b2_ref[...].astype(F32) + b3_ref[...].astype(F32)).astype(a.dtype)
    ya = jnp.dot(a, ua_ref[...], preferred_element_type=F32, precision=prec)
    yb = jnp.dot(bsum, ub_ref[...], preferred_element_type=F32, precision=prec)
    o_ref[...] = (ga_ref[...] * ya + gb_ref[...] * yb).astype(o_ref.dtype)


def _merge(o_fox, o_cmp, o_slc, o_win, u_fox, u_nsa, mg, d_model):
    n, w = o_fox.shape
    tm = _row_tile(n, MERGE_ROW_TILE)
    tn = 512
    nj = d_model // tn
    row = pl.BlockSpec((tm, w), lambda i, j: (i, 0))
    return pl.pallas_call(
        _merge_body,
        grid=(n // tm, nj),
        in_specs=[row, row, row, row,
                  pl.BlockSpec((w, tn), lambda i, j: (0, j)),
                  pl.BlockSpec((w, tn), lambda i, j: (0, j)),
                  pl.BlockSpec((tm, tn), lambda i, j: (i, j)),
                  pl.BlockSpec((tm, tn), lambda i, j: (i, j + nj))],
        out_specs=pl.BlockSpec((tm, tn), lambda i, j: (i, j)),
        out_shape=jax.ShapeDtypeStruct((n, d_model), o_fox.dtype),
        compiler_params=_cparams("parallel", "parallel"),
        name="mixer_merge",
    )(o_fox, o_cmp, o_slc, o_win, u_fox, u_nsa, mg, mg)


def _router_body(h_ref, g_ref, w_ref, b_ref, u_ref, route_ref):
    x = h_ref[...]
    xn = x * lax.rsqrt(jnp.mean(x * x, axis=-1, keepdims=True) + NORM_EPS) * g_ref[...]
    u_ref[...] = xn.astype(u_ref.dtype)
    z = jnp.dot(xn, w_ref[...], preferred_element_type=F32, precision=HIGHEST) + b_ref[...]
    c = lax.broadcasted_iota(jnp.int32, z.shape, 1)
    big = z.shape[1]
    is_g = c < N_GROUPS
    gmax = jnp.max(jnp.where(is_g, z, -jnp.inf), axis=-1, keepdims=True)
    gsel = jnp.min(jnp.where(is_g & (z == gmax), c, big), axis=-1, keepdims=True)
    gsum = jnp.sum(jnp.where(is_g, jnp.exp(z - gmax), 0.0), axis=-1, keepdims=True)
    lo = N_GROUPS + gsel * EXPERTS_PER_GROUP
    in_grp = (c >= lo) & (c < lo + EXPERTS_PER_GROUP)
    z1 = jnp.max(jnp.where(in_grp, z, -jnp.inf), axis=-1, keepdims=True)
    c1 = jnp.min(jnp.where(in_grp & (z == z1), c, big), axis=-1, keepdims=True)
    rest = in_grp & (c != c1)
    z2 = jnp.max(jnp.where(rest, z, -jnp.inf), axis=-1, keepdims=True)
    c2 = jnp.min(jnp.where(rest & (z == z2), c, big), axis=-1, keepdims=True)
    e2 = jnp.exp(z2 - z1)
    p1 = 1.0 / (1.0 + e2)
    p2 = e2 / (1.0 + e2)
    gp = 1.0 / gsum
    out = jnp.where(c == 0, (c1 - N_GROUPS).astype(F32), 0.0)
    out = jnp.where(c == 1, (c2 - N_GROUPS).astype(F32), out)
    out = jnp.where(c == 2, gp * p1, out)
    out = jnp.where(c == 3, gp * p2, out)
    route_ref[...] = out


def _router(h, gain, w_cat, b_cat):
    n, d = h.shape
    tm = _row_tile(n, 512)
    return pl.pallas_call(
        _router_body,
        grid=(n // tm,),
        in_specs=[pl.BlockSpec((tm, d), lambda i: (i, 0)), pl.BlockSpec((1, d), lambda i: (0, 0)),
                  pl.BlockSpec((d, LANES), lambda i: (0, 0)), pl.BlockSpec((1, LANES), lambda i: (0, 0))],
        out_specs=[pl.BlockSpec((tm, d), lambda i: (i, 0)), pl.BlockSpec((tm, LANES), lambda i: (i, 0))],
        out_shape=[jax.ShapeDtypeStruct((n, d), F32), jax.ShapeDtypeStruct((n, LANES), F32)],
        compiler_params=_cparams("parallel"),
        name="moe_router",
    )(h, gain.reshape(1, d), w_cat, b_cat)


def _gather_rows(src_hbm, dst_ref, sem, idx_ref, base, count, start):
    if not start:
        pltpu.make_async_copy(src_hbm.at[pl.ds(0, count)], dst_ref, sem).wait()
        return

    def issue(r, priority):
        pltpu.make_async_copy(src_hbm.at[pl.ds(idx_ref[base + r], 1)], dst_ref.at[pl.ds(r, 1)],
                              sem).start(priority=priority)

    def body(r2, c):
        issue(2 * r2, 0)
        issue(2 * r2 + 1, 1)
        return c

    assert count % 2 == 0
    lax.fori_loop(0, count // 2, body, 0, unroll=4)


def _expert_body(tm, te_ref, nv_ref, tok_ref, wg_ref, wu_ref, wd_ref, x_hbm, o_ref, xbuf, sem):
    i = pl.program_id(0)
    nv = nv_ref[0]

    def rows_of(tile, start):
        slot = lax.rem(tile, 2)
        _gather_rows(x_hbm, xbuf.at[slot], sem.at[slot], tok_ref, tile * tm, tm, start)

    @pl.when(i == 0)
    def _():
        rows_of(0, True)

    @pl.when(i + 1 < nv)
    def _():
        rows_of(i + 1, True)

    @pl.when(i < nv)
    def _():
        rows_of(i, False)
        x = xbuf[lax.rem(i, 2)].astype(BF16)
        hg = jnp.dot(x, wg_ref[0].astype(BF16), preferred_element_type=F32)
        hu = jnp.dot(x, wu_ref[0].astype(BF16), preferred_element_type=F32)
        hid = (hg * jax.nn.sigmoid(hg)) * hu
        o_ref[...] = jnp.dot(hid.astype(BF16), wd_ref[0].astype(BF16), preferred_element_type=F32)

    @pl.when(i >= nv)
    def _():
        o_ref[...] = jnp.zeros(o_ref.shape, o_ref.dtype)


def _experts(x, row_token, tile_expert, n_valid, wg, wu, wd, tm):
    rows = row_token.shape[0]
    d = x.shape[1]
    n_tiles = rows // tm
    f = wg.shape[2]

    def w_map(i, te, nv, tok):
        return (te[i], 0, 0)

    grid_spec = pltpu.PrefetchScalarGridSpec(
        num_scalar_prefetch=3,
        grid=(n_tiles,),
        in_specs=[pl.BlockSpec((1, d, f), w_map), pl.BlockSpec((1, d, f), w_map), pl.BlockSpec((1, f, d), w_map),
                  pl.BlockSpec(memory_space=pl.ANY)],
        out_specs=pl.BlockSpec((tm, d), lambda i, te, nv, tok: (i, 0)),
        scratch_shapes=[pltpu.VMEM((2, tm, d), F32), pltpu.SemaphoreType.DMA((2,))],
    )
    return pl.pallas_call(
        functools.partial(_expert_body, tm),
        grid_spec=grid_spec,
        out_shape=jax.ShapeDtypeStruct((rows, d), F32),
        compiler_params=_cparams("arbitrary"),
        name="moe_experts",
    )(tile_expert, n_valid, row_token, wg, wu, wd, x)


def _combine_body(tm, n, slot_ref, h_ref, route_ref, src_hbm, o_ref, buf, sem):
    i = pl.program_id(0)
    n_tiles = pl.num_programs(0)

    def rows_of(tile, start):
        slot = lax.rem(tile, 2)
        for k in range(2):
            _gather_rows(src_hbm, buf.at[slot, k], sem.at[slot], slot_ref, k * n + tile * tm, tm, start)

    @pl.when(i == 0)
    def _():
        rows_of(0, True)

    @pl.when(i + 1 < n_tiles)
    def _():
        rows_of(i + 1, True)

    rows_of(i, False)
    slot = lax.rem(i, 2)
    route = route_ref[...]
    o_ref[...] = h_ref[...] + route[:, 2:3] * buf[slot, 0] + route[:, 3:4] * buf[slot, 1]


def _moe_combine(h, route, out_sorted, slot_kn):
    n, d = h.shape
    tm = _row_tile(n, 256)
    grid_spec = pltpu.PrefetchScalarGridSpec(
        num_scalar_prefetch=1,
        grid=(n // tm,),
        in_specs=[pl.BlockSpec((tm, d), lambda i, sl: (i, 0)), pl.BlockSpec((tm, LANES), lambda i, sl: (i, 0)),
                  pl.BlockSpec(memory_space=pl.ANY)],
        out_specs=pl.BlockSpec((tm, d), lambda i, sl: (i, 0)),
        scratch_shapes=[pltpu.VMEM((2, 2, tm, d), F32), pltpu.SemaphoreType.DMA((2,))],
    )
    return pl.pallas_call(
        functools.partial(_combine_body, tm, n),
        grid_spec=grid_spec,
        out_shape=jax.ShapeDtypeStruct((n, d), F32),
        compiler_params=_cparams("arbitrary"),
        name="moe_combine",
    )(slot_kn, h, route, out_sorted)


def _moe(h, gain, w_router, b_router, wg, wu, wd):
    n, d = h.shape
    ut, route = _router(h, gain, w_router, b_router)
    expert = route[:, :2].astype(jnp.int32).reshape(-1)
    n_asg = 2 * n
    tm = 256 if n_asg >= 8192 else 16
    n_tiles = n_asg // tm + N_EXPERTS
    onehot = (expert[:, None] == jnp.arange(N_EXPERTS, dtype=jnp.int32)[None, :]).astype(jnp.int32)
    running = jnp.cumsum(onehot, axis=0)
    counts = running[-1]
    rank = jnp.sum(onehot * running, axis=1) - 1
    tiles_per = (counts + tm - 1) // tm
    tile_end = jnp.cumsum(tiles_per)
    start_row = (tile_end - tiles_per) * tm
    slot = (jnp.sum(onehot * start_row[None, :], axis=1) + rank).astype(jnp.int32)
    row_token = jnp.zeros((n_tiles * tm,), jnp.int32).at[slot].set(jnp.arange(n_asg, dtype=jnp.int32) // 2)
    n_valid = tile_end[-1].astype(jnp.int32)
    tile_ids = jnp.minimum(jnp.arange(n_tiles, dtype=jnp.int32), n_valid - 1)
    tile_expert = jnp.sum((tile_end[None, :] <= tile_ids[:, None]).astype(jnp.int32), axis=1)
    out_sorted = _experts(ut, row_token, tile_expert, n_valid.reshape(1), wg, wu, wd, tm)
    return _moe_combine(h, route, out_sorted, jnp.transpose(slot.reshape(n, 2)).reshape(-1))


def _rope_tables(positions):
    half = HEAD_DIM // 2
    inv = ROPE_THETA ** (-np.arange(half, dtype=np.float64) / half)
    ang = np.asarray(positions, np.float64)[:, None] * inv[None, :]
    cos = np.concatenate([np.cos(ang), np.cos(ang)], axis=-1)
    sin = np.concatenate([-np.sin(ang), np.sin(ang)], axis=-1)
    return jnp.asarray(cos, F32), jnp.asarray(sin, F32)


PROJ_ROW_TILE = 2048
MERGE_ROW_TILE = 1024


def _project(u, wts, p, b, t, q_start):
    n = u.shape[0]
    tm = _row_tile(n, PROJ_ROW_TILE)
    tn = 512
    tn_kv = 3 * HEAD_DIM
    scale = HEAD_DIM ** -0.5
    gain_spec = pl.BlockSpec((1, HEAD_DIM), lambda i, j: (0, 0))

    def run(w, epilogue, extras, specs, dtype, name, tn_=tn):
        shape = jax.ShapeDtypeStruct((n, w.shape[1]), dtype)
        spec = pl.BlockSpec((tm, tn_), lambda i, j: (i, j))
        return _matmul(u, w, epilogue, extras, specs, shape, spec, tm, tn_, name)

    cos, sin = _rope_tables(q_start + (np.arange(max(tm, t)) % t))
    n_tab = cos.shape[0] // tm
    tab_spec = pl.BlockSpec((tm, HEAD_DIM), lambda i, j: (i % n_tab, 0))

    fq = run(wts["fq"], functools.partial(_ep_head_norm, scale), [p["fox_q_norm"]], [gain_spec], u.dtype, "proj_fq")
    fk = run(wts["fk"], functools.partial(_ep_head_norm, 1.0), [p["fox_k_norm"]], [gain_spec], F32, "proj_fk")
    fv = run(wts["fv"], _ep_identity, [], [], F32, "proj_fv")
    nq = run(wts["nq"], functools.partial(_ep_head_norm_rope, scale), [p["nsa_q_norm"], cos, sin],
             [gain_spec, tab_spec, tab_spec], u.dtype, "proj_nq")
    nk = run(wts["nk"], functools.partial(_ep_head_norm_rope, 1.0), [p["nsa_k_norm"], cos, sin],
             [gain_spec, tab_spec, tab_spec], F32, "proj_nk", tn_kv)
    nv = run(wts["nv"], _ep_identity, [], [], F32, "proj_nv", tn_kv)
    mg = run(wts["mg"], _ep_sigmoid, [], [], F32, "proj_mg")
    small = run(wts["small"], _ep_small, [p["small_bias"]], [pl.BlockSpec((1, LANES), lambda i, j: (0, 0))],
                F32, "proj_small", LANES)
    return fq, fk, fv, nq, nk, nv, mg, small


def _layer(x, p, wts, q_start, past):
    b, t, d = x.shape
    n = b * t
    g = NSA_KV_GROUPS
    kvw = g * HEAD_DIM
    x2d = x.reshape(n, d)
    dense = wts["f32"] if past is not None else wts
    u = _rmsnorm(x2d, p["norm_mix"], dense["fq"].dtype)
    fq, fk, fv, nq, nk, nv, mg, small = _project(u, dense, p, b, t, q_start)

    logf = small[:, :FOX_HEADS].reshape(b, t, FOX_HEADS)
    gate3 = small.reshape(b, t, LANES)
    fq3 = fq.reshape(b, t, -1)
    nq3 = nq.reshape(b, t, -1)
    fk3 = fk.reshape(b, t, -1)
    fv3 = fv.reshape(b, t, -1)
    nk3 = nk.reshape(b, t, 3 * kvw)
    nv3 = nv.reshape(b, t, 3 * kvw)
    ck, sk, wk = (nk3[:, :, i * kvw:(i + 1) * kvw] for i in range(3))
    cv, sv, wv = (nv3[:, :, i * kvw:(i + 1) * kvw] for i in range(3))

    if past is None:
        n_keys = t
        n_lp = t // LANES
        lf_pool = jnp.transpose(logf.reshape(b * n_lp, LANES, FOX_HEADS), (0, 2, 1))
        bias = _fox_key_bias(lf_pool, jnp.arange(b * n_lp, dtype=jnp.int32).reshape(b, n_lp), None)
        o_fox = _flash_attention("fox", fq3, fk3, fv3, 0, t, bias=bias)
        kc = _compress(ck, p["cmp_pos_k"], p["cmp_w1_k"], p["cmp_w2_k"])
        vc = _compress(cv, p["cmp_pos_v"], p["cmp_w1_v"], p["cmp_w2_v"])
        o_cmp, sel = _cmp_attention(nq3, kc, vc, gate3, t, q_start, n_keys, BF16)
        o_slc = _flash_attention("slc", nq3, nk3, nv3, g, t, sel=sel, gate=gate3, gate_col=1)
        o_win = _flash_attention("win", nq3, nk3, nv3, 2 * g, t, gate=gate3, gate_col=2)
        keep = min(WINDOW, t)
        win_k, win_v = wk[:, t - keep:], wv[:, t - keep:]
    else:
        table = past["table"]
        n_pages = table.shape[1]
        page = past["page"]
        past_len = n_pages * page
        n_keys = past_len + t
        new_page = jnp.pad(jnp.transpose(logf, (0, 2, 1)), ((0, 0), (0, 0), (0, page - t)))
        bias = _fox_key_bias(past["fox_logf"], table, new_page)[:, :, :n_keys]
        causal = jnp.where(jnp.arange(t)[None, :] <= jnp.arange(t)[:, None], 0.0, NEG_INF).astype(F32)
        bias_new = bias[:, :, None, past_len:] + causal[None, None]
        qf = jnp.transpose(fq3.reshape(b, t, FOX_HEADS, HEAD_DIM), (0, 2, 1, 3))
        o = _paged_attention(qf, past["fox_k"], past["fox_v"], table, bias[:, :, None, :past_len],
                             fk3, fv3, bias_new)
        o_fox = jnp.transpose(o, (0, 2, 1, 3)).reshape(b, t, -1)
        nc = (n_keys - CMP_BLOCK) // CMP_STRIDE + 1
        assert (nc + 1) * CMP_STRIDE <= past_len and page % CMP_STRIDE == 0
        kc = _compress_paged(past["cmp_k"], table, p["cmp_pos_k"], p["cmp_w1_k"], p["cmp_w2_k"])
        vc = _compress_paged(past["cmp_v"], table, p["cmp_pos_v"], p["cmp_w1_v"], p["cmp_w2_v"])
        o_cmp, sel = _cmp_attention(nq3, kc, vc, gate3, t, q_start, n_keys, F32)
        rows = HEADS_PER_GROUP * t
        qn = jnp.transpose(nq3.reshape(b, t, g, HEADS_PER_GROUP, HEAD_DIM), (0, 2, 3, 1, 4)).reshape(b, g, rows, HEAD_DIM)
        gates = small[:, FOX_HEADS:FOX_HEADS + NSA_HEADS * N_NSA_BRANCHES].reshape(b, t, g, HEADS_PER_GROUP, N_NSA_BRANCHES)
        gates = jnp.transpose(gates, (4, 0, 2, 3, 1)).reshape(N_NSA_BRANCHES, b, g, rows, 1)
        ns = -(-n_keys // SEL_BLOCK)
        key_sel = jnp.repeat(sel[..., :ns], SEL_BLOCK, axis=-1)[..., :n_keys]
        sel_bias = jnp.where(key_sel > 0.5, 0.0, NEG_INF).astype(F32)
        sel_new = jnp.tile(sel_bias[..., past_len:] + causal[None, None], (1, 1, HEADS_PER_GROUP, 1))
        o = _paged_attention(qn, past["slc_k"], past["slc_v"], table, sel_bias[..., :past_len],
                             sk, sv, sel_new, gate=gates[1])
        o_slc = _rows_to_tokens(o, b, t)
        win_len = past["win_k"].shape[1]
        assert win_len % page == 0
        k_start = q_start - win_len
        q_abs = q_start + jnp.arange(t)[:, None]
        k_abs = k_start + jnp.arange(win_len)[None, :]
        band = jnp.where((k_abs >= 0) & (k_abs > q_abs - WINDOW), 0.0, NEG_INF).astype(F32)
        band = jnp.broadcast_to(band[None, None], (b, g, t, win_len))
        band_new = jnp.where(jnp.arange(t)[None, :] > jnp.arange(t)[:, None] - WINDOW, causal, NEG_INF)
        band_new = jnp.broadcast_to(jnp.tile(band_new, (HEADS_PER_GROUP, 1))[None, None], (b, g, rows, t))
        o = _paged_attention(qn, past["win_k_pool"], past["win_v_pool"], past["win_table"], band, wk, wv,
                             band_new, gate=gates[2])
        o_win = _rows_to_tokens(o, b, t)
        wk_all = jnp.concatenate([past["win_k"], wk], axis=1)
        wv_all = jnp.concatenate([past["win_v"], wv], axis=1)
        keep = min(WINDOW, wk_all.shape[1])
        win_k, win_v = wk_all[:, -keep:], wv_all[:, -keep:]

    m = _merge(o_fox.reshape(n, -1), o_cmp.reshape(n, -1), o_slc.reshape(n, -1), o_win.reshape(n, -1),
               dense["up_fox"], dense["up_nsa"], mg, d)
    tm = _row_tile(n, MERGE_ROW_TILE)
    h = _matmul(m, dense["out"], _ep_residual, [x2d], [pl.BlockSpec((tm, 512), lambda i, j: (i, j))],
                jax.ShapeDtypeStruct((n, d), F32), pl.BlockSpec((tm, 512), lambda i, j: (i, j)), tm, 512, "out_proj")
    y = _moe(h, p["norm_ffn"], wts["router"], p["router_bias"], wts["gate_e"], wts["up_e"], wts["down_e"])

    def heads(a, hh):
        return a.reshape(b, -1, hh, HEAD_DIM)

    state = (heads(fk3, FOX_HEADS), heads(fv3, FOX_HEADS), logf, heads(ck, g), heads(cv, g), heads(sk, g),
             heads(sv, g), heads(win_k, g), heads(win_v, g))
    return y.reshape(b, t, d), state


def _rows_to_tokens(o, b, t):
    o = o.reshape(b, NSA_KV_GROUPS, HEADS_PER_GROUP, t, HEAD_DIM)
    return jnp.transpose(o, (0, 3, 1, 2, 4)).reshape(b, t, -1)


def _prepare(l, norm_mix, w_in, fox_f_bias, fox_q_norm, fox_k_norm, nsa_q_norm, nsa_k_norm, cmp_pos_k, cmp_w1_k,
             cmp_w2_k, cmp_pos_v, cmp_w1_v, cmp_w2_v, nsa_gate_bias, w_up_fox, w_up_nsa, w_out, norm_ffn,
             w_router_group, b_router_group, w_router_expert, b_router_expert, w_gate_e, w_up_e, w_down_e):
    d = w_in.shape[1]
    fw = FOX_HEADS * HEAD_DIM
    nw = NSA_HEADS * HEAD_DIM
    kvw = NSA_KV_GROUPS * HEAD_DIM
    sizes = (fw, fw, fw, FOX_HEADS, nw, kvw, kvw, kvw, kvw, kvw, kvw, NSA_HEADS * N_NSA_BRANCHES, 2 * d)
    offs = np.concatenate([[0], np.cumsum(sizes)])
    w = w_in[l]

    def seg(i):
        return w[:, offs[i]:offs[i + 1]]

    n_small = FOX_HEADS + NSA_HEADS * N_NSA_BRANCHES
    small_w = jnp.pad(jnp.concatenate([seg(3), seg(11)], axis=1), ((0, 0), (0, LANES - n_small)))
    small_b = jnp.pad(jnp.concatenate([fox_f_bias[l], nsa_gate_bias[l]]), (0, LANES - n_small)).reshape(1, LANES)
    n_route = N_GROUPS + N_EXPERTS
    router_w = jnp.pad(jnp.concatenate([w_router_group[l], w_router_expert[l]], axis=1), ((0, 0), (0, LANES - n_route)))
    router_b = jnp.pad(jnp.concatenate([b_router_group[l], b_router_expert[l]]), (0, LANES - n_route)).reshape(1, LANES)
    dense = {
        "fq": seg(0), "fk": seg(1), "fv": seg(2), "nq": seg(4),
        "nk": jnp.concatenate([seg(5), seg(7), seg(9)], axis=1),
        "nv": jnp.concatenate([seg(6), seg(8), seg(10)], axis=1),
        "mg": seg(12), "small": small_w,
        "up_fox": w_up_fox[l], "up_nsa": w_up_nsa[l], "out": w_out[l],
    }
    wts = {
        **{k: v.astype(BF16) for k, v in dense.items()},
        "f32": dense,
        "router": router_w,
        "gate_e": w_gate_e[l], "up_e": w_up_e[l], "down_e": w_down_e[l],
    }
    p = {
        "norm_mix": norm_mix[l], "fox_q_norm": fox_q_norm[l].reshape(1, -1), "fox_k_norm": fox_k_norm[l].reshape(1, -1),
        "nsa_q_norm": nsa_q_norm[l].reshape(1, -1), "nsa_k_norm": nsa_k_norm[l].reshape(1, -1),
        "small_bias": small_b, "cmp_pos_k": cmp_pos_k[l], "cmp_w1_k": cmp_w1_k[l], "cmp_w2_k": cmp_w2_k[l],
        "cmp_pos_v": cmp_pos_v[l], "cmp_w1_v": cmp_w1_v[l], "cmp_w2_v": cmp_w2_v[l],
        "norm_ffn": norm_ffn[l], "router_bias": router_b,
    }
    return wts, p


def kernel(x_prompt, x_sample, cache_fox_k, cache_fox_v, cache_fox_logf, cache_cmp_k, cache_cmp_v, cache_slc_k, cache_slc_v, cache_win_k, cache_win_v, page_table, norm_mix, w_in, fox_f_bias, fox_q_norm, fox_k_norm, nsa_q_norm, nsa_k_norm, cmp_pos_k, cmp_w1_k, cmp_w2_k, cmp_pos_v, cmp_w1_v, cmp_w2_v, nsa_gate_bias, w_up_fox, w_up_nsa, w_out, norm_ffn, w_router_group, b_router_group, w_router_expert, b_router_expert, w_gate_e, w_up_e, w_down_e):
    depth = w_in.shape[0]
    n_phys, page = cache_fox_k.shape[1], cache_fox_k.shape[2]
    past_len = page_table.shape[1] * page
    h_p, h_s = x_prompt, x_sample
    p_states, s_states = [], []
    for l in range(depth):
        wts, p = _prepare(l, norm_mix, w_in, fox_f_bias, fox_q_norm, fox_k_norm, nsa_q_norm, nsa_k_norm, cmp_pos_k,
                          cmp_w1_k, cmp_w2_k, cmp_pos_v, cmp_w1_v, cmp_w2_v, nsa_gate_bias, w_up_fox, w_up_nsa,
                          w_out, norm_ffn, w_router_group, b_router_group, w_router_expert, b_router_expert,
                          w_gate_e, w_up_e, w_down_e)
        h_p, st_p = _layer(h_p, p, wts, 0, None)
        p_states.append(st_p)
        win_len = cache_win_k.shape[2]
        n_win = win_len // page
        dec_b = page_table.shape[0]

        def pool(c):
            return c.reshape(-1, page * c.shape[3], HEAD_DIM)

        past = {
            "table": page_table + l * n_phys, "page": page,
            "fox_k": pool(cache_fox_k), "fox_v": pool(cache_fox_v),
            "fox_logf": jnp.transpose(cache_fox_logf.reshape(-1, page, cache_fox_logf.shape[3]), (0, 2, 1)),
            "cmp_k": pool(cache_cmp_k), "cmp_v": pool(cache_cmp_v),
            "slc_k": pool(cache_slc_k), "slc_v": pool(cache_slc_v),
            "win_k_pool": pool(cache_win_k), "win_v_pool": pool(cache_win_v),
            "win_table": (l * dec_b + jnp.arange(dec_b, dtype=jnp.int32))[:, None] * n_win
                         + jnp.arange(n_win, dtype=jnp.int32)[None, :],
            "win_k": cache_win_k[l].reshape(dec_b, win_len, -1), "win_v": cache_win_v[l].reshape(dec_b, win_len, -1),
        }
        h_s, st_s = _layer(h_s, p, wts, past_len, past)
        s_states.append(st_s)
    p_out = [jnp.stack([st[i] for st in p_states], axis=0) for i in range(9)]
    s_out = [jnp.stack([st[i] for st in s_states], axis=0) for i in range(9)]
    return (h_p, h_s, *p_out, *s_out)
```
